```python
import math
import jax, jax.numpy as jnp
from jax import lax
import numpy as np

D_MODEL = 2048
BATCH = 16
SEQ = 2048
DEPTH = 1
DEC_BATCH = 16
DEC_SEQ = 16
PAST_LEN = 1024

CHUNK = 64
N_HEADS = 16
HEAD_DIM = D_MODEL // N_HEADS
ATT_WIDTH = N_HEADS * HEAD_DIM
LRU_WIDTH = D_MODEL
LRU_BLOCKS = 8
LRU_BLOCK_DIM = LRU_WIDTH // LRU_BLOCKS
LRU_C = 8.0
CONV_WIDTH = 4
Q_BLOCK = 128
N_EXPERTS = 64
TOP_K = 6
N_GROUPS = 8
TOPK_GROUPS = 4
EXPERT_FF = 512
SHARED_FF = 512
ROUTED_SCALE = 2.5
MOE_BLOCK = 128
NORM_EPS = 1e-6
IN_WIDTH = 2 * LRU_WIDTH + 3 * ATT_WIDTH + 2 * D_MODEL
IN_SPLITS = (LRU_WIDTH, 2 * LRU_WIDTH, 2 * LRU_WIDTH + ATT_WIDTH, 2 * LRU_WIDTH + 2 * ATT_WIDTH,
             2 * LRU_WIDTH + 3 * ATT_WIDTH, 2 * LRU_WIDTH + 3 * ATT_WIDTH + D_MODEL)

kernel_name = "hawk_stickbreak_moe_stream_encoder"


def rms_norm(x, g):
    xf = x.astype(jnp.float32)
    y = xf * lax.rsqrt(jnp.mean(xf * xf, axis=-1, keepdims=True) + NORM_EPS)
    return (y * g.astype(jnp.float32)).astype(x.dtype)


def causal_conv(x, prev, w, b):
    T = x.shape[1]
    if prev is None:
        prev = jnp.zeros((x.shape[0], CONV_WIDTH - 1, x.shape[2]), x.dtype)
    xp = jnp.concatenate([prev.astype(x.dtype), x], axis=1)
    y = b
    for j in range(CONV_WIDTH):
        y = y + xp[:, j:j + T] * w[j]
    return y, xp[:, T:]


def _lin_combine(e1, e2):
    a1, b1 = e1
    a2, b2 = e2
    return a1 * a2, a2 * b1 + b2


def rg_lru(xc, h0, pos, w_a, b_a, w_x, b_x, lam):
    B, T, W = xc.shape
    f32 = jnp.float32
    xf = xc.astype(f32)
    xb = xf.reshape(B, T, LRU_BLOCKS, LRU_BLOCK_DIM)
    r = jax.nn.sigmoid(jnp.einsum('btnj,njk->btnk', xb, w_a.astype(f32)) + b_a.astype(f32)).reshape(B, T, W)
    i = jax.nn.sigmoid(jnp.einsum('btnj,njk->btnk', xb, w_x.astype(f32)) + b_x.astype(f32)).reshape(B, T, W)
    log_a = -LRU_C * r * jax.nn.softplus(-lam.astype(f32))
    a = jnp.exp(log_a)
    mult = jnp.where((pos == 0)[None, :, None], 1.0, jnp.sqrt(-jnp.expm1(2.0 * log_a)))
    bterm = mult * (i * xf)
    if h0 is not None:
        bterm = bterm.at[:, 0].add(a[:, 0] * h0.astype(f32))
    _, h = lax.associative_scan(_lin_combine, (a, bterm), axis=1)
    return h, h[:, -1]


def stick_breaking(q, k, v, q_pos, k_pos):
    z = jnp.einsum('bqhd,bkhd->bhqk', q.astype(jnp.float32), k.astype(jnp.float32)) * (HEAD_DIM ** -0.5)
    mask = (k_pos[None, :] < q_pos[:, None])[None, None]
    log_keep = jnp.where(mask, jax.nn.log_sigmoid(-z), 0.0)
    log_between = lax.cumsum(log_keep, axis=3, reverse=True) - log_keep
    weight = jnp.where(mask, jnp.exp(jax.nn.log_sigmoid(z) + log_between), 0.0)
    return jnp.einsum('bhqk,bkhd->bqhd', weight, v.astype(jnp.float32))


def sb_prompt(q, k, v):
    B, S, H, d = q.shape
    nb = S // Q_BLOCK
    pos = jnp.arange(S)
    qb = q.reshape(B, nb, Q_BLOCK, H, d).transpose(1, 0, 2, 3, 4)
    pb = pos.reshape(nb, Q_BLOCK)
    out = lax.map(lambda a: stick_breaking(a[0], k, v, a[1], pos), (qb, pb))
    return out.transpose(1, 0, 2, 3, 4).reshape(B, S, H, d)


def moe_routed(xf, w_router, b_router, w_gate, w_up, w_down):
    T = xf.shape[0]
    f32 = jnp.float32
    scores = jax.nn.sigmoid(jnp.dot(xf.astype(f32), w_router.astype(f32)))
    sel = scores + b_router.astype(f32)
    grp = sel.reshape(T, N_GROUPS, N_EXPERTS // N_GROUPS)
    grp_score = lax.top_k(grp, 2)[0].sum(-1)
    _, grp_idx = lax.top_k(grp_score, TOPK_GROUPS)
    grp_mask = jax.nn.one_hot(grp_idx, N_GROUPS, dtype=f32).sum(1)
    exp_mask = jnp.repeat(grp_mask, N_EXPERTS // N_GROUPS, axis=1)
    sel = jnp.where(exp_mask > 0, sel, -jnp.inf)
    _, e_idx = lax.top_k(sel, TOP_K)
    w = jnp.take_along_axis(scores, e_idx, axis=1)
    w = w / jnp.sum(w, axis=-1, keepdims=True) * ROUTED_SCALE
    A = T * TOP_K
    flat_e = e_idx.reshape(-1)
    flat_w = w.reshape(-1)
    order = jnp.argsort(flat_e)
    e_sorted = flat_e[order]
    tok_sorted = (order // TOP_K).astype(jnp.int32)
    w_sorted = flat_w[order]
    counts = jnp.zeros((N_EXPERTS,), jnp.int32).at[flat_e].add(1)
    padded = (counts + MOE_BLOCK - 1) // MOE_BLOCK * MOE_BLOCK
    start = jnp.cumsum(counts) - counts
    pad_end = jnp.cumsum(padded)
    pad_start = pad_end - padded
    dest = pad_start[e_sorted] + (jnp.arange(A, dtype=jnp.int32) - start[e_sorted])
    n_blocks = -(-(A + N_EXPERTS * (MOE_BLOCK - 1)) // MOE_BLOCK)
    P = n_blocks * MOE_BLOCK
    row_tok = jnp.zeros((P,), jnp.int32).at[dest].set(tok_sorted)
    row_w = jnp.zeros((P,), f32).at[dest].set(w_sorted)
    block_exp = jnp.minimum(jnp.searchsorted(pad_end, jnp.arange(n_blocks, dtype=jnp.int32) * MOE_BLOCK, side='right'),
                            N_EXPERTS - 1).astype(jnp.int32)

    def expert_block(args):
        toks, e = args
        xb = xf[toks]
        hb = jax.nn.silu(jnp.dot(xb, w_gate[e])) * jnp.dot(xb, w_up[e])
        return jnp.dot(hb, w_down[e])

    y_rows = lax.map(expert_block, (row_tok.reshape(n_blocks, MOE_BLOCK), block_exp))
    y_rows = y_rows.reshape(P, -1)
    y_rows = (y_rows * row_w[:, None].astype(y_rows.dtype)).astype(xf.dtype)
    return jnp.zeros_like(xf).at[row_tok].add(y_rows)


def shared_expert(xf, ws_gate, ws_up, ws_down):
    return jnp.dot(jax.nn.silu(jnp.dot(xf, ws_gate)) * jnp.dot(xf, ws_up), ws_down)


def encoder_layer(x, c, pos, past_k, past_v, h0, conv0,
                  g_mix, w_ada, b_ada, w_in, conv_w, conv_b, w_rg_a, b_rg_a, w_rg_x, b_rg_x, lru_lambda,
                  w_out, g_ffn, w_router, b_router, w_gate, w_up, w_down, ws_gate, ws_up, ws_down):
    B, T, D = x.shape
    mod = jnp.dot(jax.nn.silu(c), w_ada) + b_ada
    sh1, sc1, gt1, sh2, sc2, gt2 = jnp.split(mod[:, None, :], 6, axis=-1)
    hn = rms_norm(x, g_mix) * (1 + sc1) + sh1
    proj = jnp.dot(hn, w_in)
    u_lru, u_gelu, q, k, v, z_lru, z_att = jnp.split(proj, IN_SPLITS, axis=-1)
    xc, conv_new = causal_conv(u_lru, conv0, conv_w, conv_b)
    h, h_last = rg_lru(xc, h0, pos, w_rg_a, b_rg_a, w_rg_x, b_rg_x, lru_lambda)
    y_lru = h.astype(x.dtype) * jax.nn.gelu(u_gelu)
    q = q.reshape(B, T, N_HEADS, HEAD_DIM)
    k = k.reshape(B, T, N_HEADS, HEAD_DIM)
    v = v.reshape(B, T, N_HEADS, HEAD_DIM)
    if past_k is None:
        o = sb_prompt(q, k, v)
    else:
        P = past_k.shape[1]
        k_all = jnp.concatenate([past_k.astype(k.dtype), k], axis=1)
        v_all = jnp.concatenate([past_v.astype(v.dtype), v], axis=1)
        o = stick_breaking(q, k_all, v_all, P + jnp.arange(T), jnp.arange(P + T))
    y_att = o.reshape(B, T, ATT_WIDTH).astype(x.dtype)
    mixed = jax.nn.sigmoid(z_lru) * y_lru + jax.nn.sigmoid(z_att) * y_att
    x = x + gt1 * jnp.dot(mixed, w_out)
    hn2 = (rms_norm(x, g_ffn) * (1 + sc2) + sh2).reshape(B * T, D)
    ffn = moe_routed(hn2, w_router, b_router, w_gate, w_up, w_down) + shared_expert(hn2, ws_gate, ws_up, ws_down)
    x = x + gt2 * ffn.reshape(B, T, D)
    return x, k, v, h_last.astype(x.dtype), conv_new


def setup_inputs(seed: int = 0) -> dict:
    key = jax.random.key(seed)
    ks = jax.random.split(key, 40)
    f32 = jnp.float32
    D, L, W, BD = D_MODEL, DEPTH, LRU_WIDTH, LRU_BLOCK_DIM

    def nrm(k, shape, s):
        return jax.random.normal(k, shape, f32) * s

    u = jax.random.uniform(ks[30], (L, W), f32, 0.9, 0.999)
    return {
        "x_prompt": nrm(ks[0], (BATCH, SEQ, D), 1.0),
        "x_sample": nrm(ks[1], (DEC_BATCH, DEC_SEQ, D), 1.0),
        "cache_k": nrm(ks[2], (L, DEC_BATCH, PAST_LEN, N_HEADS, HEAD_DIM), 1.0),
        "cache_v": nrm(ks[3], (L, DEC_BATCH, PAST_LEN, N_HEADS, HEAD_DIM), 1.0),
        "state_lru": nrm(ks[4], (L, DEC_BATCH, W), 0.5),
        "state_conv": nrm(ks[5], (L, DEC_BATCH, CONV_WIDTH - 1, W), 1.0),
        "c_prompt": nrm(ks[6], (BATCH, D), 1.0),
        "c_sample": nrm(ks[7], (DEC_BATCH, D), 1.0),
        "g_mix": 1.0 + nrm(ks[8], (L, D), 0.05),
        "w_ada": nrm(ks[9], (L, D, 6 * D), 0.5 * D ** -0.5),
        "b_ada": nrm(ks[10], (L, 6 * D), 0.02),
        "w_in": nrm(ks[11], (L, D, IN_WIDTH), D ** -0.5),
        "conv_w": nrm(ks[12], (L, CONV_WIDTH, W), CONV_WIDTH ** -0.5),
        "conv_b": nrm(ks[13], (L, W), 0.02),
        "w_rg_a": nrm(ks[14], (L, LRU_BLOCKS, BD, BD), BD ** -0.5),
        "b_rg_a": nrm(ks[15], (L, LRU_BLOCKS, BD), 0.02),
        "w_rg_x": nrm(ks[16], (L, LRU_BLOCKS, BD, BD), BD ** -0.5),
        "b_rg_x": nrm(ks[17], (L, LRU_BLOCKS, BD), 0.02),
        "lru_lambda": jnp.log(u) - jnp.log1p(-u),
        "w_out": nrm(ks[18], (L, D, D), D ** -0.5),
        "g_ffn": 1.0 + nrm(ks[19], (L, D), 0.05),
        "w_router": nrm(ks[20], (L, D, N_EXPERTS), D ** -0.5),
        "b_router": nrm(ks[21], (L, N_EXPERTS), 0.01),
        "w_gate": nrm(ks[22], (L, N_EXPERTS, D, EXPERT_FF), D ** -0.5),
        "w_up": nrm(ks[23], (L, N_EXPERTS, D, EXPERT_FF), D ** -0.5),
        "w_down": nrm(ks[24], (L, N_EXPERTS, EXPERT_FF, D), EXPERT_FF ** -0.5),
        "ws_gate": nrm(ks[25], (L, D, SHARED_FF), D ** -0.5),
        "ws_up": nrm(ks[26], (L, D, SHARED_FF), D ** -0.5),
        "ws_down": nrm(ks[27], (L, SHARED_FF, D), SHARED_FF ** -0.5),
        "g_final": 1.0 + nrm(ks[28], (D,), 0.05),
    }


def reference(x_prompt, x_sample, cache_k, cache_v, state_lru, state_conv, c_prompt, c_sample,
              g_mix, w_ada, b_ada, w_in, conv_w, conv_b, w_rg_a, b_rg_a, w_rg_x, b_rg_x, lru_lambda,
              w_out, g_ffn, w_router, b_router, w_gate, w_up, w_down, ws_gate, ws_up, ws_down, g_final):
    past_len = cache_k.shape[2]
    xp, xs = x_prompt, x_sample
    pos_p = jnp.arange(xp.shape[1])
    pos_s = past_len + jnp.arange(xs.shape[1])
    kp_l, vp_l, hp_l, cp_l, ks_l, vs_l, hs_l, cs_l = [], [], [], [], [], [], [], []
    for l in range(DEPTH):
        lw = (g_mix[l], w_ada[l], b_ada[l], w_in[l], conv_w[l], conv_b[l], w_rg_a[l], b_rg_a[l],
              w_rg_x[l], b_rg_x[l], lru_lambda[l], w_out[l], g_ffn[l], w_router[l], b_router[l],
              w_gate[l], w_up[l], w_down[l], ws_gate[l], ws_up[l], ws_down[l])
        xp, kp, vp, hp, cp = encoder_layer(xp, c_prompt, pos_p, None, None, None, None, *lw)
        xs, ks_, vs_, hs, cs = encoder_layer(xs, c_sample, pos_s, cache_k[l], cache_v[l],
                                             state_lru[l], state_conv[l], *lw)
        kp_l.append(kp); vp_l.append(vp); hp_l.append(hp); cp_l.append(cp)
        ks_l.append(ks_); vs_l.append(vs_); hs_l.append(hs); cs_l.append(cs)
    y_prompt = rms_norm(xp, g_final)
    y_sample = rms_norm(xs, g_final)
    return (y_prompt, y_sample,
            jnp.stack(kp_l), jnp.stack(vp_l), jnp.stack(hp_l), jnp.stack(cp_l),
            jnp.stack(ks_l), jnp.stack(vs_l), jnp.stack(hs_l), jnp.stack(cs_l))
```

```python
import functools

import jax
import jax.numpy as jnp
from jax import lax
from jax.experimental import pallas as pl
from jax.experimental.pallas import tpu as pltpu

F32 = jnp.float32
BF16 = jnp.bfloat16
I32 = jnp.int32
U32 = jnp.uint32

HEAD_DIM = 128
LRU_BLOCK_DIM = 256
LRU_C = 8.0
CONV_WIDTH = 4
N_EXPERTS = 64
TOP_K = 6
N_GROUPS = 8
GROUP_SIZE = N_EXPERTS // N_GROUPS
TOPK_GROUPS = 4
ROUTED_SCALE = 2.5
NORM_EPS = 1e-6
N_MOD = 6
V7X_VMEM_LIMIT_BYTES = 56 * 1024 * 1024
NT_DIMS = (((1,), (1,)), ((), ()))
NEG_INF = float("-inf")


def _cparams(semantics):
    return pltpu.CompilerParams(dimension_semantics=semantics, vmem_limit_bytes=V7X_VMEM_LIMIT_BYTES)


def _dot(a, b):
    return jnp.dot(a, b, preferred_element_type=F32)


def _split_bf16(x):
    hi = x.astype(BF16)
    lo = (x - hi.astype(F32)).astype(BF16)
    return hi, lo


def _dot3(a, b):
    ah, al = _split_bf16(a)
    bh, bl = _split_bf16(b)
    return _dot(ah, bh) + (_dot(al, bh) + _dot(ah, bl))


def _pack_pairs(h):
    n = h.shape[1] // 2
    bits = lax.bitcast_convert_type(h.astype(BF16).astype(F32), U32)
    return bits[:, :n] | (bits[:, n:] >> 16)


def _unpack_pairs(p, dtype):
    a = lax.bitcast_convert_type(p & jnp.uint32(0xFFFF0000), F32)
    b = lax.bitcast_convert_type(p << 16, F32)
    return jnp.concatenate([a.astype(dtype), b.astype(dtype)], axis=1)


def _rms(x):
    return x * lax.rsqrt(jnp.mean(x * x, axis=-1, keepdims=True) + NORM_EPS)


def _ada_kernel(c_ref, w_ref, b_ref, o_ref):
    c = c_ref[...]
    o_ref[...] = _dot3(c * jax.nn.sigmoid(c), w_ref[...]) + b_ref[...]


def _ada(c, w, b):
    m, d = c.shape
    n = w.shape[1]
    tn = min(n, 1024)
    return pl.pallas_call(
        _ada_kernel,
        grid=(n // tn,),
        in_specs=[pl.BlockSpec((m, d), lambda j: (0, 0)),
                  pl.BlockSpec((d, tn), lambda j: (0, j)),
                  pl.BlockSpec((1, tn), lambda j: (0, j))],
        out_specs=pl.BlockSpec((m, tn), lambda j: (0, j)),
        out_shape=jax.ShapeDtypeStruct((m, n), F32),
        compiler_params=_cparams(("arbitrary",)),
        name="ada",
    )(c, w, b.reshape(1, n))


def _norm_mod_kernel(x_ref, mod_ref, g_ref, obm_ref, otm_ref):
    m = mod_ref[0]
    h = (_rms(x_ref[0]) * g_ref[...] * (1.0 + m[1:2]) + m[0:1]).astype(BF16)
    obm_ref[...] = h
    otm_ref[...] = h


def _norm_mod(x, mod, g, tt):
    b, t, d = x.shape
    nt = t // tt
    return pl.pallas_call(
        _norm_mod_kernel,
        grid=(b, nt),
        in_specs=[pl.BlockSpec((1, tt, d), lambda bi, ti: (bi, ti, 0)),
                  pl.BlockSpec((1, N_MOD, d), lambda bi, ti: (bi, 0, 0)),
                  pl.BlockSpec((1, d), lambda bi, ti: (0, 0))],
        out_specs=[pl.BlockSpec((tt, d), lambda bi, ti: (bi * nt + ti, 0)),
                   pl.BlockSpec((tt, d), lambda bi, ti: (ti, bi))],
        out_shape=[jax.ShapeDtypeStruct((b * t, d), BF16), jax.ShapeDtypeStruct((t, b * d), BF16)],
        compiler_params=_cparams(("arbitrary", "arbitrary")),
        name="norm_mod",
    )(x, mod, g.reshape(1, d))


def _proj_kernel(a_ref, w_ref, o_ref, *, epilogue):
    acc = _dot(a_ref[...], w_ref[...])
    if epilogue == "gelu":
        acc = jax.nn.gelu(acc)
    elif epilogue == "sigmoid":
        acc = jax.nn.sigmoid(acc)
    elif epilogue == "qscale":
        acc = acc * (HEAD_DIM ** -0.5)
    o_ref[...] = acc.astype(o_ref.dtype)


def _proj(a, w, col, width, epilogue, out_dtype, tm):
    m, k = a.shape
    return pl.pallas_call(
        functools.partial(_proj_kernel, epilogue=epilogue),
        grid=(m // tm,),
        in_specs=[pl.BlockSpec((tm, k), lambda i: (i, 0)),
                  pl.BlockSpec((k, width), lambda i: (0, col))],
        out_specs=pl.BlockSpec((tm, width), lambda i: (i, 0)),
        out_shape=jax.ShapeDtypeStruct((m, width), out_dtype),
        compiler_params=_cparams(("arbitrary",)),
        name="proj_" + epilogue,
    )(a, w)


def _lru_kernel(u_ref, gl_ref, sl_ref, cw_ref, cb_ref, wa_ref, ba_ref, wx_ref, bx_ref, lam_ref, h0_ref, prev_ref,
                y_ref, hl_ref, cn_ref, tail_s, h_s, a_s, b_s, *, first_pos_zero):
    tc, nb, wb = u_ref.shape
    ti = pl.program_id(1)

    @pl.when(ti == 0)
    def _():
        tail_s[...] = prev_ref[...]
        h_s[...] = h0_ref[...]

    u = u_ref[...].astype(F32)
    xp = jnp.concatenate([tail_s[...], u], axis=0)
    cw = cw_ref[...]
    xc = cb_ref[...].reshape(1, 1, wb)
    for j in range(CONV_WIDTH):
        xc = xc + xp[j:j + tc] * cw[j:j + 1].reshape(1, 1, wb)
    last_rows = u[tc - (CONV_WIDTH - 1):]
    tail_s[...] = last_rows
    cn_ref[...] = last_rows

    x2 = xc.reshape(tc * nb, wb)
    xb = x2.astype(BF16)
    r = jax.nn.sigmoid(_dot(xb, wa_ref[0]) + ba_ref[0])
    gi = jax.nn.sigmoid(_dot(xb, wx_ref[0]) + bx_ref[0])
    log_a = (-LRU_C) * r * jax.nn.softplus(-lam_ref[...])
    a = jnp.exp(log_a)
    mult = jnp.sqrt(-jnp.tanh(log_a) * (a * a + 1.0))
    if first_pos_zero:
        row = lax.broadcasted_iota(I32, (tc * nb, wb), 0)
        mult = jnp.where((row < nb) & (ti == 0), 1.0, mult)
    a_s[...] = a.reshape(tc, nb, wb)
    b_s[...] = (mult * (gi * x2)).reshape(tc, nb, wb)

    def step(t, h):
        h = a_s[t] * h + b_s[t]
        gate = gl_ref[t].astype(F32) * sl_ref[t].astype(F32)
        y_ref[t] = (h * gate).astype(BF16)
        return h

    h = lax.fori_loop(0, tc, step, h_s[...], unroll=8)
    h_s[...] = h
    hl_ref[...] = h


def _lru(u, gl, sl, conv_w, conv_b, wa, ba, wx, bx, lam, h0, prev, tc, first_pos_zero):
    t, b, w = u.shape
    wb = LRU_BLOCK_DIM
    nw = w // wb
    seq = pl.BlockSpec((tc, b, wb), lambda wi, ti: (ti, 0, wi))
    chan = lambda rows: pl.BlockSpec((rows, wb), lambda wi, ti: (0, wi))
    gate_w = pl.BlockSpec((1, wb, wb), lambda wi, ti: (wi, 0, 0))
    gate_b = pl.BlockSpec((1, 1, wb), lambda wi, ti: (wi, 0, 0))
    tail = pl.BlockSpec((CONV_WIDTH - 1, b, wb), lambda wi, ti: (0, 0, wi))
    return pl.pallas_call(
        functools.partial(_lru_kernel, first_pos_zero=first_pos_zero),
        grid=(nw, t // tc),
        in_specs=[seq, seq, seq, chan(CONV_WIDTH), chan(1), gate_w, gate_b, gate_w, gate_b, chan(1),
                  chan(b), tail],
        out_specs=[seq, chan(b), tail],
        out_shape=[jax.ShapeDtypeStruct((t, b, w), BF16), jax.ShapeDtypeStruct((b, w), F32),
                   jax.ShapeDtypeStruct((CONV_WIDTH - 1, b, w), F32)],
        scratch_shapes=[pltpu.VMEM((CONV_WIDTH - 1, b, wb), F32), pltpu.VMEM((b, wb), F32),
                        pltpu.VMEM((tc, b, wb), F32), pltpu.VMEM((tc, b, wb), F32)],
        compiler_params=_cparams(("arbitrary", "arbitrary")),
        name="lru",
    )(u, gl, sl, conv_w, conv_b.reshape(1, w), wa, ba.reshape(nw, 1, wb), wx, bx.reshape(nw, 1, wb),
      lam.reshape(1, w), h0, prev)


def _log_keep_and_logsig(z):
    nz = -z
    lk = jnp.minimum(nz, 0.0) - jnp.log(1.0 + jnp.exp(jnp.minimum(z, nz)))
    return lk, z + lk


def _later_key_sums(lk, tri):
    hi, lo = _split_bf16(lk)
    return _dot(hi, tri) + _dot(lo, tri)


def _strict_lower(n, dtype):
    row = lax.broadcasted_iota(I32, (n, n), 0)
    col = lax.broadcasted_iota(I32, (n, n), 1)
    return (row > col).astype(dtype), col < row


def _attn_kernel(q_ref, k_ref, v_ref, sz_ref, yl_ref, o_ref):
    tq = q_ref.shape[1]
    qi = pl.program_id(2)
    q = q_ref[0]
    tri, causal = _strict_lower(tq, BF16)

    def tile(j, acc, c, diag):
        start = pl.multiple_of(j * tq, tq)
        kb = k_ref[0, pl.ds(start, tq), :].astype(BF16)
        vb = v_ref[0, pl.ds(start, tq), :].astype(BF16)
        z = lax.dot_general(q, kb, NT_DIMS, preferred_element_type=F32)
        lk, ls = _log_keep_and_logsig(z)
        if diag:
            lk = jnp.where(causal, lk, 0.0)
        rc = _later_key_sums(lk, tri)
        w = jnp.exp(ls + rc + c)
        if diag:
            w = jnp.where(causal, w, 0.0)
        acc = acc + _dot(w.astype(BF16), vb)
        c = c + (rc[:, 0:1] + lk[:, 0:1])
        return acc, c

    acc, c = tile(qi, jnp.zeros((tq, HEAD_DIM), F32), jnp.zeros((tq, 1), F32), True)
    acc, c = lax.fori_loop(0, qi, lambda it, carry: tile(qi - 1 - it, carry[0], carry[1], False), (acc, c))
    o_ref[0] = (yl_ref[...].astype(F32) + sz_ref[0].astype(F32) * acc).astype(BF16)


def _attn_prompt(q, k, v, sz, yl_tm, tq):
    b, t, a = q.shape
    nh = a // HEAD_DIM
    qspec = pl.BlockSpec((1, tq, HEAD_DIM), lambda bi, hi, qi: (bi, qi, hi))
    kvspec = pl.BlockSpec((1, t, HEAD_DIM), lambda bi, hi, qi: (bi, 0, hi))
    return pl.pallas_call(
        _attn_kernel,
        grid=(b, nh, t // tq),
        in_specs=[qspec, kvspec, kvspec, qspec,
                  pl.BlockSpec((tq, HEAD_DIM), lambda bi, hi, qi: (qi, bi * nh + hi))],
        out_specs=qspec,
        out_shape=jax.ShapeDtypeStruct((b, t, a), BF16),
        compiler_params=_cparams(("arbitrary", "arbitrary", "arbitrary")),
        name="attn_prompt",
    )(q, k, v, sz, yl_tm)


def _attn_sample_kernel(q_ref, kn_ref, vn_ref, kc_ref, vc_ref, sz_ref, yl_ref, o_ref, lk_s, ls_s):
    ts = q_ref.shape[1]
    nh = q_ref.shape[2] // HEAD_DIM
    past = kc_ref.shape[1]
    tri_n, causal = _strict_lower(ts, BF16)
    tri_p, _ = _strict_lower(past, BF16)
    accs, carries = [], []
    for h in range(nh):
        cols = slice(h * HEAD_DIM, (h + 1) * HEAD_DIM)
        q = q_ref[0, :, cols]
        zn = lax.dot_general(q, kn_ref[0, :, cols].astype(BF16), NT_DIMS, preferred_element_type=F32)
        lk, ls = _log_keep_and_logsig(zn)
        lk = jnp.where(causal, lk, 0.0)
        rc = _later_key_sums(lk, tri_n)
        w = jnp.where(causal, jnp.exp(ls + rc), 0.0)
        accs.append(_dot(w.astype(BF16), vn_ref[0, :, cols].astype(BF16)))
        carries.append(rc[:, 0:1] + lk[:, 0:1])
        zp = lax.dot_general(q, kc_ref[0, :, cols].astype(BF16), NT_DIMS, preferred_element_type=F32)
        lkp, lsp = _log_keep_and_logsig(zp)
        lk_s[h * ts:(h + 1) * ts, :] = lkp
        ls_s[h * ts:(h + 1) * ts, :] = lsp
    rcp = _later_key_sums(lk_s[...], tri_p)
    for h in range(nh):
        cols = slice(h * HEAD_DIM, (h + 1) * HEAD_DIM)
        rows = slice(h * ts, (h + 1) * ts)
        w = jnp.exp(ls_s[rows, :] + rcp[rows, :] + carries[h])
        acc = accs[h] + _dot(w.astype(BF16), vc_ref[0, :, cols].astype(BF16))
        o_ref[0, :, cols] = (yl_ref[:, cols].astype(F32) + sz_ref[0, :, cols].astype(F32) * acc).astype(BF16)


def _attn_sample(q, kn, vn, kc, vc, sz, yl_tm):
    b, ts, a = q.shape
    past = kc.shape[1]
    nh = a // HEAD_DIM
    cur = pl.BlockSpec((1, ts, a), lambda bi: (bi, 0, 0))
    cache = pl.BlockSpec((1, past, a), lambda bi: (bi, 0, 0))
    return pl.pallas_call(
        _attn_sample_kernel,
        grid=(b,),
        in_specs=[cur, cur, cur, cache, cache, cur, pl.BlockSpec((ts, a), lambda bi: (0, bi))],
        out_specs=cur,
        out_shape=jax.ShapeDtypeStruct((b, ts, a), BF16),
        scratch_shapes=[pltpu.VMEM((nh * ts, past), F32), pltpu.VMEM((nh * ts, past), F32)],
        compiler_params=_cparams(("arbitrary",)),
        name="attn_sample",
    )(q, kn, vn, kc, vc, sz, yl_tm)


def _outproj_kernel(a_ref, w_ref, x_ref, mod_ref, g_ref, x1_ref, hpk_ref, hlo_ref):
    nb = mod_ref.shape[0]
    tm, d = x_ref.shape
    m = mod_ref[...]
    o = _dot(a_ref[...], w_ref[...]).reshape(nb, tm // nb, d)
    x1 = x_ref[...].reshape(nb, tm // nb, d) + m[:, 2:3, :] * o
    h = _rms(x1) * g_ref[...].reshape(1, 1, d) * (1.0 + m[:, 4:5, :]) + m[:, 3:4, :]
    x1_ref[...] = x1.reshape(tm, d)
    h = h.reshape(tm, d)
    hi = h.astype(BF16).astype(F32)
    hpk_ref[...] = _pack_pairs(hi)
    hlo_ref[...] = (h - hi).astype(BF16)


def _outproj(a, w, x, mod, g, tm, rows_per_batch):
    m, d = x.shape
    nb = max(tm // rows_per_batch, 1)
    per = max(rows_per_batch // tm, 1)
    row = lambda width: pl.BlockSpec((tm, width), lambda i: (i, 0))
    return pl.pallas_call(
        _outproj_kernel,
        grid=(m // tm,),
        in_specs=[row(d), pl.BlockSpec((d, d), lambda i: (0, 0)), row(d),
                  pl.BlockSpec((nb, N_MOD, d), lambda i: (i // per, 0, 0)),
                  pl.BlockSpec((1, d), lambda i: (0, 0))],
        out_specs=[row(d), row(d // 2), row(d)],
        out_shape=[jax.ShapeDtypeStruct((m, d), F32), jax.ShapeDtypeStruct((m, d // 2), U32),
                   jax.ShapeDtypeStruct((m, d), BF16)],
        compiler_params=_cparams(("arbitrary",)),
        name="outproj",
    )(a, w, x, mod, g.reshape(1, d))


def _sub_max(x):
    return jnp.max(x, axis=0, keepdims=True)


def _sub_min(x):
    return jnp.min(x, axis=0, keepdims=True)


def _sub_sum(x):
    return jnp.sum(x, axis=0, keepdims=True)


def _router_kernel(hpk_ref, hlo_ref, wrh_ref, wrl_ref, br_ref, e_ref, w_ref):
    tm = hpk_ref.shape[0]
    hi = _unpack_pairs(hpk_ref[...], BF16)
    lo = hlo_ref[...]
    wrh = wrh_ref[...]
    logits = lax.dot_general(wrh, hi, NT_DIMS, preferred_element_type=F32)
    logits = logits + (lax.dot_general(wrh, lo, NT_DIMS, preferred_element_type=F32)
                       + lax.dot_general(wrl_ref[...], hi, NT_DIMS, preferred_element_type=F32))
    score = jax.nn.sigmoid(logits)
    sel = (score + br_ref[...]).reshape(GROUP_SIZE, N_GROUPS, tm)
    score = score.reshape(GROUP_SIZE, N_GROUPS, tm)

    m1 = sel[0]
    m2 = jnp.full_like(m1, NEG_INF)
    for e in range(1, GROUP_SIZE):
        x = sel[e]
        m2 = jnp.maximum(m2, jnp.minimum(m1, x))
        m1 = jnp.maximum(m1, x)
    cur = m1 + m2
    gidx = lax.broadcasted_iota(I32, (N_GROUPS, tm), 0)
    gmask = jnp.zeros((N_GROUPS, tm), jnp.bool_)
    for _ in range(TOPK_GROUPS):
        first = _sub_min(jnp.where(cur == _sub_max(cur), gidx, N_GROUPS))
        pick = gidx == first
        gmask = gmask | pick
        cur = jnp.where(pick, NEG_INF, cur)

    eid = (lax.broadcasted_iota(I32, (GROUP_SIZE, N_GROUPS, tm), 1) * GROUP_SIZE
           + lax.broadcasted_iota(I32, (GROUP_SIZE, N_GROUPS, tm), 0))
    cur = jnp.where(gmask[None], sel, NEG_INF)
    ids, wts = [], []
    for _ in range(TOP_K):
        mx = _sub_max(jnp.max(cur, axis=0))
        first = _sub_min(jnp.min(jnp.where(cur == mx[None], eid, N_EXPERTS), axis=0))
        pick = eid == first[None]
        ids.append(first)
        wts.append(_sub_sum(jnp.sum(jnp.where(pick, score, 0.0), axis=0)))
        cur = jnp.where(pick, NEG_INF, cur)
    total = wts[0]
    for k in range(1, TOP_K):
        total = total + wts[k]
    pad = 8 - TOP_K
    e_ref[...] = jnp.concatenate(ids + [jnp.zeros((pad, tm), I32)], axis=0)
    w_ref[...] = jnp.concatenate([w / total * ROUTED_SCALE for w in wts] + [jnp.zeros((pad, tm), F32)], axis=0)


def _router(hpk, hlo, w_router, b_router, tm):
    n, dh = hpk.shape
    d = 2 * dh
    wr = w_router.T.reshape(N_GROUPS, GROUP_SIZE, d).transpose(1, 0, 2).reshape(N_EXPERTS, d)
    br = b_router.reshape(N_GROUPS, GROUP_SIZE).T.reshape(N_EXPERTS, 1)
    wrh = wr.astype(BF16)
    wrl = (wr - wrh.astype(F32)).astype(BF16)
    full = lambda shape: pl.BlockSpec(shape, lambda i: (0, 0))
    return pl.pallas_call(
        _router_kernel,
        grid=(n // tm,),
        in_specs=[pl.BlockSpec((tm, dh), lambda i: (i, 0)), pl.BlockSpec((tm, d), lambda i: (i, 0)),
                  full((N_EXPERTS, d)), full((N_EXPERTS, d)), full((N_EXPERTS, 1))],
        out_specs=[pl.BlockSpec((8, tm), lambda i: (0, i)), pl.BlockSpec((8, tm), lambda i: (0, i))],
        out_shape=[jax.ShapeDtypeStruct((8, n), I32), jax.ShapeDtypeStruct((8, n), F32)],
        compiler_params=_cparams(("arbitrary",)),
        name="router",
    )(hpk, hlo, wrh, wrl, br)


def _row_copy(src, src_row, dst, dst_row, sem):
    return pltpu.make_async_copy(src.at[pl.ds(src_row, 1)], dst.at[pl.ds(dst_row, 1)], sem)


def _dispatch_kernel(dest_ref, h_ref, init_ref, xs_ref, sem):
    del init_ref
    tb = dest_ref.shape[1]
    base = pl.program_id(0) * tb

    def start(t, carry):
        for k in range(TOP_K):
            _row_copy(h_ref, base + t, xs_ref, dest_ref[k, t], sem).start()
        return carry

    def wait(t, carry):
        for k in range(TOP_K):
            _row_copy(h_ref, 0, xs_ref, 0, sem).wait()
        return carry

    lax.fori_loop(0, tb, start, 0)
    lax.fori_loop(0, tb, wait, 0)


def _dispatch(dest, hpk, n_rows, tb):
    n, dh = hpk.shape
    return pl.pallas_call(
        _dispatch_kernel,
        grid=(n // tb,),
        in_specs=[pl.BlockSpec((8, tb), lambda i: (0, i), memory_space=pltpu.SMEM),
                  pl.BlockSpec(memory_space=pl.ANY), pl.BlockSpec(memory_space=pl.ANY)],
        out_specs=pl.BlockSpec(memory_space=pl.ANY),
        out_shape=jax.ShapeDtypeStruct((n_rows, dh), U32),
        scratch_shapes=[pltpu.SemaphoreType.DMA(())],
        input_output_aliases={2: 0},
        compiler_params=_cparams(("arbitrary",)),
        name="dispatch",
    )(dest, hpk, jnp.zeros((n_rows, dh), U32))


def _experts_kernel(be_ref, nu_ref, x_ref, wg_ref, wu_ref, wd_ref, o_ref, wg_s, wu_s, wd_s):
    i = pl.program_id(0)
    prev = be_ref[jnp.maximum(i - 1, 0)]

    @pl.when((i == 0) | (be_ref[i] != prev))
    def _():
        wg_s[...] = wg_ref[0].astype(BF16)
        wu_s[...] = wu_ref[0].astype(BF16)
        wd_s[...] = wd_ref[0].astype(BF16)

    @pl.when(i < nu_ref[0])
    def _():
        x = _unpack_pairs(x_ref[...], BF16)
        g = _dot(x, wg_s[...])
        hidden = (g * jax.nn.sigmoid(g)) * _dot(x, wu_s[...])
        o_ref[...] = _pack_pairs(_dot(hidden.astype(BF16), wd_s[...]))

    @pl.when(i >= nu_ref[0])
    def _():
        o_ref[...] = jnp.zeros_like(o_ref)


def _experts(block_expert, n_used, xs, w_gate, w_up, w_down, tm):
    p, dh = xs.shape
    d = 2 * dh
    ff = w_gate.shape[2]
    grid_spec = pltpu.PrefetchScalarGridSpec(
        num_scalar_prefetch=2,
        grid=(p // tm,),
        in_specs=[pl.BlockSpec((tm, dh), lambda i, be, nu: (i, 0)),
                  pl.BlockSpec((1, d, ff), lambda i, be, nu: (be[i], 0, 0)),
                  pl.BlockSpec((1, d, ff), lambda i, be, nu: (be[i], 0, 0)),
                  pl.BlockSpec((1, ff, d), lambda i, be, nu: (be[i], 0, 0))],
        out_specs=pl.BlockSpec((tm, dh), lambda i, be, nu: (i, 0)),
        scratch_shapes=[pltpu.VMEM((d, ff), BF16), pltpu.VMEM((d, ff), BF16), pltpu.VMEM((ff, d), BF16)],
    )
    return pl.pallas_call(
        _experts_kernel,
        grid_spec=grid_spec,
        out_shape=jax.ShapeDtypeStruct((p, dh), U32),
        compiler_params=_cparams(("arbitrary",)),
        name="experts",
    )(block_expert, n_used, xs, w_gate, w_up, w_down)


def _combine_kernel(dest_ref, wk_ref, ys_ref, x1_ref, hpk_ref, gt_ref, wsg_ref, wsu_ref, wsd_ref, gf_ref,
                    yp_ref, ysm_ref, buf, sem, *, n_prompt_blocks):
    tb, d = x1_ref.shape
    seg = gt_ref.shape[0]
    i = pl.program_id(0)

    def start(t, carry):
        for k in range(TOP_K):
            pltpu.make_async_copy(ys_ref.at[pl.ds(dest_ref[k, t], 1)], buf.at[k, pl.ds(t, 1)], sem).start()
        return carry

    def wait(t, carry):
        for k in range(TOP_K):
            pltpu.make_async_copy(ys_ref.at[pl.ds(0, 1)], buf.at[k, pl.ds(0, 1)], sem).wait()
        return carry

    lax.fori_loop(0, tb, start, 0)
    x = _unpack_pairs(hpk_ref[...], BF16)
    g = _dot(x, wsg_ref[...])
    ffn = _dot(((g * jax.nn.sigmoid(g)) * _dot(x, wsu_ref[...])).astype(BF16), wsd_ref[...])
    lax.fori_loop(0, tb, wait, 0)
    wk = wk_ref[...]
    for k in range(TOP_K):
        ffn = ffn + wk[:, k:k + 1] * _unpack_pairs(buf[k], F32)
    x2 = x1_ref[...].reshape(seg, tb // seg, d) + gt_ref[...] * ffn.reshape(seg, tb // seg, d)
    y = (_rms(x2) * gf_ref[...].reshape(1, 1, d)).reshape(tb, d)

    @pl.when(i < n_prompt_blocks)
    def _():
        yp_ref[...] = y

    @pl.when(i >= n_prompt_blocks)
    def _():
        ysm_ref[...] = y


def _combine(dest, wk_t, ys, x1, hpk, gt2_seg, ws_gate, ws_up, ws_down, g_final, n_prompt, tb):
    n, d = x1.shape
    dh = d // 2
    ff = ws_gate.shape[1]
    npb = n_prompt // tb
    nsb = (n - n_prompt) // tb
    seg = gt2_seg.shape[0] * tb // n
    row = lambda width: pl.BlockSpec((tb, width), lambda i: (i, 0))
    full = lambda shape: pl.BlockSpec(shape, lambda i: (0,) * len(shape))
    return pl.pallas_call(
        functools.partial(_combine_kernel, n_prompt_blocks=npb),
        grid=(n // tb,),
        in_specs=[pl.BlockSpec((8, tb), lambda i: (0, i), memory_space=pltpu.SMEM),
                  row(8), pl.BlockSpec(memory_space=pl.ANY), row(d), row(dh),
                  pl.BlockSpec((seg, 1, d), lambda i: (i, 0, 0)),
                  full((d, ff)), full((d, ff)), full((ff, d)), full((1, d))],
        out_specs=[pl.BlockSpec((tb, d), lambda i: (jnp.minimum(i, npb - 1), 0)),
                   pl.BlockSpec((tb, d), lambda i: (jnp.clip(i - npb, 0, nsb - 1), 0))],
        out_shape=[jax.ShapeDtypeStruct((n_prompt, d), F32), jax.ShapeDtypeStruct((n - n_prompt, d), F32)],
        scratch_shapes=[pltpu.VMEM((TOP_K, tb, dh), U32), pltpu.SemaphoreType.DMA(())],
        compiler_params=_cparams(("arbitrary",)),
        name="combine",
    )(dest, wk_t, ys, x1, hpk, gt2_seg, ws_gate, ws_up, ws_down, g_final.reshape(1, d))


def _largest_tile(n, target):
    t = min(n, target)
    while n % t:
        t -= 1
    return t


def _mixer(x, mod, past_k, past_v, h0, conv0, p, first_pos_zero):
    b, t, d = x.shape
    m = b * t
    a = d
    tt = _largest_tile(t, 256)
    hn_bm, hn_tm = _norm_mod(x, mod, p["g_mix"], tt)
    hn_tm = hn_tm.reshape(t * b, d)
    tm = _largest_tile(m, 512)
    w_in = p["w_in"]
    u = _proj(hn_tm, w_in, 0, d, "plain", BF16, tm).reshape(t, b, d)
    gl = _proj(hn_tm, w_in, 1, d, "gelu", BF16, tm).reshape(t, b, d)
    sl = _proj(hn_tm, w_in, 5, d, "sigmoid", BF16, tm).reshape(t, b, d)
    q = _proj(hn_bm, w_in, 2, a, "qscale", BF16, tm).reshape(b, t, a)
    k = _proj(hn_bm, w_in, 3, a, "plain", F32, tm).reshape(b, t, a)
    v = _proj(hn_bm, w_in, 4, a, "plain", F32, tm).reshape(b, t, a)
    sz = _proj(hn_bm, w_in, 6, a, "sigmoid", BF16, tm).reshape(b, t, a)

    if h0 is None:
        h0 = jnp.zeros((b, d), F32)
        prev = jnp.zeros((CONV_WIDTH - 1, b, d), F32)
    else:
        prev = conv0.transpose(1, 0, 2)
    tc = _largest_tile(t, 128)
    yl, h_last, conv_tail = _lru(u, gl, sl, p["conv_w"], p["conv_b"], p["w_rg_a"], p["b_rg_a"], p["w_rg_x"],
                                 p["b_rg_x"], p["lru_lambda"], h0, prev, tc, first_pos_zero)
    yl = yl.reshape(t, b * d)
    if past_k is None:
        mixed = _attn_prompt(q, k, v, sz, yl, _largest_tile(t, 256))
    else:
        past = past_k.shape[1]
        mixed = _attn_sample(q, k, v, past_k.reshape(b, past, a), past_v.reshape(b, past, a), sz, yl)
    tmo = _largest_tile(m, 256)
    x1, hpk, hlo = _outproj(mixed.reshape(m, a), p["w_out"], x.reshape(m, d), mod, p["g_ffn"], tmo, t)
    nh = a // HEAD_DIM
    return (x1, hpk, hlo, k.reshape(b, t, nh, HEAD_DIM), v.reshape(b, t, nh, HEAD_DIM), h_last,
            conv_tail.transpose(1, 0, 2))


def _routing_tables(e_k, n_tokens, tm):
    flat_e = e_k[:TOP_K].reshape(-1)
    n_assign = flat_e.shape[0]
    order = jnp.argsort(flat_e)
    e_sorted = flat_e[order]
    counts = jnp.zeros((N_EXPERTS,), I32).at[flat_e].add(1)
    padded = (counts + tm - 1) // tm * tm
    start = jnp.cumsum(counts) - counts
    pad_end = jnp.cumsum(padded)
    pad_start = pad_end - padded
    dest_sorted = pad_start[e_sorted] + (jnp.arange(n_assign, dtype=I32) - start[e_sorted])
    dest = jnp.zeros((n_assign,), I32).at[order].set(dest_sorted.astype(I32))
    n_blocks = -(-(n_assign + N_EXPERTS * (tm - 1)) // tm)
    block_expert = jnp.minimum(
        jnp.searchsorted(pad_end, jnp.arange(n_blocks, dtype=I32) * tm, side="right"), N_EXPERTS - 1).astype(I32)
    n_used = (pad_end[-1] // tm).astype(I32).reshape(1)
    dest = jnp.concatenate([dest.reshape(TOP_K, n_tokens), jnp.zeros((8 - TOP_K, n_tokens), I32)], axis=0)
    return dest, block_expert, n_used, n_blocks * tm


def kernel(x_prompt, x_sample, cache_k, cache_v, state_lru, state_conv, c_prompt, c_sample, g_mix, w_ada, b_ada, w_in, conv_w, conv_b, w_rg_a, b_rg_a, w_rg_x, b_rg_x, lru_lambda, w_out, g_ffn, w_router, b_router, w_gate, w_up, w_down, ws_gate, ws_up, ws_down, g_final):
    assert g_mix.shape[0] == 1, "single-layer trunk only"
    b, t, d = x_prompt.shape
    bs, ts, _ = x_sample.shape
    assert ts >= CONV_WIDTH - 1
    p = dict(g_mix=g_mix[0], w_in=w_in[0].astype(BF16), conv_w=conv_w[0], conv_b=conv_b[0],
             w_rg_a=w_rg_a[0].astype(BF16), b_rg_a=b_rg_a[0], w_rg_x=w_rg_x[0].astype(BF16), b_rg_x=b_rg_x[0],
             lru_lambda=lru_lambda[0], w_out=w_out[0].astype(BF16), g_ffn=g_ffn[0])
    mod = _ada(jnp.concatenate([c_prompt, c_sample], axis=0), w_ada[0], b_ada[0]).reshape(b + bs, N_MOD, d)
    mod_p, mod_s = mod[:b], mod[b:]

    x1p, hpkp, hlop, kp, vp, hp, cp = _mixer(x_prompt, mod_p, None, None, None, None, p, True)
    x1s, hpks, hlos, ks, vs, hs, cs = _mixer(x_sample, mod_s, cache_k[0], cache_v[0], state_lru[0], state_conv[0],
                                             p, False)

    n_prompt = b * t
    x1 = jnp.concatenate([x1p, x1s], axis=0)
    hpk = jnp.concatenate([hpkp, hpks], axis=0)
    hlo = jnp.concatenate([hlop, hlos], axis=0)
    n = x1.shape[0]
    tb = _largest_tile(bs * ts, 256)
    assert n_prompt % tb == 0 and tb % ts == 0 and t % ts == 0

    e_k, w_k = _router(hpk, hlo, w_router[0], b_router[0], tb)
    tme = 256
    dest, block_expert, n_used, n_rows = _routing_tables(e_k, n, tme)
    xs = _dispatch(dest, hpk, n_rows, tb)
    ys = _experts(block_expert, n_used, xs, w_gate[0], w_up[0], w_down[0], tme)
    gt2 = jnp.concatenate([jnp.broadcast_to(mod_p[:, None, 5:6, :], (b, t // ts, 1, d)).reshape(-1, 1, d),
                           mod_s[:, 5:6, :]], axis=0)
    y_p, y_s = _combine(dest, w_k.T, ys, x1, hpk, gt2, ws_gate[0].astype(BF16), ws_up[0].astype(BF16),
                        ws_down[0].astype(BF16), g_final, n_prompt, tb)
    return (y_p.reshape(b, t, d), y_s.reshape(bs, ts, d), kp[None], vp[None], hp[None], cp[None],
            ks[None], vs[None], hs[None], cs[None])
```

```python
import functools

import jax
import jax.numpy as jnp
from jax import lax
from jax.experimental import pallas as pl
from jax.experimental.pallas import tpu as pltpu

F32 = jnp.float32
BF16 = jnp.bfloat16
I32 = jnp.int32
U32 = jnp.uint32

HEAD_DIM = 128
LRU_BLOCK_DIM = 256
LRU_C = 8.0
CONV_WIDTH = 4
N_EXPERTS = 64
TOP_K = 6
N_GROUPS = 8
GROUP_SIZE = N_EXPERTS // N_GROUPS
TOPK_GROUPS = 4
ROUTED_SCALE = 2.5
NORM_EPS = 1e-6
N_MOD = 6
TIME_GROUP = 16
LOG2_E = 1.4426950408889634
ATTN_HEADS_PER_STEP = 4
DMA_LOOP_UNROLL = 8
EXPERT_ROW_BLOCK = 256
V7X_VMEM_LIMIT_BYTES = 56 * 1024 * 1024
NT_DIMS = (((1,), (1,)), ((), ()))
NEG_INF = float("-inf")


def _cparams(semantics):
    return pltpu.CompilerParams(dimension_semantics=semantics, vmem_limit_bytes=V7X_VMEM_LIMIT_BYTES)


def _dot(a, b):
    return jnp.dot(a, b, preferred_element_type=F32)


def _split_bf16(x):
    hi = x.astype(BF16)
    lo = (x - hi.astype(F32)).astype(BF16)
    return hi, lo


def _dot3(a, b):
    ah, al = _split_bf16(a)
    bh, bl = _split_bf16(b)
    return _dot(ah, bh) + (_dot(al, bh) + _dot(ah, bl))


def _pack_pairs(h):
    n = h.shape[1] // 2
    bits = lax.bitcast_convert_type(h.astype(BF16).astype(F32), U32)
    return bits[:, :n] | (bits[:, n:] >> 16)


def _unpack_pairs(p, dtype):
    a = lax.bitcast_convert_type(p & jnp.uint32(0xFFFF0000), F32)
    b = lax.bitcast_convert_type(p << 16, F32)
    return jnp.concatenate([a.astype(dtype), b.astype(dtype)], axis=1)


def _rms(x):
    return x * lax.rsqrt(jnp.mean(x * x, axis=-1, keepdims=True) + NORM_EPS)


def _ada_kernel(c_ref, w_ref, b_ref, o_ref):
    c = c_ref[...]
    o_ref[...] = _dot3(c * jax.nn.sigmoid(c), w_ref[...]) + b_ref[...]


def _ada(c, w, b):
    m, d = c.shape
    n = w.shape[1]
    tn = min(n, 1024)
    return pl.pallas_call(
        _ada_kernel,
        grid=(n // tn,),
        in_specs=[pl.BlockSpec((m, d), lambda j: (0, 0)),
                  pl.BlockSpec((d, tn), lambda j: (0, j)),
                  pl.BlockSpec((1, tn), lambda j: (0, j))],
        out_specs=pl.BlockSpec((m, tn), lambda j: (0, j)),
        out_shape=jax.ShapeDtypeStruct((m, n), F32),
        compiler_params=_cparams(("arbitrary",)),
        name="ada",
    )(c, w, b.reshape(1, n))


def _row_permutation(n_outer, n_inner):
    n = n_outer * n_inner
    out_row = lax.broadcasted_iota(I32, (n, n), 0)
    in_row = lax.broadcasted_iota(I32, (n, n), 1)
    same = (out_row // n_outer == in_row % n_inner) & (out_row % n_outer == in_row // n_inner)
    return same.astype(BF16)


def _norm_mod_kernel(x_ref, mod_ref, g_ref, obm_ref, otm_ref):
    b, tt, d = x_ref.shape
    m = mod_ref[...]
    h = (_rms(x_ref[...]) * g_ref[...].reshape(1, 1, d) * (1.0 + m[:, 1:2, :]) + m[:, 0:1, :]).astype(BF16)
    obm_ref[...] = h
    otm_ref[...] = _dot(_row_permutation(b, tt), h.reshape(b * tt, d)).astype(BF16)


def _norm_mod(x, mod, g):
    b, t, d = x.shape
    tt = TIME_GROUP
    return pl.pallas_call(
        _norm_mod_kernel,
        grid=(t // tt,),
        in_specs=[pl.BlockSpec((b, tt, d), lambda ti: (0, ti, 0)),
                  pl.BlockSpec((b, N_MOD, d), lambda ti: (0, 0, 0)),
                  pl.BlockSpec((1, d), lambda ti: (0, 0))],
        out_specs=[pl.BlockSpec((b, tt, d), lambda ti: (0, ti, 0)),
                   pl.BlockSpec((tt * b, d), lambda ti: (ti, 0))],
        out_shape=[jax.ShapeDtypeStruct((b, t, d), BF16), jax.ShapeDtypeStruct((t * b, d), BF16)],
        compiler_params=_cparams(("arbitrary",)),
        name="norm_mod",
    )(x, mod, g.reshape(1, d))


def _proj_kernel(a_ref, w_ref, o_ref, w_s, *, epilogue):
    @pl.when(pl.program_id(0) == 0)
    def _():
        w_s[...] = w_ref[...].astype(BF16)

    acc = _dot(a_ref[...], w_s[...])
    if epilogue == "gelu":
        acc = jax.nn.gelu(acc)
    elif epilogue == "sigmoid":
        acc = jax.nn.sigmoid(acc)
    elif epilogue == "qscale":
        acc = acc * (HEAD_DIM ** -0.5 * LOG2_E)
    o_ref[...] = acc.astype(o_ref.dtype)


def _resident(shape, index_map):
    return pl.BlockSpec(shape, index_map, pipeline_mode=pl.Buffered(1))


def _proj(a, w, col, width, epilogue, out_dtype, tm):
    m, k = a.shape
    return pl.pallas_call(
        functools.partial(_proj_kernel, epilogue=epilogue),
        grid=(m // tm,),
        in_specs=[pl.BlockSpec((tm, k), lambda i: (i, 0)),
                  _resident((k, width), lambda i: (0, col))],
        out_specs=pl.BlockSpec((tm, width), lambda i: (i, 0)),
        out_shape=jax.ShapeDtypeStruct((m, width), out_dtype),
        scratch_shapes=[pltpu.VMEM((k, width), BF16)],
        compiler_params=_cparams(("arbitrary",)),
        name="proj_" + epilogue,
    )(a, w)


def _lru_kernel(u_ref, gl_ref, sl_ref, cw_ref, cb_ref, wa_ref, ba_ref, wx_ref, bx_ref, lam_ref, h0_ref, prev_ref,
                y_ref, hl_ref, cn_ref, tail_s, h_s, a_s, b_s, y_s, *, first_pos_zero):
    tc, nb, wb = u_ref.shape
    ti = pl.program_id(1)

    @pl.when(ti == 0)
    def _():
        tail_s[...] = prev_ref[...]
        h_s[...] = h0_ref[...]

    u = u_ref[...].astype(F32)
    xp = jnp.concatenate([tail_s[...], u], axis=0)
    cw = cw_ref[...]
    xc = cb_ref[...].reshape(1, 1, wb)
    for j in range(CONV_WIDTH):
        xc = xc + xp[j:j + tc] * cw[j:j + 1].reshape(1, 1, wb)
    last_rows = u[tc - (CONV_WIDTH - 1):]
    tail_s[...] = last_rows
    cn_ref[...] = last_rows

    x2 = xc.reshape(tc * nb, wb)
    xb = x2.astype(BF16)
    r = jax.nn.sigmoid(_dot(xb, wa_ref[0]) + ba_ref[0])
    gi = jax.nn.sigmoid(_dot(xb, wx_ref[0]) + bx_ref[0])
    log_a = (-LRU_C) * r * jax.nn.softplus(-lam_ref[...])
    a = jnp.exp(log_a)
    mult = jnp.sqrt(-jnp.tanh(log_a) * (a * a + 1.0))
    if first_pos_zero:
        row = lax.broadcasted_iota(I32, (tc * nb, wb), 0)
        mult = jnp.where((row < nb) & (ti == 0), 1.0, mult)
    a_s[...] = a.reshape(tc, nb, wb)
    b_s[...] = (mult * (gi * x2)).reshape(tc, nb, wb)

    def step(t, h):
        h = a_s[t] * h + b_s[t]
        gate = gl_ref[t].astype(F32) * sl_ref[t].astype(F32)
        y_s[t] = h * gate
        return h

    h = lax.fori_loop(0, tc, step, h_s[...], unroll=8)
    h_s[...] = h
    hl_ref[...] = h

    perm = _row_permutation(TIME_GROUP, nb)
    for g in range(tc // TIME_GROUP):
        rows = slice(g * TIME_GROUP, (g + 1) * TIME_GROUP)
        yg = y_s[rows].reshape(TIME_GROUP * nb, wb).astype(BF16)
        y_ref[:, rows, :] = _dot(perm, yg).astype(BF16).reshape(nb, TIME_GROUP, wb)


def _lru(u, gl, sl, conv_w, conv_b, wa, ba, wx, bx, lam, h0, prev, tc, first_pos_zero):
    t, b, w = u.shape
    wb = LRU_BLOCK_DIM
    nw = w // wb
    seq = pl.BlockSpec((tc, b, wb), lambda wi, ti: (ti, 0, wi))
    seq_out = pl.BlockSpec((b, tc, wb), lambda wi, ti: (0, ti, wi))
    chan = lambda rows: pl.BlockSpec((rows, wb), lambda wi, ti: (0, wi))
    gate_w = pl.BlockSpec((1, wb, wb), lambda wi, ti: (wi, 0, 0))
    gate_b = pl.BlockSpec((1, 1, wb), lambda wi, ti: (wi, 0, 0))
    tail = pl.BlockSpec((CONV_WIDTH - 1, b, wb), lambda wi, ti: (0, 0, wi))
    return pl.pallas_call(
        functools.partial(_lru_kernel, first_pos_zero=first_pos_zero),
        grid=(nw, t // tc),
        in_specs=[seq, seq, seq, chan(CONV_WIDTH), chan(1), gate_w, gate_b, gate_w, gate_b, chan(1),
                  chan(b), tail],
        out_specs=[seq_out, chan(b), tail],
        out_shape=[jax.ShapeDtypeStruct((b, t, w), BF16), jax.ShapeDtypeStruct((b, w), F32),
                   jax.ShapeDtypeStruct((CONV_WIDTH - 1, b, w), F32)],
        scratch_shapes=[pltpu.VMEM((CONV_WIDTH - 1, b, wb), F32), pltpu.VMEM((b, wb), F32),
                        pltpu.VMEM((tc, b, wb), F32), pltpu.VMEM((tc, b, wb), F32),
                        pltpu.VMEM((tc, b, wb), F32)],
        compiler_params=_cparams(("arbitrary", "arbitrary")),
        name="lru",
    )(u, gl, sl, conv_w, conv_b.reshape(1, w), wa, ba.reshape(nw, 1, wb), wx, bx.reshape(nw, 1, wb),
      lam.reshape(1, w), h0, prev)


def _log_keep_and_logsig(z):
    nz = -z
    lk = jnp.minimum(nz, 0.0) - jnp.log2(1.0 + jnp.exp2(jnp.minimum(z, nz)))
    return lk, z + lk


def _later_key_sums(lk, tri):
    hi, lo = _split_bf16(lk)
    return _dot(hi, tri) + _dot(lo, tri)


def _strict_lower(n, dtype):
    row = lax.broadcasted_iota(I32, (n, n), 0)
    col = lax.broadcasted_iota(I32, (n, n), 1)
    return (row > col).astype(dtype), col < row


def _attn_kernel(q_ref, k_ref, v_ref, sz_ref, yl_ref, o_ref):
    tq = q_ref.shape[1]
    nh = q_ref.shape[2] // HEAD_DIM
    qi = pl.program_id(2)
    tri, causal = _strict_lower(tq, BF16)

    def tile(j, carry, diag):
        start = pl.multiple_of(j * tq, tq)
        heads = [slice(h * HEAD_DIM, (h + 1) * HEAD_DIM) for h in range(nh)]
        zs = [lax.dot_general(q_ref[0, :, cols], k_ref[0, pl.ds(start, tq), cols].astype(BF16), NT_DIMS,
                              preferred_element_type=F32) for cols in heads]
        lks, lss = [], []
        for z in zs:
            lk, ls = _log_keep_and_logsig(z)
            lks.append(jnp.where(causal, lk, 0.0) if diag else lk)
            lss.append(ls)
        rcs = [_later_key_sums(lk, tri) for lk in lks]
        ws = []
        for h in range(nh):
            w = jnp.exp2(lss[h] + rcs[h] + carry[h][1])
            ws.append((jnp.where(causal, w, 0.0) if diag else w).astype(BF16))
        return tuple((carry[h][0] + _dot(ws[h], v_ref[0, pl.ds(start, tq), heads[h]].astype(BF16)),
                      carry[h][1] + (rcs[h][:, 0:1] + lks[h][:, 0:1])) for h in range(nh))

    init = tuple((jnp.zeros((tq, HEAD_DIM), F32), jnp.zeros((tq, 1), F32)) for _ in range(nh))
    carry = tile(qi, init, True)
    carry = lax.fori_loop(0, qi, lambda it, cr: tile(qi - 1 - it, cr, False), carry)
    acc = jnp.concatenate([carry[h][0] for h in range(nh)], axis=1)
    o_ref[0] = (yl_ref[0].astype(F32) + sz_ref[0].astype(F32) * acc).astype(BF16)


def _attn_prompt(q, k, v, sz, yl, tq):
    b, t, a = q.shape
    wblk = ATTN_HEADS_PER_STEP * HEAD_DIM
    qspec = pl.BlockSpec((1, tq, wblk), lambda bi, hi, qi: (bi, qi, hi))
    kvspec = pl.BlockSpec((1, t, wblk), lambda bi, hi, qi: (bi, 0, hi))
    return pl.pallas_call(
        _attn_kernel,
        grid=(b, a // wblk, t // tq),
        in_specs=[qspec, kvspec, kvspec, qspec, qspec],
        out_specs=qspec,
        out_shape=jax.ShapeDtypeStruct((b, t, a), BF16),
        compiler_params=_cparams(("arbitrary", "arbitrary", "arbitrary")),
        name="attn_prompt",
    )(q, k, v, sz, yl)


def _attn_sample_kernel(q_ref, kn_ref, vn_ref, kc_ref, vc_ref, sz_ref, yl_ref, o_ref, lk_s, ls_s):
    ts = q_ref.shape[1]
    nh = q_ref.shape[2] // HEAD_DIM
    past = kc_ref.shape[1]
    tri_n, causal = _strict_lower(ts, BF16)
    tri_p, _ = _strict_lower(past, BF16)
    accs, carries = [], []
    for h in range(nh):
        cols = slice(h * HEAD_DIM, (h + 1) * HEAD_DIM)
        q = q_ref[0, :, cols]
        zn = lax.dot_general(q, kn_ref[0, :, cols].astype(BF16), NT_DIMS, preferred_element_type=F32)
        lk, ls = _log_keep_and_logsig(zn)
        lk = jnp.where(causal, lk, 0.0)
        rc = _later_key_sums(lk, tri_n)
        w = jnp.where(causal, jnp.exp2(ls + rc), 0.0)
        accs.append(_dot(w.astype(BF16), vn_ref[0, :, cols].astype(BF16)))
        carries.append(rc[:, 0:1] + lk[:, 0:1])
        zp = lax.dot_general(q, kc_ref[0, :, cols].astype(BF16), NT_DIMS, preferred_element_type=F32)
        lkp, lsp = _log_keep_and_logsig(zp)
        lk_s[h * ts:(h + 1) * ts, :] = lkp
        ls_s[h * ts:(h + 1) * ts, :] = lsp
    rcp = _later_key_sums(lk_s[...], tri_p)
    for h in range(nh):
        cols = slice(h * HEAD_DIM, (h + 1) * HEAD_DIM)
        rows = slice(h * ts, (h + 1) * ts)
        w = jnp.exp2(ls_s[rows, :] + rcp[rows, :] + carries[h])
        acc = accs[h] + _dot(w.astype(BF16), vc_ref[0, :, cols].astype(BF16))
        o_ref[0, :, cols] = (yl_ref[0, :, cols].astype(F32) + sz_ref[0, :, cols].astype(F32) * acc).astype(BF16)


def _attn_sample(q, kn, vn, kc, vc, sz, yl):
    b, ts, a = q.shape
    past = kc.shape[1]
    nh = a // HEAD_DIM
    cur = pl.BlockSpec((1, ts, a), lambda bi: (bi, 0, 0))
    cache = pl.BlockSpec((1, past, a), lambda bi: (bi, 0, 0))
    return pl.pallas_call(
        _attn_sample_kernel,
        grid=(b,),
        in_specs=[cur, cur, cur, cache, cache, cur, cur],
        out_specs=cur,
        out_shape=jax.ShapeDtypeStruct((b, ts, a), BF16),
        scratch_shapes=[pltpu.VMEM((nh * ts, past), F32), pltpu.VMEM((nh * ts, past), F32)],
        compiler_params=_cparams(("arbitrary",)),
        name="attn_sample",
    )(q, kn, vn, kc, vc, sz, yl)


def _outproj_kernel(a_ref, w_ref, x_ref, mod_ref, g_ref, *rest):
    x1_ref, hpk_ref, hlo_ref, w_s = rest[-4:]
    nb = mod_ref.shape[0]
    tm, d = x_ref.shape

    @pl.when(pl.program_id(0) == 0)
    def _():
        w_s[...] = w_ref[...].astype(BF16)

    m = mod_ref[...]
    o = _dot(a_ref[...], w_s[...]).reshape(nb, tm // nb, d)
    x1 = x_ref[...].reshape(nb, tm // nb, d) + m[:, 2:3, :] * o
    h = _rms(x1) * g_ref[...].reshape(1, 1, d) * (1.0 + m[:, 4:5, :]) + m[:, 3:4, :]
    x1_ref[...] = x1.reshape(tm, d)
    h = h.reshape(tm, d)
    hi = h.astype(BF16).astype(F32)
    hpk_ref[...] = _pack_pairs(hi)
    hlo_ref[...] = (h - hi).astype(BF16)


def _outproj(a, w, x, mod, g, tm, rows_per_batch, n_total, row_offset, earlier=None):
    m, d = x.shape
    nb = max(tm // rows_per_batch, 1)
    per = max(rows_per_batch // tm, 1)
    off = row_offset // tm
    row = lambda width: pl.BlockSpec((tm, width), lambda i: (i, 0))
    out_row = lambda width: pl.BlockSpec((tm, width), lambda i: (i + off, 0))
    in_specs = [row(d), _resident((d, d), lambda i: (0, 0)), row(d),
                pl.BlockSpec((nb, N_MOD, d), lambda i: (i // per, 0, 0)),
                pl.BlockSpec((1, d), lambda i: (0, 0))]
    args = [a, w, x, mod, g.reshape(1, d)]
    aliases = {}
    if earlier is not None:
        aliases = {len(args) + j: j for j in range(len(earlier))}
        in_specs += [pl.BlockSpec(memory_space=pl.ANY)] * len(earlier)
        args += list(earlier)
    return pl.pallas_call(
        _outproj_kernel,
        grid=(m // tm,),
        in_specs=in_specs,
        out_specs=[out_row(d), out_row(d // 2), out_row(d)],
        out_shape=[jax.ShapeDtypeStruct((n_total, d), F32), jax.ShapeDtypeStruct((n_total, d // 2), U32),
                   jax.ShapeDtypeStruct((n_total, d), BF16)],
        scratch_shapes=[pltpu.VMEM((d, d), BF16)],
        input_output_aliases=aliases,
        compiler_params=_cparams(("arbitrary",)),
        name="outproj",
    )(*args)


def _sub_max(x):
    return jnp.max(x, axis=0, keepdims=True)


def _sub_min(x):
    return jnp.min(x, axis=0, keepdims=True)


def _sub_sum(x):
    return jnp.sum(x, axis=0, keepdims=True)


def _router_kernel(hpk_ref, hlo_ref, wrh_ref, wrl_ref, br_ref, e_ref, w_ref, slot_ref, cnt_ref, base_s):
    tm = hpk_ref.shape[0]

    @pl.when(pl.program_id(0) == 0)
    def _():
        base_s[...] = jnp.zeros_like(base_s)

    hi = _unpack_pairs(hpk_ref[...], BF16)
    lo = hlo_ref[...]
    wrh = wrh_ref[...]
    logits = lax.dot_general(wrh, hi, NT_DIMS, preferred_element_type=F32)
    logits = logits + (lax.dot_general(wrh, lo, NT_DIMS, preferred_element_type=F32)
                       + lax.dot_general(wrl_ref[...], hi, NT_DIMS, preferred_element_type=F32))
    score = jax.nn.sigmoid(logits)
    sel = (score + br_ref[...]).reshape(GROUP_SIZE, N_GROUPS, tm)
    score = score.reshape(GROUP_SIZE, N_GROUPS, tm)

    m1 = sel[0]
    m2 = jnp.full_like(m1, NEG_INF)
    for e in range(1, GROUP_SIZE):
        x = sel[e]
        m2 = jnp.maximum(m2, jnp.minimum(m1, x))
        m1 = jnp.maximum(m1, x)
    cur = m1 + m2
    gidx = lax.broadcasted_iota(I32, (N_GROUPS, tm), 0)
    gmask = jnp.zeros((N_GROUPS, tm), jnp.bool_)
    for _ in range(TOPK_GROUPS):
        first = _sub_min(jnp.where(cur == _sub_max(cur), gidx, N_GROUPS))
        pick = gidx == first
        gmask = gmask | pick
        cur = jnp.where(pick, NEG_INF, cur)

    eid = (lax.broadcasted_iota(I32, (GROUP_SIZE, N_GROUPS, tm), 1) * GROUP_SIZE
           + lax.broadcasted_iota(I32, (GROUP_SIZE, N_GROUPS, tm), 0))
    cur = jnp.where(gmask[None], sel, NEG_INF)
    ids, wts, picks = [], [], []
    for _ in range(TOP_K):
        mx = _sub_max(jnp.max(cur, axis=0))
        first = _sub_min(jnp.min(jnp.where(cur == mx[None], eid, N_EXPERTS), axis=0))
        pick = eid == first[None]
        ids.append(first)
        picks.append(pick)
        wts.append(_sub_sum(jnp.sum(jnp.where(pick, score, 0.0), axis=0)))
        cur = jnp.where(pick, NEG_INF, cur)
    total = wts[0]
    chosen = picks[0]
    for k in range(1, TOP_K):
        total = total + wts[k]
        chosen = chosen | picks[k]

    chosen = jnp.where(chosen, 1.0, 0.0).reshape(N_EXPERTS, tm)
    earlier_or_self = (lax.broadcasted_iota(I32, (tm, tm), 0) <= lax.broadcasted_iota(I32, (tm, tm), 1)).astype(BF16)
    incl = _dot(chosen.astype(BF16), earlier_or_self)
    rank = (base_s[...] + (incl - chosen)).reshape(GROUP_SIZE, N_GROUPS, tm)
    base_s[...] = base_s[...] + incl[:, tm - 1:tm]
    slots = [_sub_sum(jnp.sum(jnp.where(pick, rank, 0.0), axis=0)).astype(I32) for pick in picks]

    pad = 8 - TOP_K
    e_ref[...] = jnp.concatenate(ids + [jnp.zeros((pad, tm), I32)], axis=0)
    w_ref[...] = jnp.concatenate([w / total * ROUTED_SCALE for w in wts] + [jnp.zeros((pad, tm), F32)], axis=0)
    slot_ref[...] = jnp.concatenate(slots + [jnp.zeros((pad, tm), I32)], axis=0)
    cnt_ref[...] = jnp.broadcast_to(base_s[...], cnt_ref.shape)


def _router(hpk, hlo, w_router, b_router, tm):
    n, dh = hpk.shape
    d = 2 * dh
    wr = w_router.T.reshape(N_GROUPS, GROUP_SIZE, d).transpose(1, 0, 2).reshape(N_EXPERTS, d)
    br = b_router.reshape(N_GROUPS, GROUP_SIZE).T.reshape(N_EXPERTS, 1)
    wrh = wr.astype(BF16)
    wrl = (wr - wrh.astype(F32)).astype(BF16)
    full = lambda shape: pl.BlockSpec(shape, lambda i: (0, 0))
    per_token = pl.BlockSpec((8, tm), lambda i: (0, i))
    e_k, w_k, slot_k, counts = pl.pallas_call(
        _router_kernel,
        grid=(n // tm,),
        in_specs=[pl.BlockSpec((tm, dh), lambda i: (i, 0)), pl.BlockSpec((tm, d), lambda i: (i, 0)),
                  full((N_EXPERTS, d)), full((N_EXPERTS, d)), full((N_EXPERTS, 1))],
        out_specs=[per_token, per_token, per_token, full((N_EXPERTS, 128))],
        out_shape=[jax.ShapeDtypeStruct((8, n), I32), jax.ShapeDtypeStruct((8, n), F32),
                   jax.ShapeDtypeStruct((8, n), I32), jax.ShapeDtypeStruct((N_EXPERTS, 128), F32)],
        scratch_shapes=[pltpu.VMEM((N_EXPERTS, 1), F32)],
        compiler_params=_cparams(("arbitrary",)),
        name="router",
    )(hpk, hlo, wrh, wrl, br)
    counts = counts[:, 0].reshape(GROUP_SIZE, N_GROUPS).T.reshape(N_EXPERTS).astype(I32)
    return e_k, w_k, slot_k, counts


def _row_copy(src, src_row, dst, dst_row, sem):
    return pltpu.make_async_copy(src.at[pl.ds(src_row, 1)], dst.at[pl.ds(dst_row, 1)], sem)


def _dispatch_kernel(pad_end_ref, dest_ref, h_ref, xs_ref, zero_s, sem, zero_sem):
    tb = dest_ref.shape[1]
    te = zero_s.shape[0]

    def tail_fill(e):
        start = pl.multiple_of(pad_end_ref[e] - te, te)
        return pltpu.make_async_copy(zero_s, xs_ref.at[pl.ds(start, te)], zero_sem)

    def has_rows(e):
        return pad_end_ref[e] > (pad_end_ref[e - 1] if e else 0)

    @pl.when(pl.program_id(0) == 0)
    def _():
        zero_s[...] = jnp.zeros_like(zero_s)
        for e in range(N_EXPERTS):
            pl.when(has_rows(e))(lambda e=e: tail_fill(e).start())
        for e in range(N_EXPERTS):
            pl.when(has_rows(e))(lambda e=e: tail_fill(e).wait())

    def start(t, carry):
        for k in range(TOP_K):
            _row_copy(h_ref, t, xs_ref, dest_ref[k, t], sem).start()
        return carry

    def wait(t, carry):
        for k in range(TOP_K):
            _row_copy(h_ref, 0, xs_ref, 0, sem).wait()
        return carry

    lax.fori_loop(0, tb, start, 0, unroll=DMA_LOOP_UNROLL)
    lax.fori_loop(0, tb, wait, 0, unroll=DMA_LOOP_UNROLL)


def _dispatch(pad_end, dest, hpk, n_rows, tb, te):
    n, dh = hpk.shape
    grid_spec = pltpu.PrefetchScalarGridSpec(
        num_scalar_prefetch=1,
        grid=(n // tb,),
        in_specs=[pl.BlockSpec((8, tb), lambda i, pe: (0, i), memory_space=pltpu.SMEM),
                  pl.BlockSpec((tb, dh), lambda i, pe: (i, 0))],
        out_specs=pl.BlockSpec(memory_space=pl.ANY),
        scratch_shapes=[pltpu.VMEM((te, dh), U32), pltpu.SemaphoreType.DMA(()), pltpu.SemaphoreType.DMA(())],
    )
    return pl.pallas_call(
        _dispatch_kernel,
        grid_spec=grid_spec,
        out_shape=jax.ShapeDtypeStruct((n_rows, dh), U32),
        compiler_params=_cparams(("arbitrary",)),
        name="dispatch",
    )(pad_end, dest, hpk)


def _experts_kernel(be_ref, nu_ref, x_ref, wg_ref, wu_ref, wd_ref, o_ref, wg_s, wu_s, wd_s):
    i = pl.program_id(0)
    prev = be_ref[jnp.maximum(i - 1, 0)]

    @pl.when((i == 0) | (be_ref[i] != prev))
    def _():
        wg_s[...] = wg_ref[0].astype(BF16)
        wu_s[...] = wu_ref[0].astype(BF16)
        wd_s[...] = wd_ref[0].astype(BF16)

    @pl.when(i < nu_ref[0])
    def _():
        x = _unpack_pairs(x_ref[...], BF16)
        g = _dot(x, wg_s[...])
        hidden = (g * jax.nn.sigmoid(g)) * _dot(x, wu_s[...])
        o_ref[...] = _pack_pairs(_dot(hidden.astype(BF16), wd_s[...]))

    @pl.when(i >= nu_ref[0])
    def _():
        o_ref[...] = jnp.zeros_like(o_ref)


def _experts(block_expert, n_used, xs, w_gate, w_up, w_down, tm):
    p, dh = xs.shape
    d = 2 * dh
    ff = w_gate.shape[2]
    grid_spec = pltpu.PrefetchScalarGridSpec(
        num_scalar_prefetch=2,
        grid=(p // tm,),
        in_specs=[pl.BlockSpec((tm, dh), lambda i, be, nu: (i, 0)),
                  pl.BlockSpec((1, d, ff), lambda i, be, nu: (be[i], 0, 0)),
                  pl.BlockSpec((1, d, ff), lambda i, be, nu: (be[i], 0, 0)),
                  pl.BlockSpec((1, ff, d), lambda i, be, nu: (be[i], 0, 0))],
        out_specs=pl.BlockSpec((tm, dh), lambda i, be, nu: (i, 0)),
        scratch_shapes=[pltpu.VMEM((d, ff), BF16), pltpu.VMEM((d, ff), BF16), pltpu.VMEM((ff, d), BF16)],
    )
    return pl.pallas_call(
        _experts_kernel,
        grid_spec=grid_spec,
        out_shape=jax.ShapeDtypeStruct((p, dh), U32),
        compiler_params=_cparams(("arbitrary",)),
        name="experts",
    )(block_expert, n_used, xs, w_gate, w_up, w_down)


def _combine_kernel(dest_ref, wk_ref, ys_ref, x1_ref, hpk_ref, gt_ref, wsg_ref, wsu_ref, wsd_ref, gf_ref,
                    yp_ref, ysm_ref, buf, sem, *, n_prompt_blocks):
    tb, d = x1_ref.shape
    seg = gt_ref.shape[0]
    i = pl.program_id(0)

    def start(t, carry):
        for k in range(TOP_K):
            pltpu.make_async_copy(ys_ref.at[pl.ds(dest_ref[k, t], 1)], buf.at[k, pl.ds(t, 1)], sem).start()
        return carry

    def wait(t, carry):
        for k in range(TOP_K):
            pltpu.make_async_copy(ys_ref.at[pl.ds(0, 1)], buf.at[k, pl.ds(0, 1)], sem).wait()
        return carry

    lax.fori_loop(0, tb, start, 0, unroll=DMA_LOOP_UNROLL)
    x = _unpack_pairs(hpk_ref[...], BF16)
    g = _dot(x, wsg_ref[...])
    ffn = _dot(((g * jax.nn.sigmoid(g)) * _dot(x, wsu_ref[...])).astype(BF16), wsd_ref[...])
    lax.fori_loop(0, tb, wait, 0, unroll=DMA_LOOP_UNROLL)
    wk = wk_ref[...]
    for k in range(TOP_K):
        ffn = ffn + wk[:, k:k + 1] * _unpack_pairs(buf[k], F32)
    x2 = x1_ref[...].reshape(seg, tb // seg, d) + gt_ref[...] * ffn.reshape(seg, tb // seg, d)
    y = (_rms(x2) * gf_ref[...].reshape(1, 1, d)).reshape(tb, d)

    @pl.when(i < n_prompt_blocks)
    def _():
        yp_ref[...] = y

    @pl.when(i >= n_prompt_blocks)
    def _():
        ysm_ref[...] = y


def _combine(dest, wk_t, ys, x1, hpk, gt2_seg, ws_gate, ws_up, ws_down, g_final, n_prompt, tb):
    n, d = x1.shape
    dh = d // 2
    ff = ws_gate.shape[1]
    npb = n_prompt // tb
    nsb = (n - n_prompt) // tb
    seg = gt2_seg.shape[0] * tb // n
    row = lambda width: pl.BlockSpec((tb, width), lambda i: (i, 0))
    full = lambda shape: pl.BlockSpec(shape, lambda i: (0,) * len(shape))
    return pl.pallas_call(
        functools.partial(_combine_kernel, n_prompt_blocks=npb),
        grid=(n // tb,),
        in_specs=[pl.BlockSpec((8, tb), lambda i: (0, i), memory_space=pltpu.SMEM),
                  row(8), pl.BlockSpec(memory_space=pl.ANY), row(d), row(dh),
                  pl.BlockSpec((seg, 1, d), lambda i: (i, 0, 0)),
                  full((d, ff)), full((d, ff)), full((ff, d)), full((1, d))],
        out_specs=[pl.BlockSpec((tb, d), lambda i: (jnp.minimum(i, npb - 1), 0)),
                   pl.BlockSpec((tb, d), lambda i: (jnp.clip(i - npb, 0, nsb - 1), 0))],
        out_shape=[jax.ShapeDtypeStruct((n_prompt, d), F32), jax.ShapeDtypeStruct((n - n_prompt, d), F32)],
        scratch_shapes=[pltpu.VMEM((TOP_K, tb, dh), U32), pltpu.SemaphoreType.DMA(())],
        compiler_params=_cparams(("arbitrary",)),
        name="combine",
    )(dest, wk_t, ys, x1, hpk, gt2_seg, ws_gate, ws_up, ws_down, g_final.reshape(1, d))


def _largest_tile(n, target):
    t = min(n, target)
    while n % t:
        t -= 1
    return t


def _mixer(x, mod, past_k, past_v, h0, conv0, p, first_pos_zero, n_total, row_offset, earlier):
    b, t, d = x.shape
    m = b * t
    a = d
    hn_bm, hn_tm = _norm_mod(x, mod, p["g_mix"])
    hn_bm = hn_bm.reshape(m, d)
    tm = _largest_tile(m, 512)
    w_in = p["w_in"]
    u = _proj(hn_tm, w_in, 0, d, "plain", BF16, tm).reshape(t, b, d)
    gl = _proj(hn_tm, w_in, 1, d, "gelu", BF16, tm).reshape(t, b, d)
    sl = _proj(hn_tm, w_in, 5, d, "sigmoid", BF16, tm).reshape(t, b, d)
    q = _proj(hn_bm, w_in, 2, a, "qscale", BF16, tm).reshape(b, t, a)
    k = _proj(hn_bm, w_in, 3, a, "plain", F32, tm).reshape(b, t, a)
    v = _proj(hn_bm, w_in, 4, a, "plain", F32, tm).reshape(b, t, a)
    sz = _proj(hn_bm, w_in, 6, a, "sigmoid", BF16, tm).reshape(b, t, a)

    if h0 is None:
        h0 = jnp.zeros((b, d), F32)
        prev = jnp.zeros((CONV_WIDTH - 1, b, d), F32)
    else:
        prev = conv0.transpose(1, 0, 2)
    tc = _largest_tile(t, 128)
    yl, h_last, conv_tail = _lru(u, gl, sl, p["conv_w"], p["conv_b"], p["w_rg_a"], p["b_rg_a"], p["w_rg_x"],
                                 p["b_rg_x"], p["lru_lambda"], h0, prev, tc, first_pos_zero)
    if past_k is None:
        mixed = _attn_prompt(q, k, v, sz, yl, _largest_tile(t, 256))
    else:
        past = past_k.shape[1]
        mixed = _attn_sample(q, k, v, past_k.reshape(b, past, a), past_v.reshape(b, past, a), sz, yl)
    tmo = _largest_tile(m, 256)
    shared = _outproj(mixed.reshape(m, a), p["w_out"], x.reshape(m, d), mod, p["g_ffn"], tmo, t, n_total, row_offset,
                      earlier)
    nh = a // HEAD_DIM
    return (shared, k.reshape(b, t, nh, HEAD_DIM), v.reshape(b, t, nh, HEAD_DIM), h_last,
            conv_tail.transpose(1, 0, 2))


def _routing_tables(e_k, slot_k, counts, tm):
    n_tokens = e_k.shape[1]
    padded = (counts + tm - 1) // tm * tm
    pad_end = jnp.cumsum(padded).astype(I32)
    pad_start = pad_end - padded
    is_expert = e_k[:, :, None] == jnp.arange(N_EXPERTS, dtype=I32)
    dest = slot_k + jnp.sum(jnp.where(is_expert, pad_start, 0), axis=-1).astype(I32)
    n_blocks = -(-(TOP_K * n_tokens + N_EXPERTS * (tm - 1)) // tm)
    block_start = jnp.arange(n_blocks, dtype=I32) * tm
    block_expert = jnp.minimum(jnp.sum(pad_end[None, :] <= block_start[:, None], axis=1), N_EXPERTS - 1).astype(I32)
    n_used = (pad_end[-1:] // tm).astype(I32)
    return dest, pad_end, block_expert, n_used, n_blocks * tm


def kernel(x_prompt, x_sample, cache_k, cache_v, state_lru, state_conv, c_prompt, c_sample, g_mix, w_ada, b_ada, w_in, conv_w, conv_b, w_rg_a, b_rg_a, w_rg_x, b_rg_x, lru_lambda, w_out, g_ffn, w_router, b_router, w_gate, w_up, w_down, ws_gate, ws_up, ws_down, g_final):
    assert g_mix.shape[0] == 1, "single-layer trunk only"
    b, t, d = x_prompt.shape
    bs, ts, _ = x_sample.shape
    assert ts >= CONV_WIDTH - 1 and ts == TIME_GROUP and t % TIME_GROUP == 0
    p = dict(g_mix=g_mix[0], w_in=w_in[0], conv_w=conv_w[0], conv_b=conv_b[0],
             w_rg_a=w_rg_a[0].astype(BF16), b_rg_a=b_rg_a[0], w_rg_x=w_rg_x[0].astype(BF16), b_rg_x=b_rg_x[0],
             lru_lambda=lru_lambda[0], w_out=w_out[0], g_ffn=g_ffn[0])
    mod = _ada(jnp.concatenate([c_prompt, c_sample], axis=0), w_ada[0], b_ada[0]).reshape(b + bs, N_MOD, d)
    mod_p, mod_s = mod[:b], mod[b:]

    n_prompt = b * t
    n = n_prompt + bs * ts
    tb = _largest_tile(bs * ts, 256)
    assert n_prompt % tb == 0 and tb % ts == 0

    shared, kp, vp, hp, cp = _mixer(x_prompt, mod_p, None, None, None, None, p, True, n, 0, None)
    (x1, hpk, hlo), ks, vs, hs, cs = _mixer(x_sample, mod_s, cache_k[0], cache_v[0], state_lru[0], state_conv[0],
                                            p, False, n, n_prompt, shared)

    e_k, w_k, slot_k, counts = _router(hpk, hlo, w_router[0], b_router[0], tb)
    tme = EXPERT_ROW_BLOCK
    dest, pad_end, block_expert, n_used, n_rows = _routing_tables(e_k, slot_k, counts, tme)
    xs = _dispatch(pad_end, dest, hpk, n_rows, tb, tme)
    ys = _experts(block_expert, n_used, xs, w_gate[0], w_up[0], w_down[0], tme)
    gt2 = jnp.concatenate([jnp.broadcast_to(mod_p[:, None, 5:6, :], (b, t // ts, 1, d)).reshape(-1, 1, d),
                           mod_s[:, 5:6, :]], axis=0)
    y_p, y_s = _combine(dest, w_k.T, ys, x1, hpk, gt2, ws_gate[0].astype(BF16), ws_up[0].astype(BF16),
                        ws_down[0].astype(BF16), g_final, n_prompt, tb)
    return (y_p.reshape(b, t, d), y_s.reshape(bs, ts, d), kp[None], vp[None], hp[None], cp[None],
            ks[None], vs[None], hs[None], cs[None])
```

```python
import functools

import jax
import jax.numpy as jnp
from jax import lax
from jax.experimental import pallas as pl
from jax.experimental.pallas import tpu as pltpu

F32 = jnp.float32
BF16 = jnp.bfloat16
I32 = jnp.int32
U32 = jnp.uint32

HEAD_DIM = 128
LRU_BLOCK_DIM = 256
LRU_C = 8.0
CONV_WIDTH = 4
N_EXPERTS = 64
TOP_K = 6
N_GROUPS = 8
GROUP_SIZE = N_EXPERTS // N_GROUPS
TOPK_GROUPS = 4
ROUTED_SCALE = 2.5
NORM_EPS = 1e-6
N_MOD = 6
TIME_GROUP = 16
LOG2_E = 1.4426950408889634
ATTN_HEADS_PER_STEP = 4
F32_ZERO_WEIGHT_LOG2 = -150.0
DMA_LOOP_UNROLL = 8
EXPERT_ROW_BLOCK = 512
V7X_VMEM_LIMIT_BYTES = 56 * 1024 * 1024
NT_DIMS = (((1,), (1,)), ((), ()))
NEG_INF = float("-inf")


def _cparams(semantics):
    return pltpu.CompilerParams(dimension_semantics=semantics, vmem_limit_bytes=V7X_VMEM_LIMIT_BYTES)


def _dot(a, b):
    return jnp.dot(a, b, preferred_element_type=F32)


def _split_bf16(x):
    hi = x.astype(BF16)
    lo = (x - hi.astype(F32)).astype(BF16)
    return hi, lo


def _dot3(a, b):
    ah, al = _split_bf16(a)
    bh, bl = _split_bf16(b)
    return _dot(ah, bh) + (_dot(al, bh) + _dot(ah, bl))


def _pack_pairs(h):
    n = h.shape[1] // 2
    bits = lax.bitcast_convert_type(h.astype(BF16).astype(F32), U32)
    return bits[:, :n] | (bits[:, n:] >> 16)


def _unpack_pairs(p, dtype):
    a = lax.bitcast_convert_type(p & jnp.uint32(0xFFFF0000), F32)
    b = lax.bitcast_convert_type(p << 16, F32)
    return jnp.concatenate([a.astype(dtype), b.astype(dtype)], axis=1)


def _rms(x):
    return x * lax.rsqrt(jnp.mean(x * x, axis=-1, keepdims=True) + NORM_EPS)


def _ada_kernel(c_ref, w_ref, b_ref, o_ref):
    c = c_ref[...]
    o_ref[...] = _dot3(c * jax.nn.sigmoid(c), w_ref[...]) + b_ref[...]


def _ada(c, w, b):
    m, d = c.shape
    n = w.shape[1]
    tn = min(n, 1024)
    return pl.pallas_call(
        _ada_kernel,
        grid=(n // tn,),
        in_specs=[pl.BlockSpec((m, d), lambda j: (0, 0)),
                  pl.BlockSpec((d, tn), lambda j: (0, j)),
                  pl.BlockSpec((1, tn), lambda j: (0, j))],
        out_specs=pl.BlockSpec((m, tn), lambda j: (0, j)),
        out_shape=jax.ShapeDtypeStruct((m, n), F32),
        compiler_params=_cparams(("arbitrary",)),
        name="ada",
    )(c, w, b.reshape(1, n))


def _row_permutation(n_outer, n_inner):
    n = n_outer * n_inner
    out_row = lax.broadcasted_iota(I32, (n, n), 0)
    in_row = lax.broadcasted_iota(I32, (n, n), 1)
    same = (out_row // n_outer == in_row % n_inner) & (out_row % n_outer == in_row // n_inner)
    return same.astype(BF16)


def _norm_mod_kernel(x_ref, mod_ref, g_ref, obm_ref, otm_ref):
    b, tt, d = x_ref.shape
    m = mod_ref[...]
    h = (_rms(x_ref[...]) * g_ref[...].reshape(1, 1, d) * (1.0 + m[:, 1:2, :]) + m[:, 0:1, :]).astype(BF16)
    obm_ref[...] = h
    otm_ref[...] = _dot(_row_permutation(b, tt), h.reshape(b * tt, d)).astype(BF16)


def _norm_mod(x, mod, g):
    b, t, d = x.shape
    tt = TIME_GROUP
    return pl.pallas_call(
        _norm_mod_kernel,
        grid=(t // tt,),
        in_specs=[pl.BlockSpec((b, tt, d), lambda ti: (0, ti, 0)),
                  pl.BlockSpec((b, N_MOD, d), lambda ti: (0, 0, 0)),
                  pl.BlockSpec((1, d), lambda ti: (0, 0))],
        out_specs=[pl.BlockSpec((b, tt, d), lambda ti: (0, ti, 0)),
                   pl.BlockSpec((tt * b, d), lambda ti: (ti, 0))],
        out_shape=[jax.ShapeDtypeStruct((b, t, d), BF16), jax.ShapeDtypeStruct((t * b, d), BF16)],
        compiler_params=_cparams(("arbitrary",)),
        name="norm_mod",
    )(x, mod, g.reshape(1, d))


def _proj_kernel(a_ref, w_ref, o_ref, w_s, *, epilogue):
    @pl.when(pl.program_id(0) == 0)
    def _():
        w_s[...] = w_ref[...].astype(BF16)

    acc = _dot(a_ref[...], w_s[...])
    if epilogue == "gelu":
        acc = jax.nn.gelu(acc)
    elif epilogue == "sigmoid":
        acc = jax.nn.sigmoid(acc)
    elif epilogue == "qscale":
        acc = acc * (HEAD_DIM ** -0.5 * LOG2_E)
    o_ref[...] = acc.astype(o_ref.dtype)


def _resident(shape, index_map):
    return pl.BlockSpec(shape, index_map, pipeline_mode=pl.Buffered(1))


def _proj(a, w, col, width, epilogue, out_dtype, tm):
    m, k = a.shape
    return pl.pallas_call(
        functools.partial(_proj_kernel, epilogue=epilogue),
        grid=(m // tm,),
        in_specs=[pl.BlockSpec((tm, k), lambda i: (i, 0)),
                  _resident((k, width), lambda i: (0, col))],
        out_specs=pl.BlockSpec((tm, width), lambda i: (i, 0)),
        out_shape=jax.ShapeDtypeStruct((m, width), out_dtype),
        scratch_shapes=[pltpu.VMEM((k, width), BF16)],
        compiler_params=_cparams(("arbitrary",)),
        name="proj_" + epilogue,
    )(a, w)


def _lru_kernel(u_ref, gl_ref, sl_ref, cw_ref, cb_ref, wa_ref, ba_ref, wx_ref, bx_ref, lam_ref, h0_ref, prev_ref,
                y_ref, hl_ref, cn_ref, tail_s, h_s, a_s, b_s, y_s, *, first_pos_zero):
    tc, nb, wb = u_ref.shape
    ti = pl.program_id(1)

    @pl.when(ti == 0)
    def _():
        tail_s[...] = prev_ref[...]
        h_s[...] = h0_ref[...]

    u = u_ref[...].astype(F32)
    xp = jnp.concatenate([tail_s[...], u], axis=0)
    cw = cw_ref[...]
    xc = cb_ref[...].reshape(1, 1, wb)
    for j in range(CONV_WIDTH):
        xc = xc + xp[j:j + tc] * cw[j:j + 1].reshape(1, 1, wb)
    last_rows = u[tc - (CONV_WIDTH - 1):]
    tail_s[...] = last_rows
    cn_ref[...] = last_rows

    x2 = xc.reshape(tc * nb, wb)
    xb = x2.astype(BF16)
    r = jax.nn.sigmoid(_dot(xb, wa_ref[0]) + ba_ref[0])
    gi = jax.nn.sigmoid(_dot(xb, wx_ref[0]) + bx_ref[0])
    log_a = (-LRU_C) * r * jax.nn.softplus(-lam_ref[...])
    a = jnp.exp(log_a)
    mult = jnp.sqrt(-jnp.tanh(log_a) * (a * a + 1.0))
    if first_pos_zero:
        row = lax.broadcasted_iota(I32, (tc * nb, wb), 0)
        mult = jnp.where((row < nb) & (ti == 0), 1.0, mult)
    a_s[...] = a.reshape(tc, nb, wb)
    b_s[...] = (mult * (gi * x2)).reshape(tc, nb, wb)

    def step(t, h):
        h = a_s[t] * h + b_s[t]
        gate = gl_ref[t].astype(F32) * sl_ref[t].astype(F32)
        y_s[t] = h * gate
        return h

    h = lax.fori_loop(0, tc, step, h_s[...], unroll=8)
    h_s[...] = h
    hl_ref[...] = h

    perm = _row_permutation(TIME_GROUP, nb)
    for g in range(tc // TIME_GROUP):
        rows = slice(g * TIME_GROUP, (g + 1) * TIME_GROUP)
        yg = y_s[rows].reshape(TIME_GROUP * nb, wb).astype(BF16)
        y_ref[:, rows, :] = _dot(perm, yg).astype(BF16).reshape(nb, TIME_GROUP, wb)


def _lru(u, gl, sl, conv_w, conv_b, wa, ba, wx, bx, lam, h0, prev, tc, first_pos_zero):
    t, b, w = u.shape
    wb = LRU_BLOCK_DIM
    nw = w // wb
    seq = pl.BlockSpec((tc, b, wb), lambda wi, ti: (ti, 0, wi))
    seq_out = pl.BlockSpec((b, tc, wb), lambda wi, ti: (0, ti, wi))
    chan = lambda rows: pl.BlockSpec((rows, wb), lambda wi, ti: (0, wi))
    gate_w = pl.BlockSpec((1, wb, wb), lambda wi, ti: (wi, 0, 0))
    gate_b = pl.BlockSpec((1, 1, wb), lambda wi, ti: (wi, 0, 0))
    tail = pl.BlockSpec((CONV_WIDTH - 1, b, wb), lambda wi, ti: (0, 0, wi))
    return pl.pallas_call(
        functools.partial(_lru_kernel, first_pos_zero=first_pos_zero),
        grid=(nw, t // tc),
        in_specs=[seq, seq, seq, chan(CONV_WIDTH), chan(1), gate_w, gate_b, gate_w, gate_b, chan(1),
                  chan(b), tail],
        out_specs=[seq_out, chan(b), tail],
        out_shape=[jax.ShapeDtypeStruct((b, t, w), BF16), jax.ShapeDtypeStruct((b, w), F32),
                   jax.ShapeDtypeStruct((CONV_WIDTH - 1, b, w), F32)],
        scratch_shapes=[pltpu.VMEM((CONV_WIDTH - 1, b, wb), F32), pltpu.VMEM((b, wb), F32),
                        pltpu.VMEM((tc, b, wb), F32), pltpu.VMEM((tc, b, wb), F32),
                        pltpu.VMEM((tc, b, wb), F32)],
        compiler_params=_cparams(("arbitrary", "arbitrary")),
        name="lru",
    )(u, gl, sl, conv_w, conv_b.reshape(1, w), wa, ba.reshape(nw, 1, wb), wx, bx.reshape(nw, 1, wb),
      lam.reshape(1, w), h0, prev)


def _log_keep_and_logsig(z):
    nz = -z
    lk = jnp.minimum(nz, 0.0) - jnp.log2(1.0 + jnp.exp2(jnp.minimum(z, nz)))
    return lk, z + lk


def _later_key_sums(lk, tri):
    hi, lo = _split_bf16(lk)
    return _dot(hi, tri) + _dot(lo, tri)


def _strict_lower(n, dtype):
    row = lax.broadcasted_iota(I32, (n, n), 0)
    col = lax.broadcasted_iota(I32, (n, n), 1)
    return (row > col).astype(dtype), col < row


def _attn_kernel(q_ref, k_ref, v_ref, sz_ref, yl_ref, o_ref):
    tq = q_ref.shape[1]
    nh = q_ref.shape[2] // HEAD_DIM
    qi = pl.program_id(2)
    tri, causal = _strict_lower(tq, BF16)

    def tile(j, carry, diag):
        start = pl.multiple_of(j * tq, tq)
        heads = [slice(h * HEAD_DIM, (h + 1) * HEAD_DIM) for h in range(nh)]
        zs = [lax.dot_general(q_ref[0, :, cols], k_ref[0, pl.ds(start, tq), cols].astype(BF16), NT_DIMS,
                              preferred_element_type=F32) for cols in heads]
        lks, lss = [], []
        for z in zs:
            lk, ls = _log_keep_and_logsig(z)
            lks.append(jnp.where(causal, lk, 0.0) if diag else lk)
            lss.append(ls)
        rcs = [_later_key_sums(lk, tri) for lk in lks]
        ws = []
        for h in range(nh):
            w = jnp.exp2(lss[h] + rcs[h] + carry[h][1])
            ws.append((jnp.where(causal, w, 0.0) if diag else w).astype(BF16))
        return tuple((carry[h][0] + _dot(ws[h], v_ref[0, pl.ds(start, tq), heads[h]].astype(BF16)),
                      carry[h][1] + (rcs[h][:, 0:1] + lks[h][:, 0:1])) for h in range(nh))

    def any_weight_left(carry):
        c_max = carry[0][1]
        for h in range(1, nh):
            c_max = jnp.maximum(c_max, carry[h][1])
        return jnp.max(c_max) > F32_ZERO_WEIGHT_LOG2

    init = tuple((jnp.zeros((tq, HEAD_DIM), F32), jnp.zeros((tq, 1), F32)) for _ in range(nh))
    carry = tile(qi, init, True)

    def more(state):
        return (state[0] < qi) & state[1]

    def farther_tile(state):
        carry = tile(qi - 1 - state[0], state[2], False)
        return state[0] + 1, any_weight_left(carry), carry

    carry = lax.while_loop(more, farther_tile, (jnp.int32(0), any_weight_left(carry), carry))[2]
    acc = jnp.concatenate([carry[h][0] for h in range(nh)], axis=1)
    o_ref[0] = (yl_ref[0].astype(F32) + sz_ref[0].astype(F32) * acc).astype(BF16)


def _attn_prompt(q, k, v, sz, yl, tq):
    b, t, a = q.shape
    wblk = ATTN_HEADS_PER_STEP * HEAD_DIM
    qspec = pl.BlockSpec((1, tq, wblk), lambda bi, hi, qi: (bi, qi, hi))
    kvspec = pl.BlockSpec((1, t, wblk), lambda bi, hi, qi: (bi, 0, hi))
    return pl.pallas_call(
        _attn_kernel,
        grid=(b, a // wblk, t // tq),
        in_specs=[qspec, kvspec, kvspec, qspec, qspec],
        out_specs=qspec,
        out_shape=jax.ShapeDtypeStruct((b, t, a), BF16),
        compiler_params=_cparams(("arbitrary", "arbitrary", "arbitrary")),
        name="attn_prompt",
    )(q, k, v, sz, yl)


def _attn_sample_kernel(q_ref, kn_ref, vn_ref, kc_ref, vc_ref, sz_ref, yl_ref, o_ref, lk_s, ls_s):
    ts = q_ref.shape[1]
    nh = q_ref.shape[2] // HEAD_DIM
    past = kc_ref.shape[1]
    tri_n, causal = _strict_lower(ts, BF16)
    tri_p, _ = _strict_lower(past, BF16)
    accs, carries = [], []
    for h in range(nh):
        cols = slice(h * HEAD_DIM, (h + 1) * HEAD_DIM)
        q = q_ref[0, :, cols]
        zn = lax.dot_general(q, kn_ref[0, :, cols].astype(BF16), NT_DIMS, preferred_element_type=F32)
        lk, ls = _log_keep_and_logsig(zn)
        lk = jnp.where(causal, lk, 0.0)
        rc = _later_key_sums(lk, tri_n)
        w = jnp.where(causal, jnp.exp2(ls + rc), 0.0)
        accs.append(_dot(w.astype(BF16), vn_ref[0, :, cols].astype(BF16)))
        carries.append(rc[:, 0:1] + lk[:, 0:1])
        zp = lax.dot_general(q, kc_ref[0, :, cols].astype(BF16), NT_DIMS, preferred_element_type=F32)
        lkp, lsp = _log_keep_and_logsig(zp)
        lk_s[h * ts:(h + 1) * ts, :] = lkp
        ls_s[h * ts:(h + 1) * ts, :] = lsp
    rcp = _later_key_sums(lk_s[...], tri_p)
    for h in range(nh):
        cols = slice(h * HEAD_DIM, (h + 1) * HEAD_DIM)
        rows = slice(h * ts, (h + 1) * ts)
        w = jnp.exp2(ls_s[rows, :] + rcp[rows, :] + carries[h])
        acc = accs[h] + _dot(w.astype(BF16), vc_ref[0, :, cols].astype(BF16))
        o_ref[0, :, cols] = (yl_ref[0, :, cols].astype(F32) + sz_ref[0, :, cols].astype(F32) * acc).astype(BF16)


def _attn_sample(q, kn, vn, kc, vc, sz, yl):
    b, ts, a = q.shape
    past = kc.shape[1]
    nh = a // HEAD_DIM
    cur = pl.BlockSpec((1, ts, a), lambda bi: (bi, 0, 0))
    cache = pl.BlockSpec((1, past, a), lambda bi: (bi, 0, 0))
    return pl.pallas_call(
        _attn_sample_kernel,
        grid=(b,),
        in_specs=[cur, cur, cur, cache, cache, cur, cur],
        out_specs=cur,
        out_shape=jax.ShapeDtypeStruct((b, ts, a), BF16),
        scratch_shapes=[pltpu.VMEM((nh * ts, past), F32), pltpu.VMEM((nh * ts, past), F32)],
        compiler_params=_cparams(("arbitrary",)),
        name="attn_sample",
    )(q, kn, vn, kc, vc, sz, yl)


def _outproj_kernel(a_ref, w_ref, x_ref, mod_ref, g_ref, *rest):
    x1_ref, hpk_ref, hlo_ref, w_s = rest[-4:]
    nb = mod_ref.shape[0]
    tm, d = x_ref.shape

    @pl.when(pl.program_id(0) == 0)
    def _():
        w_s[...] = w_ref[...].astype(BF16)

    m = mod_ref[...]
    o = _dot(a_ref[...], w_s[...]).reshape(nb, tm // nb, d)
    x1 = x_ref[...].reshape(nb, tm // nb, d) + m[:, 2:3, :] * o
    h = _rms(x1) * g_ref[...].reshape(1, 1, d) * (1.0 + m[:, 4:5, :]) + m[:, 3:4, :]
    x1_ref[...] = x1.reshape(tm, d)
    h = h.reshape(tm, d)
    hi = h.astype(BF16).astype(F32)
    hpk_ref[...] = _pack_pairs(hi)
    hlo_ref[...] = (h - hi).astype(BF16)


def _outproj(a, w, x, mod, g, tm, rows_per_batch, n_total, row_offset, earlier=None):
    m, d = x.shape
    nb = max(tm // rows_per_batch, 1)
    per = max(rows_per_batch // tm, 1)
    off = row_offset // tm
    row = lambda width: pl.BlockSpec((tm, width), lambda i: (i, 0))
    out_row = lambda width: pl.BlockSpec((tm, width), lambda i: (i + off, 0))
    in_specs = [row(d), _resident((d, d), lambda i: (0, 0)), row(d),
                pl.BlockSpec((nb, N_MOD, d), lambda i: (i // per, 0, 0)),
                pl.BlockSpec((1, d), lambda i: (0, 0))]
    args = [a, w, x, mod, g.reshape(1, d)]
    aliases = {}
    if earlier is not None:
        aliases = {len(args) + j: j for j in range(len(earlier))}
        in_specs += [pl.BlockSpec(memory_space=pl.ANY)] * len(earlier)
        args += list(earlier)
    return pl.pallas_call(
        _outproj_kernel,
        grid=(m // tm,),
        in_specs=in_specs,
        out_specs=[out_row(d), out_row(d // 2), out_row(d)],
        out_shape=[jax.ShapeDtypeStruct((n_total, d), F32), jax.ShapeDtypeStruct((n_total, d // 2), U32),
                   jax.ShapeDtypeStruct((n_total, d), BF16)],
        scratch_shapes=[pltpu.VMEM((d, d), BF16)],
        input_output_aliases=aliases,
        compiler_params=_cparams(("arbitrary",)),
        name="outproj",
    )(*args)


def _sub_max(x):
    return jnp.max(x, axis=0, keepdims=True)


def _sub_min(x):
    return jnp.min(x, axis=0, keepdims=True)


def _sub_sum(x):
    return jnp.sum(x, axis=0, keepdims=True)


def _router_kernel(hpk_ref, hlo_ref, wrh_ref, wrl_ref, br_ref, e_ref, w_ref, slot_ref, cnt_ref, base_s):
    tm = hpk_ref.shape[0]

    @pl.when(pl.program_id(0) == 0)
    def _():
        base_s[...] = jnp.zeros_like(base_s)

    hi = _unpack_pairs(hpk_ref[...], BF16)
    lo = hlo_ref[...]
    wrh = wrh_ref[...]
    logits = lax.dot_general(wrh, hi, NT_DIMS, preferred_element_type=F32)
    logits = logits + (lax.dot_general(wrh, lo, NT_DIMS, preferred_element_type=F32)
                       + lax.dot_general(wrl_ref[...], hi, NT_DIMS, preferred_element_type=F32))
    score = jax.nn.sigmoid(logits)
    sel = (score + br_ref[...]).reshape(GROUP_SIZE, N_GROUPS, tm)
    score = score.reshape(GROUP_SIZE, N_GROUPS, tm)

    m1 = sel[0]
    m2 = jnp.full_like(m1, NEG_INF)
    for e in range(1, GROUP_SIZE):
        x = sel[e]
        m2 = jnp.maximum(m2, jnp.minimum(m1, x))
        m1 = jnp.maximum(m1, x)
    cur = m1 + m2
    gidx = lax.broadcasted_iota(I32, (N_GROUPS, tm), 0)
    gmask = jnp.zeros((N_GROUPS, tm), jnp.bool_)
    for _ in range(TOPK_GROUPS):
        first = _sub_min(jnp.where(cur == _sub_max(cur), gidx, N_GROUPS))
        pick = gidx == first
        gmask = gmask | pick
        cur = jnp.where(pick, NEG_INF, cur)

    eid = (lax.broadcasted_iota(I32, (GROUP_SIZE, N_GROUPS, tm), 1) * GROUP_SIZE
           + lax.broadcasted_iota(I32, (GROUP_SIZE, N_GROUPS, tm), 0))
    cur = jnp.where(gmask[None], sel, NEG_INF)
    ids, wts, picks = [], [], []
    for _ in range(TOP_K):
        mx = _sub_max(jnp.max(cur, axis=0))
        first = _sub_min(jnp.min(jnp.where(cur == mx[None], eid, N_EXPERTS), axis=0))
        pick = eid == first[None]
        ids.append(first)
        picks.append(pick)
        wts.append(_sub_sum(jnp.sum(jnp.where(pick, score, 0.0), axis=0)))
        cur = jnp.where(pick, NEG_INF, cur)
    total = wts[0]
    chosen = picks[0]
    for k in range(1, TOP_K):
        total = total + wts[k]
        chosen = chosen | picks[k]

    chosen = jnp.where(chosen, 1.0, 0.0).reshape(N_EXPERTS, tm)
    earlier_or_self = (lax.broadcasted_iota(I32, (tm, tm), 0) <= lax.broadcasted_iota(I32, (tm, tm), 1)).astype(BF16)
    incl = _dot(chosen.astype(BF16), earlier_or_self)
    rank = (base_s[...] + (incl - chosen)).reshape(GROUP_SIZE, N_GROUPS, tm)
    base_s[...] = base_s[...] + incl[:, tm - 1:tm]
    slots = [_sub_sum(jnp.sum(jnp.where(pick, rank, 0.0), axis=0)).astype(I32) for pick in picks]

    pad = 8 - TOP_K
    e_ref[...] = jnp.concatenate(ids + [jnp.zeros((pad, tm), I32)], axis=0)
    w_ref[...] = jnp.concatenate([w / total * ROUTED_SCALE for w in wts] + [jnp.zeros((pad, tm), F32)], axis=0)
    slot_ref[...] = jnp.concatenate(slots + [jnp.zeros((pad, tm), I32)], axis=0)
    cnt_ref[...] = jnp.broadcast_to(base_s[...], cnt_ref.shape)


def _router(hpk, hlo, w_router, b_router, tm):
    n, dh = hpk.shape
    d = 2 * dh
    wr = w_router.T.reshape(N_GROUPS, GROUP_SIZE, d).transpose(1, 0, 2).reshape(N_EXPERTS, d)
    br = b_router.reshape(N_GROUPS, GROUP_SIZE).T.reshape(N_EXPERTS, 1)
    wrh = wr.astype(BF16)
    wrl = (wr - wrh.astype(F32)).astype(BF16)
    full = lambda shape: pl.BlockSpec(shape, lambda i: (0, 0))
    per_token = pl.BlockSpec((8, tm), lambda i: (0, i))
    e_k, w_k, slot_k, counts = pl.pallas_call(
        _router_kernel,
        grid=(n // tm,),
        in_specs=[pl.BlockSpec((tm, dh), lambda i: (i, 0)), pl.BlockSpec((tm, d), lambda i: (i, 0)),
                  full((N_EXPERTS, d)), full((N_EXPERTS, d)), full((N_EXPERTS, 1))],
        out_specs=[per_token, per_token, per_token, full((N_EXPERTS, 128))],
        out_shape=[jax.ShapeDtypeStruct((8, n), I32), jax.ShapeDtypeStruct((8, n), F32),
                   jax.ShapeDtypeStruct((8, n), I32), jax.ShapeDtypeStruct((N_EXPERTS, 128), F32)],
        scratch_shapes=[pltpu.VMEM((N_EXPERTS, 1), F32)],
        compiler_params=_cparams(("arbitrary",)),
        name="router",
    )(hpk, hlo, wrh, wrl, br)
    counts = counts[:, 0].reshape(GROUP_SIZE, N_GROUPS).T.reshape(N_EXPERTS).astype(I32)
    return e_k, w_k, slot_k, counts


def _row_copy(src, src_row, dst, dst_row, sem):
    return pltpu.make_async_copy(src.at[pl.ds(src_row, 1)], dst.at[pl.ds(dst_row, 1)], sem)


def _dispatch_kernel(pad_end_ref, dest_ref, h_ref, xs_ref, zero_s, sem, zero_sem):
    tb = dest_ref.shape[1]
    te = zero_s.shape[0]

    def tail_fill(e):
        start = pl.multiple_of(pad_end_ref[e] - te, te)
        return pltpu.make_async_copy(zero_s, xs_ref.at[pl.ds(start, te)], zero_sem)

    def has_rows(e):
        return pad_end_ref[e] > (pad_end_ref[e - 1] if e else 0)

    @pl.when(pl.program_id(0) == 0)
    def _():
        zero_s[...] = jnp.zeros_like(zero_s)
        for e in range(N_EXPERTS):
            pl.when(has_rows(e))(lambda e=e: tail_fill(e).start())
        for e in range(N_EXPERTS):
            pl.when(has_rows(e))(lambda e=e: tail_fill(e).wait())

    def start(t, carry):
        for k in range(TOP_K):
            _row_copy(h_ref, t, xs_ref, dest_ref[k, t], sem).start()
        return carry

    def wait(t, carry):
        for k in range(TOP_K):
            _row_copy(h_ref, 0, xs_ref, 0, sem).wait()
        return carry

    lax.fori_loop(0, tb, start, 0, unroll=DMA_LOOP_UNROLL)
    lax.fori_loop(0, tb, wait, 0, unroll=DMA_LOOP_UNROLL)


def _dispatch(pad_end, dest, hpk, n_rows, tb, te):
    n, dh = hpk.shape
    grid_spec = pltpu.PrefetchScalarGridSpec(
        num_scalar_prefetch=1,
        grid=(n // tb,),
        in_specs=[pl.BlockSpec((8, tb), lambda i, pe: (0, i), memory_space=pltpu.SMEM),
                  pl.BlockSpec((tb, dh), lambda i, pe: (i, 0))],
        out_specs=pl.BlockSpec(memory_space=pl.ANY),
        scratch_shapes=[pltpu.VMEM((te, dh), U32), pltpu.SemaphoreType.DMA(()), pltpu.SemaphoreType.DMA(())],
    )
    return pl.pallas_call(
        _dispatch_kernel,
        grid_spec=grid_spec,
        out_shape=jax.ShapeDtypeStruct((n_rows, dh), U32),
        compiler_params=_cparams(("arbitrary",)),
        name="dispatch",
    )(pad_end, dest, hpk)


def _experts_kernel(be_ref, nu_ref, x_ref, wg_ref, wu_ref, wd_ref, o_ref, wg_s, wu_s, wd_s):
    i = pl.program_id(0)
    prev = be_ref[jnp.maximum(i - 1, 0)]

    @pl.when((i == 0) | (be_ref[i] != prev))
    def _():
        wg_s[...] = wg_ref[0].astype(BF16)
        wu_s[...] = wu_ref[0].astype(BF16)
        wd_s[...] = wd_ref[0].astype(BF16)

    @pl.when(i < nu_ref[0])
    def _():
        x = _unpack_pairs(x_ref[...], BF16)
        g = _dot(x, wg_s[...])
        hidden = (g * jax.nn.sigmoid(g)) * _dot(x, wu_s[...])
        o_ref[...] = _pack_pairs(_dot(hidden.astype(BF16), wd_s[...]))

    @pl.when(i >= nu_ref[0])
    def _():
        o_ref[...] = jnp.zeros_like(o_ref)


def _experts(block_expert, n_used, xs, w_gate, w_up, w_down, tm):
    p, dh = xs.shape
    d = 2 * dh
    ff = w_gate.shape[2]
    grid_spec = pltpu.PrefetchScalarGridSpec(
        num_scalar_prefetch=2,
        grid=(p // tm,),
        in_specs=[pl.BlockSpec((tm, dh), lambda i, be, nu: (i, 0)),
                  pl.BlockSpec((1, d, ff), lambda i, be, nu: (be[i], 0, 0)),
                  pl.BlockSpec((1, d, ff), lambda i, be, nu: (be[i], 0, 0)),
                  pl.BlockSpec((1, ff, d), lambda i, be, nu: (be[i], 0, 0))],
        out_specs=pl.BlockSpec((tm, dh), lambda i, be, nu: (i, 0)),
        scratch_shapes=[pltpu.VMEM((d, ff), BF16), pltpu.VMEM((d, ff), BF16), pltpu.VMEM((ff, d), BF16)],
    )
    return pl.pallas_call(
        _experts_kernel,
        grid_spec=grid_spec,
        out_shape=jax.ShapeDtypeStruct((p, dh), U32),
        compiler_params=_cparams(("arbitrary",)),
        name="experts",
    )(block_expert, n_used, xs, w_gate, w_up, w_down)


def _combine_kernel(dest_ref, wk_ref, ys_ref, x1_ref, hpk_ref, gt_ref, wsg_ref, wsu_ref, wsd_ref, gf_ref,
                    yp_ref, ysm_ref, buf, sem, *, n_prompt_blocks):
    tb, d = x1_ref.shape
    seg = gt_ref.shape[0]
    i = pl.program_id(0)

    def start(t, carry):
        for k in range(TOP_K):
            pltpu.make_async_copy(ys_ref.at[pl.ds(dest_ref[k, t], 1)], buf.at[k, pl.ds(t, 1)], sem).start()
        return carry

    def wait(t, carry):
        for k in range(TOP_K):
            pltpu.make_async_copy(ys_ref.at[pl.ds(0, 1)], buf.at[k, pl.ds(0, 1)], sem).wait()
        return carry

    lax.fori_loop(0, tb, start, 0, unroll=DMA_LOOP_UNROLL)
    x = _unpack_pairs(hpk_ref[...], BF16)
    g = _dot(x, wsg_ref[...])
    ffn = _dot(((g * jax.nn.sigmoid(g)) * _dot(x, wsu_ref[...])).astype(BF16), wsd_ref[...])
    lax.fori_loop(0, tb, wait, 0, unroll=DMA_LOOP_UNROLL)
    wk = wk_ref[...]
    for k in range(TOP_K):
        ffn = ffn + wk[:, k:k + 1] * _unpack_pairs(buf[k], F32)
    x2 = x1_ref[...].reshape(seg, tb // seg, d) + gt_ref[...] * ffn.reshape(seg, tb // seg, d)
    y = (_rms(x2) * gf_ref[...].reshape(1, 1, d)).reshape(tb, d)

    @pl.when(i < n_prompt_blocks)
    def _():
        yp_ref[...] = y

    @pl.when(i >= n_prompt_blocks)
    def _():
        ysm_ref[...] = y


def _combine(dest, wk_t, ys, x1, hpk, gt2_seg, ws_gate, ws_up, ws_down, g_final, n_prompt, tb):
    n, d = x1.shape
    dh = d // 2
    ff = ws_gate.shape[1]
    npb = n_prompt // tb
    nsb = (n - n_prompt) // tb
    seg = gt2_seg.shape[0] * tb // n
    row = lambda width: pl.BlockSpec((tb, width), lambda i: (i, 0))
    full = lambda shape: pl.BlockSpec(shape, lambda i: (0,) * len(shape))
    return pl.pallas_call(
        functools.partial(_combine_kernel, n_prompt_blocks=npb),
        grid=(n // tb,),
        in_specs=[pl.BlockSpec((8, tb), lambda i: (0, i), memory_space=pltpu.SMEM),
                  row(8), pl.BlockSpec(memory_space=pl.ANY), row(d), row(dh),
                  pl.BlockSpec((seg, 1, d), lambda i: (i, 0, 0)),
                  full((d, ff)), full((d, ff)), full((ff, d)), full((1, d))],
        out_specs=[pl.BlockSpec((tb, d), lambda i: (jnp.minimum(i, npb - 1), 0)),
                   pl.BlockSpec((tb, d), lambda i: (jnp.clip(i - npb, 0, nsb - 1), 0))],
        out_shape=[jax.ShapeDtypeStruct((n_prompt, d), F32), jax.ShapeDtypeStruct((n - n_prompt, d), F32)],
        scratch_shapes=[pltpu.VMEM((TOP_K, tb, dh), U32), pltpu.SemaphoreType.DMA(())],
        compiler_params=_cparams(("arbitrary",)),
        name="combine",
    )(dest, wk_t, ys, x1, hpk, gt2_seg, ws_gate, ws_up, ws_down, g_final.reshape(1, d))


def _largest_tile(n, target):
    t = min(n, target)
    while n % t:
        t -= 1
    return t


def _mixer(x, mod, past_k, past_v, h0, conv0, p, first_pos_zero, n_total, row_offset, earlier):
    b, t, d = x.shape
    m = b * t
    a = d
    hn_bm, hn_tm = _norm_mod(x, mod, p["g_mix"])
    hn_bm = hn_bm.reshape(m, d)
    tm = _largest_tile(m, 512)
    w_in = p["w_in"]
    u = _proj(hn_tm, w_in, 0, d, "plain", BF16, tm).reshape(t, b, d)
    gl = _proj(hn_tm, w_in, 1, d, "gelu", BF16, tm).reshape(t, b, d)
    sl = _proj(hn_tm, w_in, 5, d, "sigmoid", BF16, tm).reshape(t, b, d)
    q = _proj(hn_bm, w_in, 2, a, "qscale", BF16, tm).reshape(b, t, a)
    k = _proj(hn_bm, w_in, 3, a, "plain", F32, tm).reshape(b, t, a)
    v = _proj(hn_bm, w_in, 4, a, "plain", F32, tm).reshape(b, t, a)
    sz = _proj(hn_bm, w_in, 6, a, "sigmoid", BF16, tm).reshape(b, t, a)

    if h0 is None:
        h0 = jnp.zeros((b, d), F32)
        prev = jnp.zeros((CONV_WIDTH - 1, b, d), F32)
    else:
        prev = conv0.transpose(1, 0, 2)
    tc = _largest_tile(t, 128)
    yl, h_last, conv_tail = _lru(u, gl, sl, p["conv_w"], p["conv_b"], p["w_rg_a"], p["b_rg_a"], p["w_rg_x"],
                                 p["b_rg_x"], p["lru_lambda"], h0, prev, tc, first_pos_zero)
    if past_k is None:
        mixed = _attn_prompt(q, k, v, sz, yl, _largest_tile(t, 256))
    else:
        past = past_k.shape[1]
        mixed = _attn_sample(q, k, v, past_k.reshape(b, past, a), past_v.reshape(b, past, a), sz, yl)
    tmo = _largest_tile(m, 256)
    shared = _outproj(mixed.reshape(m, a), p["w_out"], x.reshape(m, d), mod, p["g_ffn"], tmo, t, n_total, row_offset,
                      earlier)
    nh = a // HEAD_DIM
    return (shared, k.reshape(b, t, nh, HEAD_DIM), v.reshape(b, t, nh, HEAD_DIM), h_last,
            conv_tail.transpose(1, 0, 2))


def _routing_tables(e_k, slot_k, counts, tm):
    n_tokens = e_k.shape[1]
    padded = (counts + tm - 1) // tm * tm
    pad_end = jnp.cumsum(padded).astype(I32)
    pad_start = pad_end - padded
    is_expert = e_k[:, :, None] == jnp.arange(N_EXPERTS, dtype=I32)
    dest = slot_k + jnp.sum(jnp.where(is_expert, pad_start, 0), axis=-1).astype(I32)
    n_blocks = -(-(TOP_K * n_tokens + N_EXPERTS * (tm - 1)) // tm)
    block_start = jnp.arange(n_blocks, dtype=I32) * tm
    block_expert = jnp.minimum(jnp.sum(pad_end[None, :] <= block_start[:, None], axis=1), N_EXPERTS - 1).astype(I32)
    n_used = (pad_end[-1:] // tm).astype(I32)
    return dest, pad_end, block_expert, n_used, n_blocks * tm


def kernel(x_prompt, x_sample, cache_k, cache_v, state_lru, state_conv, c_prompt, c_sample, g_mix, w_ada, b_ada, w_in, conv_w, conv_b, w_rg_a, b_rg_a, w_rg_x, b_rg_x, lru_lambda, w_out, g_ffn, w_router, b_router, w_gate, w_up, w_down, ws_gate, ws_up, ws_down, g_final):
    assert g_mix.shape[0] == 1, "single-layer trunk only"
    b, t, d = x_prompt.shape
    bs, ts, _ = x_sample.shape
    assert ts >= CONV_WIDTH - 1 and ts == TIME_GROUP and t % TIME_GROUP == 0
    p = dict(g_mix=g_mix[0], w_in=w_in[0], conv_w=conv_w[0], conv_b=conv_b[0],
             w_rg_a=w_rg_a[0].astype(BF16), b_rg_a=b_rg_a[0], w_rg_x=w_rg_x[0].astype(BF16), b_rg_x=b_rg_x[0],
             lru_lambda=lru_lambda[0], w_out=w_out[0], g_ffn=g_ffn[0])
    mod = _ada(jnp.concatenate([c_prompt, c_sample], axis=0), w_ada[0], b_ada[0]).reshape(b + bs, N_MOD, d)
    mod_p, mod_s = mod[:b], mod[b:]

    n_prompt = b * t
    n = n_prompt + bs * ts
    tb = _largest_tile(bs * ts, 256)
    assert n_prompt % tb == 0 and tb % ts == 0

    shared, kp, vp, hp, cp = _mixer(x_prompt, mod_p, None, None, None, None, p, True, n, 0, None)
    (x1, hpk, hlo), ks, vs, hs, cs = _mixer(x_sample, mod_s, cache_k[0], cache_v[0], state_lru[0], state_conv[0],
                                            p, False, n, n_prompt, shared)

    e_k, w_k, slot_k, counts = _router(hpk, hlo, w_router[0], b_router[0], tb)
    tme = EXPERT_ROW_BLOCK
    dest, pad_end, block_expert, n_used, n_rows = _routing_tables(e_k, slot_k, counts, tme)
    xs = _dispatch(pad_end, dest, hpk, n_rows, tb, tme)
    ys = _experts(block_expert, n_used, xs, w_gate[0], w_up[0], w_down[0], tme)
    gt2 = jnp.concatenate([jnp.broadcast_to(mod_p[:, None, 5:6, :], (b, t // ts, 1, d)).reshape(-1, 1, d),
                           mod_s[:, 5:6, :]], axis=0)
    y_p, y_s = _combine(dest, w_k.T, ys, x1, hpk, gt2, ws_gate[0].astype(BF16), ws_up[0].astype(BF16),
                        ws_down[0].astype(BF16), g_final, n_prompt, tb)
    return (y_p.reshape(b, t, d), y_s.reshape(bs, ts, d), kp[None], vp[None], hp[None], cp[None],
            ks[None], vs[None], hs[None], cs[None])
```

```python
import functools

import jax
import jax.numpy as jnp
from jax import lax
from jax.experimental import pallas as pl
from jax.experimental.pallas import tpu as pltpu

F32 = jnp.float32
BF16 = jnp.bfloat16
I32 = jnp.int32
U32 = jnp.uint32

LANES = 128
SUBLANES = 8
HEAD_DIM = 128
LRU_BLOCK_DIM = 256
LRU_C = 8.0
CONV_WIDTH = 4
N_EXPERTS = 64
TOP_K = 6
N_GROUPS = 8
GROUP_SIZE = N_EXPERTS // N_GROUPS
TOPK_GROUPS = 4
ROUTED_SCALE = 2.5
NORM_EPS = 1e-6
N_MOD = 6
TIME_GROUP = 16
LOG2_E = 1.4426950408889634
ATTN_HEADS_PER_STEP = 4
F32_ZERO_WEIGHT_LOG2 = -150.0
EXPERT_ROW_BLOCK = 512
V7X_VMEM_LIMIT_BYTES = 56 * 1024 * 1024
NT_DIMS = (((1,), (1,)), ((), ()))
NEG_INF = float("-inf")


def _cparams(semantics):
    return pltpu.CompilerParams(dimension_semantics=semantics, vmem_limit_bytes=V7X_VMEM_LIMIT_BYTES)


def _dot(a, b):
    return jnp.dot(a, b, preferred_element_type=F32)


def _split_bf16(x):
    hi = x.astype(BF16)
    lo = (x - hi.astype(F32)).astype(BF16)
    return hi, lo


def _dot3(a, b):
    ah, al = _split_bf16(a)
    bh, bl = _split_bf16(b)
    return _dot(ah, bh) + (_dot(al, bh) + _dot(ah, bl))


def _pack_pairs(h):
    n = h.shape[1] // 2
    bits = lax.bitcast_convert_type(h.astype(BF16).astype(F32), U32)
    return bits[:, :n] | (bits[:, n:] >> 16)


def _unpack_pairs(p, dtype):
    a = lax.bitcast_convert_type(p & jnp.uint32(0xFFFF0000), F32)
    b = lax.bitcast_convert_type(p << 16, F32)
    return jnp.concatenate([a.astype(dtype), b.astype(dtype)], axis=1)


def _rms(x):
    return x * lax.rsqrt(jnp.mean(x * x, axis=-1, keepdims=True) + NORM_EPS)


def _ada_kernel(c_ref, w_ref, b_ref, o_ref):
    c = c_ref[...]
    o_ref[...] = _dot3(c * jax.nn.sigmoid(c), w_ref[...]) + b_ref[...]


def _ada(c, w, b):
    m, d = c.shape
    n = w.shape[1]
    tn = min(n, 1024)
    return pl.pallas_call(
        _ada_kernel,
        grid=(n // tn,),
        in_specs=[pl.BlockSpec((m, d), lambda j: (0, 0)),
                  pl.BlockSpec((d, tn), lambda j: (0, j)),
                  pl.BlockSpec((1, tn), lambda j: (0, j))],
        out_specs=pl.BlockSpec((m, tn), lambda j: (0, j)),
        out_shape=jax.ShapeDtypeStruct((m, n), F32),
        compiler_params=_cparams(("arbitrary",)),
        name="ada",
    )(c, w, b.reshape(1, n))


def _row_permutation(n_outer, n_inner):
    n = n_outer * n_inner
    out_row = lax.broadcasted_iota(I32, (n, n), 0)
    in_row = lax.broadcasted_iota(I32, (n, n), 1)
    same = (out_row // n_outer == in_row % n_inner) & (out_row % n_outer == in_row // n_inner)
    return same.astype(BF16)


def _norm_mod_kernel(x_ref, mod_ref, g_ref, obm_ref, otm_ref):
    b, tt, d = x_ref.shape
    m = mod_ref[...]
    h = (_rms(x_ref[...]) * g_ref[...].reshape(1, 1, d) * (1.0 + m[:, 1:2, :]) + m[:, 0:1, :]).astype(BF16)
    obm_ref[...] = h
    otm_ref[...] = _dot(_row_permutation(b, tt), h.reshape(b * tt, d)).astype(BF16)


def _norm_mod(x, mod, g):
    b, t, d = x.shape
    tt = TIME_GROUP
    return pl.pallas_call(
        _norm_mod_kernel,
        grid=(t // tt,),
        in_specs=[pl.BlockSpec((b, tt, d), lambda ti: (0, ti, 0)),
                  pl.BlockSpec((b, N_MOD, d), lambda ti: (0, 0, 0)),
                  pl.BlockSpec((1, d), lambda ti: (0, 0))],
        out_specs=[pl.BlockSpec((b, tt, d), lambda ti: (0, ti, 0)),
                   pl.BlockSpec((tt * b, d), lambda ti: (ti, 0))],
        out_shape=[jax.ShapeDtypeStruct((b, t, d), BF16), jax.ShapeDtypeStruct((t * b, d), BF16)],
        compiler_params=_cparams(("arbitrary",)),
        name="norm_mod",
    )(x, mod, g.reshape(1, d))


def _proj_kernel(a_ref, w_ref, o_ref, w_s, *, epilogue):
    @pl.when(pl.program_id(0) == 0)
    def _():
        w_s[...] = w_ref[...].astype(BF16)

    acc = _dot(a_ref[...], w_s[...])
    if epilogue == "gelu":
        acc = jax.nn.gelu(acc)
    elif epilogue == "sigmoid":
        acc = jax.nn.sigmoid(acc)
    elif epilogue == "qscale":
        acc = acc * (HEAD_DIM ** -0.5 * LOG2_E)
    o_ref[...] = acc.astype(o_ref.dtype)


def _resident(shape, index_map):
    return pl.BlockSpec(shape, index_map, pipeline_mode=pl.Buffered(1))


def _proj(a, w, col, width, epilogue, out_dtype, tm):
    m, k = a.shape
    return pl.pallas_call(
        functools.partial(_proj_kernel, epilogue=epilogue),
        grid=(m // tm,),
        in_specs=[pl.BlockSpec((tm, k), lambda i: (i, 0)),
                  _resident((k, width), lambda i: (0, col))],
        out_specs=pl.BlockSpec((tm, width), lambda i: (i, 0)),
        out_shape=jax.ShapeDtypeStruct((m, width), out_dtype),
        scratch_shapes=[pltpu.VMEM((k, width), BF16)],
        compiler_params=_cparams(("arbitrary",)),
        name="proj_" + epilogue,
    )(a, w)


def _lru_kernel(u_ref, gl_ref, sl_ref, cw_ref, cb_ref, wa_ref, ba_ref, wx_ref, bx_ref, lam_ref, h0_ref, prev_ref,
                y_ref, hl_ref, cn_ref, tail_s, h_s, a_s, b_s, y_s, *, first_pos_zero):
    tc, nb, wb = u_ref.shape
    ti = pl.program_id(1)

    @pl.when(ti == 0)
    def _():
        tail_s[...] = prev_ref[...]
        h_s[...] = h0_ref[...]

    u = u_ref[...].astype(F32)
    xp = jnp.concatenate([tail_s[...], u], axis=0)
    cw = cw_ref[...]
    xc = cb_ref[...].reshape(1, 1, wb)
    for j in range(CONV_WIDTH):
        xc = xc + xp[j:j + tc] * cw[j:j + 1].reshape(1, 1, wb)
    last_rows = u[tc - (CONV_WIDTH - 1):]
    tail_s[...] = last_rows
    cn_ref[...] = last_rows

    x2 = xc.reshape(tc * nb, wb)
    xb = x2.astype(BF16)
    r = jax.nn.sigmoid(_dot(xb, wa_ref[0]) + ba_ref[0])
    gi = jax.nn.sigmoid(_dot(xb, wx_ref[0]) + bx_ref[0])
    log_a = (-LRU_C) * r * jax.nn.softplus(-lam_ref[...])
    a = jnp.exp(log_a)
    mult = jnp.sqrt(-jnp.tanh(log_a) * (a * a + 1.0))
    if first_pos_zero:
        row = lax.broadcasted_iota(I32, (tc * nb, wb), 0)
        mult = jnp.where((row < nb) & (ti == 0), 1.0, mult)
    a_s[...] = a.reshape(tc, nb, wb)
    b_s[...] = (mult * (gi * x2)).reshape(tc, nb, wb)

    def step(t, h):
        h = a_s[t] * h + b_s[t]
        gate = gl_ref[t].astype(F32) * sl_ref[t].astype(F32)
        y_s[t] = h * gate
        return h

    h = lax.fori_loop(0, tc, step, h_s[...], unroll=8)
    h_s[...] = h
    hl_ref[...] = h

    perm = _row_permutation(TIME_GROUP, nb)
    for g in range(tc // TIME_GROUP):
        rows = slice(g * TIME_GROUP, (g + 1) * TIME_GROUP)
        yg = y_s[rows].reshape(TIME_GROUP * nb, wb).astype(BF16)
        y_ref[:, rows, :] = _dot(perm, yg).astype(BF16).reshape(nb, TIME_GROUP, wb)


def _lru(u, gl, sl, conv_w, conv_b, wa, ba, wx, bx, lam, h0, prev, tc, first_pos_zero):
    t, b, w = u.shape
    wb = LRU_BLOCK_DIM
    nw = w // wb
    seq = pl.BlockSpec((tc, b, wb), lambda wi, ti: (ti, 0, wi))
    seq_out = pl.BlockSpec((b, tc, wb), lambda wi, ti: (0, ti, wi))
    chan = lambda rows: pl.BlockSpec((rows, wb), lambda wi, ti: (0, wi))
    gate_w = pl.BlockSpec((1, wb, wb), lambda wi, ti: (wi, 0, 0))
    gate_b = pl.BlockSpec((1, 1, wb), lambda wi, ti: (wi, 0, 0))
    tail = pl.BlockSpec((CONV_WIDTH - 1, b, wb), lambda wi, ti: (0, 0, wi))
    return pl.pallas_call(
        functools.partial(_lru_kernel, first_pos_zero=first_pos_zero),
        grid=(nw, t // tc),
        in_specs=[seq, seq, seq, chan(CONV_WIDTH), chan(1), gate_w, gate_b, gate_w, gate_b, chan(1),
                  chan(b), tail],
        out_specs=[seq_out, chan(b), tail],
        out_shape=[jax.ShapeDtypeStruct((b, t, w), BF16), jax.ShapeDtypeStruct((b, w), F32),
                   jax.ShapeDtypeStruct((CONV_WIDTH - 1, b, w), F32)],
        scratch_shapes=[pltpu.VMEM((CONV_WIDTH - 1, b, wb), F32), pltpu.VMEM((b, wb), F32),
                        pltpu.VMEM((tc, b, wb), F32), pltpu.VMEM((tc, b, wb), F32),
                        pltpu.VMEM((tc, b, wb), F32)],
        compiler_params=_cparams(("arbitrary", "arbitrary")),
        name="lru",
    )(u, gl, sl, conv_w, conv_b.reshape(1, w), wa, ba.reshape(nw, 1, wb), wx, bx.reshape(nw, 1, wb),
      lam.reshape(1, w), h0, prev)


def _log_keep_and_logsig(z):
    nz = -z
    lk = jnp.minimum(nz, 0.0) - jnp.log2(1.0 + jnp.exp2(jnp.minimum(z, nz)))
    return lk, z + lk


def _later_key_sums(lk, tri):
    hi, lo = _split_bf16(lk)
    return _dot(hi, tri) + _dot(lo, tri)


def _strict_lower(n, dtype):
    row = lax.broadcasted_iota(I32, (n, n), 0)
    col = lax.broadcasted_iota(I32, (n, n), 1)
    return (row > col).astype(dtype), col < row


def _attn_kernel(q_ref, k_ref, v_ref, sz_ref, yl_ref, o_ref):
    tq = q_ref.shape[1]
    nh = q_ref.shape[2] // HEAD_DIM
    qi = pl.program_id(2)
    tri, causal = _strict_lower(tq, BF16)

    def tile(j, carry, diag):
        start = pl.multiple_of(j * tq, tq)
        heads = [slice(h * HEAD_DIM, (h + 1) * HEAD_DIM) for h in range(nh)]
        zs = [lax.dot_general(q_ref[0, :, cols], k_ref[0, pl.ds(start, tq), cols].astype(BF16), NT_DIMS,
                              preferred_element_type=F32) for cols in heads]
        lks, lss = [], []
        for z in zs:
            lk, ls = _log_keep_and_logsig(z)
            lks.append(jnp.where(causal, lk, 0.0) if diag else lk)
            lss.append(ls)
        rcs = [_later_key_sums(lk, tri) for lk in lks]
        ws = []
        for h in range(nh):
            w = jnp.exp2(lss[h] + rcs[h] + carry[h][1])
            ws.append((jnp.where(causal, w, 0.0) if diag else w).astype(BF16))
        return tuple((carry[h][0] + _dot(ws[h], v_ref[0, pl.ds(start, tq), heads[h]].astype(BF16)),
                      carry[h][1] + (rcs[h][:, 0:1] + lks[h][:, 0:1])) for h in range(nh))

    def any_weight_left(carry):
        c_max = carry[0][1]
        for h in range(1, nh):
            c_max = jnp.maximum(c_max, carry[h][1])
        return jnp.max(c_max) > F32_ZERO_WEIGHT_LOG2

    init = tuple((jnp.zeros((tq, HEAD_DIM), F32), jnp.zeros((tq, 1), F32)) for _ in range(nh))
    carry = tile(qi, init, True)

    def more(state):
        return (state[0] < qi) & state[1]

    def farther_tile(state):
        carry = tile(qi - 1 - state[0], state[2], False)
        return state[0] + 1, any_weight_left(carry), carry

    carry = lax.while_loop(more, farther_tile, (jnp.int32(0), any_weight_left(carry), carry))[2]
    acc = jnp.concatenate([carry[h][0] for h in range(nh)], axis=1)
    o_ref[0] = (yl_ref[0].astype(F32) + sz_ref[0].astype(F32) * acc).astype(BF16)


def _attn_prompt(q, k, v, sz, yl, tq):
    b, t, a = q.shape
    wblk = ATTN_HEADS_PER_STEP * HEAD_DIM
    qspec = pl.BlockSpec((1, tq, wblk), lambda bi, hi, qi: (bi, qi, hi))
    kvspec = pl.BlockSpec((1, t, wblk), lambda bi, hi, qi: (bi, 0, hi))
    return pl.pallas_call(
        _attn_kernel,
        grid=(b, a // wblk, t // tq),
        in_specs=[qspec, kvspec, kvspec, qspec, qspec],
        out_specs=qspec,
        out_shape=jax.ShapeDtypeStruct((b, t, a), BF16),
        compiler_params=_cparams(("arbitrary", "arbitrary", "arbitrary")),
        name="attn_prompt",
    )(q, k, v, sz, yl)


def _attn_sample_kernel(q_ref, kn_ref, vn_ref, kc_ref, vc_ref, sz_ref, yl_ref, o_ref, lk_s, ls_s):
    ts = q_ref.shape[1]
    nh = q_ref.shape[2] // HEAD_DIM
    past = kc_ref.shape[1]
    tri_n, causal = _strict_lower(ts, BF16)
    tri_p, _ = _strict_lower(past, BF16)
    accs, carries = [], []
    for h in range(nh):
        cols = slice(h * HEAD_DIM, (h + 1) * HEAD_DIM)
        q = q_ref[0, :, cols]
        zn = lax.dot_general(q, kn_ref[0, :, cols].astype(BF16), NT_DIMS, preferred_element_type=F32)
        lk, ls = _log_keep_and_logsig(zn)
        lk = jnp.where(causal, lk, 0.0)
        rc = _later_key_sums(lk, tri_n)
        w = jnp.where(causal, jnp.exp2(ls + rc), 0.0)
        accs.append(_dot(w.astype(BF16), vn_ref[0, :, cols].astype(BF16)))
        carries.append(rc[:, 0:1] + lk[:, 0:1])
        zp = lax.dot_general(q, kc_ref[0, :, h, :].astype(BF16), NT_DIMS, preferred_element_type=F32)
        lkp, lsp = _log_keep_and_logsig(zp)
        lk_s[h * ts:(h + 1) * ts, :] = lkp
        ls_s[h * ts:(h + 1) * ts, :] = lsp
    rcp = _later_key_sums(lk_s[...], tri_p)
    for h in range(nh):
        cols = slice(h * HEAD_DIM, (h + 1) * HEAD_DIM)
        rows = slice(h * ts, (h + 1) * ts)
        w = jnp.exp2(ls_s[rows, :] + rcp[rows, :] + carries[h])
        acc = accs[h] + _dot(w.astype(BF16), vc_ref[0, :, h, :].astype(BF16))
        o_ref[0, :, cols] = (yl_ref[0, :, cols].astype(F32) + sz_ref[0, :, cols].astype(F32) * acc).astype(BF16)


def _attn_sample(q, kn, vn, kc, vc, sz, yl):
    b, ts, a = q.shape
    past = kc.shape[1]
    nh = a // HEAD_DIM
    cur = pl.BlockSpec((1, ts, a), lambda bi: (bi, 0, 0))
    cache = pl.BlockSpec((1, past, nh, HEAD_DIM), lambda bi: (bi, 0, 0, 0))
    return pl.pallas_call(
        _attn_sample_kernel,
        grid=(b,),
        in_specs=[cur, cur, cur, cache, cache, cur, cur],
        out_specs=cur,
        out_shape=jax.ShapeDtypeStruct((b, ts, a), BF16),
        scratch_shapes=[pltpu.VMEM((nh * ts, past), F32), pltpu.VMEM((nh * ts, past), F32)],
        compiler_params=_cparams(("arbitrary",)),
        name="attn_sample",
    )(q, kn, vn, kc, vc, sz, yl)


def _outproj_kernel(a_ref, w_ref, x_ref, mod_ref, g_ref, *rest):
    x1_ref, hpk_ref, hlo_ref, w_s = rest[-4:]
    nb = mod_ref.shape[0]
    tm, d = x_ref.shape

    @pl.when(pl.program_id(0) == 0)
    def _():
        w_s[...] = w_ref[...].astype(BF16)

    m = mod_ref[...]
    o = _dot(a_ref[...], w_s[...]).reshape(nb, tm // nb, d)
    x1 = x_ref[...].reshape(nb, tm // nb, d) + m[:, 2:3, :] * o
    h = _rms(x1) * g_ref[...].reshape(1, 1, d) * (1.0 + m[:, 4:5, :]) + m[:, 3:4, :]
    x1_ref[...] = x1.reshape(tm, d)
    h = h.reshape(tm, d)
    hi = h.astype(BF16).astype(F32)
    hpk_ref[...] = _pack_pairs(hi)
    hlo_ref[...] = (h - hi).astype(BF16)


def _outproj(a, w, x, mod, g, tm, rows_per_batch, n_total, row_offset, earlier=None):
    m, d = x.shape
    nb = max(tm // rows_per_batch, 1)
    per = max(rows_per_batch // tm, 1)
    off = row_offset // tm
    row = lambda width: pl.BlockSpec((tm, width), lambda i: (i, 0))
    out_row = lambda width: pl.BlockSpec((tm, width), lambda i: (i + off, 0))
    in_specs = [row(d), _resident((d, d), lambda i: (0, 0)), row(d),
                pl.BlockSpec((nb, N_MOD, d), lambda i: (i // per, 0, 0)),
                pl.BlockSpec((1, d), lambda i: (0, 0))]
    args = [a, w, x, mod, g.reshape(1, d)]
    aliases = {}
    if earlier is not None:
        aliases = {len(args) + j: j for j in range(len(earlier))}
        in_specs += [pl.BlockSpec(memory_space=pl.ANY)] * len(earlier)
        args += list(earlier)
    return pl.pallas_call(
        _outproj_kernel,
        grid=(m // tm,),
        in_specs=in_specs,
        out_specs=[out_row(d), out_row(d // 2), out_row(d)],
        out_shape=[jax.ShapeDtypeStruct((n_total, d), F32), jax.ShapeDtypeStruct((n_total, d // 2), U32),
                   jax.ShapeDtypeStruct((n_total, d), BF16)],
        scratch_shapes=[pltpu.VMEM((d, d), BF16)],
        input_output_aliases=aliases,
        compiler_params=_cparams(("arbitrary",)),
        name="outproj",
    )(*args)


def _sub_max(x):
    return jnp.max(x, axis=0, keepdims=True)


def _sub_min(x):
    return jnp.min(x, axis=0, keepdims=True)


def _sub_sum(x):
    return jnp.sum(x, axis=0, keepdims=True)


def _router_kernel(hpk_ref, hlo_ref, wrh_ref, wrl_ref, br_ref, e_ref, w_ref, slot_ref, cnt_ref, base_s):
    tm = hpk_ref.shape[0]

    @pl.when(pl.program_id(0) == 0)
    def _():
        base_s[...] = jnp.zeros_like(base_s)

    hi = _unpack_pairs(hpk_ref[...], BF16)
    lo = hlo_ref[...]
    wrh = wrh_ref[...]
    logits = lax.dot_general(wrh, hi, NT_DIMS, preferred_element_type=F32)
    logits = logits + (lax.dot_general(wrh, lo, NT_DIMS, preferred_element_type=F32)
                       + lax.dot_general(wrl_ref[...], hi, NT_DIMS, preferred_element_type=F32))
    score = jax.nn.sigmoid(logits)
    sel = (score + br_ref[...]).reshape(GROUP_SIZE, N_GROUPS, tm)
    score = score.reshape(GROUP_SIZE, N_GROUPS, tm)

    m1 = sel[0]
    m2 = jnp.full_like(m1, NEG_INF)
    for e in range(1, GROUP_SIZE):
        x = sel[e]
        m2 = jnp.maximum(m2, jnp.minimum(m1, x))
        m1 = jnp.maximum(m1, x)
    cur = m1 + m2
    gidx = lax.broadcasted_iota(I32, (N_GROUPS, tm), 0)
    gmask = jnp.zeros((N_GROUPS, tm), jnp.bool_)
    for _ in range(TOPK_GROUPS):
        first = _sub_min(jnp.where(cur == _sub_max(cur), gidx, N_GROUPS))
        pick = gidx == first
        gmask = gmask | pick
        cur = jnp.where(pick, NEG_INF, cur)

    eid = (lax.broadcasted_iota(I32, (GROUP_SIZE, N_GROUPS, tm), 1) * GROUP_SIZE
           + lax.broadcasted_iota(I32, (GROUP_SIZE, N_GROUPS, tm), 0))
    cur = jnp.where(gmask[None], sel, NEG_INF)
    ids, wts, picks = [], [], []
    for _ in range(TOP_K):
        mx = _sub_max(jnp.max(cur, axis=0))
        first = _sub_min(jnp.min(jnp.where(cur == mx[None], eid, N_EXPERTS), axis=0))
        pick = eid == first[None]
        ids.append(first)
        picks.append(pick)
        wts.append(_sub_sum(jnp.sum(jnp.where(pick, score, 0.0), axis=0)))
        cur = jnp.where(pick, NEG_INF, cur)
    total = wts[0]
    chosen = picks[0]
    for k in range(1, TOP_K):
        total = total + wts[k]
        chosen = chosen | picks[k]

    chosen = jnp.where(chosen, 1.0, 0.0).reshape(N_EXPERTS, tm)
    earlier_or_self = (lax.broadcasted_iota(I32, (tm, tm), 0) <= lax.broadcasted_iota(I32, (tm, tm), 1)).astype(BF16)
    incl = _dot(chosen.astype(BF16), earlier_or_self)
    rank = (base_s[...] + (incl - chosen)).reshape(GROUP_SIZE, N_GROUPS, tm)
    base_s[...] = base_s[...] + incl[:, tm - 1:tm]
    slots = [_sub_sum(jnp.sum(jnp.where(pick, rank, 0.0), axis=0)).astype(I32) for pick in picks]

    pad = 8 - TOP_K
    e_ref[...] = jnp.concatenate(ids + [jnp.zeros((pad, tm), I32)], axis=0)
    w_ref[...] = jnp.concatenate([w / total * ROUTED_SCALE for w in wts] + [jnp.zeros((pad, tm), F32)], axis=0)
    slot_ref[...] = jnp.concatenate(slots + [jnp.zeros((pad, tm), I32)], axis=0)
    cnt_ref[...] = jnp.broadcast_to(base_s[...], cnt_ref.shape)


def _router(hpk, hlo, w_router, b_router, tm):
    n, d = hlo.shape
    wr = w_router.T.reshape(N_GROUPS, GROUP_SIZE, d).transpose(1, 0, 2).reshape(N_EXPERTS, d)
    br = b_router.reshape(N_GROUPS, GROUP_SIZE).T.reshape(N_EXPERTS, 1)
    wrh = wr.astype(BF16)
    wrl = (wr - wrh.astype(F32)).astype(BF16)
    full = lambda shape: pl.BlockSpec(shape, lambda i: (0, 0))
    per_token = pl.BlockSpec((8, tm), lambda i: (0, i))
    e_k, w_k, slot_k, counts = pl.pallas_call(
        _router_kernel,
        grid=(n // tm,),
        in_specs=[pl.BlockSpec((tm, d // 2), lambda i: (i, 0)), pl.BlockSpec((tm, d), lambda i: (i, 0)),
                  full((N_EXPERTS, d)), full((N_EXPERTS, d)), full((N_EXPERTS, 1))],
        out_specs=[per_token, per_token, per_token, full((N_EXPERTS, 128))],
        out_shape=[jax.ShapeDtypeStruct((8, n), I32), jax.ShapeDtypeStruct((8, n), F32),
                   jax.ShapeDtypeStruct((8, n), I32), jax.ShapeDtypeStruct((N_EXPERTS, 128), F32)],
        scratch_shapes=[pltpu.VMEM((N_EXPERTS, 1), F32)],
        compiler_params=_cparams(("arbitrary",)),
        name="router",
    )(hpk, hlo, wrh, wrl, br)
    counts = counts[:, 0].reshape(GROUP_SIZE, N_GROUPS).T.reshape(N_EXPERTS).astype(I32)
    return e_k, w_k, slot_k, counts


def _row_copy(src, src_row, dst, dst_row, sem):
    return pltpu.make_async_copy(src.at[pl.ds(src_row, 1)], dst.at[pl.ds(dst_row, 1)], sem)


def _for_token_groups(n_tokens, per_token):
    def group(g, carry):
        t0 = pl.multiple_of(g * SUBLANES, SUBLANES)
        for u in range(SUBLANES):
            per_token(t0 + u)
        return carry

    lax.fori_loop(0, n_tokens // SUBLANES, group, 0)


def _dispatch_kernel(pad_end_ref, dest_ref, h_ref, xs_ref, zero_s, sem, zero_sem):
    tb = dest_ref.shape[1]
    te = zero_s.shape[0]

    def tail_fill(e):
        start = pl.multiple_of(pad_end_ref[e] - te, te)
        return pltpu.make_async_copy(zero_s, xs_ref.at[pl.ds(start, te)], zero_sem)

    def has_rows(e):
        return pad_end_ref[e] > (pad_end_ref[e - 1] if e else 0)

    @pl.when(pl.program_id(0) == 0)
    def _():
        zero_s[...] = jnp.zeros_like(zero_s)
        for e in range(N_EXPERTS):
            pl.when(has_rows(e))(lambda e=e: tail_fill(e).start())
        for e in range(N_EXPERTS):
            pl.when(has_rows(e))(lambda e=e: tail_fill(e).wait())

    def start(t):
        for k in range(TOP_K):
            _row_copy(h_ref, t, xs_ref, dest_ref[k, t], sem).start()

    def wait(t):
        for k in range(TOP_K):
            _row_copy(h_ref, 0, xs_ref, 0, sem).wait()

    _for_token_groups(tb, start)
    _for_token_groups(tb, wait)


def _dispatch(pad_end, dest, hpk, n_rows, tb, te):
    n, dh = hpk.shape
    grid_spec = pltpu.PrefetchScalarGridSpec(
        num_scalar_prefetch=1,
        grid=(n // tb,),
        in_specs=[pl.BlockSpec((8, tb), lambda i, pe: (0, i), memory_space=pltpu.SMEM),
                  pl.BlockSpec((tb, dh), lambda i, pe: (i, 0))],
        out_specs=pl.BlockSpec(memory_space=pl.ANY),
        scratch_shapes=[pltpu.VMEM((te, dh), U32), pltpu.SemaphoreType.DMA(()), pltpu.SemaphoreType.DMA(())],
    )
    return pl.pallas_call(
        _dispatch_kernel,
        grid_spec=grid_spec,
        out_shape=jax.ShapeDtypeStruct((n_rows, dh), U32),
        compiler_params=_cparams(("arbitrary",)),
        name="dispatch",
    )(pad_end, dest, hpk)


def _experts_kernel(be_ref, nu_ref, x_ref, wg_ref, wu_ref, wd_ref, o_ref, wg_s, wu_s, wd_s):
    i = pl.program_id(0)
    prev = be_ref[jnp.maximum(i - 1, 0)]

    @pl.when((i == 0) | (be_ref[i] != prev))
    def _():
        wg_s[...] = wg_ref[0].astype(BF16)
        wu_s[...] = wu_ref[0].astype(BF16)
        wd_s[...] = wd_ref[0].astype(BF16)

    @pl.when(i < nu_ref[0])
    def _():
        x = _unpack_pairs(x_ref[...], BF16)
        g = _dot(x, wg_s[...])
        hidden = (g * jax.nn.sigmoid(g)) * _dot(x, wu_s[...])
        o_ref[...] = _pack_pairs(_dot(hidden.astype(BF16), wd_s[...]))

    @pl.when(i >= nu_ref[0])
    def _():
        o_ref[...] = jnp.zeros_like(o_ref)


def _experts(block_expert, n_used, xs, w_gate, w_up, w_down, tm):
    p, dh = xs.shape
    d, ff = w_gate.shape[1:]
    rows = pl.BlockSpec((tm, dh), lambda i, be, nu: (i, 0))
    grid_spec = pltpu.PrefetchScalarGridSpec(
        num_scalar_prefetch=2,
        grid=(p // tm,),
        in_specs=[rows,
                  pl.BlockSpec((1, d, ff), lambda i, be, nu: (be[i], 0, 0)),
                  pl.BlockSpec((1, d, ff), lambda i, be, nu: (be[i], 0, 0)),
                  pl.BlockSpec((1, ff, d), lambda i, be, nu: (be[i], 0, 0))],
        out_specs=rows,
        scratch_shapes=[pltpu.VMEM((d, ff), BF16), pltpu.VMEM((d, ff), BF16), pltpu.VMEM((ff, d), BF16)],
    )
    return pl.pallas_call(
        _experts_kernel,
        grid_spec=grid_spec,
        out_shape=jax.ShapeDtypeStruct((p, dh), U32),
        compiler_params=_cparams(("arbitrary",)),
        name="experts",
    )(block_expert, n_used, xs, w_gate, w_up, w_down)


def _combine_kernel(dest_ref, wk_ref, ys_ref, x1_ref, hpk_ref, gt_ref, wsg_ref, wsu_ref, wsd_ref, gf_ref,
                    yp_ref, ysm_ref, buf, sem, *, n_prompt_blocks):
    tb, d = x1_ref.shape
    seg = gt_ref.shape[0]
    i = pl.program_id(0)

    def start(t):
        for k in range(TOP_K):
            _row_copy(ys_ref, dest_ref[k, t], buf.at[k], t, sem).start()

    def wait(t):
        for k in range(TOP_K):
            _row_copy(ys_ref, 0, buf.at[k], 0, sem).wait()

    _for_token_groups(tb, start)
    x = _unpack_pairs(hpk_ref[...], BF16)
    g = _dot(x, wsg_ref[...])
    ffn = _dot(((g * jax.nn.sigmoid(g)) * _dot(x, wsu_ref[...])).astype(BF16), wsd_ref[...])
    _for_token_groups(tb, wait)
    wk = wk_ref[...]
    for k in range(TOP_K):
        ffn = ffn + wk[:, k:k + 1] * _unpack_pairs(buf[k], F32)
    x2 = x1_ref[...].reshape(seg, tb // seg, d) + gt_ref[...] * ffn.reshape(seg, tb // seg, d)
    y = (_rms(x2) * gf_ref[...].reshape(1, 1, d)).reshape(tb, d)

    @pl.when(i < n_prompt_blocks)
    def _():
        yp_ref[...] = y

    @pl.when(i >= n_prompt_blocks)
    def _():
        ysm_ref[...] = y


def _combine(dest, wk_t, ys, x1, hpk, gt2_seg, ws_gate, ws_up, ws_down, g_final, n_prompt, tb):
    n, d = x1.shape
    dh = d // 2
    ff = ws_gate.shape[1]
    npb = n_prompt // tb
    nsb = (n - n_prompt) // tb
    seg = gt2_seg.shape[0] * tb // n
    row = lambda width: pl.BlockSpec((tb, width), lambda i: (i, 0))
    full = lambda shape: pl.BlockSpec(shape, lambda i: (0,) * len(shape))
    return pl.pallas_call(
        functools.partial(_combine_kernel, n_prompt_blocks=npb),
        grid=(n // tb,),
        in_specs=[pl.BlockSpec((8, tb), lambda i: (0, i), memory_space=pltpu.SMEM),
                  row(8), pl.BlockSpec(memory_space=pl.ANY), row(d), row(dh),
                  pl.BlockSpec((seg, 1, d), lambda i: (i, 0, 0)),
                  full((d, ff)), full((d, ff)), full((ff, d)), full((1, d))],
        out_specs=[pl.BlockSpec((tb, d), lambda i: (jnp.minimum(i, npb - 1), 0)),
                   pl.BlockSpec((tb, d), lambda i: (jnp.clip(i - npb, 0, nsb - 1), 0))],
        out_shape=[jax.ShapeDtypeStruct((n_prompt, d), F32), jax.ShapeDtypeStruct((n - n_prompt, d), F32)],
        scratch_shapes=[pltpu.VMEM((TOP_K, tb, dh), U32), pltpu.SemaphoreType.DMA(())],
        compiler_params=_cparams(("arbitrary",)),
        name="combine",
    )(dest, wk_t, ys, x1, hpk, gt2_seg, ws_gate, ws_up, ws_down, g_final.reshape(1, d))


def _largest_tile(n, target):
    t = min(n, target)
    while n % t:
        t -= 1
    return t


def _mixer(x, mod, past_k, past_v, h0, conv0, p, first_pos_zero, n_total, row_offset, earlier):
    b, t, d = x.shape
    m = b * t
    a = d
    hn_bm, hn_tm = _norm_mod(x, mod, p["g_mix"])
    hn_bm = hn_bm.reshape(m, d)
    tm = _largest_tile(m, 512)
    tmh = _largest_tile(m, 1024)
    w_in = p["w_in"]
    u = _proj(hn_tm, w_in, 0, d, "plain", BF16, tmh).reshape(t, b, d)
    gl = _proj(hn_tm, w_in, 1, d, "gelu", BF16, tmh).reshape(t, b, d)
    sl = _proj(hn_tm, w_in, 5, d, "sigmoid", BF16, tmh).reshape(t, b, d)
    q = _proj(hn_bm, w_in, 2, a, "qscale", BF16, tmh).reshape(b, t, a)
    k = _proj(hn_bm, w_in, 3, a, "plain", F32, tm).reshape(b, t, a)
    v = _proj(hn_bm, w_in, 4, a, "plain", F32, tm).reshape(b, t, a)
    sz = _proj(hn_bm, w_in, 6, a, "sigmoid", BF16, tmh).reshape(b, t, a)

    if h0 is None:
        h0 = jnp.zeros((b, d), F32)
        prev = jnp.zeros((CONV_WIDTH - 1, b, d), F32)
    else:
        prev = conv0.transpose(1, 0, 2)
    tc = _largest_tile(t, 128)
    yl, h_last, conv_tail = _lru(u, gl, sl, p["conv_w"], p["conv_b"], p["w_rg_a"], p["b_rg_a"], p["w_rg_x"],
                                 p["b_rg_x"], p["lru_lambda"], h0, prev, tc, first_pos_zero)
    if past_k is None:
        mixed = _attn_prompt(q, k, v, sz, yl, _largest_tile(t, 256))
    else:
        mixed = _attn_sample(q, k, v, past_k, past_v, sz, yl)
    tmo = _largest_tile(m, 256)
    shared = _outproj(mixed.reshape(m, a), p["w_out"], x.reshape(m, d), mod, p["g_ffn"], tmo, t, n_total, row_offset,
                      earlier)
    nh = a // HEAD_DIM
    return (shared, k.reshape(b, t, nh, HEAD_DIM), v.reshape(b, t, nh, HEAD_DIM), h_last,
            conv_tail.transpose(1, 0, 2))


def _routing_tables(e_k, slot_k, counts, tm):
    n_tokens = e_k.shape[1]
    padded = (counts + tm - 1) // tm * tm
    pad_end = jnp.cumsum(padded).astype(I32)
    pad_start = pad_end - padded
    is_expert = e_k[:, :, None] == jnp.arange(N_EXPERTS, dtype=I32)
    dest = slot_k + jnp.sum(jnp.where(is_expert, pad_start, 0), axis=-1).astype(I32)
    n_blocks = -(-(TOP_K * n_tokens + N_EXPERTS * (tm - 1)) // tm)
    block_start = jnp.arange(n_blocks, dtype=I32) * tm
    block_expert = jnp.minimum(jnp.sum(pad_end[None, :] <= block_start[:, None], axis=1), N_EXPERTS - 1).astype(I32)
    n_used = (pad_end[-1:] // tm).astype(I32)
    return dest, pad_end, block_expert, n_used, n_blocks * tm


def kernel(x_prompt, x_sample, cache_k, cache_v, state_lru, state_conv, c_prompt, c_sample, g_mix, w_ada, b_ada, w_in, conv_w, conv_b, w_rg_a, b_rg_a, w_rg_x, b_rg_x, lru_lambda, w_out, g_ffn, w_router, b_router, w_gate, w_up, w_down, ws_gate, ws_up, ws_down, g_final):
    assert g_mix.shape[0] == 1, "single-layer trunk only"
    b, t, d = x_prompt.shape
    bs, ts, _ = x_sample.shape
    assert ts >= CONV_WIDTH - 1 and ts == TIME_GROUP and t % TIME_GROUP == 0
    p = dict(g_mix=g_mix[0], w_in=w_in[0], conv_w=conv_w[0], conv_b=conv_b[0],
             w_rg_a=w_rg_a[0].astype(BF16), b_rg_a=b_rg_a[0], w_rg_x=w_rg_x[0].astype(BF16), b_rg_x=b_rg_x[0],
             lru_lambda=lru_lambda[0], w_out=w_out[0], g_ffn=g_ffn[0])
    mod = _ada(jnp.concatenate([c_prompt, c_sample], axis=0), w_ada[0], b_ada[0]).reshape(b + bs, N_MOD, d)
    mod_p, mod_s = mod[:b], mod[b:]

    n_prompt = b * t
    n = n_prompt + bs * ts
    tb = _largest_tile(bs * ts, 256)
    assert n_prompt % tb == 0 and tb % ts == 0

    shared, kp, vp, hp, cp = _mixer(x_prompt, mod_p, None, None, None, None, p, True, n, 0, None)
    (x1, hpk, hlo), ks, vs, hs, cs = _mixer(x_sample, mod_s, cache_k[0], cache_v[0], state_lru[0], state_conv[0],
                                            p, False, n, n_prompt, shared)

    e_k, w_k, slot_k, counts = _router(hpk, hlo, w_router[0], b_router[0], tb)
    tme = EXPERT_ROW_BLOCK
    dest, pad_end, block_expert, n_used, n_rows = _routing_tables(e_k, slot_k, counts, tme)
    xs = _dispatch(pad_end, dest, hpk, n_rows, tb, tme)
    ys = _experts(block_expert, n_used, xs, w_gate[0], w_up[0], w_down[0], tme)
    gt2 = jnp.concatenate([jnp.broadcast_to(mod_p[:, None, 5:6, :], (b, t // ts, 1, d)).reshape(-1, 1, d),
                           mod_s[:, 5:6, :]], axis=0)
    y_p, y_s = _combine(dest, w_k.T, ys, x1, hpk, gt2, ws_gate[0].astype(BF16), ws_up[0].astype(BF16),
                        ws_down[0].astype(BF16), g_final, n_prompt, tb)
    return (y_p.reshape(b, t, d), y_s.reshape(bs, ts, d), kp[None], vp[None], hp[None], cp[None],
            ks[None], vs[None], hs[None], cs[None])
```

```python
import functools

import jax
import jax.numpy as jnp
from jax import lax
from jax.experimental import pallas as pl
from jax.experimental.pallas import tpu as pltpu

F32 = jnp.float32
BF16 = jnp.bfloat16
I32 = jnp.int32
U32 = jnp.uint32

LANES = 128
SUBLANES = 8
HEAD_DIM = 128
LRU_BLOCK_DIM = 256
LRU_C = 8.0
CONV_WIDTH = 4
N_EXPERTS = 64
TOP_K = 6
N_GROUPS = 8
GROUP_SIZE = N_EXPERTS // N_GROUPS
TOPK_GROUPS = 4
ROUTED_SCALE = 2.5
NORM_EPS = 1e-6
N_MOD = 6
TIME_GROUP = 16
LOG2_E = 1.4426950408889634
ATTN_HEADS_PER_STEP = 4
F32_ZERO_WEIGHT_LOG2 = -150.0
EXPERT_ROW_BLOCK = 512
V7X_VMEM_LIMIT_BYTES = 56 * 1024 * 1024
NT_DIMS = (((1,), (1,)), ((), ()))
NEG_INF = float("-inf")


def _cparams(semantics):
    return pltpu.CompilerParams(dimension_semantics=semantics, vmem_limit_bytes=V7X_VMEM_LIMIT_BYTES)


def _dot(a, b):
    return jnp.dot(a, b, preferred_element_type=F32)


def _split_bf16(x):
    hi = x.astype(BF16)
    lo = (x - hi.astype(F32)).astype(BF16)
    return hi, lo


def _dot3(a, b):
    ah, al = _split_bf16(a)
    bh, bl = _split_bf16(b)
    return _dot(ah, bh) + (_dot(al, bh) + _dot(ah, bl))


def _pack_pairs(h):
    n = h.shape[1] // 2
    bits = lax.bitcast_convert_type(h.astype(BF16).astype(F32), U32)
    return bits[:, :n] | (bits[:, n:] >> 16)


def _unpack_pairs(p, dtype):
    a = lax.bitcast_convert_type(p & jnp.uint32(0xFFFF0000), F32)
    b = lax.bitcast_convert_type(p << 16, F32)
    return jnp.concatenate([a.astype(dtype), b.astype(dtype)], axis=1)


def _rms(x):
    return x * lax.rsqrt(jnp.mean(x * x, axis=-1, keepdims=True) + NORM_EPS)


def _ada_kernel(c_ref, w_ref, b_ref, o_ref):
    c = c_ref[...]
    o_ref[...] = _dot3(c * jax.nn.sigmoid(c), w_ref[...]) + b_ref[...]


def _ada(c, w, b):
    m, d = c.shape
    n = w.shape[1]
    tn = min(n, 1024)
    return pl.pallas_call(
        _ada_kernel,
        grid=(n // tn,),
        in_specs=[pl.BlockSpec((m, d), lambda j: (0, 0)),
                  pl.BlockSpec((d, tn), lambda j: (0, j)),
                  pl.BlockSpec((1, tn), lambda j: (0, j))],
        out_specs=pl.BlockSpec((m, tn), lambda j: (0, j)),
        out_shape=jax.ShapeDtypeStruct((m, n), F32),
        compiler_params=_cparams(("arbitrary",)),
        name="ada",
    )(c, w, b.reshape(1, n))


def _row_permutation(n_outer, n_inner):
    n = n_outer * n_inner
    out_row = lax.broadcasted_iota(I32, (n, n), 0)
    in_row = lax.broadcasted_iota(I32, (n, n), 1)
    same = (out_row // n_outer == in_row % n_inner) & (out_row % n_outer == in_row // n_inner)
    return same.astype(BF16)


def _norm_mod_kernel(x_ref, mod_ref, g_ref, obm_ref, otm_ref):
    b, tt, d = x_ref.shape
    m = mod_ref[...]
    h = (_rms(x_ref[...]) * g_ref[...].reshape(1, 1, d) * (1.0 + m[:, 1:2, :]) + m[:, 0:1, :]).astype(BF16)
    obm_ref[...] = h
    otm_ref[...] = _dot(_row_permutation(b, tt), h.reshape(b * tt, d)).astype(BF16)


def _norm_mod(x, mod, g):
    b, t, d = x.shape
    tt = TIME_GROUP
    return pl.pallas_call(
        _norm_mod_kernel,
        grid=(t // tt,),
        in_specs=[pl.BlockSpec((b, tt, d), lambda ti: (0, ti, 0)),
                  pl.BlockSpec((b, N_MOD, d), lambda ti: (0, 0, 0)),
                  pl.BlockSpec((1, d), lambda ti: (0, 0))],
        out_specs=[pl.BlockSpec((b, tt, d), lambda ti: (0, ti, 0)),
                   pl.BlockSpec((tt * b, d), lambda ti: (ti, 0))],
        out_shape=[jax.ShapeDtypeStruct((b, t, d), BF16), jax.ShapeDtypeStruct((t * b, d), BF16)],
        compiler_params=_cparams(("arbitrary",)),
        name="norm_mod",
    )(x, mod, g.reshape(1, d))


def _proj_kernel(a_ref, w_ref, o_ref, w_s, *, epilogue):
    @pl.when(pl.program_id(0) == 0)
    def _():
        w_s[...] = w_ref[...].astype(BF16)

    acc = _dot(a_ref[...], w_s[...])
    if epilogue == "gelu":
        acc = jax.nn.gelu(acc)
    elif epilogue == "sigmoid":
        acc = jax.nn.sigmoid(acc)
    elif epilogue == "qscale":
        acc = acc * (HEAD_DIM ** -0.5 * LOG2_E)
    o_ref[...] = acc.astype(o_ref.dtype)


def _resident(shape, index_map):
    return pl.BlockSpec(shape, index_map, pipeline_mode=pl.Buffered(1))


def _proj(a, w, col, width, epilogue, out_dtype, tm):
    m, k = a.shape
    return pl.pallas_call(
        functools.partial(_proj_kernel, epilogue=epilogue),
        grid=(m // tm,),
        in_specs=[pl.BlockSpec((tm, k), lambda i: (i, 0)),
                  _resident((k, width), lambda i: (0, col))],
        out_specs=pl.BlockSpec((tm, width), lambda i: (i, 0)),
        out_shape=jax.ShapeDtypeStruct((m, width), out_dtype),
        scratch_shapes=[pltpu.VMEM((k, width), BF16)],
        compiler_params=_cparams(("arbitrary",)),
        name="proj_" + epilogue,
    )(a, w)


def _lru_kernel(u_ref, gl_ref, sl_ref, cw_ref, cb_ref, wa_ref, ba_ref, wx_ref, bx_ref, lam_ref, h0_ref, prev_ref,
                y_ref, hl_ref, cn_ref, tail_s, h_s, a_s, b_s, y_s, *, first_pos_zero):
    tc, nb, wb = u_ref.shape
    ti = pl.program_id(1)

    @pl.when(ti == 0)
    def _():
        tail_s[...] = prev_ref[...]
        h_s[...] = h0_ref[...]

    u = u_ref[...].astype(F32)
    xp = jnp.concatenate([tail_s[...], u], axis=0)
    cw = cw_ref[...]
    xc = cb_ref[...].reshape(1, 1, wb)
    for j in range(CONV_WIDTH):
        xc = xc + xp[j:j + tc] * cw[j:j + 1].reshape(1, 1, wb)
    last_rows = u[tc - (CONV_WIDTH - 1):]
    tail_s[...] = last_rows
    cn_ref[...] = last_rows

    x2 = xc.reshape(tc * nb, wb)
    xb = x2.astype(BF16)
    r = jax.nn.sigmoid(_dot(xb, wa_ref[0]) + ba_ref[0])
    gi = jax.nn.sigmoid(_dot(xb, wx_ref[0]) + bx_ref[0])
    log_a = r * ((-LRU_C) * jax.nn.softplus(-lam_ref[...]))
    a = jnp.exp(log_a)
    m2 = -jnp.tanh(log_a) * (a * a + 1.0)
    mult = jnp.where(m2 > 0.0, m2 * lax.rsqrt(m2), 0.0)
    if first_pos_zero:
        row = lax.broadcasted_iota(I32, (tc * nb, wb), 0)
        mult = jnp.where((row < nb) & (ti == 0), 1.0, mult)
    a_s[...] = a.reshape(tc, nb, wb)
    b_s[...] = (mult * (gi * x2)).reshape(tc, nb, wb)

    def step(t, h):
        h = a_s[t] * h + b_s[t]
        gate = gl_ref[t].astype(F32) * sl_ref[t].astype(F32)
        y_s[t] = h * gate
        return h

    h = lax.fori_loop(0, tc, step, h_s[...], unroll=8)
    h_s[...] = h
    hl_ref[...] = h

    perm = _row_permutation(TIME_GROUP, nb)
    for g in range(tc // TIME_GROUP):
        rows = slice(g * TIME_GROUP, (g + 1) * TIME_GROUP)
        yg = y_s[rows].reshape(TIME_GROUP * nb, wb).astype(BF16)
        y_ref[:, rows, :] = _dot(perm, yg).astype(BF16).reshape(nb, TIME_GROUP, wb)


def _lru(u, gl, sl, conv_w, conv_b, wa, ba, wx, bx, lam, h0, prev, tc, first_pos_zero):
    t, b, w = u.shape
    wb = LRU_BLOCK_DIM
    nw = w // wb
    seq = pl.BlockSpec((tc, b, wb), lambda wi, ti: (ti, 0, wi))
    seq_out = pl.BlockSpec((b, tc, wb), lambda wi, ti: (0, ti, wi))
    chan = lambda rows: pl.BlockSpec((rows, wb), lambda wi, ti: (0, wi))
    gate_w = pl.BlockSpec((1, wb, wb), lambda wi, ti: (wi, 0, 0))
    gate_b = pl.BlockSpec((1, 1, wb), lambda wi, ti: (wi, 0, 0))
    tail = pl.BlockSpec((CONV_WIDTH - 1, b, wb), lambda wi, ti: (0, 0, wi))
    return pl.pallas_call(
        functools.partial(_lru_kernel, first_pos_zero=first_pos_zero),
        grid=(nw, t // tc),
        in_specs=[seq, seq, seq, chan(CONV_WIDTH), chan(1), gate_w, gate_b, gate_w, gate_b, chan(1),
                  chan(b), tail],
        out_specs=[seq_out, chan(b), tail],
        out_shape=[jax.ShapeDtypeStruct((b, t, w), BF16), jax.ShapeDtypeStruct((b, w), F32),
                   jax.ShapeDtypeStruct((CONV_WIDTH - 1, b, w), F32)],
        scratch_shapes=[pltpu.VMEM((CONV_WIDTH - 1, b, wb), F32), pltpu.VMEM((b, wb), F32),
                        pltpu.VMEM((tc, b, wb), F32), pltpu.VMEM((tc, b, wb), F32),
                        pltpu.VMEM((tc, b, wb), F32)],
        compiler_params=_cparams(("arbitrary", "arbitrary")),
        name="lru",
    )(u, gl, sl, conv_w, conv_b.reshape(1, w), wa, ba.reshape(nw, 1, wb), wx, bx.reshape(nw, 1, wb),
      lam.reshape(1, w), h0, prev)


def _log_keep_and_logsig(z):
    nz = -z
    lk = jnp.minimum(nz, 0.0) - jnp.log2(1.0 + jnp.exp2(jnp.minimum(z, nz)))
    return lk, z + lk


def _later_key_sums(lk, tri):
    hi, lo = _split_bf16(lk)
    return _dot(hi, tri) + _dot(lo, tri)


def _strict_lower(n, dtype):
    row = lax.broadcasted_iota(I32, (n, n), 0)
    col = lax.broadcasted_iota(I32, (n, n), 1)
    return (row > col).astype(dtype), col < row


def _attn_kernel(q_ref, k_ref, v_ref, sz_ref, yl_ref, o_ref):
    tq = q_ref.shape[1]
    nh = q_ref.shape[2] // HEAD_DIM
    qi = pl.program_id(2)
    tri, causal = _strict_lower(tq, BF16)

    def tile(j, carry, diag):
        start = pl.multiple_of(j * tq, tq)
        heads = [slice(h * HEAD_DIM, (h + 1) * HEAD_DIM) for h in range(nh)]
        zs = [lax.dot_general(q_ref[0, :, cols], k_ref[0, pl.ds(start, tq), cols].astype(BF16), NT_DIMS,
                              preferred_element_type=F32) for cols in heads]
        lks, lss = [], []
        for z in zs:
            lk, ls = _log_keep_and_logsig(z)
            lks.append(jnp.where(causal, lk, 0.0) if diag else lk)
            lss.append(ls)
        rcs = [_later_key_sums(lk, tri) for lk in lks]
        ws = []
        for h in range(nh):
            w = jnp.exp2(lss[h] + rcs[h] + carry[h][1])
            ws.append((jnp.where(causal, w, 0.0) if diag else w).astype(BF16))
        return tuple((carry[h][0] + _dot(ws[h], v_ref[0, pl.ds(start, tq), heads[h]].astype(BF16)),
                      carry[h][1] + (rcs[h][:, 0:1] + lks[h][:, 0:1])) for h in range(nh))

    def any_weight_left(carry):
        c_max = carry[0][1]
        for h in range(1, nh):
            c_max = jnp.maximum(c_max, carry[h][1])
        return jnp.max(c_max) > F32_ZERO_WEIGHT_LOG2

    init = tuple((jnp.zeros((tq, HEAD_DIM), F32), jnp.zeros((tq, 1), F32)) for _ in range(nh))
    carry = tile(qi, init, True)

    def more(state):
        return (state[0] < qi) & state[1]

    def farther_tile(state):
        carry = tile(qi - 1 - state[0], state[2], False)
        return state[0] + 1, any_weight_left(carry), carry

    carry = lax.while_loop(more, farther_tile, (jnp.int32(0), any_weight_left(carry), carry))[2]
    acc = jnp.concatenate([carry[h][0] for h in range(nh)], axis=1)
    o_ref[0] = (yl_ref[0].astype(F32) + sz_ref[0].astype(F32) * acc).astype(BF16)


def _attn_prompt(q, k, v, sz, yl, tq):
    b, t, a = q.shape
    wblk = ATTN_HEADS_PER_STEP * HEAD_DIM
    qspec = pl.BlockSpec((1, tq, wblk), lambda bi, hi, qi: (bi, qi, hi))
    kvspec = pl.BlockSpec((1, t, wblk), lambda bi, hi, qi: (bi, 0, hi))
    return pl.pallas_call(
        _attn_kernel,
        grid=(b, a // wblk, t // tq),
        in_specs=[qspec, kvspec, kvspec, qspec, qspec],
        out_specs=qspec,
        out_shape=jax.ShapeDtypeStruct((b, t, a), BF16),
        compiler_params=_cparams(("arbitrary", "arbitrary", "arbitrary")),
        name="attn_prompt",
    )(q, k, v, sz, yl)


def _attn_sample_kernel(q_ref, kn_ref, vn_ref, kc_ref, vc_ref, sz_ref, yl_ref, o_ref, lk_s, ls_s):
    ts = q_ref.shape[1]
    nh = q_ref.shape[2] // HEAD_DIM
    past = kc_ref.shape[1]
    tri_n, causal = _strict_lower(ts, BF16)
    tri_p, _ = _strict_lower(past, BF16)
    accs, carries = [], []
    for h in range(nh):
        cols = slice(h * HEAD_DIM, (h + 1) * HEAD_DIM)
        q = q_ref[0, :, cols]
        zn = lax.dot_general(q, kn_ref[0, :, cols].astype(BF16), NT_DIMS, preferred_element_type=F32)
        lk, ls = _log_keep_and_logsig(zn)
        lk = jnp.where(causal, lk, 0.0)
        rc = _later_key_sums(lk, tri_n)
        w = jnp.where(causal, jnp.exp2(ls + rc), 0.0)
        accs.append(_dot(w.astype(BF16), vn_ref[0, :, cols].astype(BF16)))
        carries.append(rc[:, 0:1] + lk[:, 0:1])
        zp = lax.dot_general(q, kc_ref[0, :, h, :].astype(BF16), NT_DIMS, preferred_element_type=F32)
        lkp, lsp = _log_keep_and_logsig(zp)
        lk_s[h * ts:(h + 1) * ts, :] = lkp
        ls_s[h * ts:(h + 1) * ts, :] = lsp
    rcp = _later_key_sums(lk_s[...], tri_p)
    for h in range(nh):
        cols = slice(h * HEAD_DIM, (h + 1) * HEAD_DIM)
        rows = slice(h * ts, (h + 1) * ts)
        w = jnp.exp2(ls_s[rows, :] + rcp[rows, :] + carries[h])
        acc = accs[h] + _dot(w.astype(BF16), vc_ref[0, :, h, :].astype(BF16))
        o_ref[0, :, cols] = (yl_ref[0, :, cols].astype(F32) + sz_ref[0, :, cols].astype(F32) * acc).astype(BF16)


def _attn_sample(q, kn, vn, kc, vc, sz, yl):
    b, ts, a = q.shape
    past = kc.shape[1]
    nh = a // HEAD_DIM
    cur = pl.BlockSpec((1, ts, a), lambda bi: (bi, 0, 0))
    cache = pl.BlockSpec((1, past, nh, HEAD_DIM), lambda bi: (bi, 0, 0, 0))
    return pl.pallas_call(
        _attn_sample_kernel,
        grid=(b,),
        in_specs=[cur, cur, cur, cache, cache, cur, cur],
        out_specs=cur,
        out_shape=jax.ShapeDtypeStruct((b, ts, a), BF16),
        scratch_shapes=[pltpu.VMEM((nh * ts, past), F32), pltpu.VMEM((nh * ts, past), F32)],
        compiler_params=_cparams(("arbitrary",)),
        name="attn_sample",
    )(q, kn, vn, kc, vc, sz, yl)


def _outproj_kernel(a_ref, w_ref, x_ref, mod_ref, g_ref, *rest):
    x1_ref, hpk_ref, hlo_ref, w_s, o_s = rest[-5:]
    nb = mod_ref.shape[0]
    tm, d = x_ref.shape
    i = pl.program_id(0)

    @pl.when(i == 0)
    def _():
        w_s[...] = w_ref[...].astype(BF16)
        o_s[...] = jnp.zeros_like(o_s)

    def step(product_slot, finish_slot):
        m = mod_ref[...]
        o = o_s[finish_slot].reshape(nb, tm // nb, d)
        x1 = x_ref[...].reshape(nb, tm // nb, d) + m[:, 2:3, :] * o
        h = _rms(x1) * g_ref[...].reshape(1, 1, d) * (1.0 + m[:, 4:5, :]) + m[:, 3:4, :]
        x1_ref[...] = x1.reshape(tm, d)
        h = h.reshape(tm, d)
        hi = h.astype(BF16).astype(F32)
        hpk_ref[...] = _pack_pairs(hi)
        hlo_ref[...] = (h - hi).astype(BF16)
        o_s[product_slot] = _dot(a_ref[...], w_s[...])

    pl.when(i % 2 == 0)(lambda: step(0, 1))
    pl.when(i % 2 == 1)(lambda: step(1, 0))


def _outproj(a, w, x, mod, g, tm, rows_per_batch, n_total, row_offset, earlier=None):
    m, d = x.shape
    nb = max(tm // rows_per_batch, 1)
    per = max(rows_per_batch // tm, 1)
    off = row_offset // tm
    n_blocks = m // tm
    done = lambda i: jnp.maximum(i - 1, 0)
    out_row = lambda width: pl.BlockSpec((tm, width), lambda i: (done(i) + off, 0))
    in_specs = [pl.BlockSpec((tm, d), lambda i: (jnp.minimum(i, n_blocks - 1), 0)),
                _resident((d, d), lambda i: (0, 0)),
                pl.BlockSpec((tm, d), lambda i: (done(i), 0)),
                pl.BlockSpec((nb, N_MOD, d), lambda i: (done(i) // per, 0, 0)),
                pl.BlockSpec((1, d), lambda i: (0, 0))]
    args = [a, w, x, mod, g.reshape(1, d)]
    aliases = {}
    if earlier is not None:
        aliases = {len(args) + j: j for j in range(len(earlier))}
        in_specs += [pl.BlockSpec(memory_space=pl.ANY)] * len(earlier)
        args += list(earlier)
    return pl.pallas_call(
        _outproj_kernel,
        grid=(n_blocks + 1,),
        in_specs=in_specs,
        out_specs=[out_row(d), out_row(d // 2), out_row(d)],
        out_shape=[jax.ShapeDtypeStruct((n_total, d), F32), jax.ShapeDtypeStruct((n_total, d // 2), U32),
                   jax.ShapeDtypeStruct((n_total, d), BF16)],
        scratch_shapes=[pltpu.VMEM((d, d), BF16), pltpu.VMEM((2, tm, d), F32)],
        input_output_aliases=aliases,
        compiler_params=_cparams(("arbitrary",)),
        name="outproj",
    )(*args)


def _sub_max(x):
    return jnp.max(x, axis=0, keepdims=True)


def _sub_min(x):
    return jnp.min(x, axis=0, keepdims=True)


def _sub_sum(x):
    return jnp.sum(x, axis=0, keepdims=True)


def _router_kernel(hpk_ref, hlo_ref, wrh_ref, wrl_ref, br_ref, e_ref, w_ref, slot_ref, cnt_ref, base_s):
    tm = hpk_ref.shape[0]

    @pl.when(pl.program_id(0) == 0)
    def _():
        base_s[...] = jnp.zeros_like(base_s)

    hi = _unpack_pairs(hpk_ref[...], BF16)
    lo = hlo_ref[...]
    wrh = wrh_ref[...]
    logits = lax.dot_general(wrh, hi, NT_DIMS, preferred_element_type=F32)
    logits = logits + (lax.dot_general(wrh, lo, NT_DIMS, preferred_element_type=F32)
                       + lax.dot_general(wrl_ref[...], hi, NT_DIMS, preferred_element_type=F32))
    score = jax.nn.sigmoid(logits)
    sel = (score + br_ref[...]).reshape(GROUP_SIZE, N_GROUPS, tm)
    score = score.reshape(GROUP_SIZE, N_GROUPS, tm)

    m1 = sel[0]
    m2 = jnp.full_like(m1, NEG_INF)
    for e in range(1, GROUP_SIZE):
        x = sel[e]
        m2 = jnp.maximum(m2, jnp.minimum(m1, x))
        m1 = jnp.maximum(m1, x)
    cur = m1 + m2
    gidx = lax.broadcasted_iota(I32, (N_GROUPS, tm), 0)
    gmask = jnp.zeros((N_GROUPS, tm), jnp.bool_)
    for _ in range(TOPK_GROUPS):
        first = _sub_min(jnp.where(cur == _sub_max(cur), gidx, N_GROUPS))
        pick = gidx == first
        gmask = gmask | pick
        cur = jnp.where(pick, NEG_INF, cur)

    eid = (lax.broadcasted_iota(I32, (GROUP_SIZE, N_GROUPS, tm), 1) * GROUP_SIZE
           + lax.broadcasted_iota(I32, (GROUP_SIZE, N_GROUPS, tm), 0))
    cur = jnp.where(gmask[None], sel, NEG_INF)
    ids, wts, picks = [], [], []
    for _ in range(TOP_K):
        mx = _sub_max(jnp.max(cur, axis=0))
        first = _sub_min(jnp.min(jnp.where(cur == mx[None], eid, N_EXPERTS), axis=0))
        pick = eid == first[None]
        ids.append(first)
        picks.append(pick)
        wts.append(_sub_sum(jnp.sum(jnp.where(pick, score, 0.0), axis=0)))
        cur = jnp.where(pick, NEG_INF, cur)
    total = wts[0]
    chosen = picks[0]
    for k in range(1, TOP_K):
        total = total + wts[k]
        chosen = chosen | picks[k]

    chosen = jnp.where(chosen, 1.0, 0.0).reshape(N_EXPERTS, tm)
    earlier_or_self = (lax.broadcasted_iota(I32, (tm, tm), 0) <= lax.broadcasted_iota(I32, (tm, tm), 1)).astype(BF16)
    incl = _dot(chosen.astype(BF16), earlier_or_self)
    rank = (base_s[...] + (incl - chosen)).reshape(GROUP_SIZE, N_GROUPS, tm)
    base_s[...] = base_s[...] + incl[:, tm - 1:tm]
    slots = [_sub_sum(jnp.sum(jnp.where(pick, rank, 0.0), axis=0)).astype(I32) for pick in picks]

    pad = 8 - TOP_K
    e_ref[...] = jnp.concatenate(ids + [jnp.zeros((pad, tm), I32)], axis=0)
    w_ref[...] = jnp.concatenate([w / total * ROUTED_SCALE for w in wts] + [jnp.zeros((pad, tm), F32)], axis=0)
    slot_ref[...] = jnp.concatenate(slots + [jnp.zeros((pad, tm), I32)], axis=0)
    cnt_ref[...] = jnp.broadcast_to(base_s[...], cnt_ref.shape)


def _router(hpk, hlo, w_router, b_router, tm):
    n, d = hlo.shape
    wr = w_router.T.reshape(N_GROUPS, GROUP_SIZE, d).transpose(1, 0, 2).reshape(N_EXPERTS, d)
    br = b_router.reshape(N_GROUPS, GROUP_SIZE).T.reshape(N_EXPERTS, 1)
    wrh = wr.astype(BF16)
    wrl = (wr - wrh.astype(F32)).astype(BF16)
    full = lambda shape: pl.BlockSpec(shape, lambda i: (0, 0))
    per_token = pl.BlockSpec((8, tm), lambda i: (0, i))
    e_k, w_k, slot_k, counts = pl.pallas_call(
        _router_kernel,
        grid=(n // tm,),
        in_specs=[pl.BlockSpec((tm, d // 2), lambda i: (i, 0)), pl.BlockSpec((tm, d), lambda i: (i, 0)),
                  full((N_EXPERTS, d)), full((N_EXPERTS, d)), full((N_EXPERTS, 1))],
        out_specs=[per_token, per_token, per_token, full((N_EXPERTS, 128))],
        out_shape=[jax.ShapeDtypeStruct((8, n), I32), jax.ShapeDtypeStruct((8, n), F32),
                   jax.ShapeDtypeStruct((8, n), I32), jax.ShapeDtypeStruct((N_EXPERTS, 128), F32)],
        scratch_shapes=[pltpu.VMEM((N_EXPERTS, 1), F32)],
        compiler_params=_cparams(("arbitrary",)),
        name="router",
    )(hpk, hlo, wrh, wrl, br)
    counts = counts[:, 0].reshape(GROUP_SIZE, N_GROUPS).T.reshape(N_EXPERTS).astype(I32)
    return e_k, w_k, slot_k, counts


def _row_copy(src, src_row, dst, dst_row, sem):
    return pltpu.make_async_copy(src.at[pl.ds(src_row, 1)], dst.at[pl.ds(dst_row, 1)], sem)


def _for_token_groups(n_tokens, per_token):
    def group(g, carry):
        t0 = pl.multiple_of(g * SUBLANES, SUBLANES)
        for u in range(SUBLANES):
            per_token(t0 + u)
        return carry

    lax.fori_loop(0, n_tokens // SUBLANES, group, 0)


def _dispatch_kernel(pad_end_ref, dest_ref, h_ref, xs_ref, zero_s, sem, zero_sem):
    tb = dest_ref.shape[1]
    te = zero_s.shape[0]

    def tail_fill(e):
        start = pl.multiple_of(pad_end_ref[e] - te, te)
        return pltpu.make_async_copy(zero_s, xs_ref.at[pl.ds(start, te)], zero_sem)

    def has_rows(e):
        return pad_end_ref[e] > (pad_end_ref[e - 1] if e else 0)

    @pl.when(pl.program_id(0) == 0)
    def _():
        zero_s[...] = jnp.zeros_like(zero_s)
        for e in range(N_EXPERTS):
            pl.when(has_rows(e))(lambda e=e: tail_fill(e).start())
        for e in range(N_EXPERTS):
            pl.when(has_rows(e))(lambda e=e: tail_fill(e).wait())

    def start(t):
        for k in range(TOP_K):
            _row_copy(h_ref, t, xs_ref, dest_ref[k, t], sem).start()

    def wait(t):
        for k in range(TOP_K):
            _row_copy(h_ref, 0, xs_ref, 0, sem).wait()

    _for_token_groups(tb, start)
    _for_token_groups(tb, wait)


def _dispatch(pad_end, dest, hpk, n_rows, tb, te):
    n, dh = hpk.shape
    grid_spec = pltpu.PrefetchScalarGridSpec(
        num_scalar_prefetch=1,
        grid=(n // tb,),
        in_specs=[pl.BlockSpec((8, tb), lambda i, pe: (0, i), memory_space=pltpu.SMEM),
                  pl.BlockSpec((tb, dh), lambda i, pe: (i, 0))],
        out_specs=pl.BlockSpec(memory_space=pl.ANY),
        scratch_shapes=[pltpu.VMEM((te, dh), U32), pltpu.SemaphoreType.DMA(()), pltpu.SemaphoreType.DMA(())],
    )
    return pl.pallas_call(
        _dispatch_kernel,
        grid_spec=grid_spec,
        out_shape=jax.ShapeDtypeStruct((n_rows, dh), U32),
        compiler_params=_cparams(("arbitrary",)),
        name="dispatch",
    )(pad_end, dest, hpk)


def _experts_kernel(be_ref, nu_ref, x_ref, wg_ref, wu_ref, wd_ref, o_ref, wg_s, wu_s, wd_s):
    i = pl.program_id(0)
    prev = be_ref[jnp.maximum(i - 1, 0)]

    @pl.when((i == 0) | (be_ref[i] != prev))
    def _():
        wg_s[...] = wg_ref[0].astype(BF16)
        wu_s[...] = wu_ref[0].astype(BF16)
        wd_s[...] = wd_ref[0].astype(BF16)

    @pl.when(i < nu_ref[0])
    def _():
        x = _unpack_pairs(x_ref[...], BF16)
        g = _dot(x, wg_s[...])
        hidden = (g * jax.nn.sigmoid(g)) * _dot(x, wu_s[...])
        o_ref[...] = _pack_pairs(_dot(hidden.astype(BF16), wd_s[...]))

    @pl.when(i >= nu_ref[0])
    def _():
        o_ref[...] = jnp.zeros_like(o_ref)


def _experts(block_expert, n_used, xs, w_gate, w_up, w_down, tm):
    p, dh = xs.shape
    d, ff = w_gate.shape[1:]
    rows = pl.BlockSpec((tm, dh), lambda i, be, nu: (i, 0))
    grid_spec = pltpu.PrefetchScalarGridSpec(
        num_scalar_prefetch=2,
        grid=(p // tm,),
        in_specs=[rows,
                  pl.BlockSpec((1, d, ff), lambda i, be, nu: (be[i], 0, 0)),
                  pl.BlockSpec((1, d, ff), lambda i, be, nu: (be[i], 0, 0)),
                  pl.BlockSpec((1, ff, d), lambda i, be, nu: (be[i], 0, 0))],
        out_specs=rows,
        scratch_shapes=[pltpu.VMEM((d, ff), BF16), pltpu.VMEM((d, ff), BF16), pltpu.VMEM((ff, d), BF16)],
    )
    return pl.pallas_call(
        _experts_kernel,
        grid_spec=grid_spec,
        out_shape=jax.ShapeDtypeStruct((p, dh), U32),
        compiler_params=_cparams(("arbitrary",)),
        name="experts",
    )(block_expert, n_used, xs, w_gate, w_up, w_down)


def _combine_kernel(dest_ref, dest_next_ref, wk_ref, ys_ref, x1_ref, hpk_ref, gt_ref, wsg_ref, wsu_ref, wsd_ref,
                    gf_ref, yp_ref, ysm_ref, buf, ffn_s, y_s, sem, *, n_prompt_blocks, n_blocks):
    tb, d = x1_ref.shape
    rows_per_seg = tb // gt_ref.shape[0]
    i = pl.program_id(0)
    cur = i % 2
    nxt = 1 - cur

    def fetch(dest, slot, t):
        for k in range(TOP_K):
            _row_copy(ys_ref, dest[k, t], buf.at[slot, k], t, sem.at[slot]).start()

    def wait_rows(slot):
        def one(t):
            for k in range(TOP_K):
                _row_copy(ys_ref, 0, buf.at[slot, k], 0, sem.at[slot]).wait()
        _for_token_groups(tb, one)

    @pl.when(i == 0)
    def _():
        _for_token_groups(tb, lambda t: fetch(dest_ref, 0, t))

    x = _unpack_pairs(hpk_ref[...], BF16)
    g = _dot(x, wsg_ref[...])
    ffn_s[...] = _dot(((g * jax.nn.sigmoid(g)) * _dot(x, wsu_ref[...])).astype(BF16), wsd_ref[...])
    wait_rows(cur)

    def group(gi, carry):
        t0 = pl.multiple_of(gi * SUBLANES, SUBLANES)
        rows = pl.ds(t0, SUBLANES)
        ffn = ffn_s[rows, :]
        wk = wk_ref[rows, :]
        for k in range(TOP_K):
            ffn = ffn + wk[:, k:k + 1] * _unpack_pairs(buf[cur, k, rows, :], F32)
        x2 = x1_ref[rows, :] + gt_ref[t0 // rows_per_seg] * ffn
        y = _rms(x2) * gf_ref[...]
        for u in range(SUBLANES):
            fetch(dest_next_ref, nxt, t0 + u)
        y_s[rows, :] = y
        return carry

    lax.fori_loop(0, tb // SUBLANES, group, 0)

    @pl.when(i == n_blocks - 1)
    def _():
        wait_rows(nxt)

    @pl.when(i < n_prompt_blocks)
    def _():
        yp_ref[...] = y_s[...]

    @pl.when(i >= n_prompt_blocks)
    def _():
        ysm_ref[...] = y_s[...]


def _combine(dest, wk_t, ys, x1, hpk, gt2_seg, ws_gate, ws_up, ws_down, g_final, n_prompt, tb):
    n, d = x1.shape
    dh = d // 2
    ff = ws_gate.shape[1]
    npb = n_prompt // tb
    nsb = (n - n_prompt) // tb
    seg = gt2_seg.shape[0] * tb // n
    nb = n // tb
    row = lambda width: pl.BlockSpec((tb, width), lambda i: (i, 0))
    full = lambda shape: pl.BlockSpec(shape, lambda i: (0,) * len(shape))
    return pl.pallas_call(
        functools.partial(_combine_kernel, n_prompt_blocks=npb, n_blocks=nb),
        grid=(nb,),
        in_specs=[pl.BlockSpec((8, tb), lambda i: (0, i), memory_space=pltpu.SMEM),
                  pl.BlockSpec((8, tb), lambda i: (0, jnp.minimum(i + 1, nb - 1)), memory_space=pltpu.SMEM),
                  row(8), pl.BlockSpec(memory_space=pl.ANY), row(d), row(dh),
                  pl.BlockSpec((seg, 1, d), lambda i: (i, 0, 0)),
                  full((d, ff)), full((d, ff)), full((ff, d)), full((1, d))],
        out_specs=[pl.BlockSpec((tb, d), lambda i: (jnp.minimum(i, npb - 1), 0)),
                   pl.BlockSpec((tb, d), lambda i: (jnp.clip(i - npb, 0, nsb - 1), 0))],
        out_shape=[jax.ShapeDtypeStruct((n_prompt, d), F32), jax.ShapeDtypeStruct((n - n_prompt, d), F32)],
        scratch_shapes=[pltpu.VMEM((2, TOP_K, tb, dh), U32), pltpu.VMEM((tb, d), F32), pltpu.VMEM((tb, d), F32),
                        pltpu.SemaphoreType.DMA((2,))],
        compiler_params=_cparams(("arbitrary",)),
        name="combine",
    )(dest, dest, wk_t, ys, x1, hpk, gt2_seg, ws_gate, ws_up, ws_down, g_final.reshape(1, d))


def _largest_tile(n, target):
    t = min(n, target)
    while n % t:
        t -= 1
    return t


def _mixer(x, mod, past_k, past_v, h0, conv0, p, first_pos_zero, n_total, row_offset, earlier):
    b, t, d = x.shape
    m = b * t
    a = d
    hn_bm, hn_tm = _norm_mod(x, mod, p["g_mix"])
    hn_bm = hn_bm.reshape(m, d)
    tm = _largest_tile(m, 512)
    tmh = _largest_tile(m, 1024)
    w_in = p["w_in"]
    u = _proj(hn_tm, w_in, 0, d, "plain", BF16, tmh).reshape(t, b, d)
    gl = _proj(hn_tm, w_in, 1, d, "gelu", BF16, tmh).reshape(t, b, d)
    sl = _proj(hn_tm, w_in, 5, d, "sigmoid", BF16, tmh).reshape(t, b, d)
    q = _proj(hn_bm, w_in, 2, a, "qscale", BF16, tmh).reshape(b, t, a)
    k = _proj(hn_bm, w_in, 3, a, "plain", F32, tm).reshape(b, t, a)
    v = _proj(hn_bm, w_in, 4, a, "plain", F32, tm).reshape(b, t, a)
    sz = _proj(hn_bm, w_in, 6, a, "sigmoid", BF16, tmh).reshape(b, t, a)

    if h0 is None:
        h0 = jnp.zeros((b, d), F32)
        prev = jnp.zeros((CONV_WIDTH - 1, b, d), F32)
    else:
        prev = conv0.transpose(1, 0, 2)
    tc = _largest_tile(t, 128)
    yl, h_last, conv_tail = _lru(u, gl, sl, p["conv_w"], p["conv_b"], p["w_rg_a"], p["b_rg_a"], p["w_rg_x"],
                                 p["b_rg_x"], p["lru_lambda"], h0, prev, tc, first_pos_zero)
    if past_k is None:
        mixed = _attn_prompt(q, k, v, sz, yl, _largest_tile(t, 256))
    else:
        mixed = _attn_sample(q, k, v, past_k, past_v, sz, yl)
    tmo = _largest_tile(m, 256)
    shared = _outproj(mixed.reshape(m, a), p["w_out"], x.reshape(m, d), mod, p["g_ffn"], tmo, t, n_total, row_offset,
                      earlier)
    nh = a // HEAD_DIM
    return (shared, k.reshape(b, t, nh, HEAD_DIM), v.reshape(b, t, nh, HEAD_DIM), h_last,
            conv_tail.transpose(1, 0, 2))


def _routing_tables(e_k, slot_k, counts, tm):
    n_tokens = e_k.shape[1]
    padded = (counts + tm - 1) // tm * tm
    pad_end = jnp.cumsum(padded).astype(I32)
    pad_start = pad_end - padded
    is_expert = e_k[:, :, None] == jnp.arange(N_EXPERTS, dtype=I32)
    dest = slot_k + jnp.sum(jnp.where(is_expert, pad_start, 0), axis=-1).astype(I32)
    n_blocks = -(-(TOP_K * n_tokens + N_EXPERTS * (tm - 1)) // tm)
    block_start = jnp.arange(n_blocks, dtype=I32) * tm
    block_expert = jnp.minimum(jnp.sum(pad_end[None, :] <= block_start[:, None], axis=1), N_EXPERTS - 1).astype(I32)
    n_used = (pad_end[-1:] // tm).astype(I32)
    return dest, pad_end, block_expert, n_used, n_blocks * tm


def kernel(x_prompt, x_sample, cache_k, cache_v, state_lru, state_conv, c_prompt, c_sample, g_mix, w_ada, b_ada, w_in, conv_w, conv_b, w_rg_a, b_rg_a, w_rg_x, b_rg_x, lru_lambda, w_out, g_ffn, w_router, b_router, w_gate, w_up, w_down, ws_gate, ws_up, ws_down, g_final):
    assert g_mix.shape[0] == 1, "single-layer trunk only"
    b, t, d = x_prompt.shape
    bs, ts, _ = x_sample.shape
    assert ts >= CONV_WIDTH - 1 and ts == TIME_GROUP and t % TIME_GROUP == 0
    p = dict(g_mix=g_mix[0], w_in=w_in[0], conv_w=conv_w[0], conv_b=conv_b[0],
             w_rg_a=w_rg_a[0].astype(BF16), b_rg_a=b_rg_a[0], w_rg_x=w_rg_x[0].astype(BF16), b_rg_x=b_rg_x[0],
             lru_lambda=lru_lambda[0], w_out=w_out[0], g_ffn=g_ffn[0])
    mod = _ada(jnp.concatenate([c_prompt, c_sample], axis=0), w_ada[0], b_ada[0]).reshape(b + bs, N_MOD, d)
    mod_p, mod_s = mod[:b], mod[b:]

    n_prompt = b * t
    n = n_prompt + bs * ts
    tb = _largest_tile(bs * ts, 256)
    assert n_prompt % tb == 0 and tb % ts == 0

    shared, kp, vp, hp, cp = _mixer(x_prompt, mod_p, None, None, None, None, p, True, n, 0, None)
    (x1, hpk, hlo), ks, vs, hs, cs = _mixer(x_sample, mod_s, cache_k[0], cache_v[0], state_lru[0], state_conv[0],
                                            p, False, n, n_prompt, shared)

    e_k, w_k, slot_k, counts = _router(hpk, hlo, w_router[0], b_router[0], tb)
    tme = EXPERT_ROW_BLOCK
    dest, pad_end, block_expert, n_used, n_rows = _routing_tables(e_k, slot_k, counts, tme)
    xs = _dispatch(pad_end, dest, hpk, n_rows, tb, tme)
    ys = _experts(block_expert, n_used, xs, w_gate[0], w_up[0], w_down[0], tme)
    gt2 = jnp.concatenate([jnp.broadcast_to(mod_p[:, None, 5:6, :], (b, t // ts, 1, d)).reshape(-1, 1, d),
                           mod_s[:, 5:6, :]], axis=0)
    y_p, y_s = _combine(dest, w_k.T, ys, x1, hpk, gt2, ws_gate[0].astype(BF16), ws_up[0].astype(BF16),
                        ws_down[0].astype(BF16), g_final, n_prompt, tb)
    return (y_p.reshape(b, t, d), y_s.reshape(bs, ts, d), kp[None], vp[None], hp[None], cp[None],
            ks[None], vs[None], hs[None], cs[None])
```

```python
import functools

import jax
import jax.numpy as jnp
from jax import lax
from jax.experimental import pallas as pl
from jax.experimental.pallas import tpu as pltpu

F32 = jnp.float32
BF16 = jnp.bfloat16
I32 = jnp.int32
U32 = jnp.uint32

LANES = 128
SUBLANES = 8
HEAD_DIM = 128
LRU_BLOCK_DIM = 256
LRU_C = 8.0
CONV_WIDTH = 4
N_EXPERTS = 64
TOP_K = 6
N_GROUPS = 8
GROUP_SIZE = N_EXPERTS // N_GROUPS
TOPK_GROUPS = 4
ROUTED_SCALE = 2.5
NORM_EPS = 1e-6
N_MOD = 6
TIME_GROUP = 16
LOG2_E = 1.4426950408889634
ATTN_HEADS_PER_STEP = 4
F32_ZERO_WEIGHT_LOG2 = -150.0
DEST_STRIDE = 8
EXPERT_ROW_BLOCK = 512
V7X_VMEM_LIMIT_BYTES = 56 * 1024 * 1024
NT_DIMS = (((1,), (1,)), ((), ()))
NEG_INF = float("-inf")


def _cparams(semantics):
    return pltpu.CompilerParams(dimension_semantics=semantics, vmem_limit_bytes=V7X_VMEM_LIMIT_BYTES)


def _dot(a, b):
    return jnp.dot(a, b, preferred_element_type=F32)


def _split_bf16(x):
    hi = x.astype(BF16)
    lo = (x - hi.astype(F32)).astype(BF16)
    return hi, lo


def _dot3(a, b):
    ah, al = _split_bf16(a)
    bh, bl = _split_bf16(b)
    return _dot(ah, bh) + (_dot(al, bh) + _dot(ah, bl))


def _pack_pairs(h):
    n = h.shape[1] // 2
    bits = lax.bitcast_convert_type(h.astype(BF16).astype(F32), U32)
    return bits[:, :n] | (bits[:, n:] >> 16)


def _store_token_tiles(ref, packed):
    m, n = packed.shape
    nsub = n // LANES
    for j in range(nsub):
        ref[pl.ds(j, m, stride=nsub), :] = packed[:, j * LANES:(j + 1) * LANES]


def _tile_rows(d):
    return d // 2 // LANES


def _load_token_tiles(ref, nsub):
    m = ref.shape[0] // nsub
    return jnp.concatenate([ref[pl.ds(j, m, stride=nsub), :] for j in range(nsub)], axis=1)


def _unpack_pairs(p, dtype):
    a = lax.bitcast_convert_type(p & jnp.uint32(0xFFFF0000), F32)
    b = lax.bitcast_convert_type(p << 16, F32)
    return jnp.concatenate([a.astype(dtype), b.astype(dtype)], axis=1)


def _rms(x):
    return x * lax.rsqrt(jnp.mean(x * x, axis=-1, keepdims=True) + NORM_EPS)


def _ada_kernel(c_ref, w_ref, b_ref, o_ref):
    c = c_ref[...]
    o_ref[...] = _dot3(c * jax.nn.sigmoid(c), w_ref[...]) + b_ref[...]


def _ada(c, w, b):
    m, d = c.shape
    n = w.shape[1]
    tn = min(n, 1024)
    return pl.pallas_call(
        _ada_kernel,
        grid=(n // tn,),
        in_specs=[pl.BlockSpec((m, d), lambda j: (0, 0)),
                  pl.BlockSpec((d, tn), lambda j: (0, j)),
                  pl.BlockSpec((1, tn), lambda j: (0, j))],
        out_specs=pl.BlockSpec((m, tn), lambda j: (0, j)),
        out_shape=jax.ShapeDtypeStruct((m, n), F32),
        compiler_params=_cparams(("arbitrary",)),
        name="ada",
    )(c, w, b.reshape(1, n))


def _row_permutation(n_outer, n_inner):
    n = n_outer * n_inner
    out_row = lax.broadcasted_iota(I32, (n, n), 0)
    in_row = lax.broadcasted_iota(I32, (n, n), 1)
    same = (out_row // n_outer == in_row % n_inner) & (out_row % n_outer == in_row // n_inner)
    return same.astype(BF16)


def _norm_mod_kernel(x_ref, mod_ref, g_ref, obm_ref, otm_ref):
    b, tt, d = x_ref.shape
    m = mod_ref[...]
    h = (_rms(x_ref[...]) * g_ref[...].reshape(1, 1, d) * (1.0 + m[:, 1:2, :]) + m[:, 0:1, :]).astype(BF16)
    obm_ref[...] = h
    otm_ref[...] = _dot(_row_permutation(b, tt), h.reshape(b * tt, d)).astype(BF16)


def _norm_mod(x, mod, g):
    b, t, d = x.shape
    tt = TIME_GROUP
    return pl.pallas_call(
        _norm_mod_kernel,
        grid=(t // tt,),
        in_specs=[pl.BlockSpec((b, tt, d), lambda ti: (0, ti, 0)),
                  pl.BlockSpec((b, N_MOD, d), lambda ti: (0, 0, 0)),
                  pl.BlockSpec((1, d), lambda ti: (0, 0))],
        out_specs=[pl.BlockSpec((b, tt, d), lambda ti: (0, ti, 0)),
                   pl.BlockSpec((tt * b, d), lambda ti: (ti, 0))],
        out_shape=[jax.ShapeDtypeStruct((b, t, d), BF16), jax.ShapeDtypeStruct((t * b, d), BF16)],
        compiler_params=_cparams(("arbitrary",)),
        name="norm_mod",
    )(x, mod, g.reshape(1, d))


def _proj_kernel(a_ref, w_ref, o_ref, w_s, *, epilogue):
    @pl.when(pl.program_id(0) == 0)
    def _():
        w_s[...] = w_ref[...].astype(BF16)

    acc = _dot(a_ref[...], w_s[...])
    if epilogue == "gelu":
        acc = jax.nn.gelu(acc)
    elif epilogue == "sigmoid":
        acc = jax.nn.sigmoid(acc)
    elif epilogue == "qscale":
        acc = acc * (HEAD_DIM ** -0.5 * LOG2_E)
    o_ref[...] = acc.astype(o_ref.dtype)


def _resident(shape, index_map):
    return pl.BlockSpec(shape, index_map, pipeline_mode=pl.Buffered(1))


def _proj(a, w, col, width, epilogue, out_dtype, tm):
    m, k = a.shape
    return pl.pallas_call(
        functools.partial(_proj_kernel, epilogue=epilogue),
        grid=(m // tm,),
        in_specs=[pl.BlockSpec((tm, k), lambda i: (i, 0)),
                  _resident((k, width), lambda i: (0, col))],
        out_specs=pl.BlockSpec((tm, width), lambda i: (i, 0)),
        out_shape=jax.ShapeDtypeStruct((m, width), out_dtype),
        scratch_shapes=[pltpu.VMEM((k, width), BF16)],
        compiler_params=_cparams(("arbitrary",)),
        name="proj_" + epilogue,
    )(a, w)


def _lru_kernel(u_ref, gl_ref, sl_ref, cw_ref, cb_ref, wa_ref, ba_ref, wx_ref, bx_ref, lam_ref, h0_ref, prev_ref,
                y_ref, hl_ref, cn_ref, tail_s, h_s, a_s, b_s, y_s, *, first_pos_zero):
    tc, nb, wb = u_ref.shape
    ti = pl.program_id(1)

    @pl.when(ti == 0)
    def _():
        tail_s[...] = prev_ref[...]
        h_s[...] = h0_ref[...]

    u = u_ref[...].astype(F32)
    xp = jnp.concatenate([tail_s[...], u], axis=0)
    cw = cw_ref[...]
    xc = cb_ref[...].reshape(1, 1, wb)
    for j in range(CONV_WIDTH):
        xc = xc + xp[j:j + tc] * cw[j:j + 1].reshape(1, 1, wb)
    last_rows = u[tc - (CONV_WIDTH - 1):]
    tail_s[...] = last_rows
    cn_ref[...] = last_rows

    x2 = xc.reshape(tc * nb, wb)
    xb = x2.astype(BF16)
    r = jax.nn.sigmoid(_dot(xb, wa_ref[0]) + ba_ref[0])
    gi = jax.nn.sigmoid(_dot(xb, wx_ref[0]) + bx_ref[0])
    log_a = r * ((-LRU_C) * jax.nn.softplus(-lam_ref[...]))
    a = jnp.exp(log_a)
    m2 = -jnp.tanh(log_a) * (a * a + 1.0)
    mult = jnp.where(m2 > 0.0, m2 * lax.rsqrt(m2), 0.0)
    if first_pos_zero:
        row = lax.broadcasted_iota(I32, (tc * nb, wb), 0)
        mult = jnp.where((row < nb) & (ti == 0), 1.0, mult)
    a_s[...] = a.reshape(tc, nb, wb)
    b_s[...] = (mult * (gi * x2)).reshape(tc, nb, wb)

    def step(t, h):
        h = a_s[t] * h + b_s[t]
        gate = gl_ref[t].astype(F32) * sl_ref[t].astype(F32)
        y_s[t] = h * gate
        return h

    h = lax.fori_loop(0, tc, step, h_s[...], unroll=8)
    h_s[...] = h
    hl_ref[...] = h

    perm = _row_permutation(TIME_GROUP, nb)
    for g in range(tc // TIME_GROUP):
        rows = slice(g * TIME_GROUP, (g + 1) * TIME_GROUP)
        yg = y_s[rows].reshape(TIME_GROUP * nb, wb).astype(BF16)
        y_ref[:, rows, :] = _dot(perm, yg).astype(BF16).reshape(nb, TIME_GROUP, wb)


def _lru(u, gl, sl, conv_w, conv_b, wa, ba, wx, bx, lam, h0, prev, tc, first_pos_zero):
    t, b, w = u.shape
    wb = LRU_BLOCK_DIM
    nw = w // wb
    seq = pl.BlockSpec((tc, b, wb), lambda wi, ti: (ti, 0, wi))
    seq_out = pl.BlockSpec((b, tc, wb), lambda wi, ti: (0, ti, wi))
    chan = lambda rows: pl.BlockSpec((rows, wb), lambda wi, ti: (0, wi))
    gate_w = pl.BlockSpec((1, wb, wb), lambda wi, ti: (wi, 0, 0))
    gate_b = pl.BlockSpec((1, 1, wb), lambda wi, ti: (wi, 0, 0))
    tail = pl.BlockSpec((CONV_WIDTH - 1, b, wb), lambda wi, ti: (0, 0, wi))
    return pl.pallas_call(
        functools.partial(_lru_kernel, first_pos_zero=first_pos_zero),
        grid=(nw, t // tc),
        in_specs=[seq, seq, seq, chan(CONV_WIDTH), chan(1), gate_w, gate_b, gate_w, gate_b, chan(1),
                  chan(b), tail],
        out_specs=[seq_out, chan(b), tail],
        out_shape=[jax.ShapeDtypeStruct((b, t, w), BF16), jax.ShapeDtypeStruct((b, w), F32),
                   jax.ShapeDtypeStruct((CONV_WIDTH - 1, b, w), F32)],
        scratch_shapes=[pltpu.VMEM((CONV_WIDTH - 1, b, wb), F32), pltpu.VMEM((b, wb), F32),
                        pltpu.VMEM((tc, b, wb), F32), pltpu.VMEM((tc, b, wb), F32),
                        pltpu.VMEM((tc, b, wb), F32)],
        compiler_params=_cparams(("arbitrary", "arbitrary")),
        name="lru",
    )(u, gl, sl, conv_w, conv_b.reshape(1, w), wa, ba.reshape(nw, 1, wb), wx, bx.reshape(nw, 1, wb),
      lam.reshape(1, w), h0, prev)


def _log_keep_and_logsig(z):
    nz = -z
    lk = jnp.minimum(nz, 0.0) - jnp.log2(1.0 + jnp.exp2(jnp.minimum(z, nz)))
    return lk, z + lk


def _later_key_sums(lk, tri):
    hi, lo = _split_bf16(lk)
    return _dot(hi, tri) + _dot(lo, tri)


def _strict_lower(n, dtype):
    row = lax.broadcasted_iota(I32, (n, n), 0)
    col = lax.broadcasted_iota(I32, (n, n), 1)
    return (row > col).astype(dtype), col < row


def _attn_kernel(q_ref, k_ref, v_ref, sz_ref, yl_ref, o_ref):
    tq = q_ref.shape[1]
    nh = q_ref.shape[2] // HEAD_DIM
    qi = pl.program_id(2)
    tri, causal = _strict_lower(tq, BF16)

    def tile(j, carry, diag):
        start = pl.multiple_of(j * tq, tq)
        heads = [slice(h * HEAD_DIM, (h + 1) * HEAD_DIM) for h in range(nh)]
        zs = [lax.dot_general(q_ref[0, :, cols], k_ref[0, pl.ds(start, tq), cols].astype(BF16), NT_DIMS,
                              preferred_element_type=F32) for cols in heads]
        lks, lss = [], []
        for z in zs:
            lk, ls = _log_keep_and_logsig(z)
            lks.append(jnp.where(causal, lk, 0.0) if diag else lk)
            lss.append(ls)
        rcs = [_later_key_sums(lk, tri) for lk in lks]
        ws = []
        for h in range(nh):
            w = jnp.exp2(lss[h] + rcs[h] + carry[h][1])
            ws.append((jnp.where(causal, w, 0.0) if diag else w).astype(BF16))
        return tuple((carry[h][0] + _dot(ws[h], v_ref[0, pl.ds(start, tq), heads[h]].astype(BF16)),
                      carry[h][1] + (rcs[h][:, 0:1] + lks[h][:, 0:1])) for h in range(nh))

    def any_weight_left(carry):
        c_max = carry[0][1]
        for h in range(1, nh):
            c_max = jnp.maximum(c_max, carry[h][1])
        return jnp.max(c_max) > F32_ZERO_WEIGHT_LOG2

    init = tuple((jnp.zeros((tq, HEAD_DIM), F32), jnp.zeros((tq, 1), F32)) for _ in range(nh))
    carry = tile(qi, init, True)

    def more(state):
        return (state[0] < qi) & state[1]

    def farther_tile(state):
        carry = tile(qi - 1 - state[0], state[2], False)
        return state[0] + 1, any_weight_left(carry), carry

    carry = lax.while_loop(more, farther_tile, (jnp.int32(0), any_weight_left(carry), carry))[2]
    acc = jnp.concatenate([carry[h][0] for h in range(nh)], axis=1)
    o_ref[0] = (yl_ref[0].astype(F32) + sz_ref[0].astype(F32) * acc).astype(BF16)


def _attn_prompt(q, k, v, sz, yl, tq):
    b, t, a = q.shape
    wblk = ATTN_HEADS_PER_STEP * HEAD_DIM
    qspec = pl.BlockSpec((1, tq, wblk), lambda bi, hi, qi: (bi, qi, hi))
    kvspec = pl.BlockSpec((1, t, wblk), lambda bi, hi, qi: (bi, 0, hi))
    return pl.pallas_call(
        _attn_kernel,
        grid=(b, a // wblk, t // tq),
        in_specs=[qspec, kvspec, kvspec, qspec, qspec],
        out_specs=qspec,
        out_shape=jax.ShapeDtypeStruct((b, t, a), BF16),
        compiler_params=_cparams(("arbitrary", "arbitrary", "arbitrary")),
        name="attn_prompt",
    )(q, k, v, sz, yl)


def _attn_sample_kernel(q_ref, kn_ref, vn_ref, kc_ref, vc_ref, sz_ref, yl_ref, o_ref, lk_s, ls_s):
    ts = q_ref.shape[1]
    nh = q_ref.shape[2] // HEAD_DIM
    past = kc_ref.shape[1]
    tri_n, causal = _strict_lower(ts, BF16)
    tri_p, _ = _strict_lower(past, BF16)
    accs, carries = [], []
    for h in range(nh):
        cols = slice(h * HEAD_DIM, (h + 1) * HEAD_DIM)
        q = q_ref[0, :, cols]
        zn = lax.dot_general(q, kn_ref[0, :, cols].astype(BF16), NT_DIMS, preferred_element_type=F32)
        lk, ls = _log_keep_and_logsig(zn)
        lk = jnp.where(causal, lk, 0.0)
        rc = _later_key_sums(lk, tri_n)
        w = jnp.where(causal, jnp.exp2(ls + rc), 0.0)
        accs.append(_dot(w.astype(BF16), vn_ref[0, :, cols].astype(BF16)))
        carries.append(rc[:, 0:1] + lk[:, 0:1])
        zp = lax.dot_general(q, kc_ref[0, :, h, :].astype(BF16), NT_DIMS, preferred_element_type=F32)
        lkp, lsp = _log_keep_and_logsig(zp)
        lk_s[h * ts:(h + 1) * ts, :] = lkp
        ls_s[h * ts:(h + 1) * ts, :] = lsp
    rcp = _later_key_sums(lk_s[...], tri_p)
    for h in range(nh):
        cols = slice(h * HEAD_DIM, (h + 1) * HEAD_DIM)
        rows = slice(h * ts, (h + 1) * ts)
        w = jnp.exp2(ls_s[rows, :] + rcp[rows, :] + carries[h])
        acc = accs[h] + _dot(w.astype(BF16), vc_ref[0, :, h, :].astype(BF16))
        o_ref[0, :, cols] = (yl_ref[0, :, cols].astype(F32) + sz_ref[0, :, cols].astype(F32) * acc).astype(BF16)


def _attn_sample(q, kn, vn, kc, vc, sz, yl):
    b, ts, a = q.shape
    past = kc.shape[1]
    nh = a // HEAD_DIM
    cur = pl.BlockSpec((1, ts, a), lambda bi: (bi, 0, 0))
    cache = pl.BlockSpec((1, past, nh, HEAD_DIM), lambda bi: (bi, 0, 0, 0))
    return pl.pallas_call(
        _attn_sample_kernel,
        grid=(b,),
        in_specs=[cur, cur, cur, cache, cache, cur, cur],
        out_specs=cur,
        out_shape=jax.ShapeDtypeStruct((b, ts, a), BF16),
        scratch_shapes=[pltpu.VMEM((nh * ts, past), F32), pltpu.VMEM((nh * ts, past), F32)],
        compiler_params=_cparams(("arbitrary",)),
        name="attn_sample",
    )(q, kn, vn, kc, vc, sz, yl)


def _outproj_kernel(a_ref, w_ref, x_ref, mod_ref, g_ref, *rest):
    x1_ref, hpk_ref, hlo_ref, w_s = rest[-4:]
    nb = mod_ref.shape[0]
    tm, d = x_ref.shape

    @pl.when(pl.program_id(0) == 0)
    def _():
        w_s[...] = w_ref[...].astype(BF16)

    m = mod_ref[...]
    o = _dot(a_ref[...], w_s[...]).reshape(nb, tm // nb, d)
    x1 = x_ref[...].reshape(nb, tm // nb, d) + m[:, 2:3, :] * o
    h = _rms(x1) * g_ref[...].reshape(1, 1, d) * (1.0 + m[:, 4:5, :]) + m[:, 3:4, :]
    x1_ref[...] = x1.reshape(tm, d)
    h = h.reshape(tm, d)
    hi = h.astype(BF16).astype(F32)
    _store_token_tiles(hpk_ref, _pack_pairs(hi))
    hlo_ref[...] = (h - hi).astype(BF16)


def _outproj(a, w, x, mod, g, tm, rows_per_batch, n_total, row_offset, earlier=None):
    m, d = x.shape
    nb = max(tm // rows_per_batch, 1)
    per = max(rows_per_batch // tm, 1)
    off = row_offset // tm
    nsub = _tile_rows(d)
    row = lambda width: pl.BlockSpec((tm, width), lambda i: (i, 0))
    out_row = lambda width: pl.BlockSpec((tm, width), lambda i: (i + off, 0))
    in_specs = [row(d), _resident((d, d), lambda i: (0, 0)), row(d),
                pl.BlockSpec((nb, N_MOD, d), lambda i: (i // per, 0, 0)),
                pl.BlockSpec((1, d), lambda i: (0, 0))]
    args = [a, w, x, mod, g.reshape(1, d)]
    aliases = {}
    if earlier is not None:
        aliases = {len(args) + j: j for j in range(len(earlier))}
        in_specs += [pl.BlockSpec(memory_space=pl.ANY)] * len(earlier)
        args += list(earlier)
    return pl.pallas_call(
        _outproj_kernel,
        grid=(m // tm,),
        in_specs=in_specs,
        out_specs=[out_row(d), pl.BlockSpec((tm * nsub, LANES), lambda i: (i + off, 0)), out_row(d)],
        out_shape=[jax.ShapeDtypeStruct((n_total, d), F32), jax.ShapeDtypeStruct((n_total * nsub, LANES), U32),
                   jax.ShapeDtypeStruct((n_total, d), BF16)],
        scratch_shapes=[pltpu.VMEM((d, d), BF16)],
        input_output_aliases=aliases,
        compiler_params=_cparams(("arbitrary",)),
        name="outproj",
    )(*args)


def _sub_max(x):
    return jnp.max(x, axis=0, keepdims=True)


def _sub_min(x):
    return jnp.min(x, axis=0, keepdims=True)


def _sub_sum(x):
    return jnp.sum(x, axis=0, keepdims=True)


def _router_kernel(hpk_ref, hlo_ref, wrh_ref, wrl_ref, br_ref, e_ref, w_ref, slot_ref, cnt_ref, base_s):
    tm = hlo_ref.shape[0]

    @pl.when(pl.program_id(0) == 0)
    def _():
        base_s[...] = jnp.zeros_like(base_s)

    hi = _unpack_pairs(_load_token_tiles(hpk_ref, _tile_rows(hlo_ref.shape[1])), BF16)
    lo = hlo_ref[...]
    wrh = wrh_ref[...]
    logits = lax.dot_general(wrh, hi, NT_DIMS, preferred_element_type=F32)
    logits = logits + (lax.dot_general(wrh, lo, NT_DIMS, preferred_element_type=F32)
                       + lax.dot_general(wrl_ref[...], hi, NT_DIMS, preferred_element_type=F32))
    score = jax.nn.sigmoid(logits)
    sel = (score + br_ref[...]).reshape(GROUP_SIZE, N_GROUPS, tm)
    score = score.reshape(GROUP_SIZE, N_GROUPS, tm)

    m1 = sel[0]
    m2 = jnp.full_like(m1, NEG_INF)
    for e in range(1, GROUP_SIZE):
        x = sel[e]
        m2 = jnp.maximum(m2, jnp.minimum(m1, x))
        m1 = jnp.maximum(m1, x)
    cur = m1 + m2
    gidx = lax.broadcasted_iota(I32, (N_GROUPS, tm), 0)
    gmask = jnp.zeros((N_GROUPS, tm), jnp.bool_)
    for _ in range(TOPK_GROUPS):
        first = _sub_min(jnp.where(cur == _sub_max(cur), gidx, N_GROUPS))
        pick = gidx == first
        gmask = gmask | pick
        cur = jnp.where(pick, NEG_INF, cur)

    eid = (lax.broadcasted_iota(I32, (GROUP_SIZE, N_GROUPS, tm), 1) * GROUP_SIZE
           + lax.broadcasted_iota(I32, (GROUP_SIZE, N_GROUPS, tm), 0))
    cur = jnp.where(gmask[None], sel, NEG_INF)
    ids, wts, picks = [], [], []
    for _ in range(TOP_K):
        mx = _sub_max(jnp.max(cur, axis=0))
        first = _sub_min(jnp.min(jnp.where(cur == mx[None], eid, N_EXPERTS), axis=0))
        pick = eid == first[None]
        ids.append(first)
        picks.append(pick)
        wts.append(_sub_sum(jnp.sum(jnp.where(pick, score, 0.0), axis=0)))
        cur = jnp.where(pick, NEG_INF, cur)
    total = wts[0]
    chosen = picks[0]
    for k in range(1, TOP_K):
        total = total + wts[k]
        chosen = chosen | picks[k]

    chosen = jnp.where(chosen, 1.0, 0.0).reshape(N_EXPERTS, tm)
    earlier_or_self = (lax.broadcasted_iota(I32, (tm, tm), 0) <= lax.broadcasted_iota(I32, (tm, tm), 1)).astype(BF16)
    incl = _dot(chosen.astype(BF16), earlier_or_self)
    rank = (base_s[...] + (incl - chosen)).reshape(GROUP_SIZE, N_GROUPS, tm)
    base_s[...] = base_s[...] + incl[:, tm - 1:tm]
    slots = [_sub_sum(jnp.sum(jnp.where(pick, rank, 0.0), axis=0)).astype(I32) for pick in picks]

    pad = 8 - TOP_K
    e_ref[...] = jnp.concatenate(ids + [jnp.zeros((pad, tm), I32)], axis=0)
    w_ref[...] = jnp.concatenate([w / total * ROUTED_SCALE for w in wts] + [jnp.zeros((pad, tm), F32)], axis=0)
    slot_ref[...] = jnp.concatenate(slots + [jnp.zeros((pad, tm), I32)], axis=0)
    cnt_ref[...] = jnp.broadcast_to(base_s[...], cnt_ref.shape)


def _router(hpk, hlo, w_router, b_router, tm):
    n, d = hlo.shape
    wr = w_router.T.reshape(N_GROUPS, GROUP_SIZE, d).transpose(1, 0, 2).reshape(N_EXPERTS, d)
    br = b_router.reshape(N_GROUPS, GROUP_SIZE).T.reshape(N_EXPERTS, 1)
    wrh = wr.astype(BF16)
    wrl = (wr - wrh.astype(F32)).astype(BF16)
    full = lambda shape: pl.BlockSpec(shape, lambda i: (0, 0))
    per_token = pl.BlockSpec((8, tm), lambda i: (0, i))
    e_k, w_k, slot_k, counts = pl.pallas_call(
        _router_kernel,
        grid=(n // tm,),
        in_specs=[pl.BlockSpec((tm * _tile_rows(d), LANES), lambda i: (i, 0)),
                  pl.BlockSpec((tm, d), lambda i: (i, 0)),
                  full((N_EXPERTS, d)), full((N_EXPERTS, d)), full((N_EXPERTS, 1))],
        out_specs=[per_token, per_token, per_token, full((N_EXPERTS, 128))],
        out_shape=[jax.ShapeDtypeStruct((8, n), I32), jax.ShapeDtypeStruct((8, n), F32),
                   jax.ShapeDtypeStruct((8, n), I32), jax.ShapeDtypeStruct((N_EXPERTS, 128), F32)],
        scratch_shapes=[pltpu.VMEM((N_EXPERTS, 1), F32)],
        compiler_params=_cparams(("arbitrary",)),
        name="router",
    )(hpk, hlo, wrh, wrl, br)
    counts = counts[:, 0].reshape(GROUP_SIZE, N_GROUPS).T.reshape(N_EXPERTS).astype(I32)
    return e_k, w_k, slot_k, counts


def _tile_window(token, nsub):
    start = token * nsub
    if nsub % SUBLANES == 0:
        start = pl.multiple_of(start, SUBLANES)
    return pl.ds(start, nsub)


def _token_copy(src, src_token, dst, dst_token, nsub, sem):
    return pltpu.make_async_copy(src.at[_tile_window(src_token, nsub)], dst.at[_tile_window(dst_token, nsub)], sem)


def _for_token_groups(n_tokens, per_token):
    def group(g, carry):
        t0 = pl.multiple_of(g * SUBLANES, SUBLANES)
        for u in range(SUBLANES):
            per_token(t0, u)
        return carry

    lax.fori_loop(0, n_tokens // SUBLANES, group, 0)


def _dispatch_kernel(pad_end_ref, dest_ref, h_ref, xs_ref, zero_s, sem, zero_sem, *, nsub):
    tb = dest_ref.shape[0] // DEST_STRIDE
    te = zero_s.shape[0] // nsub

    def tail_fill(e):
        start = pl.multiple_of((pad_end_ref[e] - te) * nsub, te * nsub)
        return pltpu.make_async_copy(zero_s, xs_ref.at[pl.ds(start, te * nsub)], zero_sem)

    def has_rows(e):
        return pad_end_ref[e] > (pad_end_ref[e - 1] if e else 0)

    @pl.when(pl.program_id(0) == 0)
    def _():
        zero_s[...] = jnp.zeros_like(zero_s)
        for e in range(N_EXPERTS):
            pl.when(has_rows(e))(lambda e=e: tail_fill(e).start())
        for e in range(N_EXPERTS):
            pl.when(has_rows(e))(lambda e=e: tail_fill(e).wait())

    def start(t0, u):
        for k in range(TOP_K):
            _token_copy(h_ref, t0 + u, xs_ref, dest_ref[t0 * DEST_STRIDE + (u * DEST_STRIDE + k)], nsub, sem).start()

    def wait(t0, u):
        for k in range(TOP_K):
            _token_copy(h_ref, 0, xs_ref, 0, nsub, sem).wait()

    _for_token_groups(tb, start)
    _for_token_groups(tb, wait)


def _dispatch(pad_end, dest, hpk, n_rows, tb, te, nsub):
    n = hpk.shape[0] // nsub
    grid_spec = pltpu.PrefetchScalarGridSpec(
        num_scalar_prefetch=1,
        grid=(n // tb,),
        in_specs=[pl.BlockSpec((tb * DEST_STRIDE,), lambda i, pe: (i,), memory_space=pltpu.SMEM),
                  pl.BlockSpec((tb * nsub, LANES), lambda i, pe: (i, 0))],
        out_specs=pl.BlockSpec(memory_space=pl.ANY),
        scratch_shapes=[pltpu.VMEM((te * nsub, LANES), U32), pltpu.SemaphoreType.DMA(()),
                        pltpu.SemaphoreType.DMA(())],
    )
    return pl.pallas_call(
        functools.partial(_dispatch_kernel, nsub=nsub),
        grid_spec=grid_spec,
        out_shape=jax.ShapeDtypeStruct((n_rows * nsub, LANES), U32),
        compiler_params=_cparams(("arbitrary",)),
        name="dispatch",
    )(pad_end, dest, hpk)


def _experts_kernel(be_ref, nu_ref, x_ref, wg_ref, wu_ref, wd_ref, o_ref, wg_s, wu_s, wd_s):
    i = pl.program_id(0)
    prev = be_ref[jnp.maximum(i - 1, 0)]

    @pl.when((i == 0) | (be_ref[i] != prev))
    def _():
        wg_s[...] = wg_ref[0].astype(BF16)
        wu_s[...] = wu_ref[0].astype(BF16)
        wd_s[...] = wd_ref[0].astype(BF16)

    @pl.when(i < nu_ref[0])
    def _():
        x = _unpack_pairs(_load_token_tiles(x_ref, _tile_rows(wg_s.shape[0])), BF16)
        g = _dot(x, wg_s[...])
        hidden = (g * jax.nn.sigmoid(g)) * _dot(x, wu_s[...])
        _store_token_tiles(o_ref, _pack_pairs(_dot(hidden.astype(BF16), wd_s[...])))

    @pl.when(i >= nu_ref[0])
    def _():
        o_ref[...] = jnp.zeros_like(o_ref)


def _experts(block_expert, n_used, xs, w_gate, w_up, w_down, tm):
    d, ff = w_gate.shape[1:]
    nsub = _tile_rows(d)
    p = xs.shape[0] // nsub
    rows = pl.BlockSpec((tm * nsub, LANES), lambda i, be, nu: (i, 0))
    grid_spec = pltpu.PrefetchScalarGridSpec(
        num_scalar_prefetch=2,
        grid=(p // tm,),
        in_specs=[rows,
                  pl.BlockSpec((1, d, ff), lambda i, be, nu: (be[i], 0, 0)),
                  pl.BlockSpec((1, d, ff), lambda i, be, nu: (be[i], 0, 0)),
                  pl.BlockSpec((1, ff, d), lambda i, be, nu: (be[i], 0, 0))],
        out_specs=rows,
        scratch_shapes=[pltpu.VMEM((d, ff), BF16), pltpu.VMEM((d, ff), BF16), pltpu.VMEM((ff, d), BF16)],
    )
    return pl.pallas_call(
        _experts_kernel,
        grid_spec=grid_spec,
        out_shape=jax.ShapeDtypeStruct(xs.shape, U32),
        compiler_params=_cparams(("arbitrary",)),
        name="experts",
    )(block_expert, n_used, xs, w_gate, w_up, w_down)


def _combine_kernel(dest_ref, wk_ref, ys_ref, x1_ref, hpk_ref, gt_ref, wsg_ref, wsu_ref, wsd_ref, gf_ref,
                    yp_ref, ysm_ref, buf, sem, *, n_prompt_blocks):
    tb, d = x1_ref.shape
    nsub = _tile_rows(d)
    seg = gt_ref.shape[0]
    i = pl.program_id(0)

    def start(t0, u):
        for k in range(TOP_K):
            _token_copy(ys_ref, dest_ref[t0 * DEST_STRIDE + (u * DEST_STRIDE + k)], buf.at[k], t0 + u, nsub,
                        sem).start()

    def wait(t0, u):
        for k in range(TOP_K):
            _token_copy(ys_ref, 0, buf.at[k], 0, nsub, sem).wait()

    _for_token_groups(tb, start)
    x = _unpack_pairs(_load_token_tiles(hpk_ref, nsub), BF16)
    g = _dot(x, wsg_ref[...])
    ffn = _dot(((g * jax.nn.sigmoid(g)) * _dot(x, wsu_ref[...])).astype(BF16), wsd_ref[...])
    _for_token_groups(tb, wait)
    wk = wk_ref[...]
    for k in range(TOP_K):
        ffn = ffn + wk[:, k:k + 1] * _unpack_pairs(_load_token_tiles(buf.at[k], nsub), F32)
    x2 = x1_ref[...].reshape(seg, tb // seg, d) + gt_ref[...] * ffn.reshape(seg, tb // seg, d)
    y = (_rms(x2) * gf_ref[...].reshape(1, 1, d)).reshape(tb, d)

    @pl.when(i < n_prompt_blocks)
    def _():
        yp_ref[...] = y

    @pl.when(i >= n_prompt_blocks)
    def _():
        ysm_ref[...] = y


def _combine(dest, wk_t, ys, x1, hpk, gt2_seg, ws_gate, ws_up, ws_down, g_final, n_prompt, tb):
    n, d = x1.shape
    nsub = _tile_rows(d)
    ff = ws_gate.shape[1]
    npb = n_prompt // tb
    nsb = (n - n_prompt) // tb
    seg = gt2_seg.shape[0] * tb // n
    row = lambda width: pl.BlockSpec((tb, width), lambda i: (i, 0))
    full = lambda shape: pl.BlockSpec(shape, lambda i: (0,) * len(shape))
    return pl.pallas_call(
        functools.partial(_combine_kernel, n_prompt_blocks=npb),
        grid=(n // tb,),
        in_specs=[pl.BlockSpec((tb * DEST_STRIDE,), lambda i: (i,), memory_space=pltpu.SMEM),
                  row(8), pl.BlockSpec(memory_space=pl.ANY), row(d),
                  pl.BlockSpec((tb * nsub, LANES), lambda i: (i, 0)),
                  pl.BlockSpec((seg, 1, d), lambda i: (i, 0, 0)),
                  full((d, ff)), full((d, ff)), full((ff, d)), full((1, d))],
        out_specs=[pl.BlockSpec((tb, d), lambda i: (jnp.minimum(i, npb - 1), 0)),
                   pl.BlockSpec((tb, d), lambda i: (jnp.clip(i - npb, 0, nsb - 1), 0))],
        out_shape=[jax.ShapeDtypeStruct((n_prompt, d), F32), jax.ShapeDtypeStruct((n - n_prompt, d), F32)],
        scratch_shapes=[pltpu.VMEM((TOP_K, tb * nsub, LANES), U32), pltpu.SemaphoreType.DMA(())],
        compiler_params=_cparams(("arbitrary",)),
        name="combine",
    )(dest, wk_t, ys, x1, hpk, gt2_seg, ws_gate, ws_up, ws_down, g_final.reshape(1, d))


def _largest_tile(n, target):
    t = min(n, target)
    while n % t:
        t -= 1
    return t


def _mixer(x, mod, past_k, past_v, h0, conv0, p, first_pos_zero, n_total, row_offset, earlier):
    b, t, d = x.shape
    m = b * t
    a = d
    hn_bm, hn_tm = _norm_mod(x, mod, p["g_mix"])
    hn_bm = hn_bm.reshape(m, d)
    tm = _largest_tile(m, 512)
    tmh = _largest_tile(m, 1024)
    w_in = p["w_in"]
    u = _proj(hn_tm, w_in, 0, d, "plain", BF16, tmh).reshape(t, b, d)
    gl = _proj(hn_tm, w_in, 1, d, "gelu", BF16, tmh).reshape(t, b, d)
    sl = _proj(hn_tm, w_in, 5, d, "sigmoid", BF16, tmh).reshape(t, b, d)
    q = _proj(hn_bm, w_in, 2, a, "qscale", BF16, tmh).reshape(b, t, a)
    k = _proj(hn_bm, w_in, 3, a, "plain", F32, tm).reshape(b, t, a)
    v = _proj(hn_bm, w_in, 4, a, "plain", F32, tm).reshape(b, t, a)
    sz = _proj(hn_bm, w_in, 6, a, "sigmoid", BF16, tmh).reshape(b, t, a)

    if h0 is None:
        h0 = jnp.zeros((b, d), F32)
        prev = jnp.zeros((CONV_WIDTH - 1, b, d), F32)
    else:
        prev = conv0.transpose(1, 0, 2)
    tc = _largest_tile(t, 128)
    yl, h_last, conv_tail = _lru(u, gl, sl, p["conv_w"], p["conv_b"], p["w_rg_a"], p["b_rg_a"], p["w_rg_x"],
                                 p["b_rg_x"], p["lru_lambda"], h0, prev, tc, first_pos_zero)
    if past_k is None:
        mixed = _attn_prompt(q, k, v, sz, yl, _largest_tile(t, 256))
    else:
        mixed = _attn_sample(q, k, v, past_k, past_v, sz, yl)
    tmo = _largest_tile(m, 256)
    shared = _outproj(mixed.reshape(m, a), p["w_out"], x.reshape(m, d), mod, p["g_ffn"], tmo, t, n_total, row_offset,
                      earlier)
    nh = a // HEAD_DIM
    return (shared, k.reshape(b, t, nh, HEAD_DIM), v.reshape(b, t, nh, HEAD_DIM), h_last,
            conv_tail.transpose(1, 0, 2))


def _routing_tables(e_k, slot_k, counts, tm):
    n_tokens = e_k.shape[1]
    padded = (counts + tm - 1) // tm * tm
    pad_end = jnp.cumsum(padded).astype(I32)
    pad_start = pad_end - padded
    is_expert = e_k[:, :, None] == jnp.arange(N_EXPERTS, dtype=I32)
    dest = slot_k + jnp.sum(jnp.where(is_expert, pad_start, 0), axis=-1).astype(I32)
    n_blocks = -(-(TOP_K * n_tokens + N_EXPERTS * (tm - 1)) // tm)
    block_start = jnp.arange(n_blocks, dtype=I32) * tm
    block_expert = jnp.minimum(jnp.sum(pad_end[None, :] <= block_start[:, None], axis=1), N_EXPERTS - 1).astype(I32)
    n_used = (pad_end[-1:] // tm).astype(I32)
    return dest, pad_end, block_expert, n_used, n_blocks * tm


def kernel(x_prompt, x_sample, cache_k, cache_v, state_lru, state_conv, c_prompt, c_sample, g_mix, w_ada, b_ada, w_in, conv_w, conv_b, w_rg_a, b_rg_a, w_rg_x, b_rg_x, lru_lambda, w_out, g_ffn, w_router, b_router, w_gate, w_up, w_down, ws_gate, ws_up, ws_down, g_final):
    assert g_mix.shape[0] == 1, "single-layer trunk only"
    b, t, d = x_prompt.shape
    bs, ts, _ = x_sample.shape
    assert ts >= CONV_WIDTH - 1 and ts == TIME_GROUP and t % TIME_GROUP == 0
    p = dict(g_mix=g_mix[0], w_in=w_in[0], conv_w=conv_w[0], conv_b=conv_b[0],
             w_rg_a=w_rg_a[0].astype(BF16), b_rg_a=b_rg_a[0], w_rg_x=w_rg_x[0].astype(BF16), b_rg_x=b_rg_x[0],
             lru_lambda=lru_lambda[0], w_out=w_out[0], g_ffn=g_ffn[0])
    mod = _ada(jnp.concatenate([c_prompt, c_sample], axis=0), w_ada[0], b_ada[0]).reshape(b + bs, N_MOD, d)
    mod_p, mod_s = mod[:b], mod[b:]

    n_prompt = b * t
    n = n_prompt + bs * ts
    tb = _largest_tile(bs * ts, 256)
    assert n_prompt % tb == 0 and tb % ts == 0

    shared, kp, vp, hp, cp = _mixer(x_prompt, mod_p, None, None, None, None, p, True, n, 0, None)
    (x1, hpk, hlo), ks, vs, hs, cs = _mixer(x_sample, mod_s, cache_k[0], cache_v[0], state_lru[0], state_conv[0],
                                            p, False, n, n_prompt, shared)

    e_k, w_k, slot_k, counts = _router(hpk, hlo, w_router[0], b_router[0], tb)
    tme = EXPERT_ROW_BLOCK
    dest, pad_end, block_expert, n_used, n_rows = _routing_tables(e_k, slot_k, counts, tme)
    dest = dest.T.reshape(-1)
    xs = _dispatch(pad_end, dest, hpk, n_rows, tb, tme, _tile_rows(d))
    ys = _experts(block_expert, n_used, xs, w_gate[0], w_up[0], w_down[0], tme)
    gt2 = jnp.concatenate([jnp.broadcast_to(mod_p[:, None, 5:6, :], (b, t // ts, 1, d)).reshape(-1, 1, d),
                           mod_s[:, 5:6, :]], axis=0)
    y_p, y_s = _combine(dest, w_k.T, ys, x1, hpk, gt2, ws_gate[0].astype(BF16), ws_up[0].astype(BF16),
                        ws_down[0].astype(BF16), g_final, n_prompt, tb)
    return (y_p.reshape(b, t, d), y_s.reshape(bs, ts, d), kp[None], vp[None], hp[None], cp[None],
            ks[None], vs[None], hs[None], cs[None])
```

```python
import functools

import jax
import jax.numpy as jnp
from jax import lax
from jax.experimental import pallas as pl
from jax.experimental.pallas import tpu as pltpu

F32 = jnp.float32
BF16 = jnp.bfloat16
I32 = jnp.int32
U32 = jnp.uint32

LANES = 128
SUBLANES = 8
HEAD_DIM = 128
LRU_BLOCK_DIM = 256
LRU_C = 8.0
CONV_WIDTH = 4
N_EXPERTS = 64
TOP_K = 6
N_GROUPS = 8
GROUP_SIZE = N_EXPERTS // N_GROUPS
TOPK_GROUPS = 4
ROUTED_SCALE = 2.5
NORM_EPS = 1e-6
N_MOD = 6
TIME_GROUP = 16
LOG2_E = 1.4426950408889634
ATTN_HEADS_PER_STEP = 4
F32_ZERO_WEIGHT_LOG2 = -150.0
DEST_STRIDE = 8
EXPERT_ROW_BLOCK = 512
V7X_VMEM_LIMIT_BYTES = 56 * 1024 * 1024
NT_DIMS = (((1,), (1,)), ((), ()))
NEG_INF = float("-inf")


def _cparams(semantics):
    return pltpu.CompilerParams(dimension_semantics=semantics, vmem_limit_bytes=V7X_VMEM_LIMIT_BYTES)


def _dot(a, b):
    return jnp.dot(a, b, preferred_element_type=F32)


def _split_bf16(x):
    hi = x.astype(BF16)
    lo = (x - hi.astype(F32)).astype(BF16)
    return hi, lo


def _dot3(a, b):
    ah, al = _split_bf16(a)
    bh, bl = _split_bf16(b)
    return _dot(ah, bh) + (_dot(al, bh) + _dot(ah, bl))


def _pack_pairs(h):
    n = h.shape[1] // 2
    bits = lax.bitcast_convert_type(h.astype(BF16).astype(F32), U32)
    return bits[:, :n] | (bits[:, n:] >> 16)


def _store_token_tiles(ref, packed):
    m, n = packed.shape
    nsub = n // LANES
    for j in range(nsub):
        ref[pl.ds(j, m, stride=nsub), :] = packed[:, j * LANES:(j + 1) * LANES]


def _tile_rows(d):
    return d // 2 // LANES


def _load_token_tiles(ref, nsub):
    m = ref.shape[0] // nsub
    return jnp.concatenate([ref[pl.ds(j, m, stride=nsub), :] for j in range(nsub)], axis=1)


def _unpack_pairs(p, dtype):
    a = lax.bitcast_convert_type(p & jnp.uint32(0xFFFF0000), F32)
    b = lax.bitcast_convert_type(p << 16, F32)
    return jnp.concatenate([a.astype(dtype), b.astype(dtype)], axis=1)


def _rms(x):
    return x * lax.rsqrt(jnp.mean(x * x, axis=-1, keepdims=True) + NORM_EPS)


def _ada_kernel(c_ref, w_ref, b_ref, o_ref):
    c = c_ref[...]
    o_ref[...] = _dot3(c * jax.nn.sigmoid(c), w_ref[...]) + b_ref[...]


def _ada(c, w, b):
    m, d = c.shape
    n = w.shape[1]
    tn = min(n, 1024)
    return pl.pallas_call(
        _ada_kernel,
        grid=(n // tn,),
        in_specs=[pl.BlockSpec((m, d), lambda j: (0, 0)),
                  pl.BlockSpec((d, tn), lambda j: (0, j)),
                  pl.BlockSpec((1, tn), lambda j: (0, j))],
        out_specs=pl.BlockSpec((m, tn), lambda j: (0, j)),
        out_shape=jax.ShapeDtypeStruct((m, n), F32),
        compiler_params=_cparams(("arbitrary",)),
        name="ada",
    )(c, w, b.reshape(1, n))


def _row_permutation(n_outer, n_inner):
    n = n_outer * n_inner
    out_row = lax.broadcasted_iota(I32, (n, n), 0)
    in_row = lax.broadcasted_iota(I32, (n, n), 1)
    same = (out_row // n_outer == in_row % n_inner) & (out_row % n_outer == in_row // n_inner)
    return same.astype(BF16)


def _norm_mod_kernel(x_ref, mod_ref, g_ref, obm_ref, otm_ref):
    b, tt, d = x_ref.shape
    m = mod_ref[...]
    h = (_rms(x_ref[...]) * g_ref[...].reshape(1, 1, d) * (1.0 + m[:, 1:2, :]) + m[:, 0:1, :]).astype(BF16)
    obm_ref[...] = h
    otm_ref[...] = _dot(_row_permutation(b, tt), h.reshape(b * tt, d)).astype(BF16)


def _norm_mod(x, mod, g):
    b, t, d = x.shape
    tt = TIME_GROUP
    return pl.pallas_call(
        _norm_mod_kernel,
        grid=(t // tt,),
        in_specs=[pl.BlockSpec((b, tt, d), lambda ti: (0, ti, 0)),
                  pl.BlockSpec((b, N_MOD, d), lambda ti: (0, 0, 0)),
                  pl.BlockSpec((1, d), lambda ti: (0, 0))],
        out_specs=[pl.BlockSpec((b, tt, d), lambda ti: (0, ti, 0)),
                   pl.BlockSpec((tt * b, d), lambda ti: (ti, 0))],
        out_shape=[jax.ShapeDtypeStruct((b, t, d), BF16), jax.ShapeDtypeStruct((t * b, d), BF16)],
        compiler_params=_cparams(("arbitrary",)),
        name="norm_mod",
    )(x, mod, g.reshape(1, d))


def _proj_kernel(a_ref, w_ref, o_ref, w_s, *, epilogue):
    @pl.when(pl.program_id(0) == 0)
    def _():
        w_s[...] = w_ref[...].astype(BF16)

    acc = _dot(a_ref[...], w_s[...])
    if epilogue == "gelu":
        acc = jax.nn.gelu(acc)
    elif epilogue == "sigmoid":
        acc = jax.nn.sigmoid(acc)
    elif epilogue == "qscale":
        acc = acc * (HEAD_DIM ** -0.5 * LOG2_E)
    o_ref[...] = acc.astype(o_ref.dtype)


def _resident(shape, index_map):
    return pl.BlockSpec(shape, index_map, pipeline_mode=pl.Buffered(1))


def _proj(a, w, col, width, epilogue, out_dtype, tm):
    m, k = a.shape
    return pl.pallas_call(
        functools.partial(_proj_kernel, epilogue=epilogue),
        grid=(m // tm,),
        in_specs=[pl.BlockSpec((tm, k), lambda i: (i, 0)),
                  _resident((k, width), lambda i: (0, col))],
        out_specs=pl.BlockSpec((tm, width), lambda i: (i, 0)),
        out_shape=jax.ShapeDtypeStruct((m, width), out_dtype),
        scratch_shapes=[pltpu.VMEM((k, width), BF16)],
        compiler_params=_cparams(("arbitrary",)),
        name="proj_" + epilogue,
    )(a, w)


def _lru_kernel(u_ref, gl_ref, sl_ref, cw_ref, cb_ref, wa_ref, ba_ref, wx_ref, bx_ref, lam_ref, h0_ref, prev_ref,
                y_ref, hl_ref, cn_ref, tail_s, h_s, a_s, b_s, y_s, *, first_pos_zero):
    tc, nb, wb = u_ref.shape
    ti = pl.program_id(1)

    @pl.when(ti == 0)
    def _():
        tail_s[...] = prev_ref[...]
        h_s[...] = h0_ref[...]

    u = u_ref[...].astype(F32)
    xp = jnp.concatenate([tail_s[...], u], axis=0)
    cw = cw_ref[...]
    xc = cb_ref[...].reshape(1, 1, wb)
    for j in range(CONV_WIDTH):
        xc = xc + xp[j:j + tc] * cw[j:j + 1].reshape(1, 1, wb)
    last_rows = u[tc - (CONV_WIDTH - 1):]
    tail_s[...] = last_rows
    cn_ref[...] = last_rows

    x2 = xc.reshape(tc * nb, wb)
    xb = x2.astype(BF16)
    r = jax.nn.sigmoid(_dot(xb, wa_ref[0]) + ba_ref[0])
    gi = jax.nn.sigmoid(_dot(xb, wx_ref[0]) + bx_ref[0])
    log_a = r * ((-LRU_C) * jax.nn.softplus(-lam_ref[...]))
    a = jnp.exp(log_a)
    m2 = -jnp.tanh(log_a) * (a * a + 1.0)
    mult = jnp.where(m2 > 0.0, m2 * lax.rsqrt(m2), 0.0)
    if first_pos_zero:
        row = lax.broadcasted_iota(I32, (tc * nb, wb), 0)
        mult = jnp.where((row < nb) & (ti == 0), 1.0, mult)
    a_s[...] = a.reshape(tc, nb, wb)
    b_s[...] = (mult * (gi * x2)).reshape(tc, nb, wb)

    def step(t, h):
        h = a_s[t] * h + b_s[t]
        gate = gl_ref[t].astype(F32) * sl_ref[t].astype(F32)
        y_s[t] = h * gate
        return h

    h = lax.fori_loop(0, tc, step, h_s[...], unroll=8)
    h_s[...] = h
    hl_ref[...] = h

    perm = _row_permutation(TIME_GROUP, nb)
    for g in range(tc // TIME_GROUP):
        rows = slice(g * TIME_GROUP, (g + 1) * TIME_GROUP)
        yg = y_s[rows].reshape(TIME_GROUP * nb, wb).astype(BF16)
        y_ref[:, rows, :] = _dot(perm, yg).astype(BF16).reshape(nb, TIME_GROUP, wb)


def _lru(u, gl, sl, conv_w, conv_b, wa, ba, wx, bx, lam, h0, prev, tc, first_pos_zero):
    t, b, w = u.shape
    wb = LRU_BLOCK_DIM
    nw = w // wb
    seq = pl.BlockSpec((tc, b, wb), lambda wi, ti: (ti, 0, wi))
    seq_out = pl.BlockSpec((b, tc, wb), lambda wi, ti: (0, ti, wi))
    chan = lambda rows: pl.BlockSpec((rows, wb), lambda wi, ti: (0, wi))
    gate_w = pl.BlockSpec((1, wb, wb), lambda wi, ti: (wi, 0, 0))
    gate_b = pl.BlockSpec((1, 1, wb), lambda wi, ti: (wi, 0, 0))
    tail = pl.BlockSpec((CONV_WIDTH - 1, b, wb), lambda wi, ti: (0, 0, wi))
    return pl.pallas_call(
        functools.partial(_lru_kernel, first_pos_zero=first_pos_zero),
        grid=(nw, t // tc),
        in_specs=[seq, seq, seq, chan(CONV_WIDTH), chan(1), gate_w, gate_b, gate_w, gate_b, chan(1),
                  chan(b), tail],
        out_specs=[seq_out, chan(b), tail],
        out_shape=[jax.ShapeDtypeStruct((b, t, w), BF16), jax.ShapeDtypeStruct((b, w), F32),
                   jax.ShapeDtypeStruct((CONV_WIDTH - 1, b, w), F32)],
        scratch_shapes=[pltpu.VMEM((CONV_WIDTH - 1, b, wb), F32), pltpu.VMEM((b, wb), F32),
                        pltpu.VMEM((tc, b, wb), F32), pltpu.VMEM((tc, b, wb), F32),
                        pltpu.VMEM((tc, b, wb), F32)],
        compiler_params=_cparams(("arbitrary", "arbitrary")),
        name="lru",
    )(u, gl, sl, conv_w, conv_b.reshape(1, w), wa, ba.reshape(nw, 1, wb), wx, bx.reshape(nw, 1, wb),
      lam.reshape(1, w), h0, prev)


def _log_keep_and_logsig(z):
    nz = -z
    lk = jnp.minimum(nz, 0.0) - jnp.log2(1.0 + jnp.exp2(jnp.minimum(z, nz)))
    return lk, z + lk


def _later_key_sums(lk, tri):
    hi, lo = _split_bf16(lk)
    return _dot(hi, tri) + _dot(lo, tri)


def _strict_lower(n, dtype):
    row = lax.broadcasted_iota(I32, (n, n), 0)
    col = lax.broadcasted_iota(I32, (n, n), 1)
    return (row > col).astype(dtype), col < row


def _attn_kernel(q_ref, k_ref, v_ref, sz_ref, yl_ref, o_ref):
    tq = q_ref.shape[1]
    nh = q_ref.shape[2] // HEAD_DIM
    qi = pl.program_id(2)
    tri, causal = _strict_lower(tq, BF16)

    def tile(j, carry, diag):
        start = pl.multiple_of(j * tq, tq)
        heads = [slice(h * HEAD_DIM, (h + 1) * HEAD_DIM) for h in range(nh)]
        zs = [lax.dot_general(q_ref[0, :, cols], k_ref[0, pl.ds(start, tq), cols].astype(BF16), NT_DIMS,
                              preferred_element_type=F32) for cols in heads]
        lks, lss = [], []
        for z in zs:
            lk, ls = _log_keep_and_logsig(z)
            lks.append(jnp.where(causal, lk, 0.0) if diag else lk)
            lss.append(ls)
        rcs = [_later_key_sums(lk, tri) for lk in lks]
        ws = []
        for h in range(nh):
            w = jnp.exp2(lss[h] + rcs[h] + carry[h][1])
            ws.append((jnp.where(causal, w, 0.0) if diag else w).astype(BF16))
        return tuple((carry[h][0] + _dot(ws[h], v_ref[0, pl.ds(start, tq), heads[h]].astype(BF16)),
                      carry[h][1] + (rcs[h][:, 0:1] + lks[h][:, 0:1])) for h in range(nh))

    def any_weight_left(carry):
        c_max = carry[0][1]
        for h in range(1, nh):
            c_max = jnp.maximum(c_max, carry[h][1])
        return jnp.max(c_max) > F32_ZERO_WEIGHT_LOG2

    init = tuple((jnp.zeros((tq, HEAD_DIM), F32), jnp.zeros((tq, 1), F32)) for _ in range(nh))
    carry = tile(qi, init, True)

    def more(state):
        return (state[0] < qi) & state[1]

    def farther_tile(state):
        carry = tile(qi - 1 - state[0], state[2], False)
        return state[0] + 1, any_weight_left(carry), carry

    carry = lax.while_loop(more, farther_tile, (jnp.int32(0), any_weight_left(carry), carry))[2]
    acc = jnp.concatenate([carry[h][0] for h in range(nh)], axis=1)
    o_ref[0] = (yl_ref[0].astype(F32) + sz_ref[0].astype(F32) * acc).astype(BF16)


def _attn_prompt(q, k, v, sz, yl, tq):
    b, t, a = q.shape
    wblk = ATTN_HEADS_PER_STEP * HEAD_DIM
    qspec = pl.BlockSpec((1, tq, wblk), lambda bi, hi, qi: (bi, qi, hi))
    kvspec = pl.BlockSpec((1, t, wblk), lambda bi, hi, qi: (bi, 0, hi))
    return pl.pallas_call(
        _attn_kernel,
        grid=(b, a // wblk, t // tq),
        in_specs=[qspec, kvspec, kvspec, qspec, qspec],
        out_specs=qspec,
        out_shape=jax.ShapeDtypeStruct((b, t, a), BF16),
        compiler_params=_cparams(("arbitrary", "arbitrary", "arbitrary")),
        name="attn_prompt",
    )(q, k, v, sz, yl)


def _attn_sample_kernel(q_ref, kn_ref, vn_ref, kc_ref, vc_ref, sz_ref, yl_ref, o_ref, lk_s, ls_s):
    ts = q_ref.shape[1]
    nh = q_ref.shape[2] // HEAD_DIM
    past = kc_ref.shape[1]
    tri_n, causal = _strict_lower(ts, BF16)
    tri_p, _ = _strict_lower(past, BF16)
    accs, carries = [], []
    for h in range(nh):
        cols = slice(h * HEAD_DIM, (h + 1) * HEAD_DIM)
        q = q_ref[0, :, cols]
        zn = lax.dot_general(q, kn_ref[0, :, cols].astype(BF16), NT_DIMS, preferred_element_type=F32)
        lk, ls = _log_keep_and_logsig(zn)
        lk = jnp.where(causal, lk, 0.0)
        rc = _later_key_sums(lk, tri_n)
        w = jnp.where(causal, jnp.exp2(ls + rc), 0.0)
        accs.append(_dot(w.astype(BF16), vn_ref[0, :, cols].astype(BF16)))
        carries.append(rc[:, 0:1] + lk[:, 0:1])
        zp = lax.dot_general(q, kc_ref[0, :, h, :].astype(BF16), NT_DIMS, preferred_element_type=F32)
        lkp, lsp = _log_keep_and_logsig(zp)
        lk_s[h * ts:(h + 1) * ts, :] = lkp
        ls_s[h * ts:(h + 1) * ts, :] = lsp
    rcp = _later_key_sums(lk_s[...], tri_p)
    for h in range(nh):
        cols = slice(h * HEAD_DIM, (h + 1) * HEAD_DIM)
        rows = slice(h * ts, (h + 1) * ts)
        w = jnp.exp2(ls_s[rows, :] + rcp[rows, :] + carries[h])
        acc = accs[h] + _dot(w.astype(BF16), vc_ref[0, :, h, :].astype(BF16))
        o_ref[0, :, cols] = (yl_ref[0, :, cols].astype(F32) + sz_ref[0, :, cols].astype(F32) * acc).astype(BF16)


def _attn_sample(q, kn, vn, kc, vc, sz, yl):
    b, ts, a = q.shape
    past = kc.shape[1]
    nh = a // HEAD_DIM
    cur = pl.BlockSpec((1, ts, a), lambda bi: (bi, 0, 0))
    cache = pl.BlockSpec((1, past, nh, HEAD_DIM), lambda bi: (bi, 0, 0, 0))
    return pl.pallas_call(
        _attn_sample_kernel,
        grid=(b,),
        in_specs=[cur, cur, cur, cache, cache, cur, cur],
        out_specs=cur,
        out_shape=jax.ShapeDtypeStruct((b, ts, a), BF16),
        scratch_shapes=[pltpu.VMEM((nh * ts, past), F32), pltpu.VMEM((nh * ts, past), F32)],
        compiler_params=_cparams(("arbitrary",)),
        name="attn_sample",
    )(q, kn, vn, kc, vc, sz, yl)


def _outproj_kernel(a_ref, w_ref, x_ref, mod_ref, g_ref, *rest):
    x1_ref, hpk_ref, hlo_ref, w_s = rest[-4:]
    nb = mod_ref.shape[0]
    tm, d = x_ref.shape

    @pl.when(pl.program_id(0) == 0)
    def _():
        w_s[...] = w_ref[...].astype(BF16)

    m = mod_ref[...]
    o = _dot(a_ref[...], w_s[...]).reshape(nb, tm // nb, d)
    x1 = x_ref[...].reshape(nb, tm // nb, d) + m[:, 2:3, :] * o
    h = _rms(x1) * g_ref[...].reshape(1, 1, d) * (1.0 + m[:, 4:5, :]) + m[:, 3:4, :]
    x1_ref[...] = x1.reshape(tm, d)
    h = h.reshape(tm, d)
    hi = h.astype(BF16).astype(F32)
    _store_token_tiles(hpk_ref, _pack_pairs(hi))
    hlo_ref[...] = (h - hi).astype(BF16)


def _outproj(a, w, x, mod, g, tm, rows_per_batch, n_total, row_offset, earlier=None):
    m, d = x.shape
    nb = max(tm // rows_per_batch, 1)
    per = max(rows_per_batch // tm, 1)
    off = row_offset // tm
    nsub = _tile_rows(d)
    row = lambda width: pl.BlockSpec((tm, width), lambda i: (i, 0))
    out_row = lambda width: pl.BlockSpec((tm, width), lambda i: (i + off, 0))
    in_specs = [row(d), _resident((d, d), lambda i: (0, 0)), row(d),
                pl.BlockSpec((nb, N_MOD, d), lambda i: (i // per, 0, 0)),
                pl.BlockSpec((1, d), lambda i: (0, 0))]
    args = [a, w, x, mod, g.reshape(1, d)]
    aliases = {}
    if earlier is not None:
        aliases = {len(args) + j: j for j in range(len(earlier))}
        in_specs += [pl.BlockSpec(memory_space=pl.ANY)] * len(earlier)
        args += list(earlier)
    return pl.pallas_call(
        _outproj_kernel,
        grid=(m // tm,),
        in_specs=in_specs,
        out_specs=[out_row(d), pl.BlockSpec((tm * nsub, LANES), lambda i: (i + off, 0)), out_row(d)],
        out_shape=[jax.ShapeDtypeStruct((n_total, d), F32), jax.ShapeDtypeStruct((n_total * nsub, LANES), U32),
                   jax.ShapeDtypeStruct((n_total, d), BF16)],
        scratch_shapes=[pltpu.VMEM((d, d), BF16)],
        input_output_aliases=aliases,
        compiler_params=_cparams(("arbitrary",)),
        name="outproj",
    )(*args)


def _sub_max(x):
    return jnp.max(x, axis=0, keepdims=True)


def _sub_min(x):
    return jnp.min(x, axis=0, keepdims=True)


def _sub_sum(x):
    return jnp.sum(x, axis=0, keepdims=True)


def _router_kernel(hpk_ref, hlo_ref, wrh_ref, wrl_ref, br_ref, e_ref, w_ref, slot_ref, cnt_ref, base_s):
    tm = hlo_ref.shape[0]

    @pl.when(pl.program_id(0) == 0)
    def _():
        base_s[...] = jnp.zeros_like(base_s)

    hi = _unpack_pairs(_load_token_tiles(hpk_ref, _tile_rows(hlo_ref.shape[1])), BF16)
    lo = hlo_ref[...]
    wrh = wrh_ref[...]
    logits = lax.dot_general(wrh, hi, NT_DIMS, preferred_element_type=F32)
    logits = logits + (lax.dot_general(wrh, lo, NT_DIMS, preferred_element_type=F32)
                       + lax.dot_general(wrl_ref[...], hi, NT_DIMS, preferred_element_type=F32))
    score = jax.nn.sigmoid(logits)
    sel = (score + br_ref[...]).reshape(GROUP_SIZE, N_GROUPS, tm)
    score = score.reshape(GROUP_SIZE, N_GROUPS, tm)

    m1 = sel[0]
    m2 = jnp.full_like(m1, NEG_INF)
    for e in range(1, GROUP_SIZE):
        x = sel[e]
        m2 = jnp.maximum(m2, jnp.minimum(m1, x))
        m1 = jnp.maximum(m1, x)
    cur = m1 + m2
    gidx = lax.broadcasted_iota(I32, (N_GROUPS, tm), 0)
    gmask = jnp.zeros((N_GROUPS, tm), jnp.bool_)
    for _ in range(TOPK_GROUPS):
        first = _sub_min(jnp.where(cur == _sub_max(cur), gidx, N_GROUPS))
        pick = gidx == first
        gmask = gmask | pick
        cur = jnp.where(pick, NEG_INF, cur)

    eid = (lax.broadcasted_iota(I32, (GROUP_SIZE, N_GROUPS, tm), 1) * GROUP_SIZE
           + lax.broadcasted_iota(I32, (GROUP_SIZE, N_GROUPS, tm), 0))
    cur = jnp.where(gmask[None], sel, NEG_INF)
    ids, wts, picks = [], [], []
    for _ in range(TOP_K):
        mx = _sub_max(jnp.max(cur, axis=0))
        first = _sub_min(jnp.min(jnp.where(cur == mx[None], eid, N_EXPERTS), axis=0))
        pick = eid == first[None]
        ids.append(first)
        picks.append(pick)
        wts.append(_sub_sum(jnp.sum(jnp.where(pick, score, 0.0), axis=0)))
        cur = jnp.where(pick, NEG_INF, cur)
    total = wts[0]
    chosen = picks[0]
    for k in range(1, TOP_K):
        total = total + wts[k]
        chosen = chosen | picks[k]

    chosen = jnp.where(chosen, 1.0, 0.0).reshape(N_EXPERTS, tm)
    earlier_or_self = (lax.broadcasted_iota(I32, (tm, tm), 0) <= lax.broadcasted_iota(I32, (tm, tm), 1)).astype(BF16)
    incl = _dot(chosen.astype(BF16), earlier_or_self)
    rank = (base_s[...] + (incl - chosen)).reshape(GROUP_SIZE, N_GROUPS, tm)
    base_s[...] = base_s[...] + incl[:, tm - 1:tm]
    slots = [_sub_sum(jnp.sum(jnp.where(pick, rank, 0.0), axis=0)).astype(I32) for pick in picks]

    pad = 8 - TOP_K
    e_ref[...] = jnp.concatenate(ids + [jnp.zeros((pad, tm), I32)], axis=0)
    w_ref[...] = jnp.concatenate([w / total * ROUTED_SCALE for w in wts] + [jnp.zeros((pad, tm), F32)], axis=0)
    slot_ref[...] = jnp.concatenate(slots + [jnp.zeros((pad, tm), I32)], axis=0)
    cnt_ref[...] = jnp.broadcast_to(base_s[...], cnt_ref.shape)


def _router(hpk, hlo, w_router, b_router, tm):
    n, d = hlo.shape
    wr = w_router.T.reshape(N_GROUPS, GROUP_SIZE, d).transpose(1, 0, 2).reshape(N_EXPERTS, d)
    br = b_router.reshape(N_GROUPS, GROUP_SIZE).T.reshape(N_EXPERTS, 1)
    wrh = wr.astype(BF16)
    wrl = (wr - wrh.astype(F32)).astype(BF16)
    full = lambda shape: pl.BlockSpec(shape, lambda i: (0, 0))
    per_token = pl.BlockSpec((8, tm), lambda i: (0, i))
    e_k, w_k, slot_k, counts = pl.pallas_call(
        _router_kernel,
        grid=(n // tm,),
        in_specs=[pl.BlockSpec((tm * _tile_rows(d), LANES), lambda i: (i, 0)),
                  pl.BlockSpec((tm, d), lambda i: (i, 0)),
                  full((N_EXPERTS, d)), full((N_EXPERTS, d)), full((N_EXPERTS, 1))],
        out_specs=[per_token, per_token, per_token, full((N_EXPERTS, 128))],
        out_shape=[jax.ShapeDtypeStruct((8, n), I32), jax.ShapeDtypeStruct((8, n), F32),
                   jax.ShapeDtypeStruct((8, n), I32), jax.ShapeDtypeStruct((N_EXPERTS, 128), F32)],
        scratch_shapes=[pltpu.VMEM((N_EXPERTS, 1), F32)],
        compiler_params=_cparams(("arbitrary",)),
        name="router",
    )(hpk, hlo, wrh, wrl, br)
    counts = counts[:, 0].reshape(GROUP_SIZE, N_GROUPS).T.reshape(N_EXPERTS).astype(I32)
    return e_k, w_k, slot_k, counts


def _tile_window(token, nsub):
    start = token * nsub
    if nsub % SUBLANES == 0:
        start = pl.multiple_of(start, SUBLANES)
    return pl.ds(start, nsub)


def _token_copy(src, src_token, dst, dst_token, nsub, sem):
    return pltpu.make_async_copy(src.at[_tile_window(src_token, nsub)], dst.at[_tile_window(dst_token, nsub)], sem)


def _for_token_groups(n_tokens, per_token):
    def group(g, carry):
        t0 = pl.multiple_of(g * SUBLANES, SUBLANES)
        for u in range(SUBLANES):
            per_token(t0, u)
        return carry

    lax.fori_loop(0, n_tokens // SUBLANES, group, 0)


def _dispatch_kernel(pad_end_ref, dest_ref, h_ref, xs_ref, zero_s, sem, zero_sem, *, nsub):
    tb = dest_ref.shape[0] // DEST_STRIDE
    te = zero_s.shape[0] // nsub

    def tail_fill(e):
        start = pl.multiple_of((pad_end_ref[e] - te) * nsub, te * nsub)
        return pltpu.make_async_copy(zero_s, xs_ref.at[pl.ds(start, te * nsub)], zero_sem)

    def has_rows(e):
        return pad_end_ref[e] > (pad_end_ref[e - 1] if e else 0)

    @pl.when(pl.program_id(0) == 0)
    def _():
        zero_s[...] = jnp.zeros_like(zero_s)
        for e in range(N_EXPERTS):
            pl.when(has_rows(e))(lambda e=e: tail_fill(e).start())
        for e in range(N_EXPERTS):
            pl.when(has_rows(e))(lambda e=e: tail_fill(e).wait())

    def start(t0, u):
        for k in range(TOP_K):
            _token_copy(h_ref, t0 + u, xs_ref, dest_ref[t0 * DEST_STRIDE + (u * DEST_STRIDE + k)], nsub,
                        sem).start(priority=k % 2)

    def wait(t0, u):
        for k in range(TOP_K):
            _token_copy(h_ref, 0, xs_ref, 0, nsub, sem).wait()

    _for_token_groups(tb, start)
    _for_token_groups(tb, wait)


def _dispatch(pad_end, dest, hpk, n_rows, tb, te, nsub):
    n = hpk.shape[0] // nsub
    grid_spec = pltpu.PrefetchScalarGridSpec(
        num_scalar_prefetch=1,
        grid=(n // tb,),
        in_specs=[pl.BlockSpec((tb * DEST_STRIDE,), lambda i, pe: (i,), memory_space=pltpu.SMEM),
                  pl.BlockSpec((tb * nsub, LANES), lambda i, pe: (i, 0))],
        out_specs=pl.BlockSpec(memory_space=pl.ANY),
        scratch_shapes=[pltpu.VMEM((te * nsub, LANES), U32), pltpu.SemaphoreType.DMA(()),
                        pltpu.SemaphoreType.DMA(())],
    )
    return pl.pallas_call(
        functools.partial(_dispatch_kernel, nsub=nsub),
        grid_spec=grid_spec,
        out_shape=jax.ShapeDtypeStruct((n_rows * nsub, LANES), U32),
        compiler_params=_cparams(("arbitrary",)),
        name="dispatch",
    )(pad_end, dest, hpk)


def _experts_kernel(be_ref, nu_ref, x_ref, wg_ref, wu_ref, wd_ref, o_ref, wg_s, wu_s, wd_s):
    i = pl.program_id(0)
    prev = be_ref[jnp.maximum(i - 1, 0)]

    @pl.when((i == 0) | (be_ref[i] != prev))
    def _():
        wg_s[...] = wg_ref[0].astype(BF16)
        wu_s[...] = wu_ref[0].astype(BF16)
        wd_s[...] = wd_ref[0].astype(BF16)

    @pl.when(i < nu_ref[0])
    def _():
        x = _unpack_pairs(_load_token_tiles(x_ref, _tile_rows(wg_s.shape[0])), BF16)
        g = _dot(x, wg_s[...])
        hidden = (g * jax.nn.sigmoid(g)) * _dot(x, wu_s[...])
        _store_token_tiles(o_ref, _pack_pairs(_dot(hidden.astype(BF16), wd_s[...])))

    @pl.when(i >= nu_ref[0])
    def _():
        o_ref[...] = jnp.zeros_like(o_ref)


def _experts(block_expert, n_used, xs, w_gate, w_up, w_down, tm):
    d, ff = w_gate.shape[1:]
    nsub = _tile_rows(d)
    p = xs.shape[0] // nsub
    rows = pl.BlockSpec((tm * nsub, LANES), lambda i, be, nu: (i, 0))
    grid_spec = pltpu.PrefetchScalarGridSpec(
        num_scalar_prefetch=2,
        grid=(p // tm,),
        in_specs=[rows,
                  pl.BlockSpec((1, d, ff), lambda i, be, nu: (be[i], 0, 0)),
                  pl.BlockSpec((1, d, ff), lambda i, be, nu: (be[i], 0, 0)),
                  pl.BlockSpec((1, ff, d), lambda i, be, nu: (be[i], 0, 0))],
        out_specs=rows,
        scratch_shapes=[pltpu.VMEM((d, ff), BF16), pltpu.VMEM((d, ff), BF16), pltpu.VMEM((ff, d), BF16)],
    )
    return pl.pallas_call(
        _experts_kernel,
        grid_spec=grid_spec,
        out_shape=jax.ShapeDtypeStruct(xs.shape, U32),
        compiler_params=_cparams(("arbitrary",)),
        name="experts",
    )(block_expert, n_used, xs, w_gate, w_up, w_down)


def _combine_kernel(dest_ref, wk_ref, ys_ref, x1_ref, hpk_ref, gt_ref, wsg_ref, wsu_ref, wsd_ref, gf_ref,
                    yp_ref, ysm_ref, buf, sem, *, n_prompt_blocks):
    tb, d = x1_ref.shape
    nsub = _tile_rows(d)
    seg = gt_ref.shape[0]
    i = pl.program_id(0)

    def start(t0, u):
        for k in range(TOP_K):
            _token_copy(ys_ref, dest_ref[t0 * DEST_STRIDE + (u * DEST_STRIDE + k)], buf.at[k], t0 + u, nsub,
                        sem).start(priority=k % 2)

    def wait(t0, u):
        for k in range(TOP_K):
            _token_copy(ys_ref, 0, buf.at[k], 0, nsub, sem).wait()

    _for_token_groups(tb, start)
    x = _unpack_pairs(_load_token_tiles(hpk_ref, nsub), BF16)
    g = _dot(x, wsg_ref[...])
    ffn = _dot(((g * jax.nn.sigmoid(g)) * _dot(x, wsu_ref[...])).astype(BF16), wsd_ref[...])
    _for_token_groups(tb, wait)
    wk = wk_ref[...]
    for k in range(TOP_K):
        ffn = ffn + wk[:, k:k + 1] * _unpack_pairs(_load_token_tiles(buf.at[k], nsub), F32)
    x2 = x1_ref[...].reshape(seg, tb // seg, d) + gt_ref[...] * ffn.reshape(seg, tb // seg, d)
    y = (_rms(x2) * gf_ref[...].reshape(1, 1, d)).reshape(tb, d)

    @pl.when(i < n_prompt_blocks)
    def _():
        yp_ref[...] = y

    @pl.when(i >= n_prompt_blocks)
    def _():
        ysm_ref[...] = y


def _combine(dest, wk_t, ys, x1, hpk, gt2_seg, ws_gate, ws_up, ws_down, g_final, n_prompt, tb):
    n, d = x1.shape
    nsub = _tile_rows(d)
    ff = ws_gate.shape[1]
    npb = n_prompt // tb
    nsb = (n - n_prompt) // tb
    seg = gt2_seg.shape[0] * tb // n
    row = lambda width: pl.BlockSpec((tb, width), lambda i: (i, 0))
    full = lambda shape: pl.BlockSpec(shape, lambda i: (0,) * len(shape))
    return pl.pallas_call(
        functools.partial(_combine_kernel, n_prompt_blocks=npb),
        grid=(n // tb,),
        in_specs=[pl.BlockSpec((tb * DEST_STRIDE,), lambda i: (i,), memory_space=pltpu.SMEM),
                  row(8), pl.BlockSpec(memory_space=pl.ANY), row(d),
                  pl.BlockSpec((tb * nsub, LANES), lambda i: (i, 0)),
                  pl.BlockSpec((seg, 1, d), lambda i: (i, 0, 0)),
                  full((d, ff)), full((d, ff)), full((ff, d)), full((1, d))],
        out_specs=[pl.BlockSpec((tb, d), lambda i: (jnp.minimum(i, npb - 1), 0)),
                   pl.BlockSpec((tb, d), lambda i: (jnp.clip(i - npb, 0, nsb - 1), 0))],
        out_shape=[jax.ShapeDtypeStruct((n_prompt, d), F32), jax.ShapeDtypeStruct((n - n_prompt, d), F32)],
        scratch_shapes=[pltpu.VMEM((TOP_K, tb * nsub, LANES), U32), pltpu.SemaphoreType.DMA(())],
        compiler_params=_cparams(("arbitrary",)),
        name="combine",
    )(dest, wk_t, ys, x1, hpk, gt2_seg, ws_gate, ws_up, ws_down, g_final.reshape(1, d))


def _largest_tile(n, target):
    t = min(n, target)
    while n % t:
        t -= 1
    return t


def _mixer(x, mod, past_k, past_v, h0, conv0, p, first_pos_zero, n_total, row_offset, earlier):
    b, t, d = x.shape
    m = b * t
    a = d
    hn_bm, hn_tm = _norm_mod(x, mod, p["g_mix"])
    hn_bm = hn_bm.reshape(m, d)
    tm = _largest_tile(m, 512)
    tmh = _largest_tile(m, 1024)
    w_in = p["w_in"]
    u = _proj(hn_tm, w_in, 0, d, "plain", BF16, tmh).reshape(t, b, d)
    gl = _proj(hn_tm, w_in, 1, d, "gelu", BF16, tmh).reshape(t, b, d)
    sl = _proj(hn_tm, w_in, 5, d, "sigmoid", BF16, tmh).reshape(t, b, d)
    q = _proj(hn_bm, w_in, 2, a, "qscale", BF16, tmh).reshape(b, t, a)
    k = _proj(hn_bm, w_in, 3, a, "plain", F32, tm).reshape(b, t, a)
    v = _proj(hn_bm, w_in, 4, a, "plain", F32, tm).reshape(b, t, a)
    sz = _proj(hn_bm, w_in, 6, a, "sigmoid", BF16, tmh).reshape(b, t, a)

    if h0 is None:
        h0 = jnp.zeros((b, d), F32)
        prev = jnp.zeros((CONV_WIDTH - 1, b, d), F32)
    else:
        prev = conv0.transpose(1, 0, 2)
    tc = _largest_tile(t, 128)
    yl, h_last, conv_tail = _lru(u, gl, sl, p["conv_w"], p["conv_b"], p["w_rg_a"], p["b_rg_a"], p["w_rg_x"],
                                 p["b_rg_x"], p["lru_lambda"], h0, prev, tc, first_pos_zero)
    if past_k is None:
        mixed = _attn_prompt(q, k, v, sz, yl, _largest_tile(t, 256))
    else:
        mixed = _attn_sample(q, k, v, past_k, past_v, sz, yl)
    tmo = _largest_tile(m, 256)
    shared = _outproj(mixed.reshape(m, a), p["w_out"], x.reshape(m, d), mod, p["g_ffn"], tmo, t, n_total, row_offset,
                      earlier)
    nh = a // HEAD_DIM
    return (shared, k.reshape(b, t, nh, HEAD_DIM), v.reshape(b, t, nh, HEAD_DIM), h_last,
            conv_tail.transpose(1, 0, 2))


def _routing_tables(e_k, slot_k, counts, tm):
    n_tokens = e_k.shape[1]
    padded = (counts + tm - 1) // tm * tm
    pad_end = jnp.cumsum(padded).astype(I32)
    pad_start = pad_end - padded
    is_expert = e_k[:, :, None] == jnp.arange(N_EXPERTS, dtype=I32)
    dest = slot_k + jnp.sum(jnp.where(is_expert, pad_start, 0), axis=-1).astype(I32)
    n_blocks = -(-(TOP_K * n_tokens + N_EXPERTS * (tm - 1)) // tm)
    block_start = jnp.arange(n_blocks, dtype=I32) * tm
    block_expert = jnp.minimum(jnp.sum(pad_end[None, :] <= block_start[:, None], axis=1), N_EXPERTS - 1).astype(I32)
    n_used = (pad_end[-1:] // tm).astype(I32)
    return dest, pad_end, block_expert, n_used, n_blocks * tm


def kernel(x_prompt, x_sample, cache_k, cache_v, state_lru, state_conv, c_prompt, c_sample, g_mix, w_ada, b_ada, w_in, conv_w, conv_b, w_rg_a, b_rg_a, w_rg_x, b_rg_x, lru_lambda, w_out, g_ffn, w_router, b_router, w_gate, w_up, w_down, ws_gate, ws_up, ws_down, g_final):
    assert g_mix.shape[0] == 1, "single-layer trunk only"
    b, t, d = x_prompt.shape
    bs, ts, _ = x_sample.shape
    assert ts >= CONV_WIDTH - 1 and ts == TIME_GROUP and t % TIME_GROUP == 0
    p = dict(g_mix=g_mix[0], w_in=w_in[0], conv_w=conv_w[0], conv_b=conv_b[0],
             w_rg_a=w_rg_a[0].astype(BF16), b_rg_a=b_rg_a[0], w_rg_x=w_rg_x[0].astype(BF16), b_rg_x=b_rg_x[0],
             lru_lambda=lru_lambda[0], w_out=w_out[0], g_ffn=g_ffn[0])
    mod = _ada(jnp.concatenate([c_prompt, c_sample], axis=0), w_ada[0], b_ada[0]).reshape(b + bs, N_MOD, d)
    mod_p, mod_s = mod[:b], mod[b:]

    n_prompt = b * t
    n = n_prompt + bs * ts
    tb = _largest_tile(bs * ts, 256)
    assert n_prompt % tb == 0 and tb % ts == 0

    shared, kp, vp, hp, cp = _mixer(x_prompt, mod_p, None, None, None, None, p, True, n, 0, None)
    (x1, hpk, hlo), ks, vs, hs, cs = _mixer(x_sample, mod_s, cache_k[0], cache_v[0], state_lru[0], state_conv[0],
                                            p, False, n, n_prompt, shared)

    e_k, w_k, slot_k, counts = _router(hpk, hlo, w_router[0], b_router[0], tb)
    tme = EXPERT_ROW_BLOCK
    dest, pad_end, block_expert, n_used, n_rows = _routing_tables(e_k, slot_k, counts, tme)
    dest = dest.T.reshape(-1)
    xs = _dispatch(pad_end, dest, hpk, n_rows, tb, tme, _tile_rows(d))
    ys = _experts(block_expert, n_used, xs, w_gate[0], w_up[0], w_down[0], tme)
    gt2 = jnp.concatenate([jnp.broadcast_to(mod_p[:, None, 5:6, :], (b, t // ts, 1, d)).reshape(-1, 1, d),
                           mod_s[:, 5:6, :]], axis=0)
    y_p, y_s = _combine(dest, w_k.T, ys, x1, hpk, gt2, ws_gate[0].astype(BF16), ws_up[0].astype(BF16),
                        ws_down[0].astype(BF16), g_final, n_prompt, tb)
    return (y_p.reshape(b, t, d), y_s.reshape(bs, ts, d), kp[None], vp[None], hp[None], cp[None],
            ks[None], vs[None], hs[None], cs[None])
```

```python
import functools

import jax
import jax.numpy as jnp
from jax import lax
from jax.experimental import pallas as pl
from jax.experimental.pallas import tpu as pltpu

F32 = jnp.float32
BF16 = jnp.bfloat16
I32 = jnp.int32
U32 = jnp.uint32

LANES = 128
SUBLANES = 8
HEAD_DIM = 128
LRU_BLOCK_DIM = 256
LRU_C = 8.0
CONV_WIDTH = 4
N_EXPERTS = 64
TOP_K = 6
N_GROUPS = 8
GROUP_SIZE = N_EXPERTS // N_GROUPS
TOPK_GROUPS = 4
ROUTED_SCALE = 2.5
NORM_EPS = 1e-6
N_MOD = 6
TIME_GROUP = 16
LOG2_E = 1.4426950408889634
ATTN_HEADS_PER_STEP = 4
F32_ZERO_WEIGHT_LOG2 = -150.0
DEST_STRIDE = 8
EXPERT_ROW_BLOCK = 512
V7X_VMEM_LIMIT_BYTES = 56 * 1024 * 1024
NT_DIMS = (((1,), (1,)), ((), ()))
NEG_INF = float("-inf")


def _cparams(semantics):
    return pltpu.CompilerParams(dimension_semantics=semantics, vmem_limit_bytes=V7X_VMEM_LIMIT_BYTES)


def _dot(a, b):
    return jnp.dot(a, b, preferred_element_type=F32)


def _split_bf16(x):
    hi = x.astype(BF16)
    lo = (x - hi.astype(F32)).astype(BF16)
    return hi, lo


def _dot3(a, b):
    ah, al = _split_bf16(a)
    bh, bl = _split_bf16(b)
    return _dot(ah, bh) + (_dot(al, bh) + _dot(ah, bl))


def _pack_pairs(h):
    n = h.shape[1] // 2
    bits = lax.bitcast_convert_type(h.astype(BF16).astype(F32), U32)
    return bits[:, :n] | (bits[:, n:] >> 16)


def _store_token_tiles(ref, packed):
    m, n = packed.shape
    nsub = n // LANES
    for j in range(nsub):
        ref[pl.ds(j, m, stride=nsub), :] = packed[:, j * LANES:(j + 1) * LANES]


def _tile_rows(d):
    return d // 2 // LANES


def _load_token_tiles(ref, nsub):
    m = ref.shape[0] // nsub
    return jnp.concatenate([ref[pl.ds(j, m, stride=nsub), :] for j in range(nsub)], axis=1)


def _unpack_pairs(p, dtype):
    a = lax.bitcast_convert_type(p & jnp.uint32(0xFFFF0000), F32)
    b = lax.bitcast_convert_type(p << 16, F32)
    return jnp.concatenate([a.astype(dtype), b.astype(dtype)], axis=1)


def _rms(x):
    return x * lax.rsqrt(jnp.mean(x * x, axis=-1, keepdims=True) + NORM_EPS)


def _ada_kernel(c_ref, w_ref, b_ref, o_ref):
    c = c_ref[...]
    o_ref[...] = _dot3(c * jax.nn.sigmoid(c), w_ref[...]) + b_ref[...]


def _ada(c, w, b):
    m, d = c.shape
    n = w.shape[1]
    tn = min(n, 1024)
    return pl.pallas_call(
        _ada_kernel,
        grid=(n // tn,),
        in_specs=[pl.BlockSpec((m, d), lambda j: (0, 0)),
                  pl.BlockSpec((d, tn), lambda j: (0, j)),
                  pl.BlockSpec((1, tn), lambda j: (0, j))],
        out_specs=pl.BlockSpec((m, tn), lambda j: (0, j)),
        out_shape=jax.ShapeDtypeStruct((m, n), F32),
        compiler_params=_cparams(("arbitrary",)),
        name="ada",
    )(c, w, b.reshape(1, n))


def _row_permutation(n_outer, n_inner):
    n = n_outer * n_inner
    out_row = lax.broadcasted_iota(I32, (n, n), 0)
    in_row = lax.broadcasted_iota(I32, (n, n), 1)
    same = (out_row // n_outer == in_row % n_inner) & (out_row % n_outer == in_row // n_inner)
    return same.astype(BF16)


def _norm_mod_kernel(x_ref, mod_ref, g_ref, obm_ref, otm_ref):
    b, tt, d = x_ref.shape
    m = mod_ref[...]
    h = (_rms(x_ref[...]) * g_ref[...].reshape(1, 1, d) * (1.0 + m[:, 1:2, :]) + m[:, 0:1, :]).astype(BF16)
    obm_ref[...] = h
    otm_ref[...] = _dot(_row_permutation(b, tt), h.reshape(b * tt, d)).astype(BF16)


def _norm_mod(x, mod, g):
    b, t, d = x.shape
    tt = TIME_GROUP
    return pl.pallas_call(
        _norm_mod_kernel,
        grid=(t // tt,),
        in_specs=[pl.BlockSpec((b, tt, d), lambda ti: (0, ti, 0)),
                  pl.BlockSpec((b, N_MOD, d), lambda ti: (0, 0, 0)),
                  pl.BlockSpec((1, d), lambda ti: (0, 0))],
        out_specs=[pl.BlockSpec((b, tt, d), lambda ti: (0, ti, 0)),
                   pl.BlockSpec((tt * b, d), lambda ti: (ti, 0))],
        out_shape=[jax.ShapeDtypeStruct((b, t, d), BF16), jax.ShapeDtypeStruct((t * b, d), BF16)],
        compiler_params=_cparams(("arbitrary",)),
        name="norm_mod",
    )(x, mod, g.reshape(1, d))


def _proj_kernel(a_ref, w_ref, o_ref, w_s, *, epilogue):
    @pl.when(pl.program_id(0) == 0)
    def _():
        w_s[...] = w_ref[...].astype(BF16)

    acc = _dot(a_ref[...], w_s[...])
    if epilogue == "gelu":
        acc = jax.nn.gelu(acc)
    elif epilogue == "sigmoid":
        acc = jax.nn.sigmoid(acc)
    elif epilogue == "qscale":
        acc = acc * (HEAD_DIM ** -0.5 * LOG2_E)
    o_ref[...] = acc.astype(o_ref.dtype)


def _resident(shape, index_map):
    return pl.BlockSpec(shape, index_map, pipeline_mode=pl.Buffered(1))


def _proj(a, w, col, width, epilogue, out_dtype, tm):
    m, k = a.shape
    return pl.pallas_call(
        functools.partial(_proj_kernel, epilogue=epilogue),
        grid=(m // tm,),
        in_specs=[pl.BlockSpec((tm, k), lambda i: (i, 0)),
                  _resident((k, width), lambda i: (0, col))],
        out_specs=pl.BlockSpec((tm, width), lambda i: (i, 0)),
        out_shape=jax.ShapeDtypeStruct((m, width), out_dtype),
        scratch_shapes=[pltpu.VMEM((k, width), BF16)],
        compiler_params=_cparams(("arbitrary",)),
        name="proj_" + epilogue,
    )(a, w)


def _lru_kernel(u_ref, gl_ref, sl_ref, cw_ref, cb_ref, wa_ref, ba_ref, wx_ref, bx_ref, lam_ref, h0_ref, prev_ref,
                y_ref, hl_ref, cn_ref, tail_s, h_s, a_s, b_s, y_s, *, first_pos_zero):
    tc, nb, wb = u_ref.shape
    ti = pl.program_id(1)

    @pl.when(ti == 0)
    def _():
        tail_s[...] = prev_ref[...]
        h_s[...] = h0_ref[...]

    u = u_ref[...].astype(F32)
    xp = jnp.concatenate([tail_s[...], u], axis=0)
    cw = cw_ref[...]
    xc = cb_ref[...].reshape(1, 1, wb)
    for j in range(CONV_WIDTH):
        xc = xc + xp[j:j + tc] * cw[j:j + 1].reshape(1, 1, wb)
    last_rows = u[tc - (CONV_WIDTH - 1):]
    tail_s[...] = last_rows
    cn_ref[...] = last_rows

    x2 = xc.reshape(tc * nb, wb)
    xb = x2.astype(BF16)
    r = jax.nn.sigmoid(_dot(xb, wa_ref[0]) + ba_ref[0])
    gi = jax.nn.sigmoid(_dot(xb, wx_ref[0]) + bx_ref[0])
    log_a = r * ((-LRU_C) * jax.nn.softplus(-lam_ref[...]))
    a = jnp.exp(log_a)
    m2 = -jnp.tanh(log_a) * (a * a + 1.0)
    mult = jnp.where(m2 > 0.0, m2 * lax.rsqrt(m2), 0.0)
    if first_pos_zero:
        row = lax.broadcasted_iota(I32, (tc * nb, wb), 0)
        mult = jnp.where((row < nb) & (ti == 0), 1.0, mult)
    a_s[...] = a.reshape(tc, nb, wb)
    b_s[...] = (mult * (gi * x2)).reshape(tc, nb, wb)

    def step(t, h):
        h = a_s[t] * h + b_s[t]
        gate = gl_ref[t].astype(F32) * sl_ref[t].astype(F32)
        y_s[t] = h * gate
        return h

    h = lax.fori_loop(0, tc, step, h_s[...], unroll=8)
    h_s[...] = h
    hl_ref[...] = h

    perm = _row_permutation(TIME_GROUP, nb)
    for g in range(tc // TIME_GROUP):
        rows = slice(g * TIME_GROUP, (g + 1) * TIME_GROUP)
        yg = y_s[rows].reshape(TIME_GROUP * nb, wb).astype(BF16)
        y_ref[:, rows, :] = _dot(perm, yg).astype(BF16).reshape(nb, TIME_GROUP, wb)


def _lru(u, gl, sl, conv_w, conv_b, wa, ba, wx, bx, lam, h0, prev, tc, first_pos_zero):
    t, b, w = u.shape
    wb = LRU_BLOCK_DIM
    nw = w // wb
    seq = pl.BlockSpec((tc, b, wb), lambda wi, ti: (ti, 0, wi))
    seq_out = pl.BlockSpec((b, tc, wb), lambda wi, ti: (0, ti, wi))
    chan = lambda rows: pl.BlockSpec((rows, wb), lambda wi, ti: (0, wi))
    gate_w = pl.BlockSpec((1, wb, wb), lambda wi, ti: (wi, 0, 0))
    gate_b = pl.BlockSpec((1, 1, wb), lambda wi, ti: (wi, 0, 0))
    tail = pl.BlockSpec((CONV_WIDTH - 1, b, wb), lambda wi, ti: (0, 0, wi))
    return pl.pallas_call(
        functools.partial(_lru_kernel, first_pos_zero=first_pos_zero),
        grid=(nw, t // tc),
        in_specs=[seq, seq, seq, chan(CONV_WIDTH), chan(1), gate_w, gate_b, gate_w, gate_b, chan(1),
                  chan(b), tail],
        out_specs=[seq_out, chan(b), tail],
        out_shape=[jax.ShapeDtypeStruct((b, t, w), BF16), jax.ShapeDtypeStruct((b, w), F32),
                   jax.ShapeDtypeStruct((CONV_WIDTH - 1, b, w), F32)],
        scratch_shapes=[pltpu.VMEM((CONV_WIDTH - 1, b, wb), F32), pltpu.VMEM((b, wb), F32),
                        pltpu.VMEM((tc, b, wb), F32), pltpu.VMEM((tc, b, wb), F32),
                        pltpu.VMEM((tc, b, wb), F32)],
        compiler_params=_cparams(("arbitrary", "arbitrary")),
        name="lru",
    )(u, gl, sl, conv_w, conv_b.reshape(1, w), wa, ba.reshape(nw, 1, wb), wx, bx.reshape(nw, 1, wb),
      lam.reshape(1, w), h0, prev)


def _log_keep_and_logsig(z):
    nz = -z
    lk = jnp.minimum(nz, 0.0) - jnp.log2(1.0 + jnp.exp2(jnp.minimum(z, nz)))
    return lk, z + lk


def _later_key_sums(lk, tri):
    hi, lo = _split_bf16(lk)
    return _dot(hi, tri) + _dot(lo, tri)


def _strict_lower(n, dtype):
    row = lax.broadcasted_iota(I32, (n, n), 0)
    col = lax.broadcasted_iota(I32, (n, n), 1)
    return (row > col).astype(dtype), col < row


def _attn_kernel(q_ref, k_ref, v_ref, sz_ref, yl_ref, o_ref):
    tq = q_ref.shape[1]
    nh = q_ref.shape[2] // HEAD_DIM
    qi = pl.program_id(2)
    tri, causal = _strict_lower(tq, BF16)

    def tile(j, carry, diag):
        start = pl.multiple_of(j * tq, tq)
        heads = [slice(h * HEAD_DIM, (h + 1) * HEAD_DIM) for h in range(nh)]
        zs = [lax.dot_general(q_ref[0, :, cols], k_ref[0, pl.ds(start, tq), cols].astype(BF16), NT_DIMS,
                              preferred_element_type=F32) for cols in heads]
        lks, lss = [], []
        for z in zs:
            lk, ls = _log_keep_and_logsig(z)
            lks.append(jnp.where(causal, lk, 0.0) if diag else lk)
            lss.append(ls)
        rcs = [_later_key_sums(lk, tri) for lk in lks]
        ws = []
        for h in range(nh):
            w = jnp.exp2(lss[h] + rcs[h] + carry[h][1])
            ws.append((jnp.where(causal, w, 0.0) if diag else w).astype(BF16))
        return tuple((carry[h][0] + _dot(ws[h], v_ref[0, pl.ds(start, tq), heads[h]].astype(BF16)),
                      carry[h][1] + (rcs[h][:, 0:1] + lks[h][:, 0:1])) for h in range(nh))

    def any_weight_left(carry):
        c_max = carry[0][1]
        for h in range(1, nh):
            c_max = jnp.maximum(c_max, carry[h][1])
        return jnp.max(c_max) > F32_ZERO_WEIGHT_LOG2

    init = tuple((jnp.zeros((tq, HEAD_DIM), F32), jnp.zeros((tq, 1), F32)) for _ in range(nh))
    carry = tile(qi, init, True)

    def more(state):
        return (state[0] < qi) & state[1]

    def farther_tile(state):
        carry = tile(qi - 1 - state[0], state[2], False)
        return state[0] + 1, any_weight_left(carry), carry

    carry = lax.while_loop(more, farther_tile, (jnp.int32(0), any_weight_left(carry), carry))[2]
    acc = jnp.concatenate([carry[h][0] for h in range(nh)], axis=1)
    o_ref[0] = (yl_ref[0].astype(F32) + sz_ref[0].astype(F32) * acc).astype(BF16)


def _attn_prompt(q, k, v, sz, yl, tq):
    b, t, a = q.shape
    wblk = ATTN_HEADS_PER_STEP * HEAD_DIM
    qspec = pl.BlockSpec((1, tq, wblk), lambda bi, hi, qi: (bi, qi, hi))
    kvspec = pl.BlockSpec((1, t, wblk), lambda bi, hi, qi: (bi, 0, hi))
    return pl.pallas_call(
        _attn_kernel,
        grid=(b, a // wblk, t // tq),
        in_specs=[qspec, kvspec, kvspec, qspec, qspec],
        out_specs=qspec,
        out_shape=jax.ShapeDtypeStruct((b, t, a), BF16),
        compiler_params=_cparams(("arbitrary", "arbitrary", "arbitrary")),
        name="attn_prompt",
    )(q, k, v, sz, yl)


def _attn_sample_kernel(q_ref, kn_ref, vn_ref, kc_ref, vc_ref, sz_ref, yl_ref, o_ref, lk_s, ls_s):
    ts = q_ref.shape[1]
    nh = q_ref.shape[2] // HEAD_DIM
    past = kc_ref.shape[1]
    tri_n, causal = _strict_lower(ts, BF16)
    tri_p, _ = _strict_lower(past, BF16)
    accs, carries = [], []
    for h in range(nh):
        cols = slice(h * HEAD_DIM, (h + 1) * HEAD_DIM)
        q = q_ref[0, :, cols]
        zn = lax.dot_general(q, kn_ref[0, :, cols].astype(BF16), NT_DIMS, preferred_element_type=F32)
        lk, ls = _log_keep_and_logsig(zn)
        lk = jnp.where(causal, lk, 0.0)
        rc = _later_key_sums(lk, tri_n)
        w = jnp.where(causal, jnp.exp2(ls + rc), 0.0)
        accs.append(_dot(w.astype(BF16), vn_ref[0, :, cols].astype(BF16)))
        carries.append(rc[:, 0:1] + lk[:, 0:1])
        zp = lax.dot_general(q, kc_ref[0, :, h, :].astype(BF16), NT_DIMS, preferred_element_type=F32)
        lkp, lsp = _log_keep_and_logsig(zp)
        lk_s[h * ts:(h + 1) * ts, :] = lkp
        ls_s[h * ts:(h + 1) * ts, :] = lsp
    rcp = _later_key_sums(lk_s[...], tri_p)
    for h in range(nh):
        cols = slice(h * HEAD_DIM, (h + 1) * HEAD_DIM)
        rows = slice(h * ts, (h + 1) * ts)
        w = jnp.exp2(ls_s[rows, :] + rcp[rows, :] + carries[h])
        acc = accs[h] + _dot(w.astype(BF16), vc_ref[0, :, h, :].astype(BF16))
        o_ref[0, :, cols] = (yl_ref[0, :, cols].astype(F32) + sz_ref[0, :, cols].astype(F32) * acc).astype(BF16)


def _attn_sample(q, kn, vn, kc, vc, sz, yl):
    b, ts, a = q.shape
    past = kc.shape[1]
    nh = a // HEAD_DIM
    cur = pl.BlockSpec((1, ts, a), lambda bi: (bi, 0, 0))
    cache = pl.BlockSpec((1, past, nh, HEAD_DIM), lambda bi: (bi, 0, 0, 0))
    return pl.pallas_call(
        _attn_sample_kernel,
        grid=(b,),
        in_specs=[cur, cur, cur, cache, cache, cur, cur],
        out_specs=cur,
        out_shape=jax.ShapeDtypeStruct((b, ts, a), BF16),
        scratch_shapes=[pltpu.VMEM((nh * ts, past), F32), pltpu.VMEM((nh * ts, past), F32)],
        compiler_params=_cparams(("arbitrary",)),
        name="attn_sample",
    )(q, kn, vn, kc, vc, sz, yl)


def _outproj_kernel(a_ref, w_ref, x_ref, mod_ref, g_ref, *rest):
    x1_ref, hpk_ref, hlo_ref, w_s = rest[-4:]
    nb = mod_ref.shape[0]
    tm, d = x_ref.shape

    @pl.when(pl.program_id(0) == 0)
    def _():
        w_s[...] = w_ref[...].astype(BF16)

    m = mod_ref[...]
    o = _dot(a_ref[...], w_s[...]).reshape(nb, tm // nb, d)
    x1 = x_ref[...].reshape(nb, tm // nb, d) + m[:, 2:3, :] * o
    h = _rms(x1) * g_ref[...].reshape(1, 1, d) * (1.0 + m[:, 4:5, :]) + m[:, 3:4, :]
    x1_ref[...] = x1.reshape(tm, d)
    h = h.reshape(tm, d)
    hi = h.astype(BF16).astype(F32)
    _store_token_tiles(hpk_ref, _pack_pairs(hi))
    hlo_ref[...] = (h - hi).astype(BF16)


def _outproj(a, w, x, mod, g, tm, rows_per_batch, n_total, row_offset, earlier=None):
    m, d = x.shape
    nb = max(tm // rows_per_batch, 1)
    per = max(rows_per_batch // tm, 1)
    off = row_offset // tm
    nsub = _tile_rows(d)
    row = lambda width: pl.BlockSpec((tm, width), lambda i: (i, 0))
    out_row = lambda width: pl.BlockSpec((tm, width), lambda i: (i + off, 0))
    in_specs = [row(d), _resident((d, d), lambda i: (0, 0)), row(d),
                pl.BlockSpec((nb, N_MOD, d), lambda i: (i // per, 0, 0)),
                pl.BlockSpec((1, d), lambda i: (0, 0))]
    args = [a, w, x, mod, g.reshape(1, d)]
    aliases = {}
    if earlier is not None:
        aliases = {len(args) + j: j for j in range(len(earlier))}
        in_specs += [pl.BlockSpec(memory_space=pl.ANY)] * len(earlier)
        args += list(earlier)
    return pl.pallas_call(
        _outproj_kernel,
        grid=(m // tm,),
        in_specs=in_specs,
        out_specs=[out_row(d), pl.BlockSpec((tm * nsub, LANES), lambda i: (i + off, 0)), out_row(d)],
        out_shape=[jax.ShapeDtypeStruct((n_total, d), F32), jax.ShapeDtypeStruct((n_total * nsub, LANES), U32),
                   jax.ShapeDtypeStruct((n_total, d), BF16)],
        scratch_shapes=[pltpu.VMEM((d, d), BF16)],
        input_output_aliases=aliases,
        compiler_params=_cparams(("arbitrary",)),
        name="outproj",
    )(*args)


def _sub_max(x):
    return jnp.max(x, axis=0, keepdims=True)


def _sub_min(x):
    return jnp.min(x, axis=0, keepdims=True)


def _sub_sum(x):
    return jnp.sum(x, axis=0, keepdims=True)


def _router_kernel(hpk_ref, hlo_ref, wrh_ref, wrl_ref, br_ref, e_ref, w_ref, slot_ref, cnt_ref, base_s):
    tm = hlo_ref.shape[0]

    @pl.when(pl.program_id(0) == 0)
    def _():
        base_s[...] = jnp.zeros_like(base_s)

    hi = _unpack_pairs(_load_token_tiles(hpk_ref, _tile_rows(hlo_ref.shape[1])), BF16)
    lo = hlo_ref[...]
    wrh = wrh_ref[...]
    logits = lax.dot_general(wrh, hi, NT_DIMS, preferred_element_type=F32)
    logits = logits + (lax.dot_general(wrh, lo, NT_DIMS, preferred_element_type=F32)
                       + lax.dot_general(wrl_ref[...], hi, NT_DIMS, preferred_element_type=F32))
    score = jax.nn.sigmoid(logits)
    sel = (score + br_ref[...]).reshape(GROUP_SIZE, N_GROUPS, tm)
    score = score.reshape(GROUP_SIZE, N_GROUPS, tm)

    m1 = sel[0]
    m2 = jnp.full_like(m1, NEG_INF)
    for e in range(1, GROUP_SIZE):
        x = sel[e]
        m2 = jnp.maximum(m2, jnp.minimum(m1, x))
        m1 = jnp.maximum(m1, x)
    cur = m1 + m2
    gidx = lax.broadcasted_iota(I32, (N_GROUPS, tm), 0)
    gmask = jnp.zeros((N_GROUPS, tm), jnp.bool_)
    for _ in range(TOPK_GROUPS):
        first = _sub_min(jnp.where(cur == _sub_max(cur), gidx, N_GROUPS))
        pick = gidx == first
        gmask = gmask | pick
        cur = jnp.where(pick, NEG_INF, cur)

    eid = (lax.broadcasted_iota(I32, (GROUP_SIZE, N_GROUPS, tm), 1) * GROUP_SIZE
           + lax.broadcasted_iota(I32, (GROUP_SIZE, N_GROUPS, tm), 0))
    cur = jnp.where(gmask[None], sel, NEG_INF)
    ids, wts, picks = [], [], []
    for _ in range(TOP_K):
        mx = _sub_max(jnp.max(cur, axis=0))
        first = _sub_min(jnp.min(jnp.where(cur == mx[None], eid, N_EXPERTS), axis=0))
        pick = eid == first[None]
        ids.append(first)
        picks.append(pick)
        wts.append(_sub_sum(jnp.sum(jnp.where(pick, score, 0.0), axis=0)))
        cur = jnp.where(pick, NEG_INF, cur)
    total = wts[0]
    chosen = picks[0]
    for k in range(1, TOP_K):
        total = total + wts[k]
        chosen = chosen | picks[k]

    chosen = jnp.where(chosen, 1.0, 0.0).reshape(N_EXPERTS, tm)
    earlier_or_self = (lax.broadcasted_iota(I32, (tm, tm), 0) <= lax.broadcasted_iota(I32, (tm, tm), 1)).astype(BF16)
    incl = _dot(chosen.astype(BF16), earlier_or_self)
    rank = (base_s[...] + (incl - chosen)).reshape(GROUP_SIZE, N_GROUPS, tm)
    base_s[...] = base_s[...] + incl[:, tm - 1:tm]
    slots = [_sub_sum(jnp.sum(jnp.where(pick, rank, 0.0), axis=0)).astype(I32) for pick in picks]

    pad = 8 - TOP_K
    e_ref[...] = jnp.concatenate(ids + [jnp.zeros((pad, tm), I32)], axis=0)
    w_ref[...] = jnp.concatenate([w / total * ROUTED_SCALE for w in wts] + [jnp.zeros((pad, tm), F32)], axis=0)
    slot_ref[...] = jnp.concatenate(slots + [jnp.zeros((pad, tm), I32)], axis=0)
    cnt_ref[...] = jnp.broadcast_to(base_s[...], cnt_ref.shape)


def _router(hpk, hlo, w_router, b_router, tm):
    n, d = hlo.shape
    wr = w_router.T.reshape(N_GROUPS, GROUP_SIZE, d).transpose(1, 0, 2).reshape(N_EXPERTS, d)
    br = b_router.reshape(N_GROUPS, GROUP_SIZE).T.reshape(N_EXPERTS, 1)
    wrh = wr.astype(BF16)
    wrl = (wr - wrh.astype(F32)).astype(BF16)
    full = lambda shape: pl.BlockSpec(shape, lambda i: (0, 0))
    per_token = pl.BlockSpec((8, tm), lambda i: (0, i))
    e_k, w_k, slot_k, counts = pl.pallas_call(
        _router_kernel,
        grid=(n // tm,),
        in_specs=[pl.BlockSpec((tm * _tile_rows(d), LANES), lambda i: (i, 0)),
                  pl.BlockSpec((tm, d), lambda i: (i, 0)),
                  full((N_EXPERTS, d)), full((N_EXPERTS, d)), full((N_EXPERTS, 1))],
        out_specs=[per_token, per_token, per_token, full((N_EXPERTS, 128))],
        out_shape=[jax.ShapeDtypeStruct((8, n), I32), jax.ShapeDtypeStruct((8, n), F32),
                   jax.ShapeDtypeStruct((8, n), I32), jax.ShapeDtypeStruct((N_EXPERTS, 128), F32)],
        scratch_shapes=[pltpu.VMEM((N_EXPERTS, 1), F32)],
        compiler_params=_cparams(("arbitrary",)),
        name="router",
    )(hpk, hlo, wrh, wrl, br)
    counts = counts[:, 0].reshape(GROUP_SIZE, N_GROUPS).T.reshape(N_EXPERTS).astype(I32)
    return e_k, w_k, slot_k, counts


def _tile_window(token, nsub):
    start = token * nsub
    if nsub % SUBLANES == 0:
        start = pl.multiple_of(start, SUBLANES)
    return pl.ds(start, nsub)


def _token_copy(src, src_token, dst, dst_token, nsub, sem):
    return pltpu.make_async_copy(src.at[_tile_window(src_token, nsub)], dst.at[_tile_window(dst_token, nsub)], sem)


def _for_token_groups(n_tokens, per_token):
    def group(g, carry):
        t0 = pl.multiple_of(g * SUBLANES, SUBLANES)
        for u in range(SUBLANES):
            per_token(t0, u)
        return carry

    lax.fori_loop(0, n_tokens // SUBLANES, group, 0)


def _dispatch_kernel(pad_end_ref, dest_ref, h_ref, xs_ref, zero_s, sem, zero_sem, *, nsub):
    tb = dest_ref.shape[0] // DEST_STRIDE
    te = zero_s.shape[0] // nsub

    def tail_fill(e):
        start = pl.multiple_of((pad_end_ref[e] - te) * nsub, te * nsub)
        return pltpu.make_async_copy(zero_s, xs_ref.at[pl.ds(start, te * nsub)], zero_sem)

    def has_rows(e):
        return pad_end_ref[e] > (pad_end_ref[e - 1] if e else 0)

    @pl.when(pl.program_id(0) == 0)
    def _():
        zero_s[...] = jnp.zeros_like(zero_s)
        for e in range(N_EXPERTS):
            pl.when(has_rows(e))(lambda e=e: tail_fill(e).start())
        for e in range(N_EXPERTS):
            pl.when(has_rows(e))(lambda e=e: tail_fill(e).wait())

    def start(t0, u):
        for k in range(TOP_K):
            _token_copy(h_ref, t0 + u, xs_ref, dest_ref[t0 * DEST_STRIDE + (u * DEST_STRIDE + k)], nsub,
                        sem).start(priority=k % 2)

    def wait(t0, u):
        for k in range(TOP_K):
            _token_copy(h_ref, 0, xs_ref, 0, nsub, sem).wait()

    _for_token_groups(tb, start)
    _for_token_groups(tb, wait)


def _dispatch(pad_end, dest, hpk, n_rows, tb, te, nsub):
    n = hpk.shape[0] // nsub
    grid_spec = pltpu.PrefetchScalarGridSpec(
        num_scalar_prefetch=1,
        grid=(n // tb,),
        in_specs=[pl.BlockSpec((tb * DEST_STRIDE,), lambda i, pe: (i,), memory_space=pltpu.SMEM),
                  pl.BlockSpec((tb * nsub, LANES), lambda i, pe: (i, 0))],
        out_specs=pl.BlockSpec(memory_space=pl.ANY),
        scratch_shapes=[pltpu.VMEM((te * nsub, LANES), U32), pltpu.SemaphoreType.DMA(()),
                        pltpu.SemaphoreType.DMA(())],
    )
    return pl.pallas_call(
        functools.partial(_dispatch_kernel, nsub=nsub),
        grid_spec=grid_spec,
        out_shape=jax.ShapeDtypeStruct((n_rows * nsub, LANES), U32),
        compiler_params=_cparams(("arbitrary",)),
        name="dispatch",
    )(pad_end, dest, hpk)


def _experts_kernel(be_ref, nu_ref, x_ref, wg_ref, wu_ref, wd_ref, o_ref, wg_s, wu_s, wd_s):
    i = pl.program_id(0)
    prev = be_ref[jnp.maximum(i - 1, 0)]

    @pl.when((i == 0) | (be_ref[i] != prev))
    def _():
        wg_s[...] = wg_ref[0].astype(BF16)
        wu_s[...] = wu_ref[0].astype(BF16)
        wd_s[...] = wd_ref[0].astype(BF16)

    @pl.when(i < nu_ref[0])
    def _():
        x = _unpack_pairs(_load_token_tiles(x_ref, _tile_rows(wg_s.shape[0])), BF16)
        g = _dot(x, wg_s[...])
        hidden = (g * jax.nn.sigmoid(g)) * _dot(x, wu_s[...])
        _store_token_tiles(o_ref, _pack_pairs(_dot(hidden.astype(BF16), wd_s[...])))

    @pl.when(i >= nu_ref[0])
    def _():
        o_ref[...] = jnp.zeros_like(o_ref)


def _experts(block_expert, n_used, xs, w_gate, w_up, w_down, tm):
    d, ff = w_gate.shape[1:]
    nsub = _tile_rows(d)
    p = xs.shape[0] // nsub
    rows = pl.BlockSpec((tm * nsub, LANES), lambda i, be, nu: (i, 0))
    grid_spec = pltpu.PrefetchScalarGridSpec(
        num_scalar_prefetch=2,
        grid=(p // tm,),
        in_specs=[rows,
                  pl.BlockSpec((1, d, ff), lambda i, be, nu: (be[i], 0, 0)),
                  pl.BlockSpec((1, d, ff), lambda i, be, nu: (be[i], 0, 0)),
                  pl.BlockSpec((1, ff, d), lambda i, be, nu: (be[i], 0, 0))],
        out_specs=rows,
        scratch_shapes=[pltpu.VMEM((d, ff), BF16), pltpu.VMEM((d, ff), BF16), pltpu.VMEM((ff, d), BF16)],
    )
    return pl.pallas_call(
        _experts_kernel,
        grid_spec=grid_spec,
        out_shape=jax.ShapeDtypeStruct(xs.shape, U32),
        compiler_params=_cparams(("arbitrary",)),
        name="experts",
    )(block_expert, n_used, xs, w_gate, w_up, w_down)


def _combine_kernel(dest_ref, dest_next_ref, wk_ref, ys_ref, x1_ref, hpk_ref, gt_ref, wsg_ref, wsu_ref, wsd_ref,
                    gf_ref, yp_ref, ysm_ref, buf, sem, *, n_prompt_blocks, n_blocks):
    tb, d = x1_ref.shape
    nsub = _tile_rows(d)
    seg = gt_ref.shape[0]
    i = pl.program_id(0)
    cur = i % 2
    nxt = 1 - cur

    def fetch(dest, slot, t0, u):
        for k in range(TOP_K):
            _token_copy(ys_ref, dest[t0 * DEST_STRIDE + (u * DEST_STRIDE + k)], buf.at[slot, k], t0 + u, nsub,
                        sem.at[slot]).start(priority=k % 2)

    def wait_tiles(slot):
        def one(t0, u):
            for k in range(TOP_K):
                _token_copy(ys_ref, 0, buf.at[slot, k], 0, nsub, sem.at[slot]).wait()
        _for_token_groups(tb, one)

    @pl.when(i == 0)
    def _():
        _for_token_groups(tb, functools.partial(fetch, dest_ref, 0))

    _for_token_groups(tb, functools.partial(fetch, dest_next_ref, nxt))
    x = _unpack_pairs(_load_token_tiles(hpk_ref, nsub), BF16)
    g = _dot(x, wsg_ref[...])
    ffn = _dot(((g * jax.nn.sigmoid(g)) * _dot(x, wsu_ref[...])).astype(BF16), wsd_ref[...])
    wait_tiles(cur)
    wk = wk_ref[...]
    for k in range(TOP_K):
        ffn = ffn + wk[:, k:k + 1] * _unpack_pairs(_load_token_tiles(buf.at[cur, k], nsub), F32)
    x2 = x1_ref[...].reshape(seg, tb // seg, d) + gt_ref[...] * ffn.reshape(seg, tb // seg, d)
    y = (_rms(x2) * gf_ref[...].reshape(1, 1, d)).reshape(tb, d)

    @pl.when(i == n_blocks - 1)
    def _():
        wait_tiles(nxt)

    @pl.when(i < n_prompt_blocks)
    def _():
        yp_ref[...] = y

    @pl.when(i >= n_prompt_blocks)
    def _():
        ysm_ref[...] = y


def _combine(dest, wk_t, ys, x1, hpk, gt2_seg, ws_gate, ws_up, ws_down, g_final, n_prompt, tb):
    n, d = x1.shape
    nsub = _tile_rows(d)
    ff = ws_gate.shape[1]
    npb = n_prompt // tb
    nsb = (n - n_prompt) // tb
    seg = gt2_seg.shape[0] * tb // n
    nb = n // tb
    row = lambda width: pl.BlockSpec((tb, width), lambda i: (i, 0))
    full = lambda shape: pl.BlockSpec(shape, lambda i: (0,) * len(shape))
    return pl.pallas_call(
        functools.partial(_combine_kernel, n_prompt_blocks=npb, n_blocks=nb),
        grid=(nb,),
        in_specs=[pl.BlockSpec((tb * DEST_STRIDE,), lambda i: (i,), memory_space=pltpu.SMEM),
                  pl.BlockSpec((tb * DEST_STRIDE,), lambda i: (jnp.minimum(i + 1, nb - 1),), memory_space=pltpu.SMEM),
                  row(8), pl.BlockSpec(memory_space=pl.ANY), row(d),
                  pl.BlockSpec((tb * nsub, LANES), lambda i: (i, 0)),
                  pl.BlockSpec((seg, 1, d), lambda i: (i, 0, 0)),
                  full((d, ff)), full((d, ff)), full((ff, d)), full((1, d))],
        out_specs=[pl.BlockSpec((tb, d), lambda i: (jnp.minimum(i, npb - 1), 0)),
                   pl.BlockSpec((tb, d), lambda i: (jnp.clip(i - npb, 0, nsb - 1), 0))],
        out_shape=[jax.ShapeDtypeStruct((n_prompt, d), F32), jax.ShapeDtypeStruct((n - n_prompt, d), F32)],
        scratch_shapes=[pltpu.VMEM((2, TOP_K, tb * nsub, LANES), U32), pltpu.SemaphoreType.DMA((2,))],
        compiler_params=_cparams(("arbitrary",)),
        name="combine",
    )(dest, dest, wk_t, ys, x1, hpk, gt2_seg, ws_gate, ws_up, ws_down, g_final.reshape(1, d))


def _largest_tile(n, target):
    t = min(n, target)
    while n % t:
        t -= 1
    return t


def _mixer(x, mod, past_k, past_v, h0, conv0, p, first_pos_zero, n_total, row_offset, earlier):
    b, t, d = x.shape
    m = b * t
    a = d
    hn_bm, hn_tm = _norm_mod(x, mod, p["g_mix"])
    hn_bm = hn_bm.reshape(m, d)
    tm = _largest_tile(m, 512)
    tmh = _largest_tile(m, 1024)
    w_in = p["w_in"]
    u = _proj(hn_tm, w_in, 0, d, "plain", BF16, tmh).reshape(t, b, d)
    gl = _proj(hn_tm, w_in, 1, d, "gelu", BF16, tmh).reshape(t, b, d)
    sl = _proj(hn_tm, w_in, 5, d, "sigmoid", BF16, tmh).reshape(t, b, d)
    q = _proj(hn_bm, w_in, 2, a, "qscale", BF16, tmh).reshape(b, t, a)
    k = _proj(hn_bm, w_in, 3, a, "plain", F32, tm).reshape(b, t, a)
    v = _proj(hn_bm, w_in, 4, a, "plain", F32, tm).reshape(b, t, a)
    sz = _proj(hn_bm, w_in, 6, a, "sigmoid", BF16, tmh).reshape(b, t, a)

    if h0 is None:
        h0 = jnp.zeros((b, d), F32)
        prev = jnp.zeros((CONV_WIDTH - 1, b, d), F32)
    else:
        prev = conv0.transpose(1, 0, 2)
    tc = _largest_tile(t, 128)
    yl, h_last, conv_tail = _lru(u, gl, sl, p["conv_w"], p["conv_b"], p["w_rg_a"], p["b_rg_a"], p["w_rg_x"],
                                 p["b_rg_x"], p["lru_lambda"], h0, prev, tc, first_pos_zero)
    if past_k is None:
        mixed = _attn_prompt(q, k, v, sz, yl, _largest_tile(t, 256))
    else:
        mixed = _attn_sample(q, k, v, past_k, past_v, sz, yl)
    tmo = _largest_tile(m, 256)
    shared = _outproj(mixed.reshape(m, a), p["w_out"], x.reshape(m, d), mod, p["g_ffn"], tmo, t, n_total, row_offset,
                      earlier)
    nh = a // HEAD_DIM
    return (shared, k.reshape(b, t, nh, HEAD_DIM), v.reshape(b, t, nh, HEAD_DIM), h_last,
            conv_tail.transpose(1, 0, 2))


def _routing_tables(e_k, slot_k, counts, tm):
    n_tokens = e_k.shape[1]
    padded = (counts + tm - 1) // tm * tm
    pad_end = jnp.cumsum(padded).astype(I32)
    pad_start = pad_end - padded
    is_expert = e_k[:, :, None] == jnp.arange(N_EXPERTS, dtype=I32)
    dest = slot_k + jnp.sum(jnp.where(is_expert, pad_start, 0), axis=-1).astype(I32)
    n_blocks = -(-(TOP_K * n_tokens + N_EXPERTS * (tm - 1)) // tm)
    block_start = jnp.arange(n_blocks, dtype=I32) * tm
    block_expert = jnp.minimum(jnp.sum(pad_end[None, :] <= block_start[:, None], axis=1), N_EXPERTS - 1).astype(I32)
    n_used = (pad_end[-1:] // tm).astype(I32)
    return dest, pad_end, block_expert, n_used, n_blocks * tm


def kernel(x_prompt, x_sample, cache_k, cache_v, state_lru, state_conv, c_prompt, c_sample, g_mix, w_ada, b_ada, w_in, conv_w, conv_b, w_rg_a, b_rg_a, w_rg_x, b_rg_x, lru_lambda, w_out, g_ffn, w_router, b_router, w_gate, w_up, w_down, ws_gate, ws_up, ws_down, g_final):
    assert g_mix.shape[0] == 1, "single-layer trunk only"
    b, t, d = x_prompt.shape
    bs, ts, _ = x_sample.shape
    assert ts >= CONV_WIDTH - 1 and ts == TIME_GROUP and t % TIME_GROUP == 0
    p = dict(g_mix=g_mix[0], w_in=w_in[0], conv_w=conv_w[0], conv_b=conv_b[0],
             w_rg_a=w_rg_a[0].astype(BF16), b_rg_a=b_rg_a[0], w_rg_x=w_rg_x[0].astype(BF16), b_rg_x=b_rg_x[0],
             lru_lambda=lru_lambda[0], w_out=w_out[0], g_ffn=g_ffn[0])
    mod = _ada(jnp.concatenate([c_prompt, c_sample], axis=0), w_ada[0], b_ada[0]).reshape(b + bs, N_MOD, d)
    mod_p, mod_s = mod[:b], mod[b:]

    n_prompt = b * t
    n = n_prompt + bs * ts
    tb = _largest_tile(bs * ts, 256)
    assert n_prompt % tb == 0 and tb % ts == 0

    shared, kp, vp, hp, cp = _mixer(x_prompt, mod_p, None, None, None, None, p, True, n, 0, None)
    (x1, hpk, hlo), ks, vs, hs, cs = _mixer(x_sample, mod_s, cache_k[0], cache_v[0], state_lru[0], state_conv[0],
                                            p, False, n, n_prompt, shared)

    e_k, w_k, slot_k, counts = _router(hpk, hlo, w_router[0], b_router[0], tb)
    tme = EXPERT_ROW_BLOCK
    dest, pad_end, block_expert, n_used, n_rows = _routing_tables(e_k, slot_k, counts, tme)
    dest = dest.T.reshape(-1)
    xs = _dispatch(pad_end, dest, hpk, n_rows, tb, tme, _tile_rows(d))
    ys = _experts(block_expert, n_used, xs, w_gate[0], w_up[0], w_down[0], tme)
    gt2 = jnp.concatenate([jnp.broadcast_to(mod_p[:, None, 5:6, :], (b, t // ts, 1, d)).reshape(-1, 1, d),
                           mod_s[:, 5:6, :]], axis=0)
    y_p, y_s = _combine(dest, w_k.T, ys, x1, hpk, gt2, ws_gate[0].astype(BF16), ws_up[0].astype(BF16),
                        ws_down[0].astype(BF16), g_final, n_prompt, tb)
    return (y_p.reshape(b, t, d), y_s.reshape(bs, ts, d), kp[None], vp[None], hp[None], cp[None],
            ks[None], vs[None], hs[None], cs[None])
```

```python
import functools

import jax
import jax.numpy as jnp
from jax import lax
from jax.experimental import pallas as pl
from jax.experimental.pallas import tpu as pltpu

F32 = jnp.float32
BF16 = jnp.bfloat16
I32 = jnp.int32
U32 = jnp.uint32

LANES = 128
SUBLANES = 8
HEAD_DIM = 128
LRU_BLOCK_DIM = 256
LRU_C = 8.0
CONV_WIDTH = 4
N_EXPERTS = 64
TOP_K = 6
N_GROUPS = 8
GROUP_SIZE = N_EXPERTS // N_GROUPS
TOPK_GROUPS = 4
ROUTED_SCALE = 2.5
NORM_EPS = 1e-6
N_MOD = 6
TIME_GROUP = 16
LOG2_E = 1.4426950408889634
ATTN_HEADS_PER_STEP = 4
F32_ZERO_WEIGHT_LOG2 = -150.0
DEST_STRIDE = 8
EXPERT_ROW_BLOCK = 512
V7X_VMEM_LIMIT_BYTES = 56 * 1024 * 1024
NT_DIMS = (((1,), (1,)), ((), ()))
NEG_INF = float("-inf")


def _cparams(semantics):
    return pltpu.CompilerParams(dimension_semantics=semantics, vmem_limit_bytes=V7X_VMEM_LIMIT_BYTES)


def _dot(a, b):
    return jnp.dot(a, b, preferred_element_type=F32)


def _split_bf16(x):
    hi = x.astype(BF16)
    lo = (x - hi.astype(F32)).astype(BF16)
    return hi, lo


def _dot3(a, b):
    ah, al = _split_bf16(a)
    bh, bl = _split_bf16(b)
    return _dot(ah, bh) + (_dot(al, bh) + _dot(ah, bl))


def _pack_pairs(h):
    n = h.shape[1] // 2
    bits = lax.bitcast_convert_type(h.astype(BF16).astype(F32), U32)
    return bits[:, :n] | (bits[:, n:] >> 16)


def _store_token_tiles(ref, packed):
    m, n = packed.shape
    nsub = n // LANES
    for j in range(nsub):
        ref[pl.ds(j, m, stride=nsub), :] = packed[:, j * LANES:(j + 1) * LANES]


def _tile_rows(d):
    return d // 2 // LANES


def _load_token_tiles(ref, nsub):
    m = ref.shape[0] // nsub
    return jnp.concatenate([ref[pl.ds(j, m, stride=nsub), :] for j in range(nsub)], axis=1)


def _unpack_pairs(p, dtype):
    a = lax.bitcast_convert_type(p & jnp.uint32(0xFFFF0000), F32)
    b = lax.bitcast_convert_type(p << 16, F32)
    return jnp.concatenate([a.astype(dtype), b.astype(dtype)], axis=1)


def _rms(x):
    return x * lax.rsqrt(jnp.mean(x * x, axis=-1, keepdims=True) + NORM_EPS)


def _ada_kernel(c_ref, w_ref, b_ref, o_ref):
    c = c_ref[...]
    o_ref[...] = _dot3(c * jax.nn.sigmoid(c), w_ref[...]) + b_ref[...]


def _ada(c, w, b):
    m, d = c.shape
    n = w.shape[1]
    tn = min(n, 1024)
    return pl.pallas_call(
        _ada_kernel,
        grid=(n // tn,),
        in_specs=[pl.BlockSpec((m, d), lambda j: (0, 0)),
                  pl.BlockSpec((d, tn), lambda j: (0, j)),
                  pl.BlockSpec((1, tn), lambda j: (0, j))],
        out_specs=pl.BlockSpec((m, tn), lambda j: (0, j)),
        out_shape=jax.ShapeDtypeStruct((m, n), F32),
        compiler_params=_cparams(("arbitrary",)),
        name="ada",
    )(c, w, b.reshape(1, n))


def _row_permutation(n_outer, n_inner):
    n = n_outer * n_inner
    out_row = lax.broadcasted_iota(I32, (n, n), 0)
    in_row = lax.broadcasted_iota(I32, (n, n), 1)
    same = (out_row // n_outer == in_row % n_inner) & (out_row % n_outer == in_row // n_inner)
    return same.astype(BF16)


def _norm_mod_kernel(x_ref, mod_ref, g_ref, obm_ref, otm_ref):
    b, tt, d = x_ref.shape
    m = mod_ref[...]
    h = (_rms(x_ref[...]) * g_ref[...].reshape(1, 1, d) * (1.0 + m[:, 1:2, :]) + m[:, 0:1, :]).astype(BF16)
    obm_ref[...] = h
    otm_ref[...] = _dot(_row_permutation(b, tt), h.reshape(b * tt, d)).astype(BF16)


def _norm_mod(x, mod, g):
    b, t, d = x.shape
    tt = TIME_GROUP
    return pl.pallas_call(
        _norm_mod_kernel,
        grid=(t // tt,),
        in_specs=[pl.BlockSpec((b, tt, d), lambda ti: (0, ti, 0)),
                  pl.BlockSpec((b, N_MOD, d), lambda ti: (0, 0, 0)),
                  pl.BlockSpec((1, d), lambda ti: (0, 0))],
        out_specs=[pl.BlockSpec((b, tt, d), lambda ti: (0, ti, 0)),
                   pl.BlockSpec((tt * b, d), lambda ti: (ti, 0))],
        out_shape=[jax.ShapeDtypeStruct((b, t, d), BF16), jax.ShapeDtypeStruct((t * b, d), BF16)],
        compiler_params=_cparams(("arbitrary",)),
        name="norm_mod",
    )(x, mod, g.reshape(1, d))


def _proj_kernel(a_ref, w_ref, o_ref, w_s, *, epilogue):
    @pl.when(pl.program_id(0) == 0)
    def _():
        w_s[...] = w_ref[...].astype(BF16)

    acc = _dot(a_ref[...], w_s[...])
    if epilogue == "gelu":
        acc = jax.nn.gelu(acc)
    elif epilogue == "sigmoid":
        acc = jax.nn.sigmoid(acc)
    elif epilogue == "qscale":
        acc = acc * (HEAD_DIM ** -0.5 * LOG2_E)
    o_ref[...] = acc.astype(o_ref.dtype)


def _resident(shape, index_map):
    return pl.BlockSpec(shape, index_map, pipeline_mode=pl.Buffered(1))


def _proj(a, w, col, width, epilogue, out_dtype, tm):
    m, k = a.shape
    return pl.pallas_call(
        functools.partial(_proj_kernel, epilogue=epilogue),
        grid=(m // tm,),
        in_specs=[pl.BlockSpec((tm, k), lambda i: (i, 0)),
                  _resident((k, width), lambda i: (0, col))],
        out_specs=pl.BlockSpec((tm, width), lambda i: (i, 0)),
        out_shape=jax.ShapeDtypeStruct((m, width), out_dtype),
        scratch_shapes=[pltpu.VMEM((k, width), BF16)],
        compiler_params=_cparams(("arbitrary",)),
        name="proj_" + epilogue,
    )(a, w)


def _lru_kernel(u_ref, gl_ref, sl_ref, cw_ref, cb_ref, wa_ref, ba_ref, wx_ref, bx_ref, lam_ref, h0_ref, prev_ref,
                y_ref, hl_ref, cn_ref, tail_s, h_s, a_s, b_s, y_s, *, first_pos_zero):
    tc, nb, wb = u_ref.shape
    ti = pl.program_id(1)

    @pl.when(ti == 0)
    def _():
        tail_s[...] = prev_ref[...]
        h_s[...] = h0_ref[...]

    u = u_ref[...].astype(F32)
    xp = jnp.concatenate([tail_s[...], u], axis=0)
    cw = cw_ref[...]
    xc = cb_ref[...].reshape(1, 1, wb)
    for j in range(CONV_WIDTH):
        xc = xc + xp[j:j + tc] * cw[j:j + 1].reshape(1, 1, wb)
    last_rows = u[tc - (CONV_WIDTH - 1):]
    tail_s[...] = last_rows
    cn_ref[...] = last_rows

    x2 = xc.reshape(tc * nb, wb)
    xb = x2.astype(BF16)
    r = jax.nn.sigmoid(_dot(xb, wa_ref[0]) + ba_ref[0])
    gi = jax.nn.sigmoid(_dot(xb, wx_ref[0]) + bx_ref[0])
    log_a = r * ((-LRU_C) * jax.nn.softplus(-lam_ref[...]))
    a = jnp.exp(log_a)
    m2 = -jnp.tanh(log_a) * (a * a + 1.0)
    mult = jnp.where(m2 > 0.0, m2 * lax.rsqrt(m2), 0.0)
    if first_pos_zero:
        row = lax.broadcasted_iota(I32, (tc * nb, wb), 0)
        mult = jnp.where((row < nb) & (ti == 0), 1.0, mult)
    a_s[...] = a.reshape(tc, nb, wb)
    b_s[...] = (mult * (gi * x2)).reshape(tc, nb, wb)

    def step(t, h):
        h = a_s[t] * h + b_s[t]
        gate = gl_ref[t].astype(F32) * sl_ref[t].astype(F32)
        y_s[t] = h * gate
        return h

    h = lax.fori_loop(0, tc, step, h_s[...], unroll=8)
    h_s[...] = h
    hl_ref[...] = h

    perm = _row_permutation(TIME_GROUP, nb)
    for g in range(tc // TIME_GROUP):
        rows = slice(g * TIME_GROUP, (g + 1) * TIME_GROUP)
        yg = y_s[rows].reshape(TIME_GROUP * nb, wb).astype(BF16)
        y_ref[:, rows, :] = _dot(perm, yg).astype(BF16).reshape(nb, TIME_GROUP, wb)


def _lru(u, gl, sl, conv_w, conv_b, wa, ba, wx, bx, lam, h0, prev, tc, first_pos_zero):
    t, b, w = u.shape
    wb = LRU_BLOCK_DIM
    nw = w // wb
    seq = pl.BlockSpec((tc, b, wb), lambda wi, ti: (ti, 0, wi))
    seq_out = pl.BlockSpec((b, tc, wb), lambda wi, ti: (0, ti, wi))
    chan = lambda rows: pl.BlockSpec((rows, wb), lambda wi, ti: (0, wi))
    gate_w = pl.BlockSpec((1, wb, wb), lambda wi, ti: (wi, 0, 0))
    gate_b = pl.BlockSpec((1, 1, wb), lambda wi, ti: (wi, 0, 0))
    tail = pl.BlockSpec((CONV_WIDTH - 1, b, wb), lambda wi, ti: (0, 0, wi))
    return pl.pallas_call(
        functools.partial(_lru_kernel, first_pos_zero=first_pos_zero),
        grid=(nw, t // tc),
        in_specs=[seq, seq, seq, chan(CONV_WIDTH), chan(1), gate_w, gate_b, gate_w, gate_b, chan(1),
                  chan(b), tail],
        out_specs=[seq_out, chan(b), tail],
        out_shape=[jax.ShapeDtypeStruct((b, t, w), BF16), jax.ShapeDtypeStruct((b, w), F32),
                   jax.ShapeDtypeStruct((CONV_WIDTH - 1, b, w), F32)],
        scratch_shapes=[pltpu.VMEM((CONV_WIDTH - 1, b, wb), F32), pltpu.VMEM((b, wb), F32),
                        pltpu.VMEM((tc, b, wb), F32), pltpu.VMEM((tc, b, wb), F32),
                        pltpu.VMEM((tc, b, wb), F32)],
        compiler_params=_cparams(("arbitrary", "arbitrary")),
        name="lru",
    )(u, gl, sl, conv_w, conv_b.reshape(1, w), wa, ba.reshape(nw, 1, wb), wx, bx.reshape(nw, 1, wb),
      lam.reshape(1, w), h0, prev)


def _log_keep_and_logsig(z):
    nz = -z
    lk = jnp.minimum(nz, 0.0) - jnp.log2(1.0 + jnp.exp2(jnp.minimum(z, nz)))
    return lk, z + lk


def _later_key_sums(lk, tri):
    hi, lo = _split_bf16(lk)
    return _dot(hi, tri) + _dot(lo, tri)


def _strict_lower(n, dtype):
    row = lax.broadcasted_iota(I32, (n, n), 0)
    col = lax.broadcasted_iota(I32, (n, n), 1)
    return (row > col).astype(dtype), col < row


def _attn_kernel(q_ref, k_ref, v_ref, sz_ref, yl_ref, o_ref):
    tq = q_ref.shape[1]
    nh = q_ref.shape[2] // HEAD_DIM
    qi = pl.program_id(2)
    tri, causal = _strict_lower(tq, BF16)

    def tile(j, carry, diag):
        start = pl.multiple_of(j * tq, tq)
        heads = [slice(h * HEAD_DIM, (h + 1) * HEAD_DIM) for h in range(nh)]
        zs = [lax.dot_general(q_ref[0, :, cols], k_ref[0, pl.ds(start, tq), cols].astype(BF16), NT_DIMS,
                              preferred_element_type=F32) for cols in heads]
        lks, lss = [], []
        for z in zs:
            lk, ls = _log_keep_and_logsig(z)
            lks.append(jnp.where(causal, lk, 0.0) if diag else lk)
            lss.append(ls)
        rcs = [_later_key_sums(lk, tri) for lk in lks]
        ws = []
        for h in range(nh):
            w = jnp.exp2(lss[h] + rcs[h] + carry[h][1])
            ws.append((jnp.where(causal, w, 0.0) if diag else w).astype(BF16))
        return tuple((carry[h][0] + _dot(ws[h], v_ref[0, pl.ds(start, tq), heads[h]].astype(BF16)),
                      carry[h][1] + (rcs[h][:, 0:1] + lks[h][:, 0:1])) for h in range(nh))

    def any_weight_left(carry):
        c_max = carry[0][1]
        for h in range(1, nh):
            c_max = jnp.maximum(c_max, carry[h][1])
        return jnp.max(c_max) > F32_ZERO_WEIGHT_LOG2

    init = tuple((jnp.zeros((tq, HEAD_DIM), F32), jnp.zeros((tq, 1), F32)) for _ in range(nh))
    carry = tile(qi, init, True)

    def more(state):
        return (state[0] < qi) & state[1]

    def farther_tile(state):
        carry = tile(qi - 1 - state[0], state[2], False)
        return state[0] + 1, any_weight_left(carry), carry

    carry = lax.while_loop(more, farther_tile, (jnp.int32(0), any_weight_left(carry), carry))[2]
    acc = jnp.concatenate([carry[h][0] for h in range(nh)], axis=1)
    o_ref[0] = (yl_ref[0].astype(F32) + sz_ref[0].astype(F32) * acc).astype(BF16)


def _attn_prompt(q, k, v, sz, yl, tq):
    b, t, a = q.shape
    wblk = ATTN_HEADS_PER_STEP * HEAD_DIM
    qspec = pl.BlockSpec((1, tq, wblk), lambda bi, hi, qi: (bi, qi, hi))
    kvspec = pl.BlockSpec((1, t, wblk), lambda bi, hi, qi: (bi, 0, hi))
    return pl.pallas_call(
        _attn_kernel,
        grid=(b, a // wblk, t // tq),
        in_specs=[qspec, kvspec, kvspec, qspec, qspec],
        out_specs=qspec,
        out_shape=jax.ShapeDtypeStruct((b, t, a), BF16),
        compiler_params=_cparams(("arbitrary", "arbitrary", "arbitrary")),
        name="attn_prompt",
    )(q, k, v, sz, yl)


def _attn_sample_kernel(q_ref, kn_ref, vn_ref, kc_ref, vc_ref, sz_ref, yl_ref, o_ref, lk_s, ls_s):
    ts = q_ref.shape[1]
    nh = q_ref.shape[2] // HEAD_DIM
    past = kc_ref.shape[1]
    tri_n, causal = _strict_lower(ts, BF16)
    tri_p, _ = _strict_lower(past, BF16)
    accs, carries = [], []
    for h in range(nh):
        cols = slice(h * HEAD_DIM, (h + 1) * HEAD_DIM)
        q = q_ref[0, :, cols]
        zn = lax.dot_general(q, kn_ref[0, :, cols].astype(BF16), NT_DIMS, preferred_element_type=F32)
        lk, ls = _log_keep_and_logsig(zn)
        lk = jnp.where(causal, lk, 0.0)
        rc = _later_key_sums(lk, tri_n)
        w = jnp.where(causal, jnp.exp2(ls + rc), 0.0)
        accs.append(_dot(w.astype(BF16), vn_ref[0, :, cols].astype(BF16)))
        carries.append(rc[:, 0:1] + lk[:, 0:1])
        zp = lax.dot_general(q, kc_ref[0, :, h, :].astype(BF16), NT_DIMS, preferred_element_type=F32)
        lkp, lsp = _log_keep_and_logsig(zp)
        lk_s[h * ts:(h + 1) * ts, :] = lkp
        ls_s[h * ts:(h + 1) * ts, :] = lsp
    rcp = _later_key_sums(lk_s[...], tri_p)
    for h in range(nh):
        cols = slice(h * HEAD_DIM, (h + 1) * HEAD_DIM)
        rows = slice(h * ts, (h + 1) * ts)
        w = jnp.exp2(ls_s[rows, :] + rcp[rows, :] + carries[h])
        acc = accs[h] + _dot(w.astype(BF16), vc_ref[0, :, h, :].astype(BF16))
        o_ref[0, :, cols] = (yl_ref[0, :, cols].astype(F32) + sz_ref[0, :, cols].astype(F32) * acc).astype(BF16)


def _attn_sample(q, kn, vn, kc, vc, sz, yl):
    b, ts, a = q.shape
    past = kc.shape[1]
    nh = a // HEAD_DIM
    cur = pl.BlockSpec((1, ts, a), lambda bi: (bi, 0, 0))
    cache = pl.BlockSpec((1, past, nh, HEAD_DIM), lambda bi: (bi, 0, 0, 0))
    return pl.pallas_call(
        _attn_sample_kernel,
        grid=(b,),
        in_specs=[cur, cur, cur, cache, cache, cur, cur],
        out_specs=cur,
        out_shape=jax.ShapeDtypeStruct((b, ts, a), BF16),
        scratch_shapes=[pltpu.VMEM((nh * ts, past), F32), pltpu.VMEM((nh * ts, past), F32)],
        compiler_params=_cparams(("arbitrary",)),
        name="attn_sample",
    )(q, kn, vn, kc, vc, sz, yl)


def _outproj_kernel(a_ref, w_ref, x_ref, mod_ref, g_ref, *rest):
    x1_ref, hpk_ref, hlo_ref, w_s = rest[-4:]
    nb = mod_ref.shape[0]
    tm, d = x_ref.shape

    @pl.when(pl.program_id(0) == 0)
    def _():
        w_s[...] = w_ref[...].astype(BF16)

    m = mod_ref[...]
    o = _dot(a_ref[...], w_s[...]).reshape(nb, tm // nb, d)
    x1 = x_ref[...].reshape(nb, tm // nb, d) + m[:, 2:3, :] * o
    h = _rms(x1) * g_ref[...].reshape(1, 1, d) * (1.0 + m[:, 4:5, :]) + m[:, 3:4, :]
    x1_ref[...] = x1.reshape(tm, d)
    h = h.reshape(tm, d)
    hi = h.astype(BF16).astype(F32)
    _store_token_tiles(hpk_ref, _pack_pairs(hi))
    hlo_ref[...] = (h - hi).astype(BF16)


def _outproj(a, w, x, mod, g, tm, rows_per_batch, n_total, row_offset, earlier=None):
    m, d = x.shape
    nb = max(tm // rows_per_batch, 1)
    per = max(rows_per_batch // tm, 1)
    off = row_offset // tm
    nsub = _tile_rows(d)
    row = lambda width: pl.BlockSpec((tm, width), lambda i: (i, 0))
    out_row = lambda width: pl.BlockSpec((tm, width), lambda i: (i + off, 0))
    in_specs = [row(d), _resident((d, d), lambda i: (0, 0)), row(d),
                pl.BlockSpec((nb, N_MOD, d), lambda i: (i // per, 0, 0)),
                pl.BlockSpec((1, d), lambda i: (0, 0))]
    args = [a, w, x, mod, g.reshape(1, d)]
    aliases = {}
    if earlier is not None:
        aliases = {len(args) + j: j for j in range(len(earlier))}
        in_specs += [pl.BlockSpec(memory_space=pl.ANY)] * len(earlier)
        args += list(earlier)
    return pl.pallas_call(
        _outproj_kernel,
        grid=(m // tm,),
        in_specs=in_specs,
        out_specs=[out_row(d), pl.BlockSpec((tm * nsub, LANES), lambda i: (i + off, 0)), out_row(d)],
        out_shape=[jax.ShapeDtypeStruct((n_total, d), F32), jax.ShapeDtypeStruct((n_total * nsub, LANES), U32),
                   jax.ShapeDtypeStruct((n_total, d), BF16)],
        scratch_shapes=[pltpu.VMEM((d, d), BF16)],
        input_output_aliases=aliases,
        compiler_params=_cparams(("arbitrary",)),
        name="outproj",
    )(*args)


def _sub_max(x):
    return jnp.max(x, axis=0, keepdims=True)


def _sub_min(x):
    return jnp.min(x, axis=0, keepdims=True)


def _sub_sum(x):
    return jnp.sum(x, axis=0, keepdims=True)


def _router_kernel(hpk_ref, hlo_ref, wrh_ref, wrl_ref, br_ref, e_ref, w_ref, slot_ref, cnt_ref, base_s):
    tm = hlo_ref.shape[0]

    @pl.when(pl.program_id(0) == 0)
    def _():
        base_s[...] = jnp.zeros_like(base_s)

    hi = _unpack_pairs(_load_token_tiles(hpk_ref, _tile_rows(hlo_ref.shape[1])), BF16)
    lo = hlo_ref[...]
    wrh = wrh_ref[...]
    logits = lax.dot_general(wrh, hi, NT_DIMS, preferred_element_type=F32)
    logits = logits + (lax.dot_general(wrh, lo, NT_DIMS, preferred_element_type=F32)
                       + lax.dot_general(wrl_ref[...], hi, NT_DIMS, preferred_element_type=F32))
    score = jax.nn.sigmoid(logits)
    sel = (score + br_ref[...]).reshape(GROUP_SIZE, N_GROUPS, tm)
    score = score.reshape(GROUP_SIZE, N_GROUPS, tm)

    m1 = sel[0]
    m2 = jnp.full_like(m1, NEG_INF)
    for e in range(1, GROUP_SIZE):
        x = sel[e]
        m2 = jnp.maximum(m2, jnp.minimum(m1, x))
        m1 = jnp.maximum(m1, x)
    cur = m1 + m2
    gidx = lax.broadcasted_iota(I32, (N_GROUPS, tm), 0)
    gmask = jnp.zeros((N_GROUPS, tm), jnp.bool_)
    for _ in range(TOPK_GROUPS):
        first = _sub_min(jnp.where(cur == _sub_max(cur), gidx, N_GROUPS))
        pick = gidx == first
        gmask = gmask | pick
        cur = jnp.where(pick, NEG_INF, cur)

    eid = (lax.broadcasted_iota(I32, (GROUP_SIZE, N_GROUPS, tm), 1) * GROUP_SIZE
           + lax.broadcasted_iota(I32, (GROUP_SIZE, N_GROUPS, tm), 0))
    cur = jnp.where(gmask[None], sel, NEG_INF)
    ids, wts, picks = [], [], []
    for _ in range(TOP_K):
        mx = _sub_max(jnp.max(cur, axis=0))
        first = _sub_min(jnp.min(jnp.where(cur == mx[None], eid, N_EXPERTS), axis=0))
        pick = eid == first[None]
        ids.append(first)
        picks.append(pick)
        wts.append(_sub_sum(jnp.sum(jnp.where(pick, score, 0.0), axis=0)))
        cur = jnp.where(pick, NEG_INF, cur)
    total = wts[0]
    chosen = picks[0]
    for k in range(1, TOP_K):
        total = total + wts[k]
        chosen = chosen | picks[k]

    chosen = jnp.where(chosen, 1.0, 0.0).reshape(N_EXPERTS, tm)
    earlier_or_self = (lax.broadcasted_iota(I32, (tm, tm), 0) <= lax.broadcasted_iota(I32, (tm, tm), 1)).astype(BF16)
    incl = _dot(chosen.astype(BF16), earlier_or_self)
    rank = (base_s[...] + (incl - chosen)).reshape(GROUP_SIZE, N_GROUPS, tm)
    base_s[...] = base_s[...] + incl[:, tm - 1:tm]
    slots = [_sub_sum(jnp.sum(jnp.where(pick, rank, 0.0), axis=0)).astype(I32) for pick in picks]

    pad = 8 - TOP_K
    e_ref[...] = jnp.concatenate(ids + [jnp.zeros((pad, tm), I32)], axis=0)
    w_ref[...] = jnp.concatenate([w / total * ROUTED_SCALE for w in wts] + [jnp.zeros((pad, tm), F32)], axis=0)
    slot_ref[...] = jnp.concatenate(slots + [jnp.zeros((pad, tm), I32)], axis=0)
    cnt_ref[...] = jnp.broadcast_to(base_s[...], cnt_ref.shape)


def _router(hpk, hlo, w_router, b_router, tm):
    n, d = hlo.shape
    wr = w_router.T.reshape(N_GROUPS, GROUP_SIZE, d).transpose(1, 0, 2).reshape(N_EXPERTS, d)
    br = b_router.reshape(N_GROUPS, GROUP_SIZE).T.reshape(N_EXPERTS, 1)
    wrh = wr.astype(BF16)
    wrl = (wr - wrh.astype(F32)).astype(BF16)
    full = lambda shape: pl.BlockSpec(shape, lambda i: (0, 0))
    per_token = pl.BlockSpec((8, tm), lambda i: (0, i))
    e_k, w_k, slot_k, counts = pl.pallas_call(
        _router_kernel,
        grid=(n // tm,),
        in_specs=[pl.BlockSpec((tm * _tile_rows(d), LANES), lambda i: (i, 0)),
                  pl.BlockSpec((tm, d), lambda i: (i, 0)),
                  full((N_EXPERTS, d)), full((N_EXPERTS, d)), full((N_EXPERTS, 1))],
        out_specs=[per_token, per_token, per_token, full((N_EXPERTS, 128))],
        out_shape=[jax.ShapeDtypeStruct((8, n), I32), jax.ShapeDtypeStruct((8, n), F32),
                   jax.ShapeDtypeStruct((8, n), I32), jax.ShapeDtypeStruct((N_EXPERTS, 128), F32)],
        scratch_shapes=[pltpu.VMEM((N_EXPERTS, 1), F32)],
        compiler_params=_cparams(("arbitrary",)),
        name="router",
    )(hpk, hlo, wrh, wrl, br)
    counts = counts[:, 0].reshape(GROUP_SIZE, N_GROUPS).T.reshape(N_EXPERTS).astype(I32)
    return e_k, w_k, slot_k, counts


def _tile_window(token, nsub):
    start = token * nsub
    if nsub % SUBLANES == 0:
        start = pl.multiple_of(start, SUBLANES)
    return pl.ds(start, nsub)


def _token_copy(src, src_token, dst, dst_token, nsub, sem):
    return pltpu.make_async_copy(src.at[_tile_window(src_token, nsub)], dst.at[_tile_window(dst_token, nsub)], sem)


def _for_token_groups(n_tokens, per_token):
    def group(g, carry):
        t0 = pl.multiple_of(g * SUBLANES, SUBLANES)
        for u in range(SUBLANES):
            per_token(t0, u)
        return carry

    lax.fori_loop(0, n_tokens // SUBLANES, group, 0)


def _dispatch_kernel(pad_end_ref, dest_ref, h_ref, wsg_ref, wsu_ref, wsd_ref, xs_ref, shared_ref, zero_s, sem,
                     zero_sem, *, nsub):
    tb = dest_ref.shape[0] // DEST_STRIDE
    te = zero_s.shape[0] // nsub

    def tail_fill(e):
        start = pl.multiple_of((pad_end_ref[e] - te) * nsub, te * nsub)
        return pltpu.make_async_copy(zero_s, xs_ref.at[pl.ds(start, te * nsub)], zero_sem)

    def has_rows(e):
        return pad_end_ref[e] > (pad_end_ref[e - 1] if e else 0)

    @pl.when(pl.program_id(0) == 0)
    def _():
        zero_s[...] = jnp.zeros_like(zero_s)
        for e in range(N_EXPERTS):
            pl.when(has_rows(e))(lambda e=e: tail_fill(e).start())
        for e in range(N_EXPERTS):
            pl.when(has_rows(e))(lambda e=e: tail_fill(e).wait())

    def start(t0, u):
        for k in range(TOP_K):
            _token_copy(h_ref, t0 + u, xs_ref, dest_ref[t0 * DEST_STRIDE + (u * DEST_STRIDE + k)], nsub,
                        sem).start(priority=k % 2)

    def wait(t0, u):
        for k in range(TOP_K):
            _token_copy(h_ref, 0, xs_ref, 0, nsub, sem).wait()

    _for_token_groups(tb, start)
    x = _unpack_pairs(_load_token_tiles(h_ref, nsub), BF16)
    g = _dot(x, wsg_ref[...])
    shared_ref[...] = _dot(((g * jax.nn.sigmoid(g)) * _dot(x, wsu_ref[...])).astype(BF16), wsd_ref[...])
    _for_token_groups(tb, wait)


def _dispatch(pad_end, dest, hpk, ws_gate, ws_up, ws_down, n_rows, tb, te):
    d, ff = ws_gate.shape
    nsub = _tile_rows(d)
    n = hpk.shape[0] // nsub
    full = lambda shape: pl.BlockSpec(shape, lambda i, pe: (0, 0))
    grid_spec = pltpu.PrefetchScalarGridSpec(
        num_scalar_prefetch=1,
        grid=(n // tb,),
        in_specs=[pl.BlockSpec((tb * DEST_STRIDE,), lambda i, pe: (i,), memory_space=pltpu.SMEM),
                  pl.BlockSpec((tb * nsub, LANES), lambda i, pe: (i, 0)),
                  full((d, ff)), full((d, ff)), full((ff, d))],
        out_specs=[pl.BlockSpec(memory_space=pl.ANY), pl.BlockSpec((tb, d), lambda i, pe: (i, 0))],
        scratch_shapes=[pltpu.VMEM((te * nsub, LANES), U32), pltpu.SemaphoreType.DMA(()),
                        pltpu.SemaphoreType.DMA(())],
    )
    return pl.pallas_call(
        functools.partial(_dispatch_kernel, nsub=nsub),
        grid_spec=grid_spec,
        out_shape=[jax.ShapeDtypeStruct((n_rows * nsub, LANES), U32), jax.ShapeDtypeStruct((n, d), F32)],
        compiler_params=_cparams(("arbitrary",)),
        name="dispatch",
    )(pad_end, dest, hpk, ws_gate, ws_up, ws_down)


def _experts_kernel(be_ref, nu_ref, x_ref, wg_ref, wu_ref, wd_ref, o_ref, wg_s, wu_s, wd_s):
    i = pl.program_id(0)
    prev = be_ref[jnp.maximum(i - 1, 0)]

    @pl.when((i == 0) | (be_ref[i] != prev))
    def _():
        wg_s[...] = wg_ref[0].astype(BF16)
        wu_s[...] = wu_ref[0].astype(BF16)
        wd_s[...] = wd_ref[0].astype(BF16)

    @pl.when(i < nu_ref[0])
    def _():
        x = _unpack_pairs(_load_token_tiles(x_ref, _tile_rows(wg_s.shape[0])), BF16)
        g = _dot(x, wg_s[...])
        hidden = (g * jax.nn.sigmoid(g)) * _dot(x, wu_s[...])
        _store_token_tiles(o_ref, _pack_pairs(_dot(hidden.astype(BF16), wd_s[...])))

    @pl.when(i >= nu_ref[0])
    def _():
        o_ref[...] = jnp.zeros_like(o_ref)


def _experts(block_expert, n_used, xs, w_gate, w_up, w_down, tm):
    d, ff = w_gate.shape[1:]
    nsub = _tile_rows(d)
    p = xs.shape[0] // nsub
    rows = pl.BlockSpec((tm * nsub, LANES), lambda i, be, nu: (i, 0))
    grid_spec = pltpu.PrefetchScalarGridSpec(
        num_scalar_prefetch=2,
        grid=(p // tm,),
        in_specs=[rows,
                  pl.BlockSpec((1, d, ff), lambda i, be, nu: (be[i], 0, 0)),
                  pl.BlockSpec((1, d, ff), lambda i, be, nu: (be[i], 0, 0)),
                  pl.BlockSpec((1, ff, d), lambda i, be, nu: (be[i], 0, 0))],
        out_specs=rows,
        scratch_shapes=[pltpu.VMEM((d, ff), BF16), pltpu.VMEM((d, ff), BF16), pltpu.VMEM((ff, d), BF16)],
    )
    return pl.pallas_call(
        _experts_kernel,
        grid_spec=grid_spec,
        out_shape=jax.ShapeDtypeStruct(xs.shape, U32),
        compiler_params=_cparams(("arbitrary",)),
        name="experts",
    )(block_expert, n_used, xs, w_gate, w_up, w_down)


def _combine_kernel(dest_ref, wk_ref, ys_ref, x1_ref, shared_ref, gt_ref, gf_ref, yp_ref, ysm_ref, buf, sem, *,
                    n_prompt_blocks):
    tb, d = x1_ref.shape
    nsub = _tile_rows(d)
    seg = gt_ref.shape[0]
    i = pl.program_id(0)

    def start(t0, u):
        for k in range(TOP_K):
            _token_copy(ys_ref, dest_ref[t0 * DEST_STRIDE + (u * DEST_STRIDE + k)], buf.at[k], t0 + u, nsub,
                        sem).start(priority=k % 2)

    def wait(t0, u):
        for k in range(TOP_K):
            _token_copy(ys_ref, 0, buf.at[k], 0, nsub, sem).wait()

    _for_token_groups(tb, start)
    _for_token_groups(tb, wait)
    ffn = shared_ref[...]
    wk = wk_ref[...]
    for k in range(TOP_K):
        ffn = ffn + wk[:, k:k + 1] * _unpack_pairs(_load_token_tiles(buf.at[k], nsub), F32)
    x2 = x1_ref[...].reshape(seg, tb // seg, d) + gt_ref[...] * ffn.reshape(seg, tb // seg, d)
    y = (_rms(x2) * gf_ref[...].reshape(1, 1, d)).reshape(tb, d)

    @pl.when(i < n_prompt_blocks)
    def _():
        yp_ref[...] = y

    @pl.when(i >= n_prompt_blocks)
    def _():
        ysm_ref[...] = y


def _combine(dest, wk_t, ys, x1, shared, gt2_seg, g_final, n_prompt, tb):
    n, d = x1.shape
    nsub = _tile_rows(d)
    npb = n_prompt // tb
    nsb = (n - n_prompt) // tb
    seg = gt2_seg.shape[0] * tb // n
    row = lambda width: pl.BlockSpec((tb, width), lambda i: (i, 0))
    return pl.pallas_call(
        functools.partial(_combine_kernel, n_prompt_blocks=npb),
        grid=(n // tb,),
        in_specs=[pl.BlockSpec((tb * DEST_STRIDE,), lambda i: (i,), memory_space=pltpu.SMEM),
                  row(8), pl.BlockSpec(memory_space=pl.ANY), row(d), row(d),
                  pl.BlockSpec((seg, 1, d), lambda i: (i, 0, 0)),
                  pl.BlockSpec((1, d), lambda i: (0, 0))],
        out_specs=[pl.BlockSpec((tb, d), lambda i: (jnp.minimum(i, npb - 1), 0)),
                   pl.BlockSpec((tb, d), lambda i: (jnp.clip(i - npb, 0, nsb - 1), 0))],
        out_shape=[jax.ShapeDtypeStruct((n_prompt, d), F32), jax.ShapeDtypeStruct((n - n_prompt, d), F32)],
        scratch_shapes=[pltpu.VMEM((TOP_K, tb * nsub, LANES), U32), pltpu.SemaphoreType.DMA(())],
        compiler_params=_cparams(("arbitrary",)),
        name="combine",
    )(dest, wk_t, ys, x1, shared, gt2_seg, g_final.reshape(1, d))


def _largest_tile(n, target):
    t = min(n, target)
    while n % t:
        t -= 1
    return t


def _mixer(x, mod, past_k, past_v, h0, conv0, p, first_pos_zero, n_total, row_offset, earlier):
    b, t, d = x.shape
    m = b * t
    a = d
    hn_bm, hn_tm = _norm_mod(x, mod, p["g_mix"])
    hn_bm = hn_bm.reshape(m, d)
    tm = _largest_tile(m, 512)
    tmh = _largest_tile(m, 1024)
    w_in = p["w_in"]
    u = _proj(hn_tm, w_in, 0, d, "plain", BF16, tmh).reshape(t, b, d)
    gl = _proj(hn_tm, w_in, 1, d, "gelu", BF16, tmh).reshape(t, b, d)
    sl = _proj(hn_tm, w_in, 5, d, "sigmoid", BF16, tmh).reshape(t, b, d)
    q = _proj(hn_bm, w_in, 2, a, "qscale", BF16, tmh).reshape(b, t, a)
    k = _proj(hn_bm, w_in, 3, a, "plain", F32, tm).reshape(b, t, a)
    v = _proj(hn_bm, w_in, 4, a, "plain", F32, tm).reshape(b, t, a)
    sz = _proj(hn_bm, w_in, 6, a, "sigmoid", BF16, tmh).reshape(b, t, a)

    if h0 is None:
        h0 = jnp.zeros((b, d), F32)
        prev = jnp.zeros((CONV_WIDTH - 1, b, d), F32)
    else:
        prev = conv0.transpose(1, 0, 2)
    tc = _largest_tile(t, 128)
    yl, h_last, conv_tail = _lru(u, gl, sl, p["conv_w"], p["conv_b"], p["w_rg_a"], p["b_rg_a"], p["w_rg_x"],
                                 p["b_rg_x"], p["lru_lambda"], h0, prev, tc, first_pos_zero)
    if past_k is None:
        mixed = _attn_prompt(q, k, v, sz, yl, _largest_tile(t, 256))
    else:
        mixed = _attn_sample(q, k, v, past_k, past_v, sz, yl)
    tmo = _largest_tile(m, 256)
    shared = _outproj(mixed.reshape(m, a), p["w_out"], x.reshape(m, d), mod, p["g_ffn"], tmo, t, n_total, row_offset,
                      earlier)
    nh = a // HEAD_DIM
    return (shared, k.reshape(b, t, nh, HEAD_DIM), v.reshape(b, t, nh, HEAD_DIM), h_last,
            conv_tail.transpose(1, 0, 2))


def _routing_tables(e_k, slot_k, counts, tm):
    n_tokens = e_k.shape[1]
    padded = (counts + tm - 1) // tm * tm
    pad_end = jnp.cumsum(padded).astype(I32)
    pad_start = pad_end - padded
    is_expert = e_k[:, :, None] == jnp.arange(N_EXPERTS, dtype=I32)
    dest = slot_k + jnp.sum(jnp.where(is_expert, pad_start, 0), axis=-1).astype(I32)
    n_blocks = -(-(TOP_K * n_tokens + N_EXPERTS * (tm - 1)) // tm)
    block_start = jnp.arange(n_blocks, dtype=I32) * tm
    block_expert = jnp.minimum(jnp.sum(pad_end[None, :] <= block_start[:, None], axis=1), N_EXPERTS - 1).astype(I32)
    n_used = (pad_end[-1:] // tm).astype(I32)
    return dest, pad_end, block_expert, n_used, n_blocks * tm


def kernel(x_prompt, x_sample, cache_k, cache_v, state_lru, state_conv, c_prompt, c_sample, g_mix, w_ada, b_ada, w_in, conv_w, conv_b, w_rg_a, b_rg_a, w_rg_x, b_rg_x, lru_lambda, w_out, g_ffn, w_router, b_router, w_gate, w_up, w_down, ws_gate, ws_up, ws_down, g_final):
    assert g_mix.shape[0] == 1, "single-layer trunk only"
    b, t, d = x_prompt.shape
    bs, ts, _ = x_sample.shape
    assert ts >= CONV_WIDTH - 1 and ts == TIME_GROUP and t % TIME_GROUP == 0
    p = dict(g_mix=g_mix[0], w_in=w_in[0], conv_w=conv_w[0], conv_b=conv_b[0],
             w_rg_a=w_rg_a[0].astype(BF16), b_rg_a=b_rg_a[0], w_rg_x=w_rg_x[0].astype(BF16), b_rg_x=b_rg_x[0],
             lru_lambda=lru_lambda[0], w_out=w_out[0], g_ffn=g_ffn[0])
    mod = _ada(jnp.concatenate([c_prompt, c_sample], axis=0), w_ada[0], b_ada[0]).reshape(b + bs, N_MOD, d)
    mod_p, mod_s = mod[:b], mod[b:]

    n_prompt = b * t
    n = n_prompt + bs * ts
    tb = _largest_tile(bs * ts, 256)
    assert n_prompt % tb == 0 and tb % ts == 0

    shared, kp, vp, hp, cp = _mixer(x_prompt, mod_p, None, None, None, None, p, True, n, 0, None)
    (x1, hpk, hlo), ks, vs, hs, cs = _mixer(x_sample, mod_s, cache_k[0], cache_v[0], state_lru[0], state_conv[0],
                                            p, False, n, n_prompt, shared)

    e_k, w_k, slot_k, counts = _router(hpk, hlo, w_router[0], b_router[0], tb)
    tme = EXPERT_ROW_BLOCK
    dest, pad_end, block_expert, n_used, n_rows = _routing_tables(e_k, slot_k, counts, tme)
    dest = dest.T.reshape(-1)
    xs, shared = _dispatch(pad_end, dest, hpk, ws_gate[0].astype(BF16), ws_up[0].astype(BF16),
                           ws_down[0].astype(BF16), n_rows, tb, tme)
    ys = _experts(block_expert, n_used, xs, w_gate[0], w_up[0], w_down[0], tme)
    gt2 = jnp.concatenate([jnp.broadcast_to(mod_p[:, None, 5:6, :], (b, t // ts, 1, d)).reshape(-1, 1, d),
                           mod_s[:, 5:6, :]], axis=0)
    y_p, y_s = _combine(dest, w_k.T, ys, x1, shared, gt2, g_final, n_prompt, tb)
    return (y_p.reshape(b, t, d), y_s.reshape(bs, ts, d), kp[None], vp[None], hp[None], cp[None],
            ks[None], vs[None], hs[None], cs[None])
```

```python
import functools

import jax
import jax.numpy as jnp
from jax import lax
from jax.experimental import pallas as pl
from jax.experimental.pallas import tpu as pltpu

F32 = jnp.float32
BF16 = jnp.bfloat16
I32 = jnp.int32
U32 = jnp.uint32

LANES = 128
SUBLANES = 8
HEAD_DIM = 128
LRU_BLOCK_DIM = 256
LRU_C = 8.0
CONV_WIDTH = 4
N_EXPERTS = 64
TOP_K = 6
N_GROUPS = 8
GROUP_SIZE = N_EXPERTS // N_GROUPS
TOPK_GROUPS = 4
ROUTED_SCALE = 2.5
NORM_EPS = 1e-6
N_MOD = 6
TIME_GROUP = 16
LOG2_E = 1.4426950408889634
ATTN_HEADS_PER_STEP = 8
F32_ZERO_WEIGHT_LOG2 = -150.0
ROUTER_TOKENS = 768
DEST_STRIDE = 8
EXPERT_ROW_BLOCK = 512
V7X_VMEM_LIMIT_BYTES = 56 * 1024 * 1024
NT_DIMS = (((1,), (1,)), ((), ()))
NEG_INF = float("-inf")


def _cparams(semantics):
    return pltpu.CompilerParams(dimension_semantics=semantics, vmem_limit_bytes=V7X_VMEM_LIMIT_BYTES)


def _dot(a, b):
    return jnp.dot(a, b, preferred_element_type=F32)


def _split_bf16(x):
    hi = x.astype(BF16)
    lo = (x - hi.astype(F32)).astype(BF16)
    return hi, lo


def _dot3(a, b):
    ah, al = _split_bf16(a)
    bh, bl = _split_bf16(b)
    return _dot(ah, bh) + (_dot(al, bh) + _dot(ah, bl))


def _pack_pairs(h):
    n = h.shape[1] // 2
    bits = lax.bitcast_convert_type(h.astype(BF16).astype(F32), U32)
    return bits[:, :n] | (bits[:, n:] >> 16)


def _store_token_tiles(ref, packed):
    m, n = packed.shape
    nsub = n // LANES
    for j in range(nsub):
        ref[pl.ds(j, m, stride=nsub), :] = packed[:, j * LANES:(j + 1) * LANES]


def _tile_rows(d):
    return d // 2 // LANES


def _load_token_tiles(ref, nsub):
    m = ref.shape[0] // nsub
    return jnp.concatenate([ref[pl.ds(j, m, stride=nsub), :] for j in range(nsub)], axis=1)


def _unpack_pairs(p, dtype):
    a = lax.bitcast_convert_type(p & jnp.uint32(0xFFFF0000), F32)
    b = lax.bitcast_convert_type(p << 16, F32)
    return jnp.concatenate([a.astype(dtype), b.astype(dtype)], axis=1)


def _rms(x):
    return x * lax.rsqrt(jnp.mean(x * x, axis=-1, keepdims=True) + NORM_EPS)


def _ada_kernel(c_ref, w_ref, b_ref, o_ref):
    c = c_ref[...]
    o_ref[...] = _dot3(c * jax.nn.sigmoid(c), w_ref[...]) + b_ref[...]


def _ada(c, w, b):
    m, d = c.shape
    n = w.shape[1]
    tn = min(n, 1024)
    return pl.pallas_call(
        _ada_kernel,
        grid=(n // tn,),
        in_specs=[pl.BlockSpec((m, d), lambda j: (0, 0)),
                  pl.BlockSpec((d, tn), lambda j: (0, j)),
                  pl.BlockSpec((1, tn), lambda j: (0, j))],
        out_specs=pl.BlockSpec((m, tn), lambda j: (0, j)),
        out_shape=jax.ShapeDtypeStruct((m, n), F32),
        compiler_params=_cparams(("arbitrary",)),
        name="ada",
    )(c, w, b.reshape(1, n))


def _row_permutation(n_outer, n_inner):
    n = n_outer * n_inner
    out_row = lax.broadcasted_iota(I32, (n, n), 0)
    in_row = lax.broadcasted_iota(I32, (n, n), 1)
    same = (out_row // n_outer == in_row % n_inner) & (out_row % n_outer == in_row // n_inner)
    return same.astype(BF16)


def _norm_mod_kernel(x_ref, mod_ref, g_ref, obm_ref, otm_ref):
    b, tt, d = x_ref.shape
    m = mod_ref[...]
    h = (_rms(x_ref[...]) * g_ref[...].reshape(1, 1, d) * (1.0 + m[:, 1:2, :]) + m[:, 0:1, :]).astype(BF16)
    obm_ref[...] = h
    otm_ref[...] = _dot(_row_permutation(b, tt), h.reshape(b * tt, d)).astype(BF16)


def _norm_mod(x, mod, g):
    b, t, d = x.shape
    tt = TIME_GROUP
    return pl.pallas_call(
        _norm_mod_kernel,
        grid=(t // tt,),
        in_specs=[pl.BlockSpec((b, tt, d), lambda ti: (0, ti, 0)),
                  pl.BlockSpec((b, N_MOD, d), lambda ti: (0, 0, 0)),
                  pl.BlockSpec((1, d), lambda ti: (0, 0))],
        out_specs=[pl.BlockSpec((b, tt, d), lambda ti: (0, ti, 0)),
                   pl.BlockSpec((tt * b, d), lambda ti: (ti, 0))],
        out_shape=[jax.ShapeDtypeStruct((b, t, d), BF16), jax.ShapeDtypeStruct((t * b, d), BF16)],
        compiler_params=_cparams(("arbitrary",)),
        name="norm_mod",
    )(x, mod, g.reshape(1, d))


def _proj_kernel(a_ref, w_ref, o_ref, w_s, *, epilogue):
    @pl.when(pl.program_id(0) == 0)
    def _():
        w_s[...] = w_ref[...].astype(BF16)

    acc = _dot(a_ref[...], w_s[...])
    if epilogue == "gelu":
        acc = jax.nn.gelu(acc)
    elif epilogue == "sigmoid":
        acc = jax.nn.sigmoid(acc)
    elif epilogue == "qscale":
        acc = acc * (HEAD_DIM ** -0.5 * LOG2_E)
    o_ref[...] = acc.astype(o_ref.dtype)


def _resident(shape, index_map):
    return pl.BlockSpec(shape, index_map, pipeline_mode=pl.Buffered(1))


def _proj(a, w, col, width, epilogue, out_dtype, tm):
    m, k = a.shape
    return pl.pallas_call(
        functools.partial(_proj_kernel, epilogue=epilogue),
        grid=(m // tm,),
        in_specs=[pl.BlockSpec((tm, k), lambda i: (i, 0)),
                  _resident((k, width), lambda i: (0, col))],
        out_specs=pl.BlockSpec((tm, width), lambda i: (i, 0)),
        out_shape=jax.ShapeDtypeStruct((m, width), out_dtype),
        scratch_shapes=[pltpu.VMEM((k, width), BF16)],
        compiler_params=_cparams(("arbitrary",)),
        name="proj_" + epilogue,
    )(a, w)


def _lru_kernel(u_ref, gl_ref, sl_ref, cw_ref, cb_ref, wa_ref, ba_ref, wx_ref, bx_ref, lam_ref, h0_ref, prev_ref,
                y_ref, hl_ref, cn_ref, tail_s, h_s, a_s, b_s, y_s, *, first_pos_zero):
    tc, nb, wb = u_ref.shape
    ti = pl.program_id(1)

    @pl.when(ti == 0)
    def _():
        tail_s[...] = prev_ref[...]
        h_s[...] = h0_ref[...]

    u = u_ref[...].astype(F32)
    xp = jnp.concatenate([tail_s[...], u], axis=0)
    cw = cw_ref[...]
    xc = cb_ref[...].reshape(1, 1, wb)
    for j in range(CONV_WIDTH):
        xc = xc + xp[j:j + tc] * cw[j:j + 1].reshape(1, 1, wb)
    last_rows = u[tc - (CONV_WIDTH - 1):]
    tail_s[...] = last_rows
    cn_ref[...] = last_rows

    x2 = xc.reshape(tc * nb, wb)
    xb = x2.astype(BF16)
    r = jax.nn.sigmoid(_dot(xb, wa_ref[0]) + ba_ref[0])
    gi = jax.nn.sigmoid(_dot(xb, wx_ref[0]) + bx_ref[0])
    log_a = r * ((-LRU_C) * jax.nn.softplus(-lam_ref[...]))
    a = jnp.exp(log_a)
    m2 = -jnp.tanh(log_a) * (a * a + 1.0)
    mult = jnp.where(m2 > 0.0, m2 * lax.rsqrt(m2), 0.0)
    if first_pos_zero:
        row = lax.broadcasted_iota(I32, (tc * nb, wb), 0)
        mult = jnp.where((row < nb) & (ti == 0), 1.0, mult)
    a_s[...] = a.reshape(tc, nb, wb)
    b_s[...] = (mult * (gi * x2)).reshape(tc, nb, wb)

    def step(t, h):
        h = a_s[t] * h + b_s[t]
        gate = gl_ref[t].astype(F32) * sl_ref[t].astype(F32)
        y_s[t] = h * gate
        return h

    h = lax.fori_loop(0, tc, step, h_s[...], unroll=8)
    h_s[...] = h
    hl_ref[...] = h

    perm = _row_permutation(TIME_GROUP, nb)
    for g in range(tc // TIME_GROUP):
        rows = slice(g * TIME_GROUP, (g + 1) * TIME_GROUP)
        yg = y_s[rows].reshape(TIME_GROUP * nb, wb).astype(BF16)
        y_ref[:, rows, :] = _dot(perm, yg).astype(BF16).reshape(nb, TIME_GROUP, wb)


def _lru(u, gl, sl, conv_w, conv_b, wa, ba, wx, bx, lam, h0, prev, tc, first_pos_zero):
    t, b, w = u.shape
    wb = LRU_BLOCK_DIM
    nw = w // wb
    seq = pl.BlockSpec((tc, b, wb), lambda wi, ti: (ti, 0, wi))
    seq_out = pl.BlockSpec((b, tc, wb), lambda wi, ti: (0, ti, wi))
    chan = lambda rows: pl.BlockSpec((rows, wb), lambda wi, ti: (0, wi))
    gate_w = pl.BlockSpec((1, wb, wb), lambda wi, ti: (wi, 0, 0))
    gate_b = pl.BlockSpec((1, 1, wb), lambda wi, ti: (wi, 0, 0))
    tail = pl.BlockSpec((CONV_WIDTH - 1, b, wb), lambda wi, ti: (0, 0, wi))
    return pl.pallas_call(
        functools.partial(_lru_kernel, first_pos_zero=first_pos_zero),
        grid=(nw, t // tc),
        in_specs=[seq, seq, seq, chan(CONV_WIDTH), chan(1), gate_w, gate_b, gate_w, gate_b, chan(1),
                  chan(b), tail],
        out_specs=[seq_out, chan(b), tail],
        out_shape=[jax.ShapeDtypeStruct((b, t, w), BF16), jax.ShapeDtypeStruct((b, w), F32),
                   jax.ShapeDtypeStruct((CONV_WIDTH - 1, b, w), F32)],
        scratch_shapes=[pltpu.VMEM((CONV_WIDTH - 1, b, wb), F32), pltpu.VMEM((b, wb), F32),
                        pltpu.VMEM((tc, b, wb), F32), pltpu.VMEM((tc, b, wb), F32),
                        pltpu.VMEM((tc, b, wb), F32)],
        compiler_params=_cparams(("arbitrary", "arbitrary")),
        name="lru",
    )(u, gl, sl, conv_w, conv_b.reshape(1, w), wa, ba.reshape(nw, 1, wb), wx, bx.reshape(nw, 1, wb),
      lam.reshape(1, w), h0, prev)


def _log_keep_and_logsig(z):
    nz = -z
    lk = jnp.minimum(nz, 0.0) - jnp.log2(1.0 + jnp.exp2(jnp.minimum(z, nz)))
    return lk, z + lk


def _later_key_sums(lk, tri):
    hi, lo = _split_bf16(lk)
    return _dot(hi, tri) + _dot(lo, tri)


def _strict_lower(n, dtype):
    row = lax.broadcasted_iota(I32, (n, n), 0)
    col = lax.broadcasted_iota(I32, (n, n), 1)
    return (row > col).astype(dtype), col < row


def _attn_kernel(q_ref, k_ref, v_ref, sz_ref, yl_ref, o_ref):
    tq = q_ref.shape[1]
    nh = q_ref.shape[2] // HEAD_DIM
    qi = pl.program_id(2)
    tri, causal = _strict_lower(tq, BF16)

    def tile(j, carry, diag):
        start = pl.multiple_of(j * tq, tq)
        heads = [slice(h * HEAD_DIM, (h + 1) * HEAD_DIM) for h in range(nh)]
        zs = [lax.dot_general(q_ref[0, :, cols], k_ref[0, pl.ds(start, tq), cols].astype(BF16), NT_DIMS,
                              preferred_element_type=F32) for cols in heads]
        lks, lss = [], []
        for z in zs:
            lk, ls = _log_keep_and_logsig(z)
            lks.append(jnp.where(causal, lk, 0.0) if diag else lk)
            lss.append(ls)
        rcs = [_later_key_sums(lk, tri) for lk in lks]
        ws = []
        for h in range(nh):
            w = jnp.exp2(lss[h] + rcs[h] + carry[h][1])
            ws.append((jnp.where(causal, w, 0.0) if diag else w).astype(BF16))
        return tuple((carry[h][0] + _dot(ws[h], v_ref[0, pl.ds(start, tq), heads[h]].astype(BF16)),
                      carry[h][1] + (rcs[h][:, 0:1] + lks[h][:, 0:1])) for h in range(nh))

    def any_weight_left(carry):
        c_max = carry[0][1]
        for h in range(1, nh):
            c_max = jnp.maximum(c_max, carry[h][1])
        return jnp.max(c_max) > F32_ZERO_WEIGHT_LOG2

    init = tuple((jnp.zeros((tq, HEAD_DIM), F32), jnp.zeros((tq, 1), F32)) for _ in range(nh))
    carry = tile(qi, init, True)

    def more(state):
        return (state[0] < qi) & state[1]

    def farther_tile(state):
        carry = tile(qi - 1 - state[0], state[2], False)
        return state[0] + 1, any_weight_left(carry), carry

    carry = lax.while_loop(more, farther_tile, (jnp.int32(0), any_weight_left(carry), carry))[2]
    acc = jnp.concatenate([carry[h][0] for h in range(nh)], axis=1)
    o_ref[0] = (yl_ref[0].astype(F32) + sz_ref[0].astype(F32) * acc).astype(BF16)


def _attn_prompt(q, k, v, sz, yl, tq):
    b, t, a = q.shape
    wblk = min(ATTN_HEADS_PER_STEP * HEAD_DIM, a)
    qspec = pl.BlockSpec((1, tq, wblk), lambda bi, hi, qi: (bi, qi, hi))
    kvspec = pl.BlockSpec((1, t, wblk), lambda bi, hi, qi: (bi, 0, hi))
    return pl.pallas_call(
        _attn_kernel,
        grid=(b, a // wblk, t // tq),
        in_specs=[qspec, kvspec, kvspec, qspec, qspec],
        out_specs=qspec,
        out_shape=jax.ShapeDtypeStruct((b, t, a), BF16),
        compiler_params=_cparams(("arbitrary", "arbitrary", "arbitrary")),
        name="attn_prompt",
    )(q, k, v, sz, yl)


def _attn_sample_kernel(q_ref, kn_ref, vn_ref, kc_ref, vc_ref, sz_ref, yl_ref, o_ref, lk_s, ls_s):
    ts = q_ref.shape[1]
    nh = q_ref.shape[2] // HEAD_DIM
    past = kc_ref.shape[1]
    tri_n, causal = _strict_lower(ts, BF16)
    tri_p, _ = _strict_lower(past, BF16)
    accs, carries = [], []
    for h in range(nh):
        cols = slice(h * HEAD_DIM, (h + 1) * HEAD_DIM)
        q = q_ref[0, :, cols]
        zn = lax.dot_general(q, kn_ref[0, :, cols].astype(BF16), NT_DIMS, preferred_element_type=F32)
        lk, ls = _log_keep_and_logsig(zn)
        lk = jnp.where(causal, lk, 0.0)
        rc = _later_key_sums(lk, tri_n)
        w = jnp.where(causal, jnp.exp2(ls + rc), 0.0)
        accs.append(_dot(w.astype(BF16), vn_ref[0, :, cols].astype(BF16)))
        carries.append(rc[:, 0:1] + lk[:, 0:1])
        zp = lax.dot_general(q, kc_ref[0, :, h, :].astype(BF16), NT_DIMS, preferred_element_type=F32)
        lkp, lsp = _log_keep_and_logsig(zp)
        lk_s[h * ts:(h + 1) * ts, :] = lkp
        ls_s[h * ts:(h + 1) * ts, :] = lsp
    rcp = _later_key_sums(lk_s[...], tri_p)
    for h in range(nh):
        cols = slice(h * HEAD_DIM, (h + 1) * HEAD_DIM)
        rows = slice(h * ts, (h + 1) * ts)
        w = jnp.exp2(ls_s[rows, :] + rcp[rows, :] + carries[h])
        acc = accs[h] + _dot(w.astype(BF16), vc_ref[0, :, h, :].astype(BF16))
        o_ref[0, :, cols] = (yl_ref[0, :, cols].astype(F32) + sz_ref[0, :, cols].astype(F32) * acc).astype(BF16)


def _attn_sample(q, kn, vn, kc, vc, sz, yl):
    b, ts, a = q.shape
    past = kc.shape[1]
    nh = a // HEAD_DIM
    cur = pl.BlockSpec((1, ts, a), lambda bi: (bi, 0, 0))
    cache = pl.BlockSpec((1, past, nh, HEAD_DIM), lambda bi: (bi, 0, 0, 0))
    return pl.pallas_call(
        _attn_sample_kernel,
        grid=(b,),
        in_specs=[cur, cur, cur, cache, cache, cur, cur],
        out_specs=cur,
        out_shape=jax.ShapeDtypeStruct((b, ts, a), BF16),
        scratch_shapes=[pltpu.VMEM((nh * ts, past), F32), pltpu.VMEM((nh * ts, past), F32)],
        compiler_params=_cparams(("arbitrary",)),
        name="attn_sample",
    )(q, kn, vn, kc, vc, sz, yl)


def _outproj_kernel(a_ref, w_ref, x_ref, mod_ref, g_ref, *rest):
    x1_ref, hpk_ref, hlo_ref, w_s = rest[-4:]
    nb = mod_ref.shape[0]
    tm, d = x_ref.shape

    @pl.when(pl.program_id(0) == 0)
    def _():
        w_s[...] = w_ref[...].astype(BF16)

    m = mod_ref[...]
    o = _dot(a_ref[...], w_s[...]).reshape(nb, tm // nb, d)
    x1 = x_ref[...].reshape(nb, tm // nb, d) + m[:, 2:3, :] * o
    h = _rms(x1) * g_ref[...].reshape(1, 1, d) * (1.0 + m[:, 4:5, :]) + m[:, 3:4, :]
    x1_ref[...] = x1.reshape(tm, d)
    h = h.reshape(tm, d)
    hi = h.astype(BF16).astype(F32)
    _store_token_tiles(hpk_ref, _pack_pairs(hi))
    hlo_ref[...] = (h - hi).astype(BF16)


def _outproj(a, w, x, mod, g, tm, rows_per_batch, n_total, row_offset, earlier=None):
    m, d = x.shape
    nb = max(tm // rows_per_batch, 1)
    per = max(rows_per_batch // tm, 1)
    off = row_offset // tm
    nsub = _tile_rows(d)
    row = lambda width: pl.BlockSpec((tm, width), lambda i: (i, 0))
    out_row = lambda width: pl.BlockSpec((tm, width), lambda i: (i + off, 0))
    in_specs = [row(d), _resident((d, d), lambda i: (0, 0)), row(d),
                pl.BlockSpec((nb, N_MOD, d), lambda i: (i // per, 0, 0)),
                pl.BlockSpec((1, d), lambda i: (0, 0))]
    args = [a, w, x, mod, g.reshape(1, d)]
    aliases = {}
    if earlier is not None:
        aliases = {len(args) + j: j for j in range(len(earlier))}
        in_specs += [pl.BlockSpec(memory_space=pl.ANY)] * len(earlier)
        args += list(earlier)
    return pl.pallas_call(
        _outproj_kernel,
        grid=(m // tm,),
        in_specs=in_specs,
        out_specs=[out_row(d), pl.BlockSpec((tm * nsub, LANES), lambda i: (i + off, 0)), out_row(d)],
        out_shape=[jax.ShapeDtypeStruct((n_total, d), F32), jax.ShapeDtypeStruct((n_total * nsub, LANES), U32),
                   jax.ShapeDtypeStruct((n_total, d), BF16)],
        scratch_shapes=[pltpu.VMEM((d, d), BF16)],
        input_output_aliases=aliases,
        compiler_params=_cparams(("arbitrary",)),
        name="outproj",
    )(*args)


def _sub_max(x):
    return jnp.max(x, axis=0, keepdims=True)


def _sub_min(x):
    return jnp.min(x, axis=0, keepdims=True)


def _sub_sum(x):
    return jnp.sum(x, axis=0, keepdims=True)


def _router_kernel(hpk_ref, hlo_ref, wrh_ref, wrl_ref, br_ref, e_ref, w_ref, slot_ref, cnt_ref, base_s):
    tm = hlo_ref.shape[0]

    @pl.when(pl.program_id(0) == 0)
    def _():
        base_s[...] = jnp.zeros_like(base_s)

    hi = _unpack_pairs(_load_token_tiles(hpk_ref, _tile_rows(hlo_ref.shape[1])), BF16)
    lo = hlo_ref[...]
    wrh = wrh_ref[...]
    logits = lax.dot_general(wrh, hi, NT_DIMS, preferred_element_type=F32)
    logits = logits + (lax.dot_general(wrh, lo, NT_DIMS, preferred_element_type=F32)
                       + lax.dot_general(wrl_ref[...], hi, NT_DIMS, preferred_element_type=F32))
    score = jax.nn.sigmoid(logits)
    sel = (score + br_ref[...]).reshape(GROUP_SIZE, N_GROUPS, tm)
    score = score.reshape(GROUP_SIZE, N_GROUPS, tm)

    m1 = sel[0]
    m2 = jnp.full_like(m1, NEG_INF)
    for e in range(1, GROUP_SIZE):
        x = sel[e]
        m2 = jnp.maximum(m2, jnp.minimum(m1, x))
        m1 = jnp.maximum(m1, x)
    cur = m1 + m2
    gidx = lax.broadcasted_iota(I32, (N_GROUPS, tm), 0)
    gmask = jnp.zeros((N_GROUPS, tm), jnp.bool_)
    for _ in range(TOPK_GROUPS):
        first = _sub_min(jnp.where(cur == _sub_max(cur), gidx, N_GROUPS))
        pick = gidx == first
        gmask = gmask | pick
        cur = jnp.where(pick, NEG_INF, cur)

    eid = (lax.broadcasted_iota(I32, (GROUP_SIZE, N_GROUPS, tm), 1) * GROUP_SIZE
           + lax.broadcasted_iota(I32, (GROUP_SIZE, N_GROUPS, tm), 0))
    cur = jnp.where(gmask[None], sel, NEG_INF)
    ids, wts, picks = [], [], []
    for _ in range(TOP_K):
        mx = _sub_max(jnp.max(cur, axis=0))
        first = _sub_min(jnp.min(jnp.where(cur == mx[None], eid, N_EXPERTS), axis=0))
        pick = eid == first[None]
        ids.append(first)
        picks.append(pick)
        wts.append(_sub_sum(jnp.sum(jnp.where(pick, score, 0.0), axis=0)))
        cur = jnp.where(pick, NEG_INF, cur)
    total = wts[0]
    chosen = picks[0]
    for k in range(1, TOP_K):
        total = total + wts[k]
        chosen = chosen | picks[k]

    chosen = jnp.where(chosen, 1.0, 0.0).reshape(N_EXPERTS, tm)
    earlier_or_self = (lax.broadcasted_iota(I32, (tm, tm), 0) <= lax.broadcasted_iota(I32, (tm, tm), 1)).astype(BF16)
    incl = _dot(chosen.astype(BF16), earlier_or_self)
    rank = (base_s[...] + (incl - chosen)).reshape(GROUP_SIZE, N_GROUPS, tm)
    base_s[...] = base_s[...] + incl[:, tm - 1:tm]
    slots = [_sub_sum(jnp.sum(jnp.where(pick, rank, 0.0), axis=0)).astype(I32) for pick in picks]

    pad = 8 - TOP_K
    e_ref[...] = jnp.concatenate(ids + [jnp.zeros((pad, tm), I32)], axis=0)
    w_ref[...] = jnp.concatenate([w / total * ROUTED_SCALE for w in wts] + [jnp.zeros((pad, tm), F32)], axis=0)
    slot_ref[...] = jnp.concatenate(slots + [jnp.zeros((pad, tm), I32)], axis=0)
    cnt_ref[...] = jnp.broadcast_to(base_s[...], cnt_ref.shape)


def _router(hpk, hlo, w_router, b_router, tm):
    n, d = hlo.shape
    wr = w_router.T.reshape(N_GROUPS, GROUP_SIZE, d).transpose(1, 0, 2).reshape(N_EXPERTS, d)
    br = b_router.reshape(N_GROUPS, GROUP_SIZE).T.reshape(N_EXPERTS, 1)
    wrh = wr.astype(BF16)
    wrl = (wr - wrh.astype(F32)).astype(BF16)
    full = lambda shape: pl.BlockSpec(shape, lambda i: (0, 0))
    per_token = pl.BlockSpec((8, tm), lambda i: (0, i))
    e_k, w_k, slot_k, counts = pl.pallas_call(
        _router_kernel,
        grid=(n // tm,),
        in_specs=[pl.BlockSpec((tm * _tile_rows(d), LANES), lambda i: (i, 0)),
                  pl.BlockSpec((tm, d), lambda i: (i, 0)),
                  full((N_EXPERTS, d)), full((N_EXPERTS, d)), full((N_EXPERTS, 1))],
        out_specs=[per_token, per_token, per_token, full((N_EXPERTS, 128))],
        out_shape=[jax.ShapeDtypeStruct((8, n), I32), jax.ShapeDtypeStruct((8, n), F32),
                   jax.ShapeDtypeStruct((8, n), I32), jax.ShapeDtypeStruct((N_EXPERTS, 128), F32)],
        scratch_shapes=[pltpu.VMEM((N_EXPERTS, 1), F32)],
        compiler_params=_cparams(("arbitrary",)),
        name="router",
    )(hpk, hlo, wrh, wrl, br)
    counts = counts[:, 0].reshape(GROUP_SIZE, N_GROUPS).T.reshape(N_EXPERTS).astype(I32)
    return e_k, w_k, slot_k, counts


def _tile_window(token, nsub):
    start = token * nsub
    if nsub % SUBLANES == 0:
        start = pl.multiple_of(start, SUBLANES)
    return pl.ds(start, nsub)


def _token_copy(src, src_token, dst, dst_token, nsub, sem):
    return pltpu.make_async_copy(src.at[_tile_window(src_token, nsub)], dst.at[_tile_window(dst_token, nsub)], sem)


def _for_token_groups(n_tokens, per_token):
    def group(g, carry):
        t0 = pl.multiple_of(g * SUBLANES, SUBLANES)
        for u in range(SUBLANES):
            per_token(t0, u)
        return carry

    lax.fori_loop(0, n_tokens // SUBLANES, group, 0)


def _dispatch_kernel(pad_end_ref, dest_ref, h_ref, xs_ref, zero_s, sem, zero_sem, *, nsub):
    tb = dest_ref.shape[0] // DEST_STRIDE
    te = zero_s.shape[0] // nsub

    def tail_fill(e):
        start = pl.multiple_of((pad_end_ref[e] - te) * nsub, te * nsub)
        return pltpu.make_async_copy(zero_s, xs_ref.at[pl.ds(start, te * nsub)], zero_sem)

    def has_rows(e):
        return pad_end_ref[e] > (pad_end_ref[e - 1] if e else 0)

    @pl.when(pl.program_id(0) == 0)
    def _():
        zero_s[...] = jnp.zeros_like(zero_s)
        for e in range(N_EXPERTS):
            pl.when(has_rows(e))(lambda e=e: tail_fill(e).start())
        for e in range(N_EXPERTS):
            pl.when(has_rows(e))(lambda e=e: tail_fill(e).wait())

    def start(t0, u):
        for k in range(TOP_K):
            _token_copy(h_ref, t0 + u, xs_ref, dest_ref[t0 * DEST_STRIDE + (u * DEST_STRIDE + k)], nsub,
                        sem).start(priority=k % 2)

    def wait(t0, u):
        for k in range(TOP_K):
            _token_copy(h_ref, 0, xs_ref, 0, nsub, sem).wait()

    _for_token_groups(tb, start)
    _for_token_groups(tb, wait)


def _dispatch(pad_end, dest, hpk, n_rows, tb, te, nsub):
    n = hpk.shape[0] // nsub
    grid_spec = pltpu.PrefetchScalarGridSpec(
        num_scalar_prefetch=1,
        grid=(n // tb,),
        in_specs=[pl.BlockSpec((tb * DEST_STRIDE,), lambda i, pe: (i,), memory_space=pltpu.SMEM),
                  pl.BlockSpec((tb * nsub, LANES), lambda i, pe: (i, 0))],
        out_specs=pl.BlockSpec(memory_space=pl.ANY),
        scratch_shapes=[pltpu.VMEM((te * nsub, LANES), U32), pltpu.SemaphoreType.DMA(()),
                        pltpu.SemaphoreType.DMA(())],
    )
    return pl.pallas_call(
        functools.partial(_dispatch_kernel, nsub=nsub),
        grid_spec=grid_spec,
        out_shape=jax.ShapeDtypeStruct((n_rows * nsub, LANES), U32),
        compiler_params=_cparams(("arbitrary",)),
        name="dispatch",
    )(pad_end, dest, hpk)


def _experts_kernel(be_ref, nu_ref, x_ref, wg_ref, wu_ref, wd_ref, o_ref, wg_s, wu_s, wd_s):
    i = pl.program_id(0)
    prev = be_ref[jnp.maximum(i - 1, 0)]

    @pl.when((i == 0) | (be_ref[i] != prev))
    def _():
        wg_s[...] = wg_ref[0].astype(BF16)
        wu_s[...] = wu_ref[0].astype(BF16)
        wd_s[...] = wd_ref[0].astype(BF16)

    @pl.when(i < nu_ref[0])
    def _():
        x = _unpack_pairs(_load_token_tiles(x_ref, _tile_rows(wg_s.shape[0])), BF16)
        g = _dot(x, wg_s[...])
        hidden = (g * jax.nn.sigmoid(g)) * _dot(x, wu_s[...])
        _store_token_tiles(o_ref, _pack_pairs(_dot(hidden.astype(BF16), wd_s[...])))

    @pl.when(i >= nu_ref[0])
    def _():
        o_ref[...] = jnp.zeros_like(o_ref)


def _experts(block_expert, n_used, xs, w_gate, w_up, w_down, tm):
    d, ff = w_gate.shape[1:]
    nsub = _tile_rows(d)
    p = xs.shape[0] // nsub
    rows = pl.BlockSpec((tm * nsub, LANES), lambda i, be, nu: (i, 0))
    grid_spec = pltpu.PrefetchScalarGridSpec(
        num_scalar_prefetch=2,
        grid=(p // tm,),
        in_specs=[rows,
                  pl.BlockSpec((1, d, ff), lambda i, be, nu: (be[i], 0, 0)),
                  pl.BlockSpec((1, d, ff), lambda i, be, nu: (be[i], 0, 0)),
                  pl.BlockSpec((1, ff, d), lambda i, be, nu: (be[i], 0, 0))],
        out_specs=rows,
        scratch_shapes=[pltpu.VMEM((d, ff), BF16), pltpu.VMEM((d, ff), BF16), pltpu.VMEM((ff, d), BF16)],
    )
    return pl.pallas_call(
        _experts_kernel,
        grid_spec=grid_spec,
        out_shape=jax.ShapeDtypeStruct(xs.shape, U32),
        compiler_params=_cparams(("arbitrary",)),
        name="experts",
    )(block_expert, n_used, xs, w_gate, w_up, w_down)


def _combine_kernel(dest_ref, wk_ref, ys_ref, x1_ref, hpk_ref, gt_ref, wsg_ref, wsu_ref, wsd_ref, gf_ref,
                    yp_ref, ysm_ref, buf, sem, *, n_prompt_blocks):
    tb, d = x1_ref.shape
    nsub = _tile_rows(d)
    seg = gt_ref.shape[0]
    i = pl.program_id(0)

    def start(t0, u):
        for k in range(TOP_K):
            _token_copy(ys_ref, dest_ref[t0 * DEST_STRIDE + (u * DEST_STRIDE + k)], buf.at[k], t0 + u, nsub,
                        sem).start(priority=k % 2)

    def wait(t0, u):
        for k in range(TOP_K):
            _token_copy(ys_ref, 0, buf.at[k], 0, nsub, sem).wait()

    _for_token_groups(tb, start)
    x = _unpack_pairs(_load_token_tiles(hpk_ref, nsub), BF16)
    g = _dot(x, wsg_ref[...])
    ffn = _dot(((g * jax.nn.sigmoid(g)) * _dot(x, wsu_ref[...])).astype(BF16), wsd_ref[...])
    _for_token_groups(tb, wait)
    wk = wk_ref[...]
    for k in range(TOP_K):
        ffn = ffn + wk[:, k:k + 1] * _unpack_pairs(_load_token_tiles(buf.at[k], nsub), F32)
    x2 = x1_ref[...].reshape(seg, tb // seg, d) + gt_ref[...] * ffn.reshape(seg, tb // seg, d)
    y = (_rms(x2) * gf_ref[...].reshape(1, 1, d)).reshape(tb, d)

    @pl.when(i < n_prompt_blocks)
    def _():
        yp_ref[...] = y

    @pl.when(i >= n_prompt_blocks)
    def _():
        ysm_ref[...] = y


def _combine(dest, wk_t, ys, x1, hpk, gt2_seg, ws_gate, ws_up, ws_down, g_final, n_prompt, tb):
    n, d = x1.shape
    nsub = _tile_rows(d)
    ff = ws_gate.shape[1]
    npb = n_prompt // tb
    nsb = (n - n_prompt) // tb
    seg = gt2_seg.shape[0] * tb // n
    row = lambda width: pl.BlockSpec((tb, width), lambda i: (i, 0))
    full = lambda shape: pl.BlockSpec(shape, lambda i: (0,) * len(shape))
    return pl.pallas_call(
        functools.partial(_combine_kernel, n_prompt_blocks=npb),
        grid=(n // tb,),
        in_specs=[pl.BlockSpec((tb * DEST_STRIDE,), lambda i: (i,), memory_space=pltpu.SMEM),
                  row(8), pl.BlockSpec(memory_space=pl.ANY), row(d),
                  pl.BlockSpec((tb * nsub, LANES), lambda i: (i, 0)),
                  pl.BlockSpec((seg, 1, d), lambda i: (i, 0, 0)),
                  full((d, ff)), full((d, ff)), full((ff, d)), full((1, d))],
        out_specs=[pl.BlockSpec((tb, d), lambda i: (jnp.minimum(i, npb - 1), 0)),
                   pl.BlockSpec((tb, d), lambda i: (jnp.clip(i - npb, 0, nsb - 1), 0))],
        out_shape=[jax.ShapeDtypeStruct((n_prompt, d), F32), jax.ShapeDtypeStruct((n - n_prompt, d), F32)],
        scratch_shapes=[pltpu.VMEM((TOP_K, tb * nsub, LANES), U32), pltpu.SemaphoreType.DMA(())],
        compiler_params=_cparams(("arbitrary",)),
        name="combine",
    )(dest, wk_t, ys, x1, hpk, gt2_seg, ws_gate, ws_up, ws_down, g_final.reshape(1, d))


def _largest_tile(n, target):
    t = min(n, target)
    while n % t:
        t -= 1
    return t


def _mixer(x, mod, past_k, past_v, h0, conv0, p, first_pos_zero, n_total, row_offset, earlier):
    b, t, d = x.shape
    m = b * t
    a = d
    hn_bm, hn_tm = _norm_mod(x, mod, p["g_mix"])
    hn_bm = hn_bm.reshape(m, d)
    tm = _largest_tile(m, 512)
    tmh = _largest_tile(m, 1024)
    w_in = p["w_in"]
    u = _proj(hn_tm, w_in, 0, d, "plain", BF16, tmh).reshape(t, b, d)
    gl = _proj(hn_tm, w_in, 1, d, "gelu", BF16, tmh).reshape(t, b, d)
    sl = _proj(hn_tm, w_in, 5, d, "sigmoid", BF16, tmh).reshape(t, b, d)
    q = _proj(hn_bm, w_in, 2, a, "qscale", BF16, tmh).reshape(b, t, a)
    k = _proj(hn_bm, w_in, 3, a, "plain", F32, tm).reshape(b, t, a)
    v = _proj(hn_bm, w_in, 4, a, "plain", F32, tm).reshape(b, t, a)
    sz = _proj(hn_bm, w_in, 6, a, "sigmoid", BF16, tmh).reshape(b, t, a)

    if h0 is None:
        h0 = jnp.zeros((b, d), F32)
        prev = jnp.zeros((CONV_WIDTH - 1, b, d), F32)
    else:
        prev = conv0.transpose(1, 0, 2)
    tc = _largest_tile(t, 128)
    yl, h_last, conv_tail = _lru(u, gl, sl, p["conv_w"], p["conv_b"], p["w_rg_a"], p["b_rg_a"], p["w_rg_x"],
                                 p["b_rg_x"], p["lru_lambda"], h0, prev, tc, first_pos_zero)
    if past_k is None:
        mixed = _attn_prompt(q, k, v, sz, yl, _largest_tile(t, 256))
    else:
        mixed = _attn_sample(q, k, v, past_k, past_v, sz, yl)
    tmo = _largest_tile(m, 256)
    shared = _outproj(mixed.reshape(m, a), p["w_out"], x.reshape(m, d), mod, p["g_ffn"], tmo, t, n_total, row_offset,
                      earlier)
    nh = a // HEAD_DIM
    return (shared, k.reshape(b, t, nh, HEAD_DIM), v.reshape(b, t, nh, HEAD_DIM), h_last,
            conv_tail.transpose(1, 0, 2))


def _routing_tables(e_k, slot_k, counts, tm):
    n_tokens = e_k.shape[1]
    padded = (counts + tm - 1) // tm * tm
    pad_end = jnp.cumsum(padded).astype(I32)
    pad_start = pad_end - padded
    is_expert = e_k[:, :, None] == jnp.arange(N_EXPERTS, dtype=I32)
    dest = slot_k + jnp.sum(jnp.where(is_expert, pad_start, 0), axis=-1).astype(I32)
    n_blocks = -(-(TOP_K * n_tokens + N_EXPERTS * (tm - 1)) // tm)
    block_start = jnp.arange(n_blocks, dtype=I32) * tm
    block_expert = jnp.minimum(jnp.sum(pad_end[None, :] <= block_start[:, None], axis=1), N_EXPERTS - 1).astype(I32)
    n_used = (pad_end[-1:] // tm).astype(I32)
    return dest, pad_end, block_expert, n_used, n_blocks * tm


def kernel(x_prompt, x_sample, cache_k, cache_v, state_lru, state_conv, c_prompt, c_sample, g_mix, w_ada, b_ada, w_in, conv_w, conv_b, w_rg_a, b_rg_a, w_rg_x, b_rg_x, lru_lambda, w_out, g_ffn, w_router, b_router, w_gate, w_up, w_down, ws_gate, ws_up, ws_down, g_final):
    assert g_mix.shape[0] == 1, "single-layer trunk only"
    b, t, d = x_prompt.shape
    bs, ts, _ = x_sample.shape
    assert ts >= CONV_WIDTH - 1 and ts == TIME_GROUP and t % TIME_GROUP == 0
    p = dict(g_mix=g_mix[0], w_in=w_in[0], conv_w=conv_w[0], conv_b=conv_b[0],
             w_rg_a=w_rg_a[0].astype(BF16), b_rg_a=b_rg_a[0], w_rg_x=w_rg_x[0].astype(BF16), b_rg_x=b_rg_x[0],
             lru_lambda=lru_lambda[0], w_out=w_out[0], g_ffn=g_ffn[0])
    mod = _ada(jnp.concatenate([c_prompt, c_sample], axis=0), w_ada[0], b_ada[0]).reshape(b + bs, N_MOD, d)
    mod_p, mod_s = mod[:b], mod[b:]

    n_prompt = b * t
    n = n_prompt + bs * ts
    tb = _largest_tile(bs * ts, 256)
    assert n_prompt % tb == 0 and tb % ts == 0

    shared, kp, vp, hp, cp = _mixer(x_prompt, mod_p, None, None, None, None, p, True, n, 0, None)
    (x1, hpk, hlo), ks, vs, hs, cs = _mixer(x_sample, mod_s, cache_k[0], cache_v[0], state_lru[0], state_conv[0],
                                            p, False, n, n_prompt, shared)

    e_k, w_k, slot_k, counts = _router(hpk, hlo, w_router[0], b_router[0], _largest_tile(n, ROUTER_TOKENS))
    tme = EXPERT_ROW_BLOCK
    dest, pad_end, block_expert, n_used, n_rows = _routing_tables(e_k, slot_k, counts, tme)
    dest = dest.T.reshape(-1)
    xs = _dispatch(pad_end, dest, hpk, n_rows, tb, tme, _tile_rows(d))
    ys = _experts(block_expert, n_used, xs, w_gate[0], w_up[0], w_down[0], tme)
    gt2 = jnp.concatenate([jnp.broadcast_to(mod_p[:, None, 5:6, :], (b, t // ts, 1, d)).reshape(-1, 1, d),
                           mod_s[:, 5:6, :]], axis=0)
    y_p, y_s = _combine(dest, w_k.T, ys, x1, hpk, gt2, ws_gate[0].astype(BF16), ws_up[0].astype(BF16),
                        ws_down[0].astype(BF16), g_final, n_prompt, tb)
    return (y_p.reshape(b, t, d), y_s.reshape(bs, ts, d), kp[None], vp[None], hp[None], cp[None],
            ks[None], vs[None], hs[None], cs[None])
```

```python
import functools

import jax
import jax.numpy as jnp
from jax import lax
from jax.experimental import pallas as pl
from jax.experimental.pallas import tpu as pltpu

F32 = jnp.float32
BF16 = jnp.bfloat16
I32 = jnp.int32
U32 = jnp.uint32

LANES = 128
SUBLANES = 8
HEAD_DIM = 128
LRU_BLOCK_DIM = 256
LRU_C = 8.0
CONV_WIDTH = 4
N_EXPERTS = 64
TOP_K = 6
N_GROUPS = 8
GROUP_SIZE = N_EXPERTS // N_GROUPS
TOPK_GROUPS = 4
ROUTED_SCALE = 2.5
NORM_EPS = 1e-6
N_MOD = 6
TIME_GROUP = 16
LOG2_E = 1.4426950408889634
ATTN_HEADS_PER_STEP = 8
F32_ZERO_WEIGHT_LOG2 = -150.0
ROUTER_TOKENS = 768
DEST_STRIDE = 8
EXPERT_ROW_BLOCK = 512
V7X_VMEM_LIMIT_BYTES = 56 * 1024 * 1024
NT_DIMS = (((1,), (1,)), ((), ()))
NEG_INF = float("-inf")


def _cparams(semantics):
    return pltpu.CompilerParams(dimension_semantics=semantics, vmem_limit_bytes=V7X_VMEM_LIMIT_BYTES)


def _dot(a, b):
    return jnp.dot(a, b, preferred_element_type=F32)


def _split_bf16(x):
    hi = x.astype(BF16)
    lo = (x - hi.astype(F32)).astype(BF16)
    return hi, lo


def _dot3(a, b):
    ah, al = _split_bf16(a)
    bh, bl = _split_bf16(b)
    return _dot(ah, bh) + (_dot(al, bh) + _dot(ah, bl))


def _pack_pairs(h):
    n = h.shape[1] // 2
    bits = lax.bitcast_convert_type(h.astype(BF16).astype(F32), U32)
    return bits[:, :n] | (bits[:, n:] >> 16)


def _store_token_tiles(ref, packed):
    m, n = packed.shape
    nsub = n // LANES
    for j in range(nsub):
        ref[pl.ds(j, m, stride=nsub), :] = packed[:, j * LANES:(j + 1) * LANES]


def _tile_rows(d):
    return d // 2 // LANES


def _load_token_tiles(ref, nsub):
    m = ref.shape[0] // nsub
    return jnp.concatenate([ref[pl.ds(j, m, stride=nsub), :] for j in range(nsub)], axis=1)


def _unpack_pairs(p, dtype):
    a = lax.bitcast_convert_type(p & jnp.uint32(0xFFFF0000), F32)
    b = lax.bitcast_convert_type(p << 16, F32)
    return jnp.concatenate([a.astype(dtype), b.astype(dtype)], axis=1)


def _rms(x):
    return x * lax.rsqrt(jnp.mean(x * x, axis=-1, keepdims=True) + NORM_EPS)


def _ada_kernel(c_ref, w_ref, b_ref, o_ref):
    c = c_ref[...]
    o_ref[...] = _dot3(c * jax.nn.sigmoid(c), w_ref[...]) + b_ref[...]


def _ada(c, w, b):
    m, d = c.shape
    n = w.shape[1]
    tn = min(n, 1024)
    return pl.pallas_call(
        _ada_kernel,
        grid=(n // tn,),
        in_specs=[pl.BlockSpec((m, d), lambda j: (0, 0)),
                  pl.BlockSpec((d, tn), lambda j: (0, j)),
                  pl.BlockSpec((1, tn), lambda j: (0, j))],
        out_specs=pl.BlockSpec((m, tn), lambda j: (0, j)),
        out_shape=jax.ShapeDtypeStruct((m, n), F32),
        compiler_params=_cparams(("arbitrary",)),
        name="ada",
    )(c, w, b.reshape(1, n))


def _row_permutation(n_outer, n_inner):
    n = n_outer * n_inner
    out_row = lax.broadcasted_iota(I32, (n, n), 0)
    in_row = lax.broadcasted_iota(I32, (n, n), 1)
    same = (out_row // n_outer == in_row % n_inner) & (out_row % n_outer == in_row // n_inner)
    return same.astype(BF16)


def _norm_mod_kernel(x_ref, mod_ref, g_ref, obm_ref, otm_ref):
    b, tt, d = x_ref.shape
    m = mod_ref[...]
    h = (_rms(x_ref[...]) * g_ref[...].reshape(1, 1, d) * (1.0 + m[:, 1:2, :]) + m[:, 0:1, :]).astype(BF16)
    obm_ref[...] = h
    otm_ref[...] = _dot(_row_permutation(b, tt), h.reshape(b * tt, d)).astype(BF16)


def _norm_mod(x, mod, g):
    b, t, d = x.shape
    tt = TIME_GROUP
    return pl.pallas_call(
        _norm_mod_kernel,
        grid=(t // tt,),
        in_specs=[pl.BlockSpec((b, tt, d), lambda ti: (0, ti, 0)),
                  pl.BlockSpec((b, N_MOD, d), lambda ti: (0, 0, 0)),
                  pl.BlockSpec((1, d), lambda ti: (0, 0))],
        out_specs=[pl.BlockSpec((b, tt, d), lambda ti: (0, ti, 0)),
                   pl.BlockSpec((tt * b, d), lambda ti: (ti, 0))],
        out_shape=[jax.ShapeDtypeStruct((b, t, d), BF16), jax.ShapeDtypeStruct((t * b, d), BF16)],
        compiler_params=_cparams(("arbitrary",)),
        name="norm_mod",
    )(x, mod, g.reshape(1, d))


def _proj_kernel(a_ref, w_ref, o_ref, w_s, *, epilogue):
    @pl.when(pl.program_id(0) == 0)
    def _():
        w_s[...] = w_ref[...].astype(BF16)

    acc = _dot(a_ref[...], w_s[...])
    if epilogue == "gelu":
        acc = jax.nn.gelu(acc)
    elif epilogue == "sigmoid":
        acc = jax.nn.sigmoid(acc)
    elif epilogue == "qscale":
        acc = acc * (HEAD_DIM ** -0.5 * LOG2_E)
    o_ref[...] = acc.astype(o_ref.dtype)


def _resident(shape, index_map):
    return pl.BlockSpec(shape, index_map, pipeline_mode=pl.Buffered(1))


def _proj(a, w, col, width, epilogue, out_dtype, tm):
    m, k = a.shape
    return pl.pallas_call(
        functools.partial(_proj_kernel, epilogue=epilogue),
        grid=(m // tm,),
        in_specs=[pl.BlockSpec((tm, k), lambda i: (i, 0)),
                  _resident((k, width), lambda i: (0, col))],
        out_specs=pl.BlockSpec((tm, width), lambda i: (i, 0)),
        out_shape=jax.ShapeDtypeStruct((m, width), out_dtype),
        scratch_shapes=[pltpu.VMEM((k, width), BF16)],
        compiler_params=_cparams(("arbitrary",)),
        name="proj_" + epilogue,
    )(a, w)


def _lru_kernel(u_ref, gl_ref, sl_ref, cw_ref, cb_ref, wa_ref, ba_ref, wx_ref, bx_ref, lam_ref, h0_ref, prev_ref,
                y_ref, hl_ref, cn_ref, tail_s, h_s, a_s, b_s, y_s, *, first_pos_zero):
    tc, nb, wb = u_ref.shape
    ti = pl.program_id(1)

    @pl.when(ti == 0)
    def _():
        tail_s[...] = prev_ref[...]
        h_s[...] = h0_ref[...]

    u = u_ref[...].astype(F32)
    xp = jnp.concatenate([tail_s[...], u], axis=0)
    cw = cw_ref[...]
    xc = cb_ref[...].reshape(1, 1, wb)
    for j in range(CONV_WIDTH):
        xc = xc + xp[j:j + tc] * cw[j:j + 1].reshape(1, 1, wb)
    last_rows = u[tc - (CONV_WIDTH - 1):]
    tail_s[...] = last_rows
    cn_ref[...] = last_rows

    x2 = xc.reshape(tc * nb, wb)
    xb = x2.astype(BF16)
    r = jax.nn.sigmoid(_dot(xb, wa_ref[0]) + ba_ref[0])
    gi = jax.nn.sigmoid(_dot(xb, wx_ref[0]) + bx_ref[0])
    log_a = r * ((-LRU_C) * jax.nn.softplus(-lam_ref[...]))
    a = jnp.exp(log_a)
    m2 = -jnp.tanh(log_a) * (a * a + 1.0)
    mult = jnp.where(m2 > 0.0, m2 * lax.rsqrt(m2), 0.0)
    if first_pos_zero:
        row = lax.broadcasted_iota(I32, (tc * nb, wb), 0)
        mult = jnp.where((row < nb) & (ti == 0), 1.0, mult)
    a_s[...] = a.reshape(tc, nb, wb)
    b_s[...] = (mult * (gi * x2)).reshape(tc, nb, wb)

    def step(t, h):
        h = a_s[t] * h + b_s[t]
        gate = gl_ref[t].astype(F32) * sl_ref[t].astype(F32)
        y_s[t] = h * gate
        return h

    h = lax.fori_loop(0, tc, step, h_s[...], unroll=8)
    h_s[...] = h
    hl_ref[...] = h

    perm = _row_permutation(TIME_GROUP, nb)
    for g in range(tc // TIME_GROUP):
        rows = slice(g * TIME_GROUP, (g + 1) * TIME_GROUP)
        yg = y_s[rows].reshape(TIME_GROUP * nb, wb).astype(BF16)
        y_ref[:, rows, :] = _dot(perm, yg).astype(BF16).reshape(nb, TIME_GROUP, wb)


def _lru(u, gl, sl, conv_w, conv_b, wa, ba, wx, bx, lam, h0, prev, tc, first_pos_zero):
    t, b, w = u.shape
    wb = LRU_BLOCK_DIM
    nw = w // wb
    seq = pl.BlockSpec((tc, b, wb), lambda wi, ti: (ti, 0, wi))
    seq_out = pl.BlockSpec((b, tc, wb), lambda wi, ti: (0, ti, wi))
    chan = lambda rows: pl.BlockSpec((rows, wb), lambda wi, ti: (0, wi))
    gate_w = pl.BlockSpec((1, wb, wb), lambda wi, ti: (wi, 0, 0))
    gate_b = pl.BlockSpec((1, 1, wb), lambda wi, ti: (wi, 0, 0))
    tail = pl.BlockSpec((CONV_WIDTH - 1, b, wb), lambda wi, ti: (0, 0, wi))
    return pl.pallas_call(
        functools.partial(_lru_kernel, first_pos_zero=first_pos_zero),
        grid=(nw, t // tc),
        in_specs=[seq, seq, seq, chan(CONV_WIDTH), chan(1), gate_w, gate_b, gate_w, gate_b, chan(1),
                  chan(b), tail],
        out_specs=[seq_out, chan(b), tail],
        out_shape=[jax.ShapeDtypeStruct((b, t, w), BF16), jax.ShapeDtypeStruct((b, w), F32),
                   jax.ShapeDtypeStruct((CONV_WIDTH - 1, b, w), F32)],
        scratch_shapes=[pltpu.VMEM((CONV_WIDTH - 1, b, wb), F32), pltpu.VMEM((b, wb), F32),
                        pltpu.VMEM((tc, b, wb), F32), pltpu.VMEM((tc, b, wb), F32),
                        pltpu.VMEM((tc, b, wb), F32)],
        compiler_params=_cparams(("arbitrary", "arbitrary")),
        name="lru",
    )(u, gl, sl, conv_w, conv_b.reshape(1, w), wa, ba.reshape(nw, 1, wb), wx, bx.reshape(nw, 1, wb),
      lam.reshape(1, w), h0, prev)


def _log_keep_and_logsig(z):
    nz = -z
    lk = jnp.minimum(nz, 0.0) - jnp.log2(1.0 + jnp.exp2(jnp.minimum(z, nz)))
    return lk, z + lk


def _later_key_sums(lk, tri):
    hi, lo = _split_bf16(lk)
    return _dot(hi, tri) + _dot(lo, tri)


def _strict_lower(n, dtype):
    row = lax.broadcasted_iota(I32, (n, n), 0)
    col = lax.broadcasted_iota(I32, (n, n), 1)
    return (row > col).astype(dtype), col < row


def _attn_kernel(q_ref, k_ref, v_ref, sz_ref, yl_ref, o_ref):
    tq = q_ref.shape[1]
    nh = q_ref.shape[2] // HEAD_DIM
    qi = pl.program_id(2)
    tri, causal = _strict_lower(tq, BF16)

    def tile(j, carry, diag):
        start = pl.multiple_of(j * tq, tq)
        heads = [slice(h * HEAD_DIM, (h + 1) * HEAD_DIM) for h in range(nh)]
        zs = [lax.dot_general(q_ref[0, :, cols], k_ref[0, pl.ds(start, tq), cols].astype(BF16), NT_DIMS,
                              preferred_element_type=F32) for cols in heads]
        lks, lss = [], []
        for z in zs:
            lk, ls = _log_keep_and_logsig(z)
            lks.append(jnp.where(causal, lk, 0.0) if diag else lk)
            lss.append(ls)
        rcs = [_later_key_sums(lk, tri) for lk in lks]
        ws = []
        for h in range(nh):
            w = jnp.exp2(lss[h] + rcs[h] + carry[h][1])
            ws.append((jnp.where(causal, w, 0.0) if diag else w).astype(BF16))
        return tuple((carry[h][0] + _dot(ws[h], v_ref[0, pl.ds(start, tq), heads[h]].astype(BF16)),
                      carry[h][1] + (rcs[h][:, 0:1] + lks[h][:, 0:1])) for h in range(nh))

    def any_weight_left(carry):
        c_max = carry[0][1]
        for h in range(1, nh):
            c_max = jnp.maximum(c_max, carry[h][1])
        return jnp.max(c_max) > F32_ZERO_WEIGHT_LOG2

    init = tuple((jnp.zeros((tq, HEAD_DIM), F32), jnp.zeros((tq, 1), F32)) for _ in range(nh))
    carry = tile(qi, init, True)

    def more(state):
        return (state[0] < qi) & state[1]

    def farther_tile(state):
        carry = tile(qi - 1 - state[0], state[2], False)
        return state[0] + 1, any_weight_left(carry), carry

    carry = lax.while_loop(more, farther_tile, (jnp.int32(0), any_weight_left(carry), carry))[2]
    acc = jnp.concatenate([carry[h][0] for h in range(nh)], axis=1)
    o_ref[0] = (yl_ref[0].astype(F32) + sz_ref[0].astype(F32) * acc).astype(BF16)


def _attn_prompt(q, k, v, sz, yl, tq):
    b, t, a = q.shape
    wblk = min(ATTN_HEADS_PER_STEP * HEAD_DIM, a)
    qspec = pl.BlockSpec((1, tq, wblk), lambda bi, hi, qi: (bi, qi, hi))
    kvspec = pl.BlockSpec((1, t, wblk), lambda bi, hi, qi: (bi, 0, hi))
    return pl.pallas_call(
        _attn_kernel,
        grid=(b, a // wblk, t // tq),
        in_specs=[qspec, kvspec, kvspec, qspec, qspec],
        out_specs=qspec,
        out_shape=jax.ShapeDtypeStruct((b, t, a), BF16),
        compiler_params=_cparams(("arbitrary", "arbitrary", "arbitrary")),
        name="attn_prompt",
    )(q, k, v, sz, yl)


def _attn_sample_kernel(q_ref, kn_ref, vn_ref, kc_ref, vc_ref, sz_ref, yl_ref, o_ref, lk_s, ls_s):
    ts = q_ref.shape[1]
    nh = q_ref.shape[2] // HEAD_DIM
    past = kc_ref.shape[1] // nh
    tri_n, causal = _strict_lower(ts, BF16)
    tri_p, _ = _strict_lower(past, BF16)
    accs, carries = [], []
    for h in range(nh):
        cols = slice(h * HEAD_DIM, (h + 1) * HEAD_DIM)
        q = q_ref[0, :, cols]
        zn = lax.dot_general(q, kn_ref[0, :, cols].astype(BF16), NT_DIMS, preferred_element_type=F32)
        lk, ls = _log_keep_and_logsig(zn)
        lk = jnp.where(causal, lk, 0.0)
        rc = _later_key_sums(lk, tri_n)
        w = jnp.where(causal, jnp.exp2(ls + rc), 0.0)
        accs.append(_dot(w.astype(BF16), vn_ref[0, :, cols].astype(BF16)))
        carries.append(rc[:, 0:1] + lk[:, 0:1])
        head_rows = pl.ds(h, past, stride=nh)
        zp = lax.dot_general(q, kc_ref[0, head_rows, :].astype(BF16), NT_DIMS, preferred_element_type=F32)
        lkp, lsp = _log_keep_and_logsig(zp)
        lk_s[h * ts:(h + 1) * ts, :] = lkp
        ls_s[h * ts:(h + 1) * ts, :] = lsp
    rcp = _later_key_sums(lk_s[...], tri_p)
    for h in range(nh):
        cols = slice(h * HEAD_DIM, (h + 1) * HEAD_DIM)
        rows = slice(h * ts, (h + 1) * ts)
        w = jnp.exp2(ls_s[rows, :] + rcp[rows, :] + carries[h])
        acc = accs[h] + _dot(w.astype(BF16), vc_ref[0, pl.ds(h, past, stride=nh), :].astype(BF16))
        o_ref[0, :, cols] = (yl_ref[0, :, cols].astype(F32) + sz_ref[0, :, cols].astype(F32) * acc).astype(BF16)


def _attn_sample(q, kn, vn, kc, vc, sz, yl):
    b, ts, a = q.shape
    past = kc.shape[1]
    nh = a // HEAD_DIM
    kc = kc.reshape(b, past * nh, HEAD_DIM)
    vc = vc.reshape(b, past * nh, HEAD_DIM)
    cur = pl.BlockSpec((1, ts, a), lambda bi: (bi, 0, 0))
    cache = pl.BlockSpec((1, past * nh, HEAD_DIM), lambda bi: (bi, 0, 0))
    return pl.pallas_call(
        _attn_sample_kernel,
        grid=(b,),
        in_specs=[cur, cur, cur, cache, cache, cur, cur],
        out_specs=cur,
        out_shape=jax.ShapeDtypeStruct((b, ts, a), BF16),
        scratch_shapes=[pltpu.VMEM((nh * ts, past), F32), pltpu.VMEM((nh * ts, past), F32)],
        compiler_params=_cparams(("arbitrary",)),
        name="attn_sample",
    )(q, kn, vn, kc, vc, sz, yl)


def _outproj_kernel(a_ref, w_ref, x_ref, mod_ref, g_ref, *rest):
    x1_ref, hpk_ref, hlo_ref, w_s = rest[-4:]
    nb = mod_ref.shape[0]
    tm, d = x_ref.shape

    @pl.when(pl.program_id(0) == 0)
    def _():
        w_s[...] = w_ref[...].astype(BF16)

    m = mod_ref[...]
    o = _dot(a_ref[...], w_s[...]).reshape(nb, tm // nb, d)
    x1 = x_ref[...].reshape(nb, tm // nb, d) + m[:, 2:3, :] * o
    h = _rms(x1) * g_ref[...].reshape(1, 1, d) * (1.0 + m[:, 4:5, :]) + m[:, 3:4, :]
    x1_ref[...] = x1.reshape(tm, d)
    h = h.reshape(tm, d)
    hi = h.astype(BF16).astype(F32)
    _store_token_tiles(hpk_ref, _pack_pairs(hi))
    hlo_ref[...] = (h - hi).astype(BF16)


def _outproj(a, w, x, mod, g, tm, rows_per_batch, n_total, row_offset, earlier=None):
    m, d = x.shape
    nb = max(tm // rows_per_batch, 1)
    per = max(rows_per_batch // tm, 1)
    off = row_offset // tm
    nsub = _tile_rows(d)
    row = lambda width: pl.BlockSpec((tm, width), lambda i: (i, 0))
    out_row = lambda width: pl.BlockSpec((tm, width), lambda i: (i + off, 0))
    in_specs = [row(d), _resident((d, d), lambda i: (0, 0)), row(d),
                pl.BlockSpec((nb, N_MOD, d), lambda i: (i // per, 0, 0)),
                pl.BlockSpec((1, d), lambda i: (0, 0))]
    args = [a, w, x, mod, g.reshape(1, d)]
    aliases = {}
    if earlier is not None:
        aliases = {len(args) + j: j for j in range(len(earlier))}
        in_specs += [pl.BlockSpec(memory_space=pl.ANY)] * len(earlier)
        args += list(earlier)
    return pl.pallas_call(
        _outproj_kernel,
        grid=(m // tm,),
        in_specs=in_specs,
        out_specs=[out_row(d), pl.BlockSpec((tm * nsub, LANES), lambda i: (i + off, 0)), out_row(d)],
        out_shape=[jax.ShapeDtypeStruct((n_total, d), F32), jax.ShapeDtypeStruct((n_total * nsub, LANES), U32),
                   jax.ShapeDtypeStruct((n_total, d), BF16)],
        scratch_shapes=[pltpu.VMEM((d, d), BF16)],
        input_output_aliases=aliases,
        compiler_params=_cparams(("arbitrary",)),
        name="outproj",
    )(*args)


def _sub_max(x):
    return jnp.max(x, axis=0, keepdims=True)


def _sub_min(x):
    return jnp.min(x, axis=0, keepdims=True)


def _sub_sum(x):
    return jnp.sum(x, axis=0, keepdims=True)


def _router_kernel(hpk_ref, hlo_ref, wrh_ref, wrl_ref, br_ref, e_ref, w_ref, slot_ref, cnt_ref, base_s):
    tm = hlo_ref.shape[0]

    @pl.when(pl.program_id(0) == 0)
    def _():
        base_s[...] = jnp.zeros_like(base_s)

    hi = _unpack_pairs(_load_token_tiles(hpk_ref, _tile_rows(hlo_ref.shape[1])), BF16)
    lo = hlo_ref[...]
    wrh = wrh_ref[...]
    logits = lax.dot_general(wrh, hi, NT_DIMS, preferred_element_type=F32)
    logits = logits + (lax.dot_general(wrh, lo, NT_DIMS, preferred_element_type=F32)
                       + lax.dot_general(wrl_ref[...], hi, NT_DIMS, preferred_element_type=F32))
    score = jax.nn.sigmoid(logits)
    sel = (score + br_ref[...]).reshape(GROUP_SIZE, N_GROUPS, tm)
    score = score.reshape(GROUP_SIZE, N_GROUPS, tm)

    m1 = sel[0]
    m2 = jnp.full_like(m1, NEG_INF)
    for e in range(1, GROUP_SIZE):
        x = sel[e]
        m2 = jnp.maximum(m2, jnp.minimum(m1, x))
        m1 = jnp.maximum(m1, x)
    cur = m1 + m2
    gidx = lax.broadcasted_iota(I32, (N_GROUPS, tm), 0)
    gmask = jnp.zeros((N_GROUPS, tm), jnp.bool_)
    for _ in range(TOPK_GROUPS):
        first = _sub_min(jnp.where(cur == _sub_max(cur), gidx, N_GROUPS))
        pick = gidx == first
        gmask = gmask | pick
        cur = jnp.where(pick, NEG_INF, cur)

    eid = (lax.broadcasted_iota(I32, (GROUP_SIZE, N_GROUPS, tm), 1) * GROUP_SIZE
           + lax.broadcasted_iota(I32, (GROUP_SIZE, N_GROUPS, tm), 0))
    cur = jnp.where(gmask[None], sel, NEG_INF)
    ids, wts, picks = [], [], []
    for _ in range(TOP_K):
        mx = _sub_max(jnp.max(cur, axis=0))
        first = _sub_min(jnp.min(jnp.where(cur == mx[None], eid, N_EXPERTS), axis=0))
        pick = eid == first[None]
        ids.append(first)
        picks.append(pick)
        wts.append(_sub_sum(jnp.sum(jnp.where(pick, score, 0.0), axis=0)))
        cur = jnp.where(pick, NEG_INF, cur)
    total = wts[0]
    chosen = picks[0]
    for k in range(1, TOP_K):
        total = total + wts[k]
        chosen = chosen | picks[k]

    chosen = jnp.where(chosen, 1.0, 0.0).reshape(N_EXPERTS, tm)
    earlier_or_self = (lax.broadcasted_iota(I32, (tm, tm), 0) <= lax.broadcasted_iota(I32, (tm, tm), 1)).astype(BF16)
    incl = _dot(chosen.astype(BF16), earlier_or_self)
    rank = (base_s[...] + (incl - chosen)).reshape(GROUP_SIZE, N_GROUPS, tm)
    base_s[...] = base_s[...] + incl[:, tm - 1:tm]
    slots = [_sub_sum(jnp.sum(jnp.where(pick, rank, 0.0), axis=0)).astype(I32) for pick in picks]

    pad = 8 - TOP_K
    e_ref[...] = jnp.concatenate(ids + [jnp.zeros((pad, tm), I32)], axis=0)
    w_ref[...] = jnp.concatenate([w / total * ROUTED_SCALE for w in wts] + [jnp.zeros((pad, tm), F32)], axis=0)
    slot_ref[...] = jnp.concatenate(slots + [jnp.zeros((pad, tm), I32)], axis=0)
    cnt_ref[...] = jnp.broadcast_to(base_s[...], cnt_ref.shape)


def _router(hpk, hlo, w_router, b_router, tm):
    n, d = hlo.shape
    wr = w_router.T.reshape(N_GROUPS, GROUP_SIZE, d).transpose(1, 0, 2).reshape(N_EXPERTS, d)
    br = b_router.reshape(N_GROUPS, GROUP_SIZE).T.reshape(N_EXPERTS, 1)
    wrh = wr.astype(BF16)
    wrl = (wr - wrh.astype(F32)).astype(BF16)
    full = lambda shape: pl.BlockSpec(shape, lambda i: (0, 0))
    per_token = pl.BlockSpec((8, tm), lambda i: (0, i))
    e_k, w_k, slot_k, counts = pl.pallas_call(
        _router_kernel,
        grid=(n // tm,),
        in_specs=[pl.BlockSpec((tm * _tile_rows(d), LANES), lambda i: (i, 0)),
                  pl.BlockSpec((tm, d), lambda i: (i, 0)),
                  full((N_EXPERTS, d)), full((N_EXPERTS, d)), full((N_EXPERTS, 1))],
        out_specs=[per_token, per_token, per_token, full((N_EXPERTS, 128))],
        out_shape=[jax.ShapeDtypeStruct((8, n), I32), jax.ShapeDtypeStruct((8, n), F32),
                   jax.ShapeDtypeStruct((8, n), I32), jax.ShapeDtypeStruct((N_EXPERTS, 128), F32)],
        scratch_shapes=[pltpu.VMEM((N_EXPERTS, 1), F32)],
        compiler_params=_cparams(("arbitrary",)),
        name="router",
    )(hpk, hlo, wrh, wrl, br)
    counts = counts[:, 0].reshape(GROUP_SIZE, N_GROUPS).T.reshape(N_EXPERTS).astype(I32)
    return e_k, w_k, slot_k, counts


def _tile_window(token, nsub):
    start = token * nsub
    if nsub % SUBLANES == 0:
        start = pl.multiple_of(start, SUBLANES)
    return pl.ds(start, nsub)


def _token_copy(src, src_token, dst, dst_token, nsub, sem):
    return pltpu.make_async_copy(src.at[_tile_window(src_token, nsub)], dst.at[_tile_window(dst_token, nsub)], sem)


def _for_token_groups(n_tokens, per_token):
    def group(g, carry):
        t0 = pl.multiple_of(g * SUBLANES, SUBLANES)
        for u in range(SUBLANES):
            per_token(t0, u)
        return carry

    lax.fori_loop(0, n_tokens // SUBLANES, group, 0)


def _dispatch_kernel(pad_end_ref, dest_ref, h_ref, xs_ref, zero_s, sem, zero_sem, *, nsub):
    tb = dest_ref.shape[0] // DEST_STRIDE
    te = zero_s.shape[0] // nsub

    def tail_fill(e):
        start = pl.multiple_of((pad_end_ref[e] - te) * nsub, te * nsub)
        return pltpu.make_async_copy(zero_s, xs_ref.at[pl.ds(start, te * nsub)], zero_sem)

    def has_rows(e):
        return pad_end_ref[e] > (pad_end_ref[e - 1] if e else 0)

    @pl.when(pl.program_id(0) == 0)
    def _():
        zero_s[...] = jnp.zeros_like(zero_s)
        for e in range(N_EXPERTS):
            pl.when(has_rows(e))(lambda e=e: tail_fill(e).start())
        for e in range(N_EXPERTS):
            pl.when(has_rows(e))(lambda e=e: tail_fill(e).wait())

    def start(t0, u):
        for k in range(TOP_K):
            _token_copy(h_ref, t0 + u, xs_ref, dest_ref[t0 * DEST_STRIDE + (u * DEST_STRIDE + k)], nsub,
                        sem).start(priority=k % 2)

    def wait(t0, u):
        for k in range(TOP_K):
            _token_copy(h_ref, 0, xs_ref, 0, nsub, sem).wait()

    _for_token_groups(tb, start)
    _for_token_groups(tb, wait)


def _dispatch(pad_end, dest, hpk, n_rows, tb, te, nsub):
    n = hpk.shape[0] // nsub
    grid_spec = pltpu.PrefetchScalarGridSpec(
        num_scalar_prefetch=1,
        grid=(n // tb,),
        in_specs=[pl.BlockSpec((tb * DEST_STRIDE,), lambda i, pe: (i,), memory_space=pltpu.SMEM),
                  pl.BlockSpec((tb * nsub, LANES), lambda i, pe: (i, 0))],
        out_specs=pl.BlockSpec(memory_space=pl.ANY),
        scratch_shapes=[pltpu.VMEM((te * nsub, LANES), U32), pltpu.SemaphoreType.DMA(()),
                        pltpu.SemaphoreType.DMA(())],
    )
    return pl.pallas_call(
        functools.partial(_dispatch_kernel, nsub=nsub),
        grid_spec=grid_spec,
        out_shape=jax.ShapeDtypeStruct((n_rows * nsub, LANES), U32),
        compiler_params=_cparams(("arbitrary",)),
        name="dispatch",
    )(pad_end, dest, hpk)


def _experts_kernel(be_ref, nu_ref, x_ref, wg_ref, wu_ref, wd_ref, o_ref, wg_s, wu_s, wd_s):
    i = pl.program_id(0)
    prev = be_ref[jnp.maximum(i - 1, 0)]

    @pl.when((i == 0) | (be_ref[i] != prev))
    def _():
        wg_s[...] = wg_ref[0].astype(BF16)
        wu_s[...] = wu_ref[0].astype(BF16)
        wd_s[...] = wd_ref[0].astype(BF16)

    @pl.when(i < nu_ref[0])
    def _():
        x = _unpack_pairs(_load_token_tiles(x_ref, _tile_rows(wg_s.shape[0])), BF16)
        g = _dot(x, wg_s[...])
        hidden = (g * jax.nn.sigmoid(g)) * _dot(x, wu_s[...])
        _store_token_tiles(o_ref, _pack_pairs(_dot(hidden.astype(BF16), wd_s[...])))

    @pl.when(i >= nu_ref[0])
    def _():
        o_ref[...] = jnp.zeros_like(o_ref)


def _experts(block_expert, n_used, xs, w_gate, w_up, w_down, tm):
    d, ff = w_gate.shape[1:]
    nsub = _tile_rows(d)
    p = xs.shape[0] // nsub
    rows = pl.BlockSpec((tm * nsub, LANES), lambda i, be, nu: (i, 0))
    grid_spec = pltpu.PrefetchScalarGridSpec(
        num_scalar_prefetch=2,
        grid=(p // tm,),
        in_specs=[rows,
                  pl.BlockSpec((1, d, ff), lambda i, be, nu: (be[i], 0, 0)),
                  pl.BlockSpec((1, d, ff), lambda i, be, nu: (be[i], 0, 0)),
                  pl.BlockSpec((1, ff, d), lambda i, be, nu: (be[i], 0, 0))],
        out_specs=rows,
        scratch_shapes=[pltpu.VMEM((d, ff), BF16), pltpu.VMEM((d, ff), BF16), pltpu.VMEM((ff, d), BF16)],
    )
    return pl.pallas_call(
        _experts_kernel,
        grid_spec=grid_spec,
        out_shape=jax.ShapeDtypeStruct(xs.shape, U32),
        compiler_params=_cparams(("arbitrary",)),
        name="experts",
    )(block_expert, n_used, xs, w_gate, w_up, w_down)


def _combine_kernel(dest_ref, wk_ref, ys_ref, x1_ref, hpk_ref, gt_ref, wsg_ref, wsu_ref, wsd_ref, gf_ref,
                    yp_ref, ysm_ref, buf, sem, *, n_prompt_blocks):
    tb, d = x1_ref.shape
    nsub = _tile_rows(d)
    seg = gt_ref.shape[0]
    i = pl.program_id(0)

    def start(t0, u):
        for k in range(TOP_K):
            _token_copy(ys_ref, dest_ref[t0 * DEST_STRIDE + (u * DEST_STRIDE + k)], buf.at[k], t0 + u, nsub,
                        sem).start(priority=k % 2)

    def wait(t0, u):
        for k in range(TOP_K):
            _token_copy(ys_ref, 0, buf.at[k], 0, nsub, sem).wait()

    _for_token_groups(tb, start)
    x = _unpack_pairs(_load_token_tiles(hpk_ref, nsub), BF16)
    g = _dot(x, wsg_ref[...])
    ffn = _dot(((g * jax.nn.sigmoid(g)) * _dot(x, wsu_ref[...])).astype(BF16), wsd_ref[...])
    _for_token_groups(tb, wait)
    wk = wk_ref[...]
    for k in range(TOP_K):
        ffn = ffn + wk[:, k:k + 1] * _unpack_pairs(_load_token_tiles(buf.at[k], nsub), F32)
    x2 = x1_ref[...].reshape(seg, tb // seg, d) + gt_ref[...] * ffn.reshape(seg, tb // seg, d)
    y = (_rms(x2) * gf_ref[...].reshape(1, 1, d)).reshape(tb, d)

    @pl.when(i < n_prompt_blocks)
    def _():
        yp_ref[...] = y

    @pl.when(i >= n_prompt_blocks)
    def _():
        ysm_ref[...] = y


def _combine(dest, wk_t, ys, x1, hpk, gt2_seg, ws_gate, ws_up, ws_down, g_final, n_prompt, tb):
    n, d = x1.shape
    nsub = _tile_rows(d)
    ff = ws_gate.shape[1]
    npb = n_prompt // tb
    nsb = (n - n_prompt) // tb
    seg = gt2_seg.shape[0] * tb // n
    row = lambda width: pl.BlockSpec((tb, width), lambda i: (i, 0))
    full = lambda shape: pl.BlockSpec(shape, lambda i: (0,) * len(shape))
    return pl.pallas_call(
        functools.partial(_combine_kernel, n_prompt_blocks=npb),
        grid=(n // tb,),
        in_specs=[pl.BlockSpec((tb * DEST_STRIDE,), lambda i: (i,), memory_space=pltpu.SMEM),
                  row(8), pl.BlockSpec(memory_space=pl.ANY), row(d),
                  pl.BlockSpec((tb * nsub, LANES), lambda i: (i, 0)),
                  pl.BlockSpec((seg, 1, d), lambda i: (i, 0, 0)),
                  full((d, ff)), full((d, ff)), full((ff, d)), full((1, d))],
        out_specs=[pl.BlockSpec((tb, d), lambda i: (jnp.minimum(i, npb - 1), 0)),
                   pl.BlockSpec((tb, d), lambda i: (jnp.clip(i - npb, 0, nsb - 1), 0))],
        out_shape=[jax.ShapeDtypeStruct((n_prompt, d), F32), jax.ShapeDtypeStruct((n - n_prompt, d), F32)],
        scratch_shapes=[pltpu.VMEM((TOP_K, tb * nsub, LANES), U32), pltpu.SemaphoreType.DMA(())],
        compiler_params=_cparams(("arbitrary",)),
        name="combine",
    )(dest, wk_t, ys, x1, hpk, gt2_seg, ws_gate, ws_up, ws_down, g_final.reshape(1, d))


def _largest_tile(n, target):
    t = min(n, target)
    while n % t:
        t -= 1
    return t


def _mixer(x, mod, past_k, past_v, h0, conv0, p, first_pos_zero, n_total, row_offset, earlier):
    b, t, d = x.shape
    m = b * t
    a = d
    hn_bm, hn_tm = _norm_mod(x, mod, p["g_mix"])
    hn_bm = hn_bm.reshape(m, d)
    tm = _largest_tile(m, 512)
    tmh = _largest_tile(m, 1024)
    w_in = p["w_in"]
    u = _proj(hn_tm, w_in, 0, d, "plain", BF16, tmh).reshape(t, b, d)
    gl = _proj(hn_tm, w_in, 1, d, "gelu", BF16, tmh).reshape(t, b, d)
    sl = _proj(hn_tm, w_in, 5, d, "sigmoid", BF16, tmh).reshape(t, b, d)
    q = _proj(hn_bm, w_in, 2, a, "qscale", BF16, tmh).reshape(b, t, a)
    k = _proj(hn_bm, w_in, 3, a, "plain", F32, tm).reshape(b, t, a)
    v = _proj(hn_bm, w_in, 4, a, "plain", F32, tm).reshape(b, t, a)
    sz = _proj(hn_bm, w_in, 6, a, "sigmoid", BF16, tmh).reshape(b, t, a)

    if h0 is None:
        h0 = jnp.zeros((b, d), F32)
        prev = jnp.zeros((CONV_WIDTH - 1, b, d), F32)
    else:
        prev = conv0.transpose(1, 0, 2)
    tc = _largest_tile(t, 128)
    yl, h_last, conv_tail = _lru(u, gl, sl, p["conv_w"], p["conv_b"], p["w_rg_a"], p["b_rg_a"], p["w_rg_x"],
                                 p["b_rg_x"], p["lru_lambda"], h0, prev, tc, first_pos_zero)
    if past_k is None:
        mixed = _attn_prompt(q, k, v, sz, yl, _largest_tile(t, 256))
    else:
        mixed = _attn_sample(q, k, v, past_k, past_v, sz, yl)
    tmo = _largest_tile(m, 256)
    shared = _outproj(mixed.reshape(m, a), p["w_out"], x.reshape(m, d), mod, p["g_ffn"], tmo, t, n_total, row_offset,
                      earlier)
    nh = a // HEAD_DIM
    return (shared, k.reshape(b, t, nh, HEAD_DIM), v.reshape(b, t, nh, HEAD_DIM), h_last,
            conv_tail.transpose(1, 0, 2))


def _routing_tables(e_k, slot_k, counts, tm):
    n_tokens = e_k.shape[1]
    padded = (counts + tm - 1) // tm * tm
    pad_end = jnp.cumsum(padded).astype(I32)
    pad_start = pad_end - padded
    is_expert = e_k[:, :, None] == jnp.arange(N_EXPERTS, dtype=I32)
    dest = slot_k + jnp.sum(jnp.where(is_expert, pad_start, 0), axis=-1).astype(I32)
    n_blocks = -(-(TOP_K * n_tokens + N_EXPERTS * (tm - 1)) // tm)
    block_start = jnp.arange(n_blocks, dtype=I32) * tm
    block_expert = jnp.minimum(jnp.sum(pad_end[None, :] <= block_start[:, None], axis=1), N_EXPERTS - 1).astype(I32)
    n_used = (pad_end[-1:] // tm).astype(I32)
    return dest, pad_end, block_expert, n_used, n_blocks * tm


def kernel(x_prompt, x_sample, cache_k, cache_v, state_lru, state_conv, c_prompt, c_sample, g_mix, w_ada, b_ada, w_in, conv_w, conv_b, w_rg_a, b_rg_a, w_rg_x, b_rg_x, lru_lambda, w_out, g_ffn, w_router, b_router, w_gate, w_up, w_down, ws_gate, ws_up, ws_down, g_final):
    assert g_mix.shape[0] == 1, "single-layer trunk only"
    b, t, d = x_prompt.shape
    bs, ts, _ = x_sample.shape
    assert ts >= CONV_WIDTH - 1 and ts == TIME_GROUP and t % TIME_GROUP == 0
    p = dict(g_mix=g_mix[0], w_in=w_in[0], conv_w=conv_w[0], conv_b=conv_b[0],
             w_rg_a=w_rg_a[0].astype(BF16), b_rg_a=b_rg_a[0], w_rg_x=w_rg_x[0].astype(BF16), b_rg_x=b_rg_x[0],
             lru_lambda=lru_lambda[0], w_out=w_out[0], g_ffn=g_ffn[0])
    mod = _ada(jnp.concatenate([c_prompt, c_sample], axis=0), w_ada[0], b_ada[0]).reshape(b + bs, N_MOD, d)
    mod_p, mod_s = mod[:b], mod[b:]

    n_prompt = b * t
    n = n_prompt + bs * ts
    tb = _largest_tile(bs * ts, 256)
    assert n_prompt % tb == 0 and tb % ts == 0

    shared, kp, vp, hp, cp = _mixer(x_prompt, mod_p, None, None, None, None, p, True, n, 0, None)
    (x1, hpk, hlo), ks, vs, hs, cs = _mixer(x_sample, mod_s, cache_k[0], cache_v[0], state_lru[0], state_conv[0],
                                            p, False, n, n_prompt, shared)

    e_k, w_k, slot_k, counts = _router(hpk, hlo, w_router[0], b_router[0], _largest_tile(n, ROUTER_TOKENS))
    tme = EXPERT_ROW_BLOCK
    dest, pad_end, block_expert, n_used, n_rows = _routing_tables(e_k, slot_k, counts, tme)
    dest = dest.T.reshape(-1)
    xs = _dispatch(pad_end, dest, hpk, n_rows, tb, tme, _tile_rows(d))
    ys = _experts(block_expert, n_used, xs, w_gate[0], w_up[0], w_down[0], tme)
    gt2 = jnp.concatenate([jnp.broadcast_to(mod_p[:, None, 5:6, :], (b, t // ts, 1, d)).reshape(-1, 1, d),
                           mod_s[:, 5:6, :]], axis=0)
    y_p, y_s = _combine(dest, w_k.T, ys, x1, hpk, gt2, ws_gate[0].astype(BF16), ws_up[0].astype(BF16),
                        ws_down[0].astype(BF16), g_final, n_prompt, tb)
    return (y_p.reshape(b, t, d), y_s.reshape(bs, ts, d), kp[None], vp[None], hp[None], cp[None],
            ks[None], vs[None], hs[None], cs[None])
```

```python
import functools

import jax
import jax.numpy as jnp
from jax import lax
from jax.experimental import pallas as pl
from jax.experimental.pallas import tpu as pltpu

F32 = jnp.float32
BF16 = jnp.bfloat16
I32 = jnp.int32
U32 = jnp.uint32

LANES = 128
SUBLANES = 8
HEAD_DIM = 128
LRU_BLOCK_DIM = 256
LRU_C = 8.0
CONV_WIDTH = 4
N_EXPERTS = 64
TOP_K = 6
N_GROUPS = 8
GROUP_SIZE = N_EXPERTS // N_GROUPS
TOPK_GROUPS = 4
ROUTED_SCALE = 2.5
NORM_EPS = 1e-6
N_MOD = 6
TIME_GROUP = 16
LOG2_E = 1.4426950408889634
ATTN_HEADS_PER_STEP = 8
F32_ZERO_WEIGHT_LOG2 = -150.0
ROUTER_TOKENS = 768
DEST_ROWS = 8
EXPERT_ROW_BLOCK = 512
V7X_VMEM_LIMIT_BYTES = 56 * 1024 * 1024
NT_DIMS = (((1,), (1,)), ((), ()))
NEG_INF = float("-inf")


def _cparams(semantics):
    return pltpu.CompilerParams(dimension_semantics=semantics, vmem_limit_bytes=V7X_VMEM_LIMIT_BYTES)


def _dot(a, b):
    return jnp.dot(a, b, preferred_element_type=F32)


def _split_bf16(x):
    hi = x.astype(BF16)
    lo = (x - hi.astype(F32)).astype(BF16)
    return hi, lo


def _dot3(a, b):
    ah, al = _split_bf16(a)
    bh, bl = _split_bf16(b)
    return _dot(ah, bh) + (_dot(al, bh) + _dot(ah, bl))


def _pack_pairs(h):
    n = h.shape[1] // 2
    bits = lax.bitcast_convert_type(h.astype(BF16).astype(F32), U32)
    return bits[:, :n] | (bits[:, n:] >> 16)


def _store_token_tiles(ref, packed):
    m, n = packed.shape
    nsub = n // LANES
    for j in range(nsub):
        ref[pl.ds(j, m, stride=nsub), :] = packed[:, j * LANES:(j + 1) * LANES]


def _tile_rows(d):
    return d // 2 // LANES


def _load_token_tiles(ref, nsub):
    m = ref.shape[0] // nsub
    return jnp.concatenate([ref[pl.ds(j, m, stride=nsub), :] for j in range(nsub)], axis=1)


def _unpack_pairs(p, dtype):
    a = lax.bitcast_convert_type(p & jnp.uint32(0xFFFF0000), F32)
    b = lax.bitcast_convert_type(p << 16, F32)
    return jnp.concatenate([a.astype(dtype), b.astype(dtype)], axis=1)


def _rms(x):
    return x * lax.rsqrt(jnp.mean(x * x, axis=-1, keepdims=True) + NORM_EPS)


def _ada_kernel(c_ref, w_ref, b_ref, o_ref):
    c = c_ref[...]
    o_ref[...] = _dot3(c * jax.nn.sigmoid(c), w_ref[...]) + b_ref[...]


def _ada(c, w, b):
    m, d = c.shape
    n = w.shape[1]
    tn = min(n, 1024)
    return pl.pallas_call(
        _ada_kernel,
        grid=(n // tn,),
        in_specs=[pl.BlockSpec((m, d), lambda j: (0, 0)),
                  pl.BlockSpec((d, tn), lambda j: (0, j)),
                  pl.BlockSpec((1, tn), lambda j: (0, j))],
        out_specs=pl.BlockSpec((m, tn), lambda j: (0, j)),
        out_shape=jax.ShapeDtypeStruct((m, n), F32),
        compiler_params=_cparams(("arbitrary",)),
        name="ada",
    )(c, w, b.reshape(1, n))


def _row_permutation(n_outer, n_inner):
    n = n_outer * n_inner
    out_row = lax.broadcasted_iota(I32, (n, n), 0)
    in_row = lax.broadcasted_iota(I32, (n, n), 1)
    same = (out_row // n_outer == in_row % n_inner) & (out_row % n_outer == in_row // n_inner)
    return same.astype(BF16)


def _norm_mod_kernel(x_ref, mod_ref, g_ref, obm_ref, otm_ref):
    b, tt, d = x_ref.shape
    m = mod_ref[...]
    h = (_rms(x_ref[...]) * g_ref[...].reshape(1, 1, d) * (1.0 + m[:, 1:2, :]) + m[:, 0:1, :]).astype(BF16)
    obm_ref[...] = h
    otm_ref[...] = _dot(_row_permutation(b, tt), h.reshape(b * tt, d)).astype(BF16)


def _norm_mod(x, mod, g):
    b, t, d = x.shape
    tt = TIME_GROUP
    return pl.pallas_call(
        _norm_mod_kernel,
        grid=(t // tt,),
        in_specs=[pl.BlockSpec((b, tt, d), lambda ti: (0, ti, 0)),
                  pl.BlockSpec((b, N_MOD, d), lambda ti: (0, 0, 0)),
                  pl.BlockSpec((1, d), lambda ti: (0, 0))],
        out_specs=[pl.BlockSpec((b, tt, d), lambda ti: (0, ti, 0)),
                   pl.BlockSpec((tt * b, d), lambda ti: (ti, 0))],
        out_shape=[jax.ShapeDtypeStruct((b, t, d), BF16), jax.ShapeDtypeStruct((t * b, d), BF16)],
        compiler_params=_cparams(("arbitrary",)),
        name="norm_mod",
    )(x, mod, g.reshape(1, d))


def _proj_kernel(a_ref, w_ref, o_ref, w_s, *, epilogue):
    @pl.when(pl.program_id(0) == 0)
    def _():
        w_s[...] = w_ref[...].astype(BF16)

    acc = _dot(a_ref[...], w_s[...])
    if epilogue == "gelu":
        acc = jax.nn.gelu(acc)
    elif epilogue == "sigmoid":
        acc = jax.nn.sigmoid(acc)
    elif epilogue == "qscale":
        acc = acc * (HEAD_DIM ** -0.5 * LOG2_E)
    o_ref[...] = acc.astype(o_ref.dtype)


def _resident(shape, index_map):
    return pl.BlockSpec(shape, index_map, pipeline_mode=pl.Buffered(1))


def _proj(a, w, col, width, epilogue, out_dtype, tm):
    m, k = a.shape
    return pl.pallas_call(
        functools.partial(_proj_kernel, epilogue=epilogue),
        grid=(m // tm,),
        in_specs=[pl.BlockSpec((tm, k), lambda i: (i, 0)),
                  _resident((k, width), lambda i: (0, col))],
        out_specs=pl.BlockSpec((tm, width), lambda i: (i, 0)),
        out_shape=jax.ShapeDtypeStruct((m, width), out_dtype),
        scratch_shapes=[pltpu.VMEM((k, width), BF16)],
        compiler_params=_cparams(("arbitrary",)),
        name="proj_" + epilogue,
    )(a, w)


def _lru_kernel(u_ref, gl_ref, sl_ref, cw_ref, cb_ref, wa_ref, ba_ref, wx_ref, bx_ref, lam_ref, h0_ref, prev_ref,
                y_ref, hl_ref, cn_ref, tail_s, h_s, a_s, b_s, y_s, *, first_pos_zero):
    tc, nb, wb = u_ref.shape
    ti = pl.program_id(1)

    @pl.when(ti == 0)
    def _():
        tail_s[...] = prev_ref[...]
        h_s[...] = h0_ref[...]

    u = u_ref[...].astype(F32)
    xp = jnp.concatenate([tail_s[...], u], axis=0)
    cw = cw_ref[...]
    xc = cb_ref[...].reshape(1, 1, wb)
    for j in range(CONV_WIDTH):
        xc = xc + xp[j:j + tc] * cw[j:j + 1].reshape(1, 1, wb)
    last_rows = u[tc - (CONV_WIDTH - 1):]
    tail_s[...] = last_rows
    cn_ref[...] = last_rows

    x2 = xc.reshape(tc * nb, wb)
    xb = x2.astype(BF16)
    r = jax.nn.sigmoid(_dot(xb, wa_ref[0]) + ba_ref[0])
    gi = jax.nn.sigmoid(_dot(xb, wx_ref[0]) + bx_ref[0])
    log_a = r * ((-LRU_C) * jax.nn.softplus(-lam_ref[...]))
    a = jnp.exp(log_a)
    m2 = -jnp.tanh(log_a) * (a * a + 1.0)
    mult = jnp.where(m2 > 0.0, m2 * lax.rsqrt(m2), 0.0)
    if first_pos_zero:
        row = lax.broadcasted_iota(I32, (tc * nb, wb), 0)
        mult = jnp.where((row < nb) & (ti == 0), 1.0, mult)
    a_s[...] = a.reshape(tc, nb, wb)
    b_s[...] = (mult * (gi * x2)).reshape(tc, nb, wb)

    def step(t, h):
        h = a_s[t] * h + b_s[t]
        gate = gl_ref[t].astype(F32) * sl_ref[t].astype(F32)
        y_s[t] = h * gate
        return h

    h = lax.fori_loop(0, tc, step, h_s[...], unroll=8)
    h_s[...] = h
    hl_ref[...] = h

    perm = _row_permutation(TIME_GROUP, nb)
    for g in range(tc // TIME_GROUP):
        rows = slice(g * TIME_GROUP, (g + 1) * TIME_GROUP)
        yg = y_s[rows].reshape(TIME_GROUP * nb, wb).astype(BF16)
        y_ref[:, rows, :] = _dot(perm, yg).astype(BF16).reshape(nb, TIME_GROUP, wb)


def _lru(u, gl, sl, conv_w, conv_b, wa, ba, wx, bx, lam, h0, prev, tc, first_pos_zero):
    t, b, w = u.shape
    wb = LRU_BLOCK_DIM
    nw = w // wb
    seq = pl.BlockSpec((tc, b, wb), lambda wi, ti: (ti, 0, wi))
    seq_out = pl.BlockSpec((b, tc, wb), lambda wi, ti: (0, ti, wi))
    chan = lambda rows: pl.BlockSpec((rows, wb), lambda wi, ti: (0, wi))
    gate_w = pl.BlockSpec((1, wb, wb), lambda wi, ti: (wi, 0, 0))
    gate_b = pl.BlockSpec((1, 1, wb), lambda wi, ti: (wi, 0, 0))
    tail = pl.BlockSpec((CONV_WIDTH - 1, b, wb), lambda wi, ti: (0, 0, wi))
    return pl.pallas_call(
        functools.partial(_lru_kernel, first_pos_zero=first_pos_zero),
        grid=(nw, t // tc),
        in_specs=[seq, seq, seq, chan(CONV_WIDTH), chan(1), gate_w, gate_b, gate_w, gate_b, chan(1),
                  chan(b), tail],
        out_specs=[seq_out, chan(b), tail],
        out_shape=[jax.ShapeDtypeStruct((b, t, w), BF16), jax.ShapeDtypeStruct((b, w), F32),
                   jax.ShapeDtypeStruct((CONV_WIDTH - 1, b, w), F32)],
        scratch_shapes=[pltpu.VMEM((CONV_WIDTH - 1, b, wb), F32), pltpu.VMEM((b, wb), F32),
                        pltpu.VMEM((tc, b, wb), F32), pltpu.VMEM((tc, b, wb), F32),
                        pltpu.VMEM((tc, b, wb), F32)],
        compiler_params=_cparams(("arbitrary", "arbitrary")),
        name="lru",
    )(u, gl, sl, conv_w, conv_b.reshape(1, w), wa, ba.reshape(nw, 1, wb), wx, bx.reshape(nw, 1, wb),
      lam.reshape(1, w), h0, prev)


def _log_keep_and_logsig(z):
    nz = -z
    lk = jnp.minimum(nz, 0.0) - jnp.log2(1.0 + jnp.exp2(jnp.minimum(z, nz)))
    return lk, z + lk


def _later_key_sums(lk, tri):
    hi, lo = _split_bf16(lk)
    return _dot(hi, tri) + _dot(lo, tri)


def _strict_lower(n, dtype):
    row = lax.broadcasted_iota(I32, (n, n), 0)
    col = lax.broadcasted_iota(I32, (n, n), 1)
    return (row > col).astype(dtype), col < row


def _attn_kernel(q_ref, k_ref, v_ref, sz_ref, yl_ref, o_ref):
    tq = q_ref.shape[1]
    nh = q_ref.shape[2] // HEAD_DIM
    qi = pl.program_id(2)
    tri, causal = _strict_lower(tq, BF16)

    def tile(j, carry, diag):
        start = pl.multiple_of(j * tq, tq)
        heads = [slice(h * HEAD_DIM, (h + 1) * HEAD_DIM) for h in range(nh)]
        zs = [lax.dot_general(q_ref[0, :, cols], k_ref[0, pl.ds(start, tq), cols].astype(BF16), NT_DIMS,
                              preferred_element_type=F32) for cols in heads]
        lks, lss = [], []
        for z in zs:
            lk, ls = _log_keep_and_logsig(z)
            lks.append(jnp.where(causal, lk, 0.0) if diag else lk)
            lss.append(ls)
        rcs = [_later_key_sums(lk, tri) for lk in lks]
        ws = []
        for h in range(nh):
            w = jnp.exp2(lss[h] + rcs[h] + carry[h][1])
            ws.append((jnp.where(causal, w, 0.0) if diag else w).astype(BF16))
        return tuple((carry[h][0] + _dot(ws[h], v_ref[0, pl.ds(start, tq), heads[h]].astype(BF16)),
                      carry[h][1] + (rcs[h][:, 0:1] + lks[h][:, 0:1])) for h in range(nh))

    def any_weight_left(carry):
        c_max = carry[0][1]
        for h in range(1, nh):
            c_max = jnp.maximum(c_max, carry[h][1])
        return jnp.max(c_max) > F32_ZERO_WEIGHT_LOG2

    init = tuple((jnp.zeros((tq, HEAD_DIM), F32), jnp.zeros((tq, 1), F32)) for _ in range(nh))
    carry = tile(qi, init, True)

    def more(state):
        return (state[0] < qi) & state[1]

    def farther_tile(state):
        carry = tile(qi - 1 - state[0], state[2], False)
        return state[0] + 1, any_weight_left(carry), carry

    carry = lax.while_loop(more, farther_tile, (jnp.int32(0), any_weight_left(carry), carry))[2]
    acc = jnp.concatenate([carry[h][0] for h in range(nh)], axis=1)
    o_ref[0] = (yl_ref[0].astype(F32) + sz_ref[0].astype(F32) * acc).astype(BF16)


def _attn_prompt(q, k, v, sz, yl, tq):
    b, t, a = q.shape
    wblk = min(ATTN_HEADS_PER_STEP * HEAD_DIM, a)
    qspec = pl.BlockSpec((1, tq, wblk), lambda bi, hi, qi: (bi, qi, hi))
    kvspec = pl.BlockSpec((1, t, wblk), lambda bi, hi, qi: (bi, 0, hi))
    return pl.pallas_call(
        _attn_kernel,
        grid=(b, a // wblk, t // tq),
        in_specs=[qspec, kvspec, kvspec, qspec, qspec],
        out_specs=qspec,
        out_shape=jax.ShapeDtypeStruct((b, t, a), BF16),
        compiler_params=_cparams(("arbitrary", "arbitrary", "arbitrary")),
        name="attn_prompt",
    )(q, k, v, sz, yl)


def _attn_sample_kernel(q_ref, kn_ref, vn_ref, kc_ref, vc_ref, sz_ref, yl_ref, o_ref, lk_s, ls_s):
    ts = q_ref.shape[1]
    nh = q_ref.shape[2] // HEAD_DIM
    past = kc_ref.shape[1] // nh
    tri_n, causal = _strict_lower(ts, BF16)
    tri_p, _ = _strict_lower(past, BF16)
    accs, carries = [], []
    for h in range(nh):
        cols = slice(h * HEAD_DIM, (h + 1) * HEAD_DIM)
        q = q_ref[0, :, cols]
        zn = lax.dot_general(q, kn_ref[0, :, cols].astype(BF16), NT_DIMS, preferred_element_type=F32)
        lk, ls = _log_keep_and_logsig(zn)
        lk = jnp.where(causal, lk, 0.0)
        rc = _later_key_sums(lk, tri_n)
        w = jnp.where(causal, jnp.exp2(ls + rc), 0.0)
        accs.append(_dot(w.astype(BF16), vn_ref[0, :, cols].astype(BF16)))
        carries.append(rc[:, 0:1] + lk[:, 0:1])
        head_rows = pl.ds(h, past, stride=nh)
        zp = lax.dot_general(q, kc_ref[0, head_rows, :].astype(BF16), NT_DIMS, preferred_element_type=F32)
        lkp, lsp = _log_keep_and_logsig(zp)
        lk_s[h * ts:(h + 1) * ts, :] = lkp
        ls_s[h * ts:(h + 1) * ts, :] = lsp
    rcp = _later_key_sums(lk_s[...], tri_p)
    for h in range(nh):
        cols = slice(h * HEAD_DIM, (h + 1) * HEAD_DIM)
        rows = slice(h * ts, (h + 1) * ts)
        w = jnp.exp2(ls_s[rows, :] + rcp[rows, :] + carries[h])
        acc = accs[h] + _dot(w.astype(BF16), vc_ref[0, pl.ds(h, past, stride=nh), :].astype(BF16))
        o_ref[0, :, cols] = (yl_ref[0, :, cols].astype(F32) + sz_ref[0, :, cols].astype(F32) * acc).astype(BF16)


def _attn_sample(q, kn, vn, kc, vc, sz, yl):
    b, ts, a = q.shape
    past = kc.shape[1]
    nh = a // HEAD_DIM
    kc = kc.reshape(b, past * nh, HEAD_DIM)
    vc = vc.reshape(b, past * nh, HEAD_DIM)
    cur = pl.BlockSpec((1, ts, a), lambda bi: (bi, 0, 0))
    cache = pl.BlockSpec((1, past * nh, HEAD_DIM), lambda bi: (bi, 0, 0))
    return pl.pallas_call(
        _attn_sample_kernel,
        grid=(b,),
        in_specs=[cur, cur, cur, cache, cache, cur, cur],
        out_specs=cur,
        out_shape=jax.ShapeDtypeStruct((b, ts, a), BF16),
        scratch_shapes=[pltpu.VMEM((nh * ts, past), F32), pltpu.VMEM((nh * ts, past), F32)],
        compiler_params=_cparams(("arbitrary",)),
        name="attn_sample",
    )(q, kn, vn, kc, vc, sz, yl)


def _outproj_kernel(a_ref, w_ref, x_ref, mod_ref, g_ref, *rest):
    x1_ref, hpk_ref, hlo_ref, w_s = rest[-4:]
    nb = mod_ref.shape[0]
    tm, d = x_ref.shape

    @pl.when(pl.program_id(0) == 0)
    def _():
        w_s[...] = w_ref[...].astype(BF16)

    m = mod_ref[...]
    o = _dot(a_ref[...], w_s[...]).reshape(nb, tm // nb, d)
    x1 = x_ref[...].reshape(nb, tm // nb, d) + m[:, 2:3, :] * o
    h = _rms(x1) * g_ref[...].reshape(1, 1, d) * (1.0 + m[:, 4:5, :]) + m[:, 3:4, :]
    x1_ref[...] = x1.reshape(tm, d)
    h = h.reshape(tm, d)
    hi = h.astype(BF16).astype(F32)
    _store_token_tiles(hpk_ref, _pack_pairs(hi))
    hlo_ref[...] = (h - hi).astype(BF16)


def _outproj(a, w, x, mod, g, tm, rows_per_batch, n_total, row_offset, earlier=None):
    m, d = x.shape
    nb = max(tm // rows_per_batch, 1)
    per = max(rows_per_batch // tm, 1)
    off = row_offset // tm
    nsub = _tile_rows(d)
    row = lambda width: pl.BlockSpec((tm, width), lambda i: (i, 0))
    out_row = lambda width: pl.BlockSpec((tm, width), lambda i: (i + off, 0))
    in_specs = [row(d), _resident((d, d), lambda i: (0, 0)), row(d),
                pl.BlockSpec((nb, N_MOD, d), lambda i: (i // per, 0, 0)),
                pl.BlockSpec((1, d), lambda i: (0, 0))]
    args = [a, w, x, mod, g.reshape(1, d)]
    aliases = {}
    if earlier is not None:
        aliases = {len(args) + j: j for j in range(len(earlier))}
        in_specs += [pl.BlockSpec(memory_space=pl.ANY)] * len(earlier)
        args += list(earlier)
    return pl.pallas_call(
        _outproj_kernel,
        grid=(m // tm,),
        in_specs=in_specs,
        out_specs=[out_row(d), pl.BlockSpec((tm * nsub, LANES), lambda i: (i + off, 0)), out_row(d)],
        out_shape=[jax.ShapeDtypeStruct((n_total, d), F32), jax.ShapeDtypeStruct((n_total * nsub, LANES), U32),
                   jax.ShapeDtypeStruct((n_total, d), BF16)],
        scratch_shapes=[pltpu.VMEM((d, d), BF16)],
        input_output_aliases=aliases,
        compiler_params=_cparams(("arbitrary",)),
        name="outproj",
    )(*args)


def _sub_max(x):
    return jnp.max(x, axis=0, keepdims=True)


def _sub_min(x):
    return jnp.min(x, axis=0, keepdims=True)


def _sub_sum(x):
    return jnp.sum(x, axis=0, keepdims=True)


def _router_kernel(hpk_ref, hlo_ref, wrh_ref, wrl_ref, br_ref, e_ref, w_ref, slot_ref, cnt_ref, base_s):
    tm = hlo_ref.shape[0]

    @pl.when(pl.program_id(0) == 0)
    def _():
        base_s[...] = jnp.zeros_like(base_s)

    hi = _unpack_pairs(_load_token_tiles(hpk_ref, _tile_rows(hlo_ref.shape[1])), BF16)
    lo = hlo_ref[...]
    wrh = wrh_ref[...]
    logits = lax.dot_general(wrh, hi, NT_DIMS, preferred_element_type=F32)
    logits = logits + (lax.dot_general(wrh, lo, NT_DIMS, preferred_element_type=F32)
                       + lax.dot_general(wrl_ref[...], hi, NT_DIMS, preferred_element_type=F32))
    score = jax.nn.sigmoid(logits)
    sel = (score + br_ref[...]).reshape(GROUP_SIZE, N_GROUPS, tm)
    score = score.reshape(GROUP_SIZE, N_GROUPS, tm)

    m1 = sel[0]
    m2 = jnp.full_like(m1, NEG_INF)
    for e in range(1, GROUP_SIZE):
        x = sel[e]
        m2 = jnp.maximum(m2, jnp.minimum(m1, x))
        m1 = jnp.maximum(m1, x)
    cur = m1 + m2
    gidx = lax.broadcasted_iota(I32, (N_GROUPS, tm), 0)
    gmask = jnp.zeros((N_GROUPS, tm), jnp.bool_)
    for _ in range(TOPK_GROUPS):
        first = _sub_min(jnp.where(cur == _sub_max(cur), gidx, N_GROUPS))
        pick = gidx == first
        gmask = gmask | pick
        cur = jnp.where(pick, NEG_INF, cur)

    eid = (lax.broadcasted_iota(I32, (GROUP_SIZE, N_GROUPS, tm), 1) * GROUP_SIZE
           + lax.broadcasted_iota(I32, (GROUP_SIZE, N_GROUPS, tm), 0))
    cur = jnp.where(gmask[None], sel, NEG_INF)
    ids, wts, picks = [], [], []
    for _ in range(TOP_K):
        mx = _sub_max(jnp.max(cur, axis=0))
        first = _sub_min(jnp.min(jnp.where(cur == mx[None], eid, N_EXPERTS), axis=0))
        pick = eid == first[None]
        ids.append(first)
        picks.append(pick)
        wts.append(_sub_sum(jnp.sum(jnp.where(pick, score, 0.0), axis=0)))
        cur = jnp.where(pick, NEG_INF, cur)
    total = wts[0]
    chosen = picks[0]
    for k in range(1, TOP_K):
        total = total + wts[k]
        chosen = chosen | picks[k]

    chosen = jnp.where(chosen, 1.0, 0.0).reshape(N_EXPERTS, tm)
    earlier_or_self = (lax.broadcasted_iota(I32, (tm, tm), 0) <= lax.broadcasted_iota(I32, (tm, tm), 1)).astype(BF16)
    incl = _dot(chosen.astype(BF16), earlier_or_self)
    rank = (base_s[...] + (incl - chosen)).reshape(GROUP_SIZE, N_GROUPS, tm)
    base_s[...] = base_s[...] + incl[:, tm - 1:tm]
    slots = [_sub_sum(jnp.sum(jnp.where(pick, rank, 0.0), axis=0)).astype(I32) for pick in picks]

    pad = 8 - TOP_K
    e_ref[...] = jnp.concatenate(ids + [jnp.zeros((pad, tm), I32)], axis=0)
    w_ref[...] = jnp.concatenate([w / total * ROUTED_SCALE for w in wts] + [jnp.zeros((pad, tm), F32)], axis=0)
    slot_ref[...] = jnp.concatenate(slots + [jnp.zeros((pad, tm), I32)], axis=0)
    cnt_ref[...] = jnp.broadcast_to(base_s[...], cnt_ref.shape)


def _router(hpk, hlo, w_router, b_router, tm):
    n, d = hlo.shape
    wr = w_router.T.reshape(N_GROUPS, GROUP_SIZE, d).transpose(1, 0, 2).reshape(N_EXPERTS, d)
    br = b_router.reshape(N_GROUPS, GROUP_SIZE).T.reshape(N_EXPERTS, 1)
    wrh = wr.astype(BF16)
    wrl = (wr - wrh.astype(F32)).astype(BF16)
    full = lambda shape: pl.BlockSpec(shape, lambda i: (0, 0))
    per_token = pl.BlockSpec((8, tm), lambda i: (0, i))
    e_k, w_k, slot_k, counts = pl.pallas_call(
        _router_kernel,
        grid=(n // tm,),
        in_specs=[pl.BlockSpec((tm * _tile_rows(d), LANES), lambda i: (i, 0)),
                  pl.BlockSpec((tm, d), lambda i: (i, 0)),
                  full((N_EXPERTS, d)), full((N_EXPERTS, d)), full((N_EXPERTS, 1))],
        out_specs=[per_token, per_token, per_token, full((N_EXPERTS, 128))],
        out_shape=[jax.ShapeDtypeStruct((8, n), I32), jax.ShapeDtypeStruct((8, n), F32),
                   jax.ShapeDtypeStruct((8, n), I32), jax.ShapeDtypeStruct((N_EXPERTS, 128), F32)],
        scratch_shapes=[pltpu.VMEM((N_EXPERTS, 1), F32)],
        compiler_params=_cparams(("arbitrary",)),
        name="router",
    )(hpk, hlo, wrh, wrl, br)
    counts = counts[:, 0].reshape(GROUP_SIZE, N_GROUPS).T.reshape(N_EXPERTS).astype(I32)
    return e_k, w_k, slot_k, counts


def _tile_window(token, nsub):
    start = token * nsub
    if nsub % SUBLANES == 0:
        start = pl.multiple_of(start, SUBLANES)
    return pl.ds(start, nsub)


def _token_copy(src, src_token, dst, dst_token, nsub, sem):
    return pltpu.make_async_copy(src.at[_tile_window(src_token, nsub)], dst.at[_tile_window(dst_token, nsub)], sem)


def _for_token_groups(n_tokens, per_token):
    def group(g, carry):
        t0 = pl.multiple_of(g * SUBLANES, SUBLANES)
        for u in range(SUBLANES):
            per_token(t0, u)
        return carry

    lax.fori_loop(0, n_tokens // SUBLANES, group, 0)


def _dispatch_kernel(pad_end_ref, dest_ref, h_ref, xs_ref, zero_s, sem, zero_sem, *, nsub):
    tb = dest_ref.shape[1]
    te = zero_s.shape[0] // nsub

    def tail_fill(e):
        start = pl.multiple_of((pad_end_ref[e] - te) * nsub, te * nsub)
        return pltpu.make_async_copy(zero_s, xs_ref.at[pl.ds(start, te * nsub)], zero_sem)

    def has_rows(e):
        return pad_end_ref[e] > (pad_end_ref[e - 1] if e else 0)

    @pl.when(pl.program_id(0) == 0)
    def _():
        zero_s[...] = jnp.zeros_like(zero_s)
        for e in range(N_EXPERTS):
            pl.when(has_rows(e))(lambda e=e: tail_fill(e).start())
        for e in range(N_EXPERTS):
            pl.when(has_rows(e))(lambda e=e: tail_fill(e).wait())

    def start(t0, u):
        for k in range(TOP_K):
            _token_copy(h_ref, t0 + u, xs_ref, dest_ref[k, t0 + u], nsub,
                        sem).start(priority=k % 2)

    def wait(t0, u):
        for k in range(TOP_K):
            _token_copy(h_ref, 0, xs_ref, 0, nsub, sem).wait()

    _for_token_groups(tb, start)
    _for_token_groups(tb, wait)


def _dispatch(pad_end, dest, hpk, n_rows, tb, te, nsub):
    n = hpk.shape[0] // nsub
    grid_spec = pltpu.PrefetchScalarGridSpec(
        num_scalar_prefetch=1,
        grid=(n // tb,),
        in_specs=[pl.BlockSpec((DEST_ROWS, tb), lambda i, pe: (0, i), memory_space=pltpu.SMEM),
                  pl.BlockSpec((tb * nsub, LANES), lambda i, pe: (i, 0))],
        out_specs=pl.BlockSpec(memory_space=pl.ANY),
        scratch_shapes=[pltpu.VMEM((te * nsub, LANES), U32), pltpu.SemaphoreType.DMA(()),
                        pltpu.SemaphoreType.DMA(())],
    )
    return pl.pallas_call(
        functools.partial(_dispatch_kernel, nsub=nsub),
        grid_spec=grid_spec,
        out_shape=jax.ShapeDtypeStruct((n_rows * nsub, LANES), U32),
        compiler_params=_cparams(("arbitrary",)),
        name="dispatch",
    )(pad_end, dest, hpk)


def _experts_kernel(be_ref, nu_ref, x_ref, wg_ref, wu_ref, wd_ref, o_ref, wg_s, wu_s, wd_s):
    i = pl.program_id(0)
    prev = be_ref[jnp.maximum(i - 1, 0)]

    @pl.when((i == 0) | (be_ref[i] != prev))
    def _():
        wg_s[...] = wg_ref[0].astype(BF16)
        wu_s[...] = wu_ref[0].astype(BF16)
        wd_s[...] = wd_ref[0].astype(BF16)

    @pl.when(i < nu_ref[0])
    def _():
        x = _unpack_pairs(_load_token_tiles(x_ref, _tile_rows(wg_s.shape[0])), BF16)
        g = _dot(x, wg_s[...])
        hidden = (g * jax.nn.sigmoid(g)) * _dot(x, wu_s[...])
        _store_token_tiles(o_ref, _pack_pairs(_dot(hidden.astype(BF16), wd_s[...])))

    @pl.when(i >= nu_ref[0])
    def _():
        o_ref[...] = jnp.zeros_like(o_ref)


def _experts(block_expert, n_used, xs, w_gate, w_up, w_down, tm):
    d, ff = w_gate.shape[1:]
    nsub = _tile_rows(d)
    p = xs.shape[0] // nsub
    rows = pl.BlockSpec((tm * nsub, LANES), lambda i, be, nu: (i, 0))
    grid_spec = pltpu.PrefetchScalarGridSpec(
        num_scalar_prefetch=2,
        grid=(p // tm,),
        in_specs=[rows,
                  pl.BlockSpec((1, d, ff), lambda i, be, nu: (be[i], 0, 0)),
                  pl.BlockSpec((1, d, ff), lambda i, be, nu: (be[i], 0, 0)),
                  pl.BlockSpec((1, ff, d), lambda i, be, nu: (be[i], 0, 0))],
        out_specs=rows,
        scratch_shapes=[pltpu.VMEM((d, ff), BF16), pltpu.VMEM((d, ff), BF16), pltpu.VMEM((ff, d), BF16)],
    )
    return pl.pallas_call(
        _experts_kernel,
        grid_spec=grid_spec,
        out_shape=jax.ShapeDtypeStruct(xs.shape, U32),
        compiler_params=_cparams(("arbitrary",)),
        name="experts",
    )(block_expert, n_used, xs, w_gate, w_up, w_down)


def _combine_kernel(dest_ref, wk_ref, ys_ref, x1_ref, hpk_ref, gt_ref, wsg_ref, wsu_ref, wsd_ref, gf_ref,
                    yp_ref, ysm_ref, buf, sem, *, n_prompt_blocks):
    tb, d = x1_ref.shape
    nsub = _tile_rows(d)
    seg = gt_ref.shape[0]
    i = pl.program_id(0)

    def start(t0, u):
        for k in range(TOP_K):
            _token_copy(ys_ref, dest_ref[k, t0 + u], buf.at[k], t0 + u, nsub,
                        sem).start(priority=k % 2)

    def wait(t0, u):
        for k in range(TOP_K):
            _token_copy(ys_ref, 0, buf.at[k], 0, nsub, sem).wait()

    _for_token_groups(tb, start)
    x = _unpack_pairs(_load_token_tiles(hpk_ref, nsub), BF16)
    g = _dot(x, wsg_ref[...])
    ffn = _dot(((g * jax.nn.sigmoid(g)) * _dot(x, wsu_ref[...])).astype(BF16), wsd_ref[...])
    _for_token_groups(tb, wait)
    wk = wk_ref[...]
    for k in range(TOP_K):
        ffn = ffn + wk[:, k:k + 1] * _unpack_pairs(_load_token_tiles(buf.at[k], nsub), F32)
    x2 = x1_ref[...].reshape(seg, tb // seg, d) + gt_ref[...] * ffn.reshape(seg, tb // seg, d)
    y = (_rms(x2) * gf_ref[...].reshape(1, 1, d)).reshape(tb, d)

    @pl.when(i < n_prompt_blocks)
    def _():
        yp_ref[...] = y

    @pl.when(i >= n_prompt_blocks)
    def _():
        ysm_ref[...] = y


def _combine(dest, wk_t, ys, x1, hpk, gt2_seg, ws_gate, ws_up, ws_down, g_final, n_prompt, tb):
    n, d = x1.shape
    nsub = _tile_rows(d)
    ff = ws_gate.shape[1]
    npb = n_prompt // tb
    nsb = (n - n_prompt) // tb
    seg = gt2_seg.shape[0] * tb // n
    row = lambda width: pl.BlockSpec((tb, width), lambda i: (i, 0))
    full = lambda shape: pl.BlockSpec(shape, lambda i: (0,) * len(shape))
    return pl.pallas_call(
        functools.partial(_combine_kernel, n_prompt_blocks=npb),
        grid=(n // tb,),
        in_specs=[pl.BlockSpec((DEST_ROWS, tb), lambda i: (0, i), memory_space=pltpu.SMEM),
                  row(8), pl.BlockSpec(memory_space=pl.ANY), row(d),
                  pl.BlockSpec((tb * nsub, LANES), lambda i: (i, 0)),
                  pl.BlockSpec((seg, 1, d), lambda i: (i, 0, 0)),
                  full((d, ff)), full((d, ff)), full((ff, d)), full((1, d))],
        out_specs=[pl.BlockSpec((tb, d), lambda i: (jnp.minimum(i, npb - 1), 0)),
                   pl.BlockSpec((tb, d), lambda i: (jnp.clip(i - npb, 0, nsb - 1), 0))],
        out_shape=[jax.ShapeDtypeStruct((n_prompt, d), F32), jax.ShapeDtypeStruct((n - n_prompt, d), F32)],
        scratch_shapes=[pltpu.VMEM((TOP_K, tb * nsub, LANES), U32), pltpu.SemaphoreType.DMA(())],
        compiler_params=_cparams(("arbitrary",)),
        name="combine",
    )(dest, wk_t, ys, x1, hpk, gt2_seg, ws_gate, ws_up, ws_down, g_final.reshape(1, d))


def _largest_tile(n, target):
    t = min(n, target)
    while n % t:
        t -= 1
    return t


def _mixer(x, mod, past_k, past_v, h0, conv0, p, first_pos_zero, n_total, row_offset, earlier):
    b, t, d = x.shape
    m = b * t
    a = d
    hn_bm, hn_tm = _norm_mod(x, mod, p["g_mix"])
    hn_bm = hn_bm.reshape(m, d)
    tm = _largest_tile(m, 512)
    tmh = _largest_tile(m, 1024)
    w_in = p["w_in"]
    u = _proj(hn_tm, w_in, 0, d, "plain", BF16, tmh).reshape(t, b, d)
    gl = _proj(hn_tm, w_in, 1, d, "gelu", BF16, tmh).reshape(t, b, d)
    sl = _proj(hn_tm, w_in, 5, d, "sigmoid", BF16, tmh).reshape(t, b, d)
    q = _proj(hn_bm, w_in, 2, a, "qscale", BF16, tmh).reshape(b, t, a)
    k = _proj(hn_bm, w_in, 3, a, "plain", F32, tm).reshape(b, t, a)
    v = _proj(hn_bm, w_in, 4, a, "plain", F32, tm).reshape(b, t, a)
    sz = _proj(hn_bm, w_in, 6, a, "sigmoid", BF16, tmh).reshape(b, t, a)

    if h0 is None:
        h0 = jnp.zeros((b, d), F32)
        prev = jnp.zeros((CONV_WIDTH - 1, b, d), F32)
    else:
        prev = conv0.transpose(1, 0, 2)
    tc = _largest_tile(t, 128)
    yl, h_last, conv_tail = _lru(u, gl, sl, p["conv_w"], p["conv_b"], p["w_rg_a"], p["b_rg_a"], p["w_rg_x"],
                                 p["b_rg_x"], p["lru_lambda"], h0, prev, tc, first_pos_zero)
    if past_k is None:
        mixed = _attn_prompt(q, k, v, sz, yl, _largest_tile(t, 256))
    else:
        mixed = _attn_sample(q, k, v, past_k, past_v, sz, yl)
    tmo = _largest_tile(m, 256)
    shared = _outproj(mixed.reshape(m, a), p["w_out"], x.reshape(m, d), mod, p["g_ffn"], tmo, t, n_total, row_offset,
                      earlier)
    nh = a // HEAD_DIM
    return (shared, k.reshape(b, t, nh, HEAD_DIM), v.reshape(b, t, nh, HEAD_DIM), h_last,
            conv_tail.transpose(1, 0, 2))


def _routing_tables(e_k, slot_k, counts, tm):
    n_tokens = e_k.shape[1]
    padded = (counts + tm - 1) // tm * tm
    pad_end = jnp.cumsum(padded).astype(I32)
    pad_start = pad_end - padded
    is_expert = e_k[:, :, None] == jnp.arange(N_EXPERTS, dtype=I32)
    dest = slot_k + jnp.sum(jnp.where(is_expert, pad_start, 0), axis=-1).astype(I32)
    n_blocks = -(-(TOP_K * n_tokens + N_EXPERTS * (tm - 1)) // tm)
    block_start = jnp.arange(n_blocks, dtype=I32) * tm
    block_expert = jnp.minimum(jnp.sum(pad_end[None, :] <= block_start[:, None], axis=1), N_EXPERTS - 1).astype(I32)
    n_used = (pad_end[-1:] // tm).astype(I32)
    return dest, pad_end, block_expert, n_used, n_blocks * tm


def kernel(x_prompt, x_sample, cache_k, cache_v, state_lru, state_conv, c_prompt, c_sample, g_mix, w_ada, b_ada, w_in, conv_w, conv_b, w_rg_a, b_rg_a, w_rg_x, b_rg_x, lru_lambda, w_out, g_ffn, w_router, b_router, w_gate, w_up, w_down, ws_gate, ws_up, ws_down, g_final):
    assert g_mix.shape[0] == 1, "single-layer trunk only"
    b, t, d = x_prompt.shape
    bs, ts, _ = x_sample.shape
    assert ts >= CONV_WIDTH - 1 and ts == TIME_GROUP and t % TIME_GROUP == 0
    p = dict(g_mix=g_mix[0], w_in=w_in[0], conv_w=conv_w[0], conv_b=conv_b[0],
             w_rg_a=w_rg_a[0].astype(BF16), b_rg_a=b_rg_a[0], w_rg_x=w_rg_x[0].astype(BF16), b_rg_x=b_rg_x[0],
             lru_lambda=lru_lambda[0], w_out=w_out[0], g_ffn=g_ffn[0])
    mod = _ada(jnp.concatenate([c_prompt, c_sample], axis=0), w_ada[0], b_ada[0]).reshape(b + bs, N_MOD, d)
    mod_p, mod_s = mod[:b], mod[b:]

    n_prompt = b * t
    n = n_prompt + bs * ts
    tb = _largest_tile(bs * ts, 256)
    assert n_prompt % tb == 0 and tb % ts == 0

    shared, kp, vp, hp, cp = _mixer(x_prompt, mod_p, None, None, None, None, p, True, n, 0, None)
    (x1, hpk, hlo), ks, vs, hs, cs = _mixer(x_sample, mod_s, cache_k[0], cache_v[0], state_lru[0], state_conv[0],
                                            p, False, n, n_prompt, shared)

    e_k, w_k, slot_k, counts = _router(hpk, hlo, w_router[0], b_router[0], _largest_tile(n, ROUTER_TOKENS))
    tme = EXPERT_ROW_BLOCK
    dest, pad_end, block_expert, n_used, n_rows = _routing_tables(e_k, slot_k, counts, tme)
    xs = _dispatch(pad_end, dest, hpk, n_rows, tb, tme, _tile_rows(d))
    ys = _experts(block_expert, n_used, xs, w_gate[0], w_up[0], w_down[0], tme)
    gt2 = jnp.concatenate([jnp.broadcast_to(mod_p[:, None, 5:6, :], (b, t // ts, 1, d)).reshape(-1, 1, d),
                           mod_s[:, 5:6, :]], axis=0)
    y_p, y_s = _combine(dest, w_k.T, ys, x1, hpk, gt2, ws_gate[0].astype(BF16), ws_up[0].astype(BF16),
                        ws_down[0].astype(BF16), g_final, n_prompt, tb)
    return (y_p.reshape(b, t, d), y_s.reshape(bs, ts, d), kp[None], vp[None], hp[None], cp[None],
            ks[None], vs[None], hs[None], cs[None])
```

```python
import functools

import jax
import jax.numpy as jnp
from jax import lax
from jax.experimental import pallas as pl
from jax.experimental.pallas import tpu as pltpu

F32 = jnp.float32
BF16 = jnp.bfloat16
I32 = jnp.int32
U32 = jnp.uint32

LANES = 128
SUBLANES = 8
HEAD_DIM = 128
LRU_BLOCK_DIM = 256
LRU_C = 8.0
CONV_WIDTH = 4
N_EXPERTS = 64
TOP_K = 6
N_GROUPS = 8
GROUP_SIZE = N_EXPERTS // N_GROUPS
TOPK_GROUPS = 4
ROUTED_SCALE = 2.5
NORM_EPS = 1e-6
N_MOD = 6
TIME_GROUP = 16
LOG2_E = 1.4426950408889634
ATTN_HEADS_PER_STEP = 8
F32_ZERO_WEIGHT_LOG2 = -150.0
ROUTER_TOKENS = 768
DEST_ROWS = 8
EXPERT_ROW_BLOCK = 512
V7X_VMEM_LIMIT_BYTES = 56 * 1024 * 1024
NT_DIMS = (((1,), (1,)), ((), ()))
NEG_INF = float("-inf")


def _cparams(semantics):
    return pltpu.CompilerParams(dimension_semantics=semantics, vmem_limit_bytes=V7X_VMEM_LIMIT_BYTES)


def _dot(a, b):
    return jnp.dot(a, b, preferred_element_type=F32)


def _split_bf16(x):
    hi = x.astype(BF16)
    lo = (x - hi.astype(F32)).astype(BF16)
    return hi, lo


def _dot3(a, b):
    ah, al = _split_bf16(a)
    bh, bl = _split_bf16(b)
    return _dot(ah, bh) + (_dot(al, bh) + _dot(ah, bl))


def _pack_pairs(h):
    n = h.shape[1] // 2
    bits = lax.bitcast_convert_type(h.astype(BF16).astype(F32), U32)
    return bits[:, :n] | (bits[:, n:] >> 16)


def _store_token_tiles(ref, packed):
    m, n = packed.shape
    nsub = n // LANES
    for j in range(nsub):
        ref[pl.ds(j, m, stride=nsub), :] = packed[:, j * LANES:(j + 1) * LANES]


def _tile_rows(d):
    return d // 2 // LANES


def _load_token_tiles(ref, nsub):
    m = ref.shape[0] // nsub
    return jnp.concatenate([ref[pl.ds(j, m, stride=nsub), :] for j in range(nsub)], axis=1)


def _unpack_pairs(p, dtype):
    a = lax.bitcast_convert_type(p & jnp.uint32(0xFFFF0000), F32)
    b = lax.bitcast_convert_type(p << 16, F32)
    return jnp.concatenate([a.astype(dtype), b.astype(dtype)], axis=1)


def _rms(x):
    return x * lax.rsqrt(jnp.mean(x * x, axis=-1, keepdims=True) + NORM_EPS)


def _ada_kernel(c_ref, w_ref, b_ref, o_ref):
    c = c_ref[...]
    o_ref[...] = _dot3(c * jax.nn.sigmoid(c), w_ref[...]) + b_ref[...]


def _ada(c, w, b):
    m, d = c.shape
    n = w.shape[1]
    tn = min(n, 1024)
    return pl.pallas_call(
        _ada_kernel,
        grid=(n // tn,),
        in_specs=[pl.BlockSpec((m, d), lambda j: (0, 0)),
                  pl.BlockSpec((d, tn), lambda j: (0, j)),
                  pl.BlockSpec((1, tn), lambda j: (0, j))],
        out_specs=pl.BlockSpec((m, tn), lambda j: (0, j)),
        out_shape=jax.ShapeDtypeStruct((m, n), F32),
        compiler_params=_cparams(("arbitrary",)),
        name="ada",
    )(c, w, b.reshape(1, n))


def _row_permutation(n_outer, n_inner):
    n = n_outer * n_inner
    out_row = lax.broadcasted_iota(I32, (n, n), 0)
    in_row = lax.broadcasted_iota(I32, (n, n), 1)
    same = (out_row // n_outer == in_row % n_inner) & (out_row % n_outer == in_row // n_inner)
    return same.astype(BF16)


def _norm_mod_kernel(x_ref, mod_ref, g_ref, obm_ref, otm_ref):
    b, tt, d = x_ref.shape
    m = mod_ref[...]
    h = (_rms(x_ref[...]) * g_ref[...].reshape(1, 1, d) * (1.0 + m[:, 1:2, :]) + m[:, 0:1, :]).astype(BF16)
    obm_ref[...] = h
    otm_ref[...] = _dot(_row_permutation(b, tt), h.reshape(b * tt, d)).astype(BF16)


def _norm_mod(x, mod, g):
    b, t, d = x.shape
    tt = TIME_GROUP
    return pl.pallas_call(
        _norm_mod_kernel,
        grid=(t // tt,),
        in_specs=[pl.BlockSpec((b, tt, d), lambda ti: (0, ti, 0)),
                  pl.BlockSpec((b, N_MOD, d), lambda ti: (0, 0, 0)),
                  pl.BlockSpec((1, d), lambda ti: (0, 0))],
        out_specs=[pl.BlockSpec((b, tt, d), lambda ti: (0, ti, 0)),
                   pl.BlockSpec((tt * b, d), lambda ti: (ti, 0))],
        out_shape=[jax.ShapeDtypeStruct((b, t, d), BF16), jax.ShapeDtypeStruct((t * b, d), BF16)],
        compiler_params=_cparams(("arbitrary",)),
        name="norm_mod",
    )(x, mod, g.reshape(1, d))


def _proj_kernel(a_ref, w_ref, o_ref, w_s, *, epilogue):
    @pl.when(pl.program_id(0) == 0)
    def _():
        w_s[...] = w_ref[...].astype(BF16)

    acc = _dot(a_ref[...], w_s[...])
    if epilogue == "gelu":
        acc = jax.nn.gelu(acc)
    elif epilogue == "sigmoid":
        acc = jax.nn.sigmoid(acc)
    elif epilogue == "qscale":
        acc = acc * (HEAD_DIM ** -0.5 * LOG2_E)
    o_ref[...] = acc.astype(o_ref.dtype)


def _resident(shape, index_map):
    return pl.BlockSpec(shape, index_map, pipeline_mode=pl.Buffered(1))


def _proj(a, w, col, width, epilogue, out_dtype, tm):
    m, k = a.shape
    return pl.pallas_call(
        functools.partial(_proj_kernel, epilogue=epilogue),
        grid=(m // tm,),
        in_specs=[pl.BlockSpec((tm, k), lambda i: (i, 0)),
                  _resident((k, width), lambda i: (0, col))],
        out_specs=pl.BlockSpec((tm, width), lambda i: (i, 0)),
        out_shape=jax.ShapeDtypeStruct((m, width), out_dtype),
        scratch_shapes=[pltpu.VMEM((k, width), BF16)],
        compiler_params=_cparams(("arbitrary",)),
        name="proj_" + epilogue,
    )(a, w)


def _lru_kernel(u_ref, gl_ref, sl_ref, cw_ref, cb_ref, wa_ref, ba_ref, wx_ref, bx_ref, lam_ref, h0_ref, prev_ref,
                y_ref, hl_ref, cn_ref, tail_s, h_s, a_s, b_s, y_s, *, first_pos_zero):
    tc, nb, wb = u_ref.shape
    ti = pl.program_id(1)

    @pl.when(ti == 0)
    def _():
        tail_s[...] = prev_ref[...]
        h_s[...] = h0_ref[...]

    u = u_ref[...].astype(F32)
    xp = jnp.concatenate([tail_s[...], u], axis=0)
    cw = cw_ref[...]
    xc = cb_ref[...].reshape(1, 1, wb)
    for j in range(CONV_WIDTH):
        xc = xc + xp[j:j + tc] * cw[j:j + 1].reshape(1, 1, wb)
    last_rows = u[tc - (CONV_WIDTH - 1):]
    tail_s[...] = last_rows
    cn_ref[...] = last_rows

    x2 = xc.reshape(tc * nb, wb)
    xb = x2.astype(BF16)
    r = jax.nn.sigmoid(_dot(xb, wa_ref[0]) + ba_ref[0])
    gi = jax.nn.sigmoid(_dot(xb, wx_ref[0]) + bx_ref[0])
    log_a = r * ((-LRU_C) * jax.nn.softplus(-lam_ref[...]))
    a = jnp.exp(log_a)
    m2 = -jnp.tanh(log_a) * (a * a + 1.0)
    mult = jnp.where(m2 > 0.0, m2 * lax.rsqrt(m2), 0.0)
    if first_pos_zero:
        row = lax.broadcasted_iota(I32, (tc * nb, wb), 0)
        mult = jnp.where((row < nb) & (ti == 0), 1.0, mult)
    a_s[...] = a.reshape(tc, nb, wb)
    b_s[...] = (mult * (gi * x2)).reshape(tc, nb, wb)

    def step(t, h):
        h = a_s[t] * h + b_s[t]
        gate = gl_ref[t].astype(F32) * sl_ref[t].astype(F32)
        y_s[t] = h * gate
        return h

    h = lax.fori_loop(0, tc, step, h_s[...], unroll=8)
    h_s[...] = h
    hl_ref[...] = h

    perm = _row_permutation(TIME_GROUP, nb)
    for g in range(tc // TIME_GROUP):
        rows = slice(g * TIME_GROUP, (g + 1) * TIME_GROUP)
        yg = y_s[rows].reshape(TIME_GROUP * nb, wb).astype(BF16)
        y_ref[:, rows, :] = _dot(perm, yg).astype(BF16).reshape(nb, TIME_GROUP, wb)


def _lru(u, gl, sl, conv_w, conv_b, wa, ba, wx, bx, lam, h0, prev, tc, first_pos_zero):
    t, b, w = u.shape
    wb = LRU_BLOCK_DIM
    nw = w // wb
    seq = pl.BlockSpec((tc, b, wb), lambda wi, ti: (ti, 0, wi))
    seq_out = pl.BlockSpec((b, tc, wb), lambda wi, ti: (0, ti, wi))
    chan = lambda rows: pl.BlockSpec((rows, wb), lambda wi, ti: (0, wi))
    gate_w = pl.BlockSpec((1, wb, wb), lambda wi, ti: (wi, 0, 0))
    gate_b = pl.BlockSpec((1, 1, wb), lambda wi, ti: (wi, 0, 0))
    tail = pl.BlockSpec((CONV_WIDTH - 1, b, wb), lambda wi, ti: (0, 0, wi))
    return pl.pallas_call(
        functools.partial(_lru_kernel, first_pos_zero=first_pos_zero),
        grid=(nw, t // tc),
        in_specs=[seq, seq, seq, chan(CONV_WIDTH), chan(1), gate_w, gate_b, gate_w, gate_b, chan(1),
                  chan(b), tail],
        out_specs=[seq_out, chan(b), tail],
        out_shape=[jax.ShapeDtypeStruct((b, t, w), BF16), jax.ShapeDtypeStruct((b, w), F32),
                   jax.ShapeDtypeStruct((CONV_WIDTH - 1, b, w), F32)],
        scratch_shapes=[pltpu.VMEM((CONV_WIDTH - 1, b, wb), F32), pltpu.VMEM((b, wb), F32),
                        pltpu.VMEM((tc, b, wb), F32), pltpu.VMEM((tc, b, wb), F32),
                        pltpu.VMEM((tc, b, wb), F32)],
        compiler_params=_cparams(("arbitrary", "arbitrary")),
        name="lru",
    )(u, gl, sl, conv_w, conv_b.reshape(1, w), wa, ba.reshape(nw, 1, wb), wx, bx.reshape(nw, 1, wb),
      lam.reshape(1, w), h0, prev)


def _log_keep_and_logsig(z):
    nz = -z
    lk = jnp.minimum(nz, 0.0) - jnp.log2(1.0 + jnp.exp2(jnp.minimum(z, nz)))
    return lk, z + lk


def _later_key_sums(lk, tri):
    hi, lo = _split_bf16(lk)
    return _dot(hi, tri) + _dot(lo, tri)


def _strict_lower(n, dtype):
    row = lax.broadcasted_iota(I32, (n, n), 0)
    col = lax.broadcasted_iota(I32, (n, n), 1)
    return (row > col).astype(dtype), col < row


def _attn_kernel(q_ref, k_ref, v_ref, sz_ref, yl_ref, o_ref):
    tq = q_ref.shape[1]
    nh = q_ref.shape[2] // HEAD_DIM
    qi = pl.program_id(2)
    tri, causal = _strict_lower(tq, BF16)

    def tile(j, carry, diag):
        start = pl.multiple_of(j * tq, tq)
        heads = [slice(h * HEAD_DIM, (h + 1) * HEAD_DIM) for h in range(nh)]
        zs = [lax.dot_general(q_ref[0, :, cols], k_ref[0, pl.ds(start, tq), cols].astype(BF16), NT_DIMS,
                              preferred_element_type=F32) for cols in heads]
        lks, lss = [], []
        for z in zs:
            lk, ls = _log_keep_and_logsig(z)
            lks.append(jnp.where(causal, lk, 0.0) if diag else lk)
            lss.append(ls)
        rcs = [_later_key_sums(lk, tri) for lk in lks]
        ws = []
        for h in range(nh):
            w = jnp.exp2(lss[h] + rcs[h] + carry[h][1])
            ws.append((jnp.where(causal, w, 0.0) if diag else w).astype(BF16))
        return tuple((carry[h][0] + _dot(ws[h], v_ref[0, pl.ds(start, tq), heads[h]].astype(BF16)),
                      carry[h][1] + (rcs[h][:, 0:1] + lks[h][:, 0:1])) for h in range(nh))

    def any_weight_left(carry):
        c_max = carry[0][1]
        for h in range(1, nh):
            c_max = jnp.maximum(c_max, carry[h][1])
        return jnp.max(c_max) > F32_ZERO_WEIGHT_LOG2

    init = tuple((jnp.zeros((tq, HEAD_DIM), F32), jnp.zeros((tq, 1), F32)) for _ in range(nh))
    carry = tile(qi, init, True)

    def more(state):
        return (state[0] < qi) & state[1]

    def farther_tile(state):
        carry = tile(qi - 1 - state[0], state[2], False)
        return state[0] + 1, any_weight_left(carry), carry

    carry = lax.while_loop(more, farther_tile, (jnp.int32(0), any_weight_left(carry), carry))[2]
    acc = jnp.concatenate([carry[h][0] for h in range(nh)], axis=1)
    o_ref[0] = (yl_ref[0].astype(F32) + sz_ref[0].astype(F32) * acc).astype(BF16)


def _attn_prompt(q, k, v, sz, yl, tq):
    b, t, a = q.shape
    wblk = min(ATTN_HEADS_PER_STEP * HEAD_DIM, a)
    qspec = pl.BlockSpec((1, tq, wblk), lambda bi, hi, qi: (bi, qi, hi))
    kvspec = pl.BlockSpec((1, t, wblk), lambda bi, hi, qi: (bi, 0, hi))
    return pl.pallas_call(
        _attn_kernel,
        grid=(b, a // wblk, t // tq),
        in_specs=[qspec, kvspec, kvspec, qspec, qspec],
        out_specs=qspec,
        out_shape=jax.ShapeDtypeStruct((b, t, a), BF16),
        compiler_params=_cparams(("arbitrary", "arbitrary", "arbitrary")),
        name="attn_prompt",
    )(q, k, v, sz, yl)


def _attn_sample_kernel(q_ref, kn_ref, vn_ref, kc_ref, vc_ref, sz_ref, yl_ref, o_ref, lk_s, ls_s):
    ts = q_ref.shape[1]
    nh = q_ref.shape[2] // HEAD_DIM
    past = kc_ref.shape[1] // nh
    tri_n, causal = _strict_lower(ts, BF16)
    tri_p, _ = _strict_lower(past, BF16)
    accs, carries = [], []
    for h in range(nh):
        cols = slice(h * HEAD_DIM, (h + 1) * HEAD_DIM)
        q = q_ref[0, :, cols]
        zn = lax.dot_general(q, kn_ref[0, :, cols].astype(BF16), NT_DIMS, preferred_element_type=F32)
        lk, ls = _log_keep_and_logsig(zn)
        lk = jnp.where(causal, lk, 0.0)
        rc = _later_key_sums(lk, tri_n)
        w = jnp.where(causal, jnp.exp2(ls + rc), 0.0)
        accs.append(_dot(w.astype(BF16), vn_ref[0, :, cols].astype(BF16)))
        carries.append(rc[:, 0:1] + lk[:, 0:1])
        head_rows = pl.ds(h, past, stride=nh)
        zp = lax.dot_general(q, kc_ref[0, head_rows, :].astype(BF16), NT_DIMS, preferred_element_type=F32)
        lkp, lsp = _log_keep_and_logsig(zp)
        lk_s[h * ts:(h + 1) * ts, :] = lkp
        ls_s[h * ts:(h + 1) * ts, :] = lsp
    rcp = _later_key_sums(lk_s[...], tri_p)
    for h in range(nh):
        cols = slice(h * HEAD_DIM, (h + 1) * HEAD_DIM)
        rows = slice(h * ts, (h + 1) * ts)
        w = jnp.exp2(ls_s[rows, :] + rcp[rows, :] + carries[h])
        acc = accs[h] + _dot(w.astype(BF16), vc_ref[0, pl.ds(h, past, stride=nh), :].astype(BF16))
        o_ref[0, :, cols] = (yl_ref[0, :, cols].astype(F32) + sz_ref[0, :, cols].astype(F32) * acc).astype(BF16)


def _attn_sample(q, kn, vn, kc, vc, sz, yl):
    b, ts, a = q.shape
    past = kc.shape[1]
    nh = a // HEAD_DIM
    kc = kc.reshape(b, past * nh, HEAD_DIM)
    vc = vc.reshape(b, past * nh, HEAD_DIM)
    cur = pl.BlockSpec((1, ts, a), lambda bi: (bi, 0, 0))
    cache = pl.BlockSpec((1, past * nh, HEAD_DIM), lambda bi: (bi, 0, 0))
    return pl.pallas_call(
        _attn_sample_kernel,
        grid=(b,),
        in_specs=[cur, cur, cur, cache, cache, cur, cur],
        out_specs=cur,
        out_shape=jax.ShapeDtypeStruct((b, ts, a), BF16),
        scratch_shapes=[pltpu.VMEM((nh * ts, past), F32), pltpu.VMEM((nh * ts, past), F32)],
        compiler_params=_cparams(("arbitrary",)),
        name="attn_sample",
    )(q, kn, vn, kc, vc, sz, yl)


def _outproj_kernel(a_ref, w_ref, x_ref, mod_ref, g_ref, *rest):
    x1_ref, hpk_ref, hlo_ref, w_s = rest[-4:]
    nb = mod_ref.shape[0]
    tm, d = x_ref.shape

    @pl.when(pl.program_id(0) == 0)
    def _():
        w_s[...] = w_ref[...].astype(BF16)

    m = mod_ref[...]
    o = _dot(a_ref[...], w_s[...]).reshape(nb, tm // nb, d)
    x1 = x_ref[...].reshape(nb, tm // nb, d) + m[:, 2:3, :] * o
    h = _rms(x1) * g_ref[...].reshape(1, 1, d) * (1.0 + m[:, 4:5, :]) + m[:, 3:4, :]
    x1_ref[...] = x1.reshape(tm, d)
    h = h.reshape(tm, d)
    hi = h.astype(BF16).astype(F32)
    _store_token_tiles(hpk_ref, _pack_pairs(hi))
    hlo_ref[...] = (h - hi).astype(BF16)


def _outproj(a, w, x, mod, g, tm, rows_per_batch, n_total, row_offset, earlier=None):
    m, d = x.shape
    nb = max(tm // rows_per_batch, 1)
    per = max(rows_per_batch // tm, 1)
    off = row_offset // tm
    nsub = _tile_rows(d)
    row = lambda width: pl.BlockSpec((tm, width), lambda i: (i, 0))
    out_row = lambda width: pl.BlockSpec((tm, width), lambda i: (i + off, 0))
    in_specs = [row(d), _resident((d, d), lambda i: (0, 0)), row(d),
                pl.BlockSpec((nb, N_MOD, d), lambda i: (i // per, 0, 0)),
                pl.BlockSpec((1, d), lambda i: (0, 0))]
    args = [a, w, x, mod, g.reshape(1, d)]
    aliases = {}
    if earlier is not None:
        aliases = {len(args) + j: j for j in range(len(earlier))}
        in_specs += [pl.BlockSpec(memory_space=pl.ANY)] * len(earlier)
        args += list(earlier)
    return pl.pallas_call(
        _outproj_kernel,
        grid=(m // tm,),
        in_specs=in_specs,
        out_specs=[out_row(d), pl.BlockSpec((tm * nsub, LANES), lambda i: (i + off, 0)), out_row(d)],
        out_shape=[jax.ShapeDtypeStruct((n_total, d), F32), jax.ShapeDtypeStruct((n_total * nsub, LANES), U32),
                   jax.ShapeDtypeStruct((n_total, d), BF16)],
        scratch_shapes=[pltpu.VMEM((d, d), BF16)],
        input_output_aliases=aliases,
        compiler_params=_cparams(("arbitrary",)),
        name="outproj",
    )(*args)


def _sub_max(x):
    return jnp.max(x, axis=0, keepdims=True)


def _sub_min(x):
    return jnp.min(x, axis=0, keepdims=True)


def _sub_sum(x):
    return jnp.sum(x, axis=0, keepdims=True)


def _router_kernel(hpk_ref, hlo_ref, wrh_ref, wrl_ref, br_ref, e_ref, w_ref, slot_ref, cnt_ref, base_s):
    tm = hlo_ref.shape[0]

    @pl.when(pl.program_id(0) == 0)
    def _():
        base_s[...] = jnp.zeros_like(base_s)

    hi = _unpack_pairs(_load_token_tiles(hpk_ref, _tile_rows(hlo_ref.shape[1])), BF16)
    lo = hlo_ref[...]
    wrh = wrh_ref[...]
    logits = lax.dot_general(wrh, hi, NT_DIMS, preferred_element_type=F32)
    logits = logits + (lax.dot_general(wrh, lo, NT_DIMS, preferred_element_type=F32)
                       + lax.dot_general(wrl_ref[...], hi, NT_DIMS, preferred_element_type=F32))
    score = jax.nn.sigmoid(logits)
    sel = (score + br_ref[...]).reshape(GROUP_SIZE, N_GROUPS, tm)
    score = score.reshape(GROUP_SIZE, N_GROUPS, tm)

    m1 = sel[0]
    m2 = jnp.full_like(m1, NEG_INF)
    for e in range(1, GROUP_SIZE):
        x = sel[e]
        m2 = jnp.maximum(m2, jnp.minimum(m1, x))
        m1 = jnp.maximum(m1, x)
    cur = m1 + m2
    gidx = lax.broadcasted_iota(I32, (N_GROUPS, tm), 0)
    gmask = jnp.zeros((N_GROUPS, tm), jnp.bool_)
    for _ in range(TOPK_GROUPS):
        first = _sub_min(jnp.where(cur == _sub_max(cur), gidx, N_GROUPS))
        pick = gidx == first
        gmask = gmask | pick
        cur = jnp.where(pick, NEG_INF, cur)

    eid = (lax.broadcasted_iota(I32, (GROUP_SIZE, N_GROUPS, tm), 1) * GROUP_SIZE
           + lax.broadcasted_iota(I32, (GROUP_SIZE, N_GROUPS, tm), 0))
    cur = jnp.where(gmask[None], sel, NEG_INF)
    ids, wts, picks = [], [], []
    for _ in range(TOP_K):
        mx = _sub_max(jnp.max(cur, axis=0))
        first = _sub_min(jnp.min(jnp.where(cur == mx[None], eid, N_EXPERTS), axis=0))
        pick = eid == first[None]
        ids.append(first)
        picks.append(pick)
        wts.append(_sub_sum(jnp.sum(jnp.where(pick, score, 0.0), axis=0)))
        cur = jnp.where(pick, NEG_INF, cur)
    total = wts[0]
    chosen = picks[0]
    for k in range(1, TOP_K):
        total = total + wts[k]
        chosen = chosen | picks[k]

    chosen = jnp.where(chosen, 1.0, 0.0).reshape(N_EXPERTS, tm)
    earlier_or_self = (lax.broadcasted_iota(I32, (tm, tm), 0) <= lax.broadcasted_iota(I32, (tm, tm), 1)).astype(BF16)
    incl = _dot(chosen.astype(BF16), earlier_or_self)
    rank = (base_s[...] + (incl - chosen)).reshape(GROUP_SIZE, N_GROUPS, tm)
    base_s[...] = base_s[...] + incl[:, tm - 1:tm]
    slots = [_sub_sum(jnp.sum(jnp.where(pick, rank, 0.0), axis=0)).astype(I32) for pick in picks]

    pad = 8 - TOP_K
    e_ref[...] = jnp.concatenate(ids + [jnp.zeros((pad, tm), I32)], axis=0)
    w_ref[...] = jnp.concatenate([w / total * ROUTED_SCALE for w in wts] + [jnp.zeros((pad, tm), F32)], axis=0)
    slot_ref[...] = jnp.concatenate(slots + [jnp.zeros((pad, tm), I32)], axis=0)
    cnt_ref[...] = jnp.broadcast_to(base_s[...], cnt_ref.shape)


def _router(hpk, hlo, w_router, b_router, tm):
    n, d = hlo.shape
    wr = w_router.T.reshape(N_GROUPS, GROUP_SIZE, d).transpose(1, 0, 2).reshape(N_EXPERTS, d)
    br = b_router.reshape(N_GROUPS, GROUP_SIZE).T.reshape(N_EXPERTS, 1)
    wrh = wr.astype(BF16)
    wrl = (wr - wrh.astype(F32)).astype(BF16)
    full = lambda shape: pl.BlockSpec(shape, lambda i: (0, 0))
    per_token = pl.BlockSpec((8, tm), lambda i: (0, i))
    e_k, w_k, slot_k, counts = pl.pallas_call(
        _router_kernel,
        grid=(n // tm,),
        in_specs=[pl.BlockSpec((tm * _tile_rows(d), LANES), lambda i: (i, 0)),
                  pl.BlockSpec((tm, d), lambda i: (i, 0)),
                  full((N_EXPERTS, d)), full((N_EXPERTS, d)), full((N_EXPERTS, 1))],
        out_specs=[per_token, per_token, per_token, full((N_EXPERTS, 128))],
        out_shape=[jax.ShapeDtypeStruct((8, n), I32), jax.ShapeDtypeStruct((8, n), F32),
                   jax.ShapeDtypeStruct((8, n), I32), jax.ShapeDtypeStruct((N_EXPERTS, 128), F32)],
        scratch_shapes=[pltpu.VMEM((N_EXPERTS, 1), F32)],
        compiler_params=_cparams(("arbitrary",)),
        name="router",
    )(hpk, hlo, wrh, wrl, br)
    counts = counts[:, 0].reshape(GROUP_SIZE, N_GROUPS).T.reshape(N_EXPERTS).astype(I32)
    return e_k, w_k, slot_k, counts


def _tile_window(token, nsub):
    start = token * nsub
    if nsub % SUBLANES == 0:
        start = pl.multiple_of(start, SUBLANES)
    return pl.ds(start, nsub)


def _token_copy(src, src_token, dst, dst_token, nsub, sem):
    return pltpu.make_async_copy(src.at[_tile_window(src_token, nsub)], dst.at[_tile_window(dst_token, nsub)], sem)


def _for_token_groups(n_tokens, per_token):
    def group(g, carry):
        t0 = pl.multiple_of(g * SUBLANES, SUBLANES)
        for u in range(SUBLANES):
            per_token(t0, u)
        return carry

    lax.fori_loop(0, n_tokens // SUBLANES, group, 0)


def _dispatch_kernel(pad_end_ref, dest_ref, h_ref, xs_ref, zero_s, sem, zero_sem, *, nsub):
    tb = dest_ref.shape[1]
    te = zero_s.shape[0] // nsub

    def tail_fill(e):
        start = pl.multiple_of((pad_end_ref[e] - te) * nsub, te * nsub)
        return pltpu.make_async_copy(zero_s, xs_ref.at[pl.ds(start, te * nsub)], zero_sem)

    def has_rows(e):
        return pad_end_ref[e] > (pad_end_ref[e - 1] if e else 0)

    @pl.when(pl.program_id(0) == 0)
    def _():
        zero_s[...] = jnp.zeros_like(zero_s)
        for e in range(N_EXPERTS):
            pl.when(has_rows(e))(lambda e=e: tail_fill(e).start())
        for e in range(N_EXPERTS):
            pl.when(has_rows(e))(lambda e=e: tail_fill(e).wait())

    def start(t0, u):
        for k in range(TOP_K):
            _token_copy(h_ref, t0 + u, xs_ref, dest_ref[k, t0 + u], nsub,
                        sem).start(priority=k % 2)

    def wait(t0, u):
        for k in range(TOP_K):
            _token_copy(h_ref, 0, xs_ref, 0, nsub, sem).wait()

    _for_token_groups(tb, start)
    _for_token_groups(tb, wait)


def _dispatch(pad_end, dest, hpk, n_rows, tb, te, nsub):
    n = hpk.shape[0] // nsub
    grid_spec = pltpu.PrefetchScalarGridSpec(
        num_scalar_prefetch=1,
        grid=(n // tb,),
        in_specs=[pl.BlockSpec((DEST_ROWS, tb), lambda i, pe: (0, i), memory_space=pltpu.SMEM),
                  pl.BlockSpec((tb * nsub, LANES), lambda i, pe: (i, 0))],
        out_specs=pl.BlockSpec(memory_space=pl.ANY),
        scratch_shapes=[pltpu.VMEM((te * nsub, LANES), U32), pltpu.SemaphoreType.DMA(()),
                        pltpu.SemaphoreType.DMA(())],
    )
    return pl.pallas_call(
        functools.partial(_dispatch_kernel, nsub=nsub),
        grid_spec=grid_spec,
        out_shape=jax.ShapeDtypeStruct((n_rows * nsub, LANES), U32),
        compiler_params=_cparams(("arbitrary",)),
        name="dispatch",
    )(pad_end, dest, hpk)


def _experts_kernel(be_ref, nu_ref, x_ref, wg_ref, wu_ref, wd_ref, o_ref, wg_s, wu_s, wd_s):
    i = pl.program_id(0)
    prev = be_ref[jnp.maximum(i - 1, 0)]

    @pl.when((i == 0) | (be_ref[i] != prev))
    def _():
        wg_s[...] = wg_ref[0].astype(BF16)
        wu_s[...] = wu_ref[0].astype(BF16)
        wd_s[...] = wd_ref[0].astype(BF16)

    @pl.when(i < nu_ref[0])
    def _():
        x = _unpack_pairs(_load_token_tiles(x_ref, _tile_rows(wg_s.shape[0])), BF16)
        g = _dot(x, wg_s[...])
        hidden = (g * jax.nn.sigmoid(g)) * _dot(x, wu_s[...])
        _store_token_tiles(o_ref, _pack_pairs(_dot(hidden.astype(BF16), wd_s[...])))

    @pl.when(i >= nu_ref[0])
    def _():
        o_ref[...] = jnp.zeros_like(o_ref)


def _experts(block_expert, n_used, xs, w_gate, w_up, w_down, tm):
    d, ff = w_gate.shape[1:]
    nsub = _tile_rows(d)
    p = xs.shape[0] // nsub
    rows = pl.BlockSpec((tm * nsub, LANES), lambda i, be, nu: (i, 0))
    grid_spec = pltpu.PrefetchScalarGridSpec(
        num_scalar_prefetch=2,
        grid=(p // tm,),
        in_specs=[rows,
                  pl.BlockSpec((1, d, ff), lambda i, be, nu: (be[i], 0, 0)),
                  pl.BlockSpec((1, d, ff), lambda i, be, nu: (be[i], 0, 0)),
                  pl.BlockSpec((1, ff, d), lambda i, be, nu: (be[i], 0, 0))],
        out_specs=rows,
        scratch_shapes=[pltpu.VMEM((d, ff), BF16), pltpu.VMEM((d, ff), BF16), pltpu.VMEM((ff, d), BF16)],
    )
    return pl.pallas_call(
        _experts_kernel,
        grid_spec=grid_spec,
        out_shape=jax.ShapeDtypeStruct(xs.shape, U32),
        compiler_params=_cparams(("arbitrary",)),
        name="experts",
    )(block_expert, n_used, xs, w_gate, w_up, w_down)


def _combine_kernel(dest_ref, wk_ref, ys_ref, x1_ref, hpk_ref, gtp_ref, gts_ref, wsg_ref, wsu_ref, wsd_ref, gf_ref,
                    yp_ref, ysm_ref, buf, sem, *, n_prompt_blocks):
    tb, d = x1_ref.shape
    nsub = _tile_rows(d)
    seg = gts_ref.shape[0]
    i = pl.program_id(0)

    def start(t0, u):
        for k in range(TOP_K):
            _token_copy(ys_ref, dest_ref[k, t0 + u], buf.at[k], t0 + u, nsub,
                        sem).start(priority=k % 2)

    def wait(t0, u):
        for k in range(TOP_K):
            _token_copy(ys_ref, 0, buf.at[k], 0, nsub, sem).wait()

    _for_token_groups(tb, start)
    x = _unpack_pairs(_load_token_tiles(hpk_ref, nsub), BF16)
    g = _dot(x, wsg_ref[...])
    ffn = _dot(((g * jax.nn.sigmoid(g)) * _dot(x, wsu_ref[...])).astype(BF16), wsd_ref[...])
    _for_token_groups(tb, wait)
    wk = wk_ref[...]
    for k in range(TOP_K):
        ffn = ffn + wk[:, k:k + 1] * _unpack_pairs(_load_token_tiles(buf.at[k], nsub), F32)
    gate2 = jnp.where(i >= n_prompt_blocks, gts_ref[...], gtp_ref[...])
    x2 = x1_ref[...].reshape(seg, tb // seg, d) + gate2 * ffn.reshape(seg, tb // seg, d)
    y = (_rms(x2) * gf_ref[...].reshape(1, 1, d)).reshape(tb, d)

    @pl.when(i < n_prompt_blocks)
    def _():
        yp_ref[...] = y

    @pl.when(i >= n_prompt_blocks)
    def _():
        ysm_ref[...] = y


def _combine(dest, wk_t, ys, x1, hpk, gate2_p, gate2_s, ws_gate, ws_up, ws_down, g_final, n_prompt, tb):
    n, d = x1.shape
    nsub = _tile_rows(d)
    ff = ws_gate.shape[1]
    npb = n_prompt // tb
    nsb = (n - n_prompt) // tb
    b, bs = gate2_p.shape[0], gate2_s.shape[0]
    per = npb // b
    assert nsb == 1 and npb % b == 0 and tb % bs == 0
    row = lambda width: pl.BlockSpec((tb, width), lambda i: (i, 0))
    full = lambda shape: pl.BlockSpec(shape, lambda i: (0,) * len(shape))
    return pl.pallas_call(
        functools.partial(_combine_kernel, n_prompt_blocks=npb),
        grid=(n // tb,),
        in_specs=[pl.BlockSpec((DEST_ROWS, tb), lambda i: (0, i), memory_space=pltpu.SMEM),
                  row(8), pl.BlockSpec(memory_space=pl.ANY), row(d),
                  pl.BlockSpec((tb * nsub, LANES), lambda i: (i, 0)),
                  pl.BlockSpec((1, 1, d), lambda i: (jnp.minimum(i // per, b - 1), 0, 0)),
                  full((bs, 1, d)),
                  full((d, ff)), full((d, ff)), full((ff, d)), full((1, d))],
        out_specs=[pl.BlockSpec((tb, d), lambda i: (jnp.minimum(i, npb - 1), 0)),
                   pl.BlockSpec((tb, d), lambda i: (jnp.clip(i - npb, 0, nsb - 1), 0))],
        out_shape=[jax.ShapeDtypeStruct((n_prompt, d), F32), jax.ShapeDtypeStruct((n - n_prompt, d), F32)],
        scratch_shapes=[pltpu.VMEM((TOP_K, tb * nsub, LANES), U32), pltpu.SemaphoreType.DMA(())],
        compiler_params=_cparams(("arbitrary",)),
        name="combine",
    )(dest, wk_t, ys, x1, hpk, gate2_p, gate2_s, ws_gate, ws_up, ws_down, g_final.reshape(1, d))


def _largest_tile(n, target):
    t = min(n, target)
    while n % t:
        t -= 1
    return t


def _mixer(x, mod, past_k, past_v, h0, conv0, p, first_pos_zero, n_total, row_offset, earlier):
    b, t, d = x.shape
    m = b * t
    a = d
    hn_bm, hn_tm = _norm_mod(x, mod, p["g_mix"])
    hn_bm = hn_bm.reshape(m, d)
    tm = _largest_tile(m, 512)
    tmh = _largest_tile(m, 1024)
    w_in = p["w_in"]
    u = _proj(hn_tm, w_in, 0, d, "plain", BF16, tmh).reshape(t, b, d)
    gl = _proj(hn_tm, w_in, 1, d, "gelu", BF16, tmh).reshape(t, b, d)
    sl = _proj(hn_tm, w_in, 5, d, "sigmoid", BF16, tmh).reshape(t, b, d)
    q = _proj(hn_bm, w_in, 2, a, "qscale", BF16, tmh).reshape(b, t, a)
    k = _proj(hn_bm, w_in, 3, a, "plain", F32, tm).reshape(b, t, a)
    v = _proj(hn_bm, w_in, 4, a, "plain", F32, tm).reshape(b, t, a)
    sz = _proj(hn_bm, w_in, 6, a, "sigmoid", BF16, tmh).reshape(b, t, a)

    if h0 is None:
        h0 = jnp.zeros((b, d), F32)
        prev = jnp.zeros((CONV_WIDTH - 1, b, d), F32)
    else:
        prev = conv0.transpose(1, 0, 2)
    tc = _largest_tile(t, 128)
    yl, h_last, conv_tail = _lru(u, gl, sl, p["conv_w"], p["conv_b"], p["w_rg_a"], p["b_rg_a"], p["w_rg_x"],
                                 p["b_rg_x"], p["lru_lambda"], h0, prev, tc, first_pos_zero)
    if past_k is None:
        mixed = _attn_prompt(q, k, v, sz, yl, _largest_tile(t, 256))
    else:
        mixed = _attn_sample(q, k, v, past_k, past_v, sz, yl)
    tmo = _largest_tile(m, 256)
    shared = _outproj(mixed.reshape(m, a), p["w_out"], x.reshape(m, d), mod, p["g_ffn"], tmo, t, n_total, row_offset,
                      earlier)
    nh = a // HEAD_DIM
    return (shared, k.reshape(b, t, nh, HEAD_DIM), v.reshape(b, t, nh, HEAD_DIM), h_last,
            conv_tail.transpose(1, 0, 2))


def _routing_tables(e_k, slot_k, counts, tm):
    n_tokens = e_k.shape[1]
    padded = (counts + tm - 1) // tm * tm
    pad_end = jnp.cumsum(padded).astype(I32)
    pad_start = pad_end - padded
    is_expert = e_k[:, :, None] == jnp.arange(N_EXPERTS, dtype=I32)
    dest = slot_k + jnp.sum(jnp.where(is_expert, pad_start, 0), axis=-1).astype(I32)
    n_blocks = -(-(TOP_K * n_tokens + N_EXPERTS * (tm - 1)) // tm)
    block_start = jnp.arange(n_blocks, dtype=I32) * tm
    block_expert = jnp.minimum(jnp.sum(pad_end[None, :] <= block_start[:, None], axis=1), N_EXPERTS - 1).astype(I32)
    n_used = (pad_end[-1:] // tm).astype(I32)
    return dest, pad_end, block_expert, n_used, n_blocks * tm


def kernel(x_prompt, x_sample, cache_k, cache_v, state_lru, state_conv, c_prompt, c_sample, g_mix, w_ada, b_ada, w_in, conv_w, conv_b, w_rg_a, b_rg_a, w_rg_x, b_rg_x, lru_lambda, w_out, g_ffn, w_router, b_router, w_gate, w_up, w_down, ws_gate, ws_up, ws_down, g_final):
    assert g_mix.shape[0] == 1, "single-layer trunk only"
    b, t, d = x_prompt.shape
    bs, ts, _ = x_sample.shape
    assert ts >= CONV_WIDTH - 1 and ts == TIME_GROUP and t % TIME_GROUP == 0
    p = dict(g_mix=g_mix[0], w_in=w_in[0], conv_w=conv_w[0], conv_b=conv_b[0],
             w_rg_a=w_rg_a[0].astype(BF16), b_rg_a=b_rg_a[0], w_rg_x=w_rg_x[0].astype(BF16), b_rg_x=b_rg_x[0],
             lru_lambda=lru_lambda[0], w_out=w_out[0], g_ffn=g_ffn[0])
    mod = _ada(jnp.concatenate([c_prompt, c_sample], axis=0), w_ada[0], b_ada[0]).reshape(b + bs, N_MOD, d)
    mod_p, mod_s = mod[:b], mod[b:]

    n_prompt = b * t
    n = n_prompt + bs * ts
    tb = _largest_tile(bs * ts, 256)
    assert n_prompt % tb == 0 and tb % ts == 0

    shared, kp, vp, hp, cp = _mixer(x_prompt, mod_p, None, None, None, None, p, True, n, 0, None)
    (x1, hpk, hlo), ks, vs, hs, cs = _mixer(x_sample, mod_s, cache_k[0], cache_v[0], state_lru[0], state_conv[0],
                                            p, False, n, n_prompt, shared)

    e_k, w_k, slot_k, counts = _router(hpk, hlo, w_router[0], b_router[0], _largest_tile(n, ROUTER_TOKENS))
    tme = EXPERT_ROW_BLOCK
    dest, pad_end, block_expert, n_used, n_rows = _routing_tables(e_k, slot_k, counts, tme)
    xs = _dispatch(pad_end, dest, hpk, n_rows, tb, tme, _tile_rows(d))
    ys = _experts(block_expert, n_used, xs, w_gate[0], w_up[0], w_down[0], tme)
    y_p, y_s = _combine(dest, w_k.T, ys, x1, hpk, mod_p[:, 5:6, :], mod_s[:, 5:6, :], ws_gate[0].astype(BF16),
                        ws_up[0].astype(BF16), ws_down[0].astype(BF16), g_final, n_prompt, tb)
    return (y_p.reshape(b, t, d), y_s.reshape(bs, ts, d), kp[None], vp[None], hp[None], cp[None],
            ks[None], vs[None], hs[None], cs[None])
```

```python
import functools

import jax
import jax.numpy as jnp
from jax import lax
from jax.experimental import pallas as pl
from jax.experimental.pallas import tpu as pltpu

F32 = jnp.float32
BF16 = jnp.bfloat16
I32 = jnp.int32
U32 = jnp.uint32

LANES = 128
SUBLANES = 8
HEAD_DIM = 128
LRU_BLOCK_DIM = 256
LRU_C = 8.0
CONV_WIDTH = 4
N_EXPERTS = 64
TOP_K = 6
N_GROUPS = 8
GROUP_SIZE = N_EXPERTS // N_GROUPS
TOPK_GROUPS = 4
ROUTED_SCALE = 2.5
NORM_EPS = 1e-6
N_MOD = 6
TIME_GROUP = 16
LOG2_E = 1.4426950408889634
ATTN_HEADS_PER_STEP = 8
F32_ZERO_WEIGHT_LOG2 = -150.0
ROUTER_TOKENS = 768
COMBINE_ROW_CHUNKS = 4
DEST_ROWS = 8
EXPERT_ROW_BLOCK = 512
V7X_VMEM_LIMIT_BYTES = 56 * 1024 * 1024
NT_DIMS = (((1,), (1,)), ((), ()))
NEG_INF = float("-inf")


def _cparams(semantics):
    return pltpu.CompilerParams(dimension_semantics=semantics, vmem_limit_bytes=V7X_VMEM_LIMIT_BYTES)


def _dot(a, b):
    return jnp.dot(a, b, preferred_element_type=F32)


def _split_bf16(x):
    hi = x.astype(BF16)
    lo = (x - hi.astype(F32)).astype(BF16)
    return hi, lo


def _dot3(a, b):
    ah, al = _split_bf16(a)
    bh, bl = _split_bf16(b)
    return _dot(ah, bh) + (_dot(al, bh) + _dot(ah, bl))


def _pack_pairs(h):
    n = h.shape[1] // 2
    bits = lax.bitcast_convert_type(h.astype(BF16).astype(F32), U32)
    return bits[:, :n] | (bits[:, n:] >> 16)


def _store_token_tiles(ref, packed):
    m, n = packed.shape
    nsub = n // LANES
    for j in range(nsub):
        ref[pl.ds(j, m, stride=nsub), :] = packed[:, j * LANES:(j + 1) * LANES]


def _tile_rows(d):
    return d // 2 // LANES


def _load_token_tiles(ref, nsub):
    m = ref.shape[0] // nsub
    return jnp.concatenate([ref[pl.ds(j, m, stride=nsub), :] for j in range(nsub)], axis=1)


def _unpack_pairs(p, dtype):
    a = lax.bitcast_convert_type(p & jnp.uint32(0xFFFF0000), F32)
    b = lax.bitcast_convert_type(p << 16, F32)
    return jnp.concatenate([a.astype(dtype), b.astype(dtype)], axis=1)


def _rms(x):
    return x * lax.rsqrt(jnp.mean(x * x, axis=-1, keepdims=True) + NORM_EPS)


def _ada_kernel(c_ref, w_ref, b_ref, o_ref):
    c = c_ref[...]
    o_ref[...] = _dot3(c * jax.nn.sigmoid(c), w_ref[...]) + b_ref[...]


def _ada(c, w, b):
    m, d = c.shape
    n = w.shape[1]
    tn = min(n, 1024)
    return pl.pallas_call(
        _ada_kernel,
        grid=(n // tn,),
        in_specs=[pl.BlockSpec((m, d), lambda j: (0, 0)),
                  pl.BlockSpec((d, tn), lambda j: (0, j)),
                  pl.BlockSpec((1, tn), lambda j: (0, j))],
        out_specs=pl.BlockSpec((m, tn), lambda j: (0, j)),
        out_shape=jax.ShapeDtypeStruct((m, n), F32),
        compiler_params=_cparams(("arbitrary",)),
        name="ada",
    )(c, w, b.reshape(1, n))


def _row_permutation(n_outer, n_inner):
    n = n_outer * n_inner
    out_row = lax.broadcasted_iota(I32, (n, n), 0)
    in_row = lax.broadcasted_iota(I32, (n, n), 1)
    same = (out_row // n_outer == in_row % n_inner) & (out_row % n_outer == in_row // n_inner)
    return same.astype(BF16)


def _norm_mod_kernel(x_ref, mod_ref, g_ref, obm_ref, otm_ref):
    b, tt, d = x_ref.shape
    m = mod_ref[...]
    h = (_rms(x_ref[...]) * g_ref[...].reshape(1, 1, d) * (1.0 + m[:, 1:2, :]) + m[:, 0:1, :]).astype(BF16)
    obm_ref[...] = h
    otm_ref[...] = _dot(_row_permutation(b, tt), h.reshape(b * tt, d)).astype(BF16)


def _norm_mod(x, mod, g):
    b, t, d = x.shape
    tt = TIME_GROUP
    return pl.pallas_call(
        _norm_mod_kernel,
        grid=(t // tt,),
        in_specs=[pl.BlockSpec((b, tt, d), lambda ti: (0, ti, 0)),
                  pl.BlockSpec((b, N_MOD, d), lambda ti: (0, 0, 0)),
                  pl.BlockSpec((1, d), lambda ti: (0, 0))],
        out_specs=[pl.BlockSpec((b, tt, d), lambda ti: (0, ti, 0)),
                   pl.BlockSpec((tt * b, d), lambda ti: (ti, 0))],
        out_shape=[jax.ShapeDtypeStruct((b, t, d), BF16), jax.ShapeDtypeStruct((t * b, d), BF16)],
        compiler_params=_cparams(("arbitrary",)),
        name="norm_mod",
    )(x, mod, g.reshape(1, d))


def _proj_kernel(a_ref, w_ref, o_ref, w_s, *, epilogue):
    @pl.when(pl.program_id(0) == 0)
    def _():
        w_s[...] = w_ref[...].astype(BF16)

    acc = _dot(a_ref[...], w_s[...])
    if epilogue == "gelu":
        acc = jax.nn.gelu(acc)
    elif epilogue == "sigmoid":
        acc = jax.nn.sigmoid(acc)
    elif epilogue == "qscale":
        acc = acc * (HEAD_DIM ** -0.5 * LOG2_E)
    o_ref[...] = acc.astype(o_ref.dtype)


def _resident(shape, index_map):
    return pl.BlockSpec(shape, index_map, pipeline_mode=pl.Buffered(1))


def _proj(a, w, col, width, epilogue, out_dtype, tm):
    m, k = a.shape
    return pl.pallas_call(
        functools.partial(_proj_kernel, epilogue=epilogue),
        grid=(m // tm,),
        in_specs=[pl.BlockSpec((tm, k), lambda i: (i, 0)),
                  _resident((k, width), lambda i: (0, col))],
        out_specs=pl.BlockSpec((tm, width), lambda i: (i, 0)),
        out_shape=jax.ShapeDtypeStruct((m, width), out_dtype),
        scratch_shapes=[pltpu.VMEM((k, width), BF16)],
        compiler_params=_cparams(("arbitrary",)),
        name="proj_" + epilogue,
    )(a, w)


def _lru_kernel(u_ref, gl_ref, sl_ref, cw_ref, cb_ref, wa_ref, ba_ref, wx_ref, bx_ref, lam_ref, h0_ref, prev_ref,
                y_ref, hl_ref, cn_ref, tail_s, h_s, a_s, b_s, y_s, *, first_pos_zero):
    tc, nb, wb = u_ref.shape
    ti = pl.program_id(1)

    @pl.when(ti == 0)
    def _():
        tail_s[...] = prev_ref[...]
        h_s[...] = h0_ref[...]

    u = u_ref[...].astype(F32)
    xp = jnp.concatenate([tail_s[...], u], axis=0)
    cw = cw_ref[...]
    xc = cb_ref[...].reshape(1, 1, wb)
    for j in range(CONV_WIDTH):
        xc = xc + xp[j:j + tc] * cw[j:j + 1].reshape(1, 1, wb)
    last_rows = u[tc - (CONV_WIDTH - 1):]
    tail_s[...] = last_rows
    cn_ref[...] = last_rows

    x2 = xc.reshape(tc * nb, wb)
    xb = x2.astype(BF16)
    r = jax.nn.sigmoid(_dot(xb, wa_ref[0]) + ba_ref[0])
    gi = jax.nn.sigmoid(_dot(xb, wx_ref[0]) + bx_ref[0])
    log_a = r * ((-LRU_C) * jax.nn.softplus(-lam_ref[...]))
    a = jnp.exp(log_a)
    m2 = -jnp.tanh(log_a) * (a * a + 1.0)
    mult = jnp.where(m2 > 0.0, m2 * lax.rsqrt(m2), 0.0)
    if first_pos_zero:
        row = lax.broadcasted_iota(I32, (tc * nb, wb), 0)
        mult = jnp.where((row < nb) & (ti == 0), 1.0, mult)
    a_s[...] = a.reshape(tc, nb, wb)
    b_s[...] = (mult * (gi * x2)).reshape(tc, nb, wb)

    def step(t, h):
        h = a_s[t] * h + b_s[t]
        gate = gl_ref[t].astype(F32) * sl_ref[t].astype(F32)
        y_s[t] = h * gate
        return h

    h = lax.fori_loop(0, tc, step, h_s[...], unroll=8)
    h_s[...] = h
    hl_ref[...] = h

    perm = _row_permutation(TIME_GROUP, nb)
    for g in range(tc // TIME_GROUP):
        rows = slice(g * TIME_GROUP, (g + 1) * TIME_GROUP)
        yg = y_s[rows].reshape(TIME_GROUP * nb, wb).astype(BF16)
        y_ref[:, rows, :] = _dot(perm, yg).astype(BF16).reshape(nb, TIME_GROUP, wb)


def _lru(u, gl, sl, conv_w, conv_b, wa, ba, wx, bx, lam, h0, prev, tc, first_pos_zero):
    t, b, w = u.shape
    wb = LRU_BLOCK_DIM
    nw = w // wb
    seq = pl.BlockSpec((tc, b, wb), lambda wi, ti: (ti, 0, wi))
    seq_out = pl.BlockSpec((b, tc, wb), lambda wi, ti: (0, ti, wi))
    chan = lambda rows: pl.BlockSpec((rows, wb), lambda wi, ti: (0, wi))
    gate_w = pl.BlockSpec((1, wb, wb), lambda wi, ti: (wi, 0, 0))
    gate_b = pl.BlockSpec((1, 1, wb), lambda wi, ti: (wi, 0, 0))
    tail = pl.BlockSpec((CONV_WIDTH - 1, b, wb), lambda wi, ti: (0, 0, wi))
    return pl.pallas_call(
        functools.partial(_lru_kernel, first_pos_zero=first_pos_zero),
        grid=(nw, t // tc),
        in_specs=[seq, seq, seq, chan(CONV_WIDTH), chan(1), gate_w, gate_b, gate_w, gate_b, chan(1),
                  chan(b), tail],
        out_specs=[seq_out, chan(b), tail],
        out_shape=[jax.ShapeDtypeStruct((b, t, w), BF16), jax.ShapeDtypeStruct((b, w), F32),
                   jax.ShapeDtypeStruct((CONV_WIDTH - 1, b, w), F32)],
        scratch_shapes=[pltpu.VMEM((CONV_WIDTH - 1, b, wb), F32), pltpu.VMEM((b, wb), F32),
                        pltpu.VMEM((tc, b, wb), F32), pltpu.VMEM((tc, b, wb), F32),
                        pltpu.VMEM((tc, b, wb), F32)],
        compiler_params=_cparams(("arbitrary", "arbitrary")),
        name="lru",
    )(u, gl, sl, conv_w, conv_b.reshape(1, w), wa, ba.reshape(nw, 1, wb), wx, bx.reshape(nw, 1, wb),
      lam.reshape(1, w), h0, prev)


def _log_keep_and_logsig(z):
    nz = -z
    lk = jnp.minimum(nz, 0.0) - jnp.log2(1.0 + jnp.exp2(jnp.minimum(z, nz)))
    return lk, z + lk


def _later_key_sums(lk, tri):
    hi, lo = _split_bf16(lk)
    return _dot(hi, tri) + _dot(lo, tri)


def _strict_lower(n, dtype):
    row = lax.broadcasted_iota(I32, (n, n), 0)
    col = lax.broadcasted_iota(I32, (n, n), 1)
    return (row > col).astype(dtype), col < row


def _attn_kernel(q_ref, k_ref, v_ref, sz_ref, yl_ref, o_ref):
    tq = q_ref.shape[1]
    nh = q_ref.shape[2] // HEAD_DIM
    qi = pl.program_id(2)
    tri, causal = _strict_lower(tq, BF16)

    def tile(j, carry, diag):
        start = pl.multiple_of(j * tq, tq)
        heads = [slice(h * HEAD_DIM, (h + 1) * HEAD_DIM) for h in range(nh)]
        zs = [lax.dot_general(q_ref[0, :, cols], k_ref[0, pl.ds(start, tq), cols].astype(BF16), NT_DIMS,
                              preferred_element_type=F32) for cols in heads]
        lks, lss = [], []
        for z in zs:
            lk, ls = _log_keep_and_logsig(z)
            lks.append(jnp.where(causal, lk, 0.0) if diag else lk)
            lss.append(ls)
        rcs = [_later_key_sums(lk, tri) for lk in lks]
        ws = []
        for h in range(nh):
            w = jnp.exp2(lss[h] + rcs[h] + carry[h][1])
            ws.append((jnp.where(causal, w, 0.0) if diag else w).astype(BF16))
        return tuple((carry[h][0] + _dot(ws[h], v_ref[0, pl.ds(start, tq), heads[h]].astype(BF16)),
                      carry[h][1] + (rcs[h][:, 0:1] + lks[h][:, 0:1])) for h in range(nh))

    def any_weight_left(carry):
        c_max = carry[0][1]
        for h in range(1, nh):
            c_max = jnp.maximum(c_max, carry[h][1])
        return jnp.max(c_max) > F32_ZERO_WEIGHT_LOG2

    init = tuple((jnp.zeros((tq, HEAD_DIM), F32), jnp.zeros((tq, 1), F32)) for _ in range(nh))
    carry = tile(qi, init, True)

    def more(state):
        return (state[0] < qi) & state[1]

    def farther_tile(state):
        carry = tile(qi - 1 - state[0], state[2], False)
        return state[0] + 1, any_weight_left(carry), carry

    carry = lax.while_loop(more, farther_tile, (jnp.int32(0), any_weight_left(carry), carry))[2]
    acc = jnp.concatenate([carry[h][0] for h in range(nh)], axis=1)
    o_ref[0] = (yl_ref[0].astype(F32) + sz_ref[0].astype(F32) * acc).astype(BF16)


def _attn_prompt(q, k, v, sz, yl, tq):
    b, t, a = q.shape
    wblk = min(ATTN_HEADS_PER_STEP * HEAD_DIM, a)
    qspec = pl.BlockSpec((1, tq, wblk), lambda bi, hi, qi: (bi, qi, hi))
    kvspec = pl.BlockSpec((1, t, wblk), lambda bi, hi, qi: (bi, 0, hi))
    return pl.pallas_call(
        _attn_kernel,
        grid=(b, a // wblk, t // tq),
        in_specs=[qspec, kvspec, kvspec, qspec, qspec],
        out_specs=qspec,
        out_shape=jax.ShapeDtypeStruct((b, t, a), BF16),
        compiler_params=_cparams(("arbitrary", "arbitrary", "arbitrary")),
        name="attn_prompt",
    )(q, k, v, sz, yl)


def _attn_sample_kernel(q_ref, kn_ref, vn_ref, kc_ref, vc_ref, sz_ref, yl_ref, o_ref, lk_s, ls_s):
    ts = q_ref.shape[1]
    nh = q_ref.shape[2] // HEAD_DIM
    past = kc_ref.shape[1] // nh
    tri_n, causal = _strict_lower(ts, BF16)
    tri_p, _ = _strict_lower(past, BF16)
    accs, carries = [], []
    for h in range(nh):
        cols = slice(h * HEAD_DIM, (h + 1) * HEAD_DIM)
        q = q_ref[0, :, cols]
        zn = lax.dot_general(q, kn_ref[0, :, cols].astype(BF16), NT_DIMS, preferred_element_type=F32)
        lk, ls = _log_keep_and_logsig(zn)
        lk = jnp.where(causal, lk, 0.0)
        rc = _later_key_sums(lk, tri_n)
        w = jnp.where(causal, jnp.exp2(ls + rc), 0.0)
        accs.append(_dot(w.astype(BF16), vn_ref[0, :, cols].astype(BF16)))
        carries.append(rc[:, 0:1] + lk[:, 0:1])
        head_rows = pl.ds(h, past, stride=nh)
        zp = lax.dot_general(q, kc_ref[0, head_rows, :].astype(BF16), NT_DIMS, preferred_element_type=F32)
        lkp, lsp = _log_keep_and_logsig(zp)
        lk_s[h * ts:(h + 1) * ts, :] = lkp
        ls_s[h * ts:(h + 1) * ts, :] = lsp
    rcp = _later_key_sums(lk_s[...], tri_p)
    for h in range(nh):
        cols = slice(h * HEAD_DIM, (h + 1) * HEAD_DIM)
        rows = slice(h * ts, (h + 1) * ts)
        w = jnp.exp2(ls_s[rows, :] + rcp[rows, :] + carries[h])
        acc = accs[h] + _dot(w.astype(BF16), vc_ref[0, pl.ds(h, past, stride=nh), :].astype(BF16))
        o_ref[0, :, cols] = (yl_ref[0, :, cols].astype(F32) + sz_ref[0, :, cols].astype(F32) * acc).astype(BF16)


def _attn_sample(q, kn, vn, kc, vc, sz, yl):
    b, ts, a = q.shape
    past = kc.shape[1]
    nh = a // HEAD_DIM
    kc = kc.reshape(b, past * nh, HEAD_DIM)
    vc = vc.reshape(b, past * nh, HEAD_DIM)
    cur = pl.BlockSpec((1, ts, a), lambda bi: (bi, 0, 0))
    cache = pl.BlockSpec((1, past * nh, HEAD_DIM), lambda bi: (bi, 0, 0))
    return pl.pallas_call(
        _attn_sample_kernel,
        grid=(b,),
        in_specs=[cur, cur, cur, cache, cache, cur, cur],
        out_specs=cur,
        out_shape=jax.ShapeDtypeStruct((b, ts, a), BF16),
        scratch_shapes=[pltpu.VMEM((nh * ts, past), F32), pltpu.VMEM((nh * ts, past), F32)],
        compiler_params=_cparams(("arbitrary",)),
        name="attn_sample",
    )(q, kn, vn, kc, vc, sz, yl)


def _outproj_kernel(a_ref, w_ref, x_ref, mod_ref, g_ref, *rest):
    x1_ref, hpk_ref, hlo_ref, w_s = rest[-4:]
    nb = mod_ref.shape[0]
    tm, d = x_ref.shape

    @pl.when(pl.program_id(0) == 0)
    def _():
        w_s[...] = w_ref[...].astype(BF16)

    m = mod_ref[...]
    o = _dot(a_ref[...], w_s[...]).reshape(nb, tm // nb, d)
    x1 = x_ref[...].reshape(nb, tm // nb, d) + m[:, 2:3, :] * o
    h = _rms(x1) * g_ref[...].reshape(1, 1, d) * (1.0 + m[:, 4:5, :]) + m[:, 3:4, :]
    x1_ref[...] = x1.reshape(tm, d)
    h = h.reshape(tm, d)
    hi = h.astype(BF16).astype(F32)
    _store_token_tiles(hpk_ref, _pack_pairs(hi))
    hlo_ref[...] = (h - hi).astype(BF16)


def _outproj(a, w, x, mod, g, tm, rows_per_batch, n_total, row_offset, earlier=None):
    m, d = x.shape
    nb = max(tm // rows_per_batch, 1)
    per = max(rows_per_batch // tm, 1)
    off = row_offset // tm
    nsub = _tile_rows(d)
    row = lambda width: pl.BlockSpec((tm, width), lambda i: (i, 0))
    out_row = lambda width: pl.BlockSpec((tm, width), lambda i: (i + off, 0))
    in_specs = [row(d), _resident((d, d), lambda i: (0, 0)), row(d),
                pl.BlockSpec((nb, N_MOD, d), lambda i: (i // per, 0, 0)),
                pl.BlockSpec((1, d), lambda i: (0, 0))]
    args = [a, w, x, mod, g.reshape(1, d)]
    aliases = {}
    if earlier is not None:
        aliases = {len(args) + j: j for j in range(len(earlier))}
        in_specs += [pl.BlockSpec(memory_space=pl.ANY)] * len(earlier)
        args += list(earlier)
    return pl.pallas_call(
        _outproj_kernel,
        grid=(m // tm,),
        in_specs=in_specs,
        out_specs=[out_row(d), pl.BlockSpec((tm * nsub, LANES), lambda i: (i + off, 0)), out_row(d)],
        out_shape=[jax.ShapeDtypeStruct((n_total, d), F32), jax.ShapeDtypeStruct((n_total * nsub, LANES), U32),
                   jax.ShapeDtypeStruct((n_total, d), BF16)],
        scratch_shapes=[pltpu.VMEM((d, d), BF16)],
        input_output_aliases=aliases,
        compiler_params=_cparams(("arbitrary",)),
        name="outproj",
    )(*args)


def _sub_max(x):
    return jnp.max(x, axis=0, keepdims=True)


def _sub_min(x):
    return jnp.min(x, axis=0, keepdims=True)


def _sub_sum(x):
    return jnp.sum(x, axis=0, keepdims=True)


def _router_kernel(hpk_ref, hlo_ref, wrh_ref, wrl_ref, br_ref, e_ref, w_ref, slot_ref, cnt_ref, base_s):
    tm = hlo_ref.shape[0]

    @pl.when(pl.program_id(0) == 0)
    def _():
        base_s[...] = jnp.zeros_like(base_s)

    hi = _unpack_pairs(_load_token_tiles(hpk_ref, _tile_rows(hlo_ref.shape[1])), BF16)
    lo = hlo_ref[...]
    wrh = wrh_ref[...]
    logits = lax.dot_general(wrh, hi, NT_DIMS, preferred_element_type=F32)
    logits = logits + (lax.dot_general(wrh, lo, NT_DIMS, preferred_element_type=F32)
                       + lax.dot_general(wrl_ref[...], hi, NT_DIMS, preferred_element_type=F32))
    score = jax.nn.sigmoid(logits)
    sel = (score + br_ref[...]).reshape(GROUP_SIZE, N_GROUPS, tm)
    score = score.reshape(GROUP_SIZE, N_GROUPS, tm)

    m1 = sel[0]
    m2 = jnp.full_like(m1, NEG_INF)
    for e in range(1, GROUP_SIZE):
        x = sel[e]
        m2 = jnp.maximum(m2, jnp.minimum(m1, x))
        m1 = jnp.maximum(m1, x)
    cur = m1 + m2
    gidx = lax.broadcasted_iota(I32, (N_GROUPS, tm), 0)
    gmask = jnp.zeros((N_GROUPS, tm), jnp.bool_)
    for _ in range(TOPK_GROUPS):
        first = _sub_min(jnp.where(cur == _sub_max(cur), gidx, N_GROUPS))
        pick = gidx == first
        gmask = gmask | pick
        cur = jnp.where(pick, NEG_INF, cur)

    eid = (lax.broadcasted_iota(I32, (GROUP_SIZE, N_GROUPS, tm), 1) * GROUP_SIZE
           + lax.broadcasted_iota(I32, (GROUP_SIZE, N_GROUPS, tm), 0))
    cur = jnp.where(gmask[None], sel, NEG_INF)
    ids, wts, picks = [], [], []
    for _ in range(TOP_K):
        mx = _sub_max(jnp.max(cur, axis=0))
        first = _sub_min(jnp.min(jnp.where(cur == mx[None], eid, N_EXPERTS), axis=0))
        pick = eid == first[None]
        ids.append(first)
        picks.append(pick)
        wts.append(_sub_sum(jnp.sum(jnp.where(pick, score, 0.0), axis=0)))
        cur = jnp.where(pick, NEG_INF, cur)
    total = wts[0]
    chosen = picks[0]
    for k in range(1, TOP_K):
        total = total + wts[k]
        chosen = chosen | picks[k]

    chosen = jnp.where(chosen, 1.0, 0.0).reshape(N_EXPERTS, tm)
    earlier_or_self = (lax.broadcasted_iota(I32, (tm, tm), 0) <= lax.broadcasted_iota(I32, (tm, tm), 1)).astype(BF16)
    incl = _dot(chosen.astype(BF16), earlier_or_self)
    rank = (base_s[...] + (incl - chosen)).reshape(GROUP_SIZE, N_GROUPS, tm)
    base_s[...] = base_s[...] + incl[:, tm - 1:tm]
    slots = [_sub_sum(jnp.sum(jnp.where(pick, rank, 0.0), axis=0)).astype(I32) for pick in picks]

    pad = 8 - TOP_K
    e_ref[...] = jnp.concatenate(ids + [jnp.zeros((pad, tm), I32)], axis=0)
    w_ref[...] = jnp.concatenate([w / total * ROUTED_SCALE for w in wts] + [jnp.zeros((pad, tm), F32)], axis=0)
    slot_ref[...] = jnp.concatenate(slots + [jnp.zeros((pad, tm), I32)], axis=0)
    cnt_ref[...] = jnp.broadcast_to(base_s[...], cnt_ref.shape)


def _router(hpk, hlo, w_router, b_router, tm):
    n, d = hlo.shape
    wr = w_router.T.reshape(N_GROUPS, GROUP_SIZE, d).transpose(1, 0, 2).reshape(N_EXPERTS, d)
    br = b_router.reshape(N_GROUPS, GROUP_SIZE).T.reshape(N_EXPERTS, 1)
    wrh = wr.astype(BF16)
    wrl = (wr - wrh.astype(F32)).astype(BF16)
    full = lambda shape: pl.BlockSpec(shape, lambda i: (0, 0))
    per_token = pl.BlockSpec((8, tm), lambda i: (0, i))
    e_k, w_k, slot_k, counts = pl.pallas_call(
        _router_kernel,
        grid=(n // tm,),
        in_specs=[pl.BlockSpec((tm * _tile_rows(d), LANES), lambda i: (i, 0)),
                  pl.BlockSpec((tm, d), lambda i: (i, 0)),
                  full((N_EXPERTS, d)), full((N_EXPERTS, d)), full((N_EXPERTS, 1))],
        out_specs=[per_token, per_token, per_token, full((N_EXPERTS, 128))],
        out_shape=[jax.ShapeDtypeStruct((8, n), I32), jax.ShapeDtypeStruct((8, n), F32),
                   jax.ShapeDtypeStruct((8, n), I32), jax.ShapeDtypeStruct((N_EXPERTS, 128), F32)],
        scratch_shapes=[pltpu.VMEM((N_EXPERTS, 1), F32)],
        compiler_params=_cparams(("arbitrary",)),
        name="router",
    )(hpk, hlo, wrh, wrl, br)
    counts = counts[:, 0].reshape(GROUP_SIZE, N_GROUPS).T.reshape(N_EXPERTS).astype(I32)
    return e_k, w_k, slot_k, counts


def _tile_window(token, nsub):
    start = token * nsub
    if nsub % SUBLANES == 0:
        start = pl.multiple_of(start, SUBLANES)
    return pl.ds(start, nsub)


def _token_copy(src, src_token, dst, dst_token, nsub, sem):
    return pltpu.make_async_copy(src.at[_tile_window(src_token, nsub)], dst.at[_tile_window(dst_token, nsub)], sem)


def _for_token_groups(n_tokens, per_token):
    def group(g, carry):
        t0 = pl.multiple_of(g * SUBLANES, SUBLANES)
        for u in range(SUBLANES):
            per_token(t0, u)
        return carry

    lax.fori_loop(0, n_tokens // SUBLANES, group, 0)


def _dispatch_kernel(pad_end_ref, dest_ref, h_ref, xs_ref, zero_s, sem, zero_sem, *, nsub):
    tb = dest_ref.shape[1]
    te = zero_s.shape[0] // nsub

    def tail_fill(e):
        start = pl.multiple_of((pad_end_ref[e] - te) * nsub, te * nsub)
        return pltpu.make_async_copy(zero_s, xs_ref.at[pl.ds(start, te * nsub)], zero_sem)

    def has_rows(e):
        return pad_end_ref[e] > (pad_end_ref[e - 1] if e else 0)

    @pl.when(pl.program_id(0) == 0)
    def _():
        zero_s[...] = jnp.zeros_like(zero_s)
        for e in range(N_EXPERTS):
            pl.when(has_rows(e))(lambda e=e: tail_fill(e).start())
        for e in range(N_EXPERTS):
            pl.when(has_rows(e))(lambda e=e: tail_fill(e).wait())

    def start(t0, u):
        for k in range(TOP_K):
            _token_copy(h_ref, t0 + u, xs_ref, dest_ref[k, t0 + u], nsub,
                        sem).start(priority=k % 2)

    def wait(t0, u):
        for k in range(TOP_K):
            _token_copy(h_ref, 0, xs_ref, 0, nsub, sem).wait()

    _for_token_groups(tb, start)
    _for_token_groups(tb, wait)


def _dispatch(pad_end, dest, hpk, n_rows, tb, te, nsub):
    n = hpk.shape[0] // nsub
    grid_spec = pltpu.PrefetchScalarGridSpec(
        num_scalar_prefetch=1,
        grid=(n // tb,),
        in_specs=[pl.BlockSpec((DEST_ROWS, tb), lambda i, pe: (0, i), memory_space=pltpu.SMEM),
                  pl.BlockSpec((tb * nsub, LANES), lambda i, pe: (i, 0))],
        out_specs=pl.BlockSpec(memory_space=pl.ANY),
        scratch_shapes=[pltpu.VMEM((te * nsub, LANES), U32), pltpu.SemaphoreType.DMA(()),
                        pltpu.SemaphoreType.DMA(())],
    )
    return pl.pallas_call(
        functools.partial(_dispatch_kernel, nsub=nsub),
        grid_spec=grid_spec,
        out_shape=jax.ShapeDtypeStruct((n_rows * nsub, LANES), U32),
        compiler_params=_cparams(("arbitrary",)),
        name="dispatch",
    )(pad_end, dest, hpk)


def _experts_kernel(be_ref, nu_ref, x_ref, wg_ref, wu_ref, wd_ref, o_ref, wg_s, wu_s, wd_s):
    i = pl.program_id(0)
    prev = be_ref[jnp.maximum(i - 1, 0)]

    @pl.when((i == 0) | (be_ref[i] != prev))
    def _():
        wg_s[...] = wg_ref[0].astype(BF16)
        wu_s[...] = wu_ref[0].astype(BF16)
        wd_s[...] = wd_ref[0].astype(BF16)

    @pl.when(i < nu_ref[0])
    def _():
        x = _unpack_pairs(_load_token_tiles(x_ref, _tile_rows(wg_s.shape[0])), BF16)
        g = _dot(x, wg_s[...])
        hidden = (g * jax.nn.sigmoid(g)) * _dot(x, wu_s[...])
        _store_token_tiles(o_ref, _pack_pairs(_dot(hidden.astype(BF16), wd_s[...])))

    @pl.when(i >= nu_ref[0])
    def _():
        o_ref[...] = jnp.zeros_like(o_ref)


def _experts(block_expert, n_used, xs, w_gate, w_up, w_down, tm):
    d, ff = w_gate.shape[1:]
    nsub = _tile_rows(d)
    p = xs.shape[0] // nsub
    rows = pl.BlockSpec((tm * nsub, LANES), lambda i, be, nu: (i, 0))
    grid_spec = pltpu.PrefetchScalarGridSpec(
        num_scalar_prefetch=2,
        grid=(p // tm,),
        in_specs=[rows,
                  pl.BlockSpec((1, d, ff), lambda i, be, nu: (be[i], 0, 0)),
                  pl.BlockSpec((1, d, ff), lambda i, be, nu: (be[i], 0, 0)),
                  pl.BlockSpec((1, ff, d), lambda i, be, nu: (be[i], 0, 0))],
        out_specs=rows,
        scratch_shapes=[pltpu.VMEM((d, ff), BF16), pltpu.VMEM((d, ff), BF16), pltpu.VMEM((ff, d), BF16)],
    )
    return pl.pallas_call(
        _experts_kernel,
        grid_spec=grid_spec,
        out_shape=jax.ShapeDtypeStruct(xs.shape, U32),
        compiler_params=_cparams(("arbitrary",)),
        name="experts",
    )(block_expert, n_used, xs, w_gate, w_up, w_down)


def _combine_kernel(dest_ref, wk_ref, ys_ref, x1_ref, hpk_ref, gtp_ref, gts_ref, wsg_ref, wsu_ref, wsd_ref, gf_ref,
                    yp_ref, ysm_ref, buf, sem, *, n_prompt_blocks):
    tb, d = x1_ref.shape
    nsub = _tile_rows(d)
    seg = gts_ref.shape[0]
    i = pl.program_id(0)

    def start(t0, u):
        for k in range(TOP_K):
            _token_copy(ys_ref, dest_ref[k, t0 + u], buf.at[k], t0 + u, nsub,
                        sem).start(priority=k % 2)

    def wait(t0, u):
        for k in range(TOP_K):
            _token_copy(ys_ref, 0, buf.at[k], 0, nsub, sem).wait()

    _for_token_groups(tb, start)
    x = _unpack_pairs(_load_token_tiles(hpk_ref, nsub), BF16)
    g = _dot(x, wsg_ref[...])
    ffn = _dot(((g * jax.nn.sigmoid(g)) * _dot(x, wsu_ref[...])).astype(BF16), wsd_ref[...])
    _for_token_groups(tb, wait)
    gate2 = jnp.broadcast_to(jnp.where(i >= n_prompt_blocks, gts_ref[...], gtp_ref[...]), (seg, 1, d))
    rc = tb // COMBINE_ROW_CHUNKS
    sc = seg // COMBINE_ROW_CHUNKS
    chunks = []
    for c in range(COMBINE_ROW_CHUNKS):
        rows = slice(c * rc, (c + 1) * rc)
        acc = ffn[rows]
        wk = wk_ref[rows, :]
        for k in range(TOP_K):
            tiles = buf.at[k]
            packed = jnp.concatenate([tiles[pl.ds(c * rc * nsub + j, rc, stride=nsub), :] for j in range(nsub)],
                                     axis=1)
            acc = acc + wk[:, k:k + 1] * _unpack_pairs(packed, F32)
        x2 = x1_ref[rows, :].reshape(sc, rc // sc, d) + gate2[c * sc:(c + 1) * sc] * acc.reshape(sc, rc // sc, d)
        chunks.append((_rms(x2) * gf_ref[...].reshape(1, 1, d)).reshape(rc, d))
    y = jnp.concatenate(chunks, axis=0)

    @pl.when(i < n_prompt_blocks)
    def _():
        yp_ref[...] = y

    @pl.when(i >= n_prompt_blocks)
    def _():
        ysm_ref[...] = y


def _combine(dest, wk_t, ys, x1, hpk, gate2_p, gate2_s, ws_gate, ws_up, ws_down, g_final, n_prompt, tb):
    n, d = x1.shape
    nsub = _tile_rows(d)
    ff = ws_gate.shape[1]
    npb = n_prompt // tb
    nsb = (n - n_prompt) // tb
    b, bs = gate2_p.shape[0], gate2_s.shape[0]
    per = npb // b
    assert nsb == 1 and npb % b == 0 and tb % bs == 0
    row = lambda width: pl.BlockSpec((tb, width), lambda i: (i, 0))
    full = lambda shape: pl.BlockSpec(shape, lambda i: (0,) * len(shape))
    return pl.pallas_call(
        functools.partial(_combine_kernel, n_prompt_blocks=npb),
        grid=(n // tb,),
        in_specs=[pl.BlockSpec((DEST_ROWS, tb), lambda i: (0, i), memory_space=pltpu.SMEM),
                  row(8), pl.BlockSpec(memory_space=pl.ANY), row(d),
                  pl.BlockSpec((tb * nsub, LANES), lambda i: (i, 0)),
                  pl.BlockSpec((1, 1, d), lambda i: (jnp.minimum(i // per, b - 1), 0, 0)),
                  full((bs, 1, d)),
                  full((d, ff)), full((d, ff)), full((ff, d)), full((1, d))],
        out_specs=[pl.BlockSpec((tb, d), lambda i: (jnp.minimum(i, npb - 1), 0)),
                   pl.BlockSpec((tb, d), lambda i: (jnp.clip(i - npb, 0, nsb - 1), 0))],
        out_shape=[jax.ShapeDtypeStruct((n_prompt, d), F32), jax.ShapeDtypeStruct((n - n_prompt, d), F32)],
        scratch_shapes=[pltpu.VMEM((TOP_K, tb * nsub, LANES), U32), pltpu.SemaphoreType.DMA(())],
        compiler_params=_cparams(("arbitrary",)),
        name="combine",
    )(dest, wk_t, ys, x1, hpk, gate2_p, gate2_s, ws_gate, ws_up, ws_down, g_final.reshape(1, d))


def _largest_tile(n, target):
    t = min(n, target)
    while n % t:
        t -= 1
    return t


def _mixer(x, mod, past_k, past_v, h0, conv0, p, first_pos_zero, n_total, row_offset, earlier):
    b, t, d = x.shape
    m = b * t
    a = d
    hn_bm, hn_tm = _norm_mod(x, mod, p["g_mix"])
    hn_bm = hn_bm.reshape(m, d)
    tm = _largest_tile(m, 512)
    tmh = _largest_tile(m, 1024)
    w_in = p["w_in"]
    u = _proj(hn_tm, w_in, 0, d, "plain", BF16, tmh).reshape(t, b, d)
    gl = _proj(hn_tm, w_in, 1, d, "gelu", BF16, tmh).reshape(t, b, d)
    sl = _proj(hn_tm, w_in, 5, d, "sigmoid", BF16, tmh).reshape(t, b, d)
    q = _proj(hn_bm, w_in, 2, a, "qscale", BF16, tmh).reshape(b, t, a)
    k = _proj(hn_bm, w_in, 3, a, "plain", F32, tm).reshape(b, t, a)
    v = _proj(hn_bm, w_in, 4, a, "plain", F32, tm).reshape(b, t, a)
    sz = _proj(hn_bm, w_in, 6, a, "sigmoid", BF16, tmh).reshape(b, t, a)

    if h0 is None:
        h0 = jnp.zeros((b, d), F32)
        prev = jnp.zeros((CONV_WIDTH - 1, b, d), F32)
    else:
        prev = conv0.transpose(1, 0, 2)
    tc = _largest_tile(t, 128)
    yl, h_last, conv_tail = _lru(u, gl, sl, p["conv_w"], p["conv_b"], p["w_rg_a"], p["b_rg_a"], p["w_rg_x"],
                                 p["b_rg_x"], p["lru_lambda"], h0, prev, tc, first_pos_zero)
    if past_k is None:
        mixed = _attn_prompt(q, k, v, sz, yl, _largest_tile(t, 256))
    else:
        mixed = _attn_sample(q, k, v, past_k, past_v, sz, yl)
    tmo = _largest_tile(m, 256)
    shared = _outproj(mixed.reshape(m, a), p["w_out"], x.reshape(m, d), mod, p["g_ffn"], tmo, t, n_total, row_offset,
                      earlier)
    nh = a // HEAD_DIM
    return (shared, k.reshape(b, t, nh, HEAD_DIM), v.reshape(b, t, nh, HEAD_DIM), h_last,
            conv_tail.transpose(1, 0, 2))


def _routing_tables(e_k, slot_k, counts, tm):
    n_tokens = e_k.shape[1]
    padded = (counts + tm - 1) // tm * tm
    pad_end = jnp.cumsum(padded).astype(I32)
    pad_start = pad_end - padded
    is_expert = e_k[:, :, None] == jnp.arange(N_EXPERTS, dtype=I32)
    dest = slot_k + jnp.sum(jnp.where(is_expert, pad_start, 0), axis=-1).astype(I32)
    n_blocks = -(-(TOP_K * n_tokens + N_EXPERTS * (tm - 1)) // tm)
    block_start = jnp.arange(n_blocks, dtype=I32) * tm
    block_expert = jnp.minimum(jnp.sum(pad_end[None, :] <= block_start[:, None], axis=1), N_EXPERTS - 1).astype(I32)
    n_used = (pad_end[-1:] // tm).astype(I32)
    return dest, pad_end, block_expert, n_used, n_blocks * tm


def kernel(x_prompt, x_sample, cache_k, cache_v, state_lru, state_conv, c_prompt, c_sample, g_mix, w_ada, b_ada, w_in, conv_w, conv_b, w_rg_a, b_rg_a, w_rg_x, b_rg_x, lru_lambda, w_out, g_ffn, w_router, b_router, w_gate, w_up, w_down, ws_gate, ws_up, ws_down, g_final):
    assert g_mix.shape[0] == 1, "single-layer trunk only"
    b, t, d = x_prompt.shape
    bs, ts, _ = x_sample.shape
    assert ts >= CONV_WIDTH - 1 and ts == TIME_GROUP and t % TIME_GROUP == 0
    p = dict(g_mix=g_mix[0], w_in=w_in[0], conv_w=conv_w[0], conv_b=conv_b[0],
             w_rg_a=w_rg_a[0].astype(BF16), b_rg_a=b_rg_a[0], w_rg_x=w_rg_x[0].astype(BF16), b_rg_x=b_rg_x[0],
             lru_lambda=lru_lambda[0], w_out=w_out[0], g_ffn=g_ffn[0])
    mod = _ada(jnp.concatenate([c_prompt, c_sample], axis=0), w_ada[0], b_ada[0]).reshape(b + bs, N_MOD, d)
    mod_p, mod_s = mod[:b], mod[b:]

    n_prompt = b * t
    n = n_prompt + bs * ts
    tb = _largest_tile(bs * ts, 256)
    assert n_prompt % tb == 0 and tb % ts == 0

    shared, kp, vp, hp, cp = _mixer(x_prompt, mod_p, None, None, None, None, p, True, n, 0, None)
    (x1, hpk, hlo), ks, vs, hs, cs = _mixer(x_sample, mod_s, cache_k[0], cache_v[0], state_lru[0], state_conv[0],
                                            p, False, n, n_prompt, shared)

    e_k, w_k, slot_k, counts = _router(hpk, hlo, w_router[0], b_router[0], _largest_tile(n, ROUTER_TOKENS))
    tme = EXPERT_ROW_BLOCK
    dest, pad_end, block_expert, n_used, n_rows = _routing_tables(e_k, slot_k, counts, tme)
    xs = _dispatch(pad_end, dest, hpk, n_rows, tb, tme, _tile_rows(d))
    ys = _experts(block_expert, n_used, xs, w_gate[0], w_up[0], w_down[0], tme)
    y_p, y_s = _combine(dest, w_k.T, ys, x1, hpk, mod_p[:, 5:6, :], mod_s[:, 5:6, :], ws_gate[0].astype(BF16),
                        ws_up[0].astype(BF16), ws_down[0].astype(BF16), g_final, n_prompt, tb)
    return (y_p.reshape(b, t, d), y_s.reshape(bs, ts, d), kp[None], vp[None], hp[None], cp[None],
            ks[None], vs[None], hs[None], cs[None])
```
